```python
import jax, jax.numpy as jnp
from jax import lax
import numpy as np

D_MODEL = 1024
BATCH = 2
SEQ = 16384
DEPTH = 1

ATT_Q_HEADS = 8
ATT_KV_HEADS = 2
ATT_HEAD_DIM = 64
WINDOW = 128
ATT_BLOCK = 128
GLA_HEADS = 4
GLA_DK = 64
GLA_DV = 128
GLA_GATE_RANK = 16
GLA_GATE_NORMALIZER = 16.0
GLA_CHUNK = 64
N_EXPERTS = 32
TOP_K = 4
D_EXPERT = 1024
SWIGLU_LIMIT = 7.0
SWIGLU_ALPHA = 1.702
MOE_BLOCK = 128
NORM_EPS = 1e-6

ATT_WIDTH = ATT_Q_HEADS * ATT_HEAD_DIM
ATT_KV_WIDTH = ATT_KV_HEADS * ATT_HEAD_DIM
GLA_K_WIDTH = GLA_HEADS * GLA_DK
GLA_V_WIDTH = GLA_HEADS * GLA_DV
MIX_WIDTH = ATT_WIDTH + GLA_V_WIDTH
IN_SPLITS = (ATT_WIDTH, ATT_KV_WIDTH, ATT_KV_WIDTH, GLA_K_WIDTH, GLA_K_WIDTH, GLA_V_WIDTH, GLA_V_WIDTH, GLA_GATE_RANK)
IN_WIDTH = 2 * ATT_WIDTH + 2 * ATT_KV_WIDTH + 2 * GLA_K_WIDTH + 2 * GLA_V_WIDTH + GLA_GATE_RANK - ATT_WIDTH

kernel_name = "hymba_swa_sink_gla_moe_sandwich_adaln"


def rms_norm(t, g):
    tf = t.astype(jnp.float32)
    tf = tf * lax.rsqrt(jnp.mean(tf * tf, axis=-1, keepdims=True) + NORM_EPS)
    return (tf * g.astype(jnp.float32)).astype(t.dtype)


def split_cols(t, sizes):
    idx = np.cumsum(sizes)[:-1].tolist()
    return jnp.split(t, idx, axis=-1)


def sliding_window_sink_attention(q, k, v, sinks):
    B, S = q.shape[0], q.shape[1]
    nb = S // ATT_BLOCK
    G = ATT_Q_HEADS // ATT_KV_HEADS
    qb = q.reshape(B, nb, ATT_BLOCK, ATT_KV_HEADS, G, ATT_HEAD_DIM)

    def with_prev(t):
        tb = t.reshape(B, nb, ATT_BLOCK, ATT_KV_HEADS, ATT_HEAD_DIM)
        prev = jnp.pad(tb[:, :-1], ((0, 0), (1, 0), (0, 0), (0, 0), (0, 0)))
        return jnp.concatenate([prev, tb], axis=2)

    kb, vb = with_prev(k), with_prev(v)
    scale = ATT_HEAD_DIM ** -0.5
    s = jnp.einsum('bnqhgd,bnkhd->bnhgqk', qb, kb).astype(jnp.float32) * scale
    qi = jnp.arange(ATT_BLOCK)[:, None]
    kj = jnp.arange(2 * ATT_BLOCK)[None, :]
    rel = qi + ATT_BLOCK - kj
    blk = jnp.arange(nb)[:, None, None]
    valid = (rel >= 0) & (rel < WINDOW) & ((blk > 0) | (kj >= ATT_BLOCK))
    s = jnp.where(valid[None, :, None, None], s, -jnp.inf)
    sink = sinks.astype(jnp.float32).reshape(ATT_KV_HEADS, G)[None, None, :, :, None, None]
    m = jnp.maximum(jnp.max(s, axis=-1, keepdims=True), sink)
    p = jnp.exp(s - m)
    p = p / (jnp.sum(p, axis=-1, keepdims=True) + jnp.exp(sink - m))
    o = jnp.einsum('bnhgqk,bnkhd->bnqhgd', p.astype(v.dtype), vb)
    return o.reshape(B, S, ATT_WIDTH)


def gla_chunked(q, k, v, g):
    B, S, H, dk = q.shape
    dv = v.shape[-1]
    C = GLA_CHUNK
    n = S // C
    r = lambda t: t.astype(jnp.float32).reshape(B, n, C, H, t.shape[-1])
    q, k, v, g = r(q), r(k), r(v), r(g)
    b = jnp.cumsum(g, axis=2)
    b_last = b[:, :, -1:]
    b_mid = b[:, :, C // 2 - 1:C // 2]
    A = jnp.einsum('bnihd,bnjhd->bnhij', q * jnp.exp(b - b_mid), k * jnp.exp(b_mid - b))
    causal = jnp.tril(jnp.ones((C, C), dtype=bool))
    A = jnp.where(causal, A, 0.0)
    o = jnp.einsum('bnhij,bnjhv->bnihv', A, v)
    kv = jnp.einsum('bnjhd,bnjhv->bnhdv', k * jnp.exp(b_last - b), v)
    decay = jnp.exp(b_last[:, :, 0])

    def step(state, inp):
        d, u = inp
        return d[..., None] * state + u, state

    _, states = lax.scan(step, jnp.zeros((B, H, dk, dv), jnp.float32),
                         (jnp.moveaxis(decay, 1, 0), jnp.moveaxis(kv, 1, 0)))
    states = jnp.moveaxis(states, 0, 1)
    o = o + jnp.einsum('bnihd,bnhdv->bnihv', q * jnp.exp(b), states)
    return o.reshape(B, S, H, dv)


def hybrid_mixer(h, w_in, w_gla_gate_up, b_gla_gate, g_gla_norm, sinks, w_out):
    B, S, _ = h.shape
    proj = h @ w_in
    aq, ak, av, gq, gk, gv, gg, ga = split_cols(proj, IN_SPLITS)
    att = sliding_window_sink_attention(
        aq.reshape(B, S, ATT_Q_HEADS, ATT_HEAD_DIM),
        ak.reshape(B, S, ATT_KV_HEADS, ATT_HEAD_DIM),
        av.reshape(B, S, ATT_KV_HEADS, ATT_HEAD_DIM), sinks)
    glog = jax.nn.log_sigmoid((ga @ w_gla_gate_up + b_gla_gate).astype(jnp.float32)) / GLA_GATE_NORMALIZER
    o = gla_chunked(gq.reshape(B, S, GLA_HEADS, GLA_DK) * (GLA_DK ** -0.5),
                    gk.reshape(B, S, GLA_HEADS, GLA_DK),
                    gv.reshape(B, S, GLA_HEADS, GLA_DV),
                    glog.reshape(B, S, GLA_HEADS, GLA_DK))
    o = rms_norm(o, g_gla_norm).reshape(B, S, GLA_V_WIDTH).astype(h.dtype) * jax.nn.silu(gg)
    return jnp.concatenate([att.astype(h.dtype), o], axis=-1) @ w_out


def moe_ffn(h, w_router, b_router, w_mlp1, b_mlp1, w_mlp2, b_mlp2):
    T, D = h.shape
    logits = (h @ w_router).astype(jnp.float32) + b_router.astype(jnp.float32)
    top_v, top_i = lax.top_k(logits, TOP_K)
    gates = jax.nn.softmax(top_v, axis=-1)
    n_assign = T * TOP_K
    flat_e = top_i.reshape(-1)
    order = jnp.argsort(flat_e)
    sorted_e = flat_e[order]
    counts = jnp.bincount(flat_e, length=N_EXPERTS)
    padded = (counts + MOE_BLOCK - 1) // MOE_BLOCK * MOE_BLOCK
    start = jnp.cumsum(counts) - counts
    pstart = jnp.cumsum(padded) - padded
    dest = pstart[sorted_e] + jnp.arange(n_assign) - start[sorted_e]
    n_slots = n_assign + N_EXPERTS * MOE_BLOCK
    n_blocks = n_slots // MOE_BLOCK
    slot_assign = jnp.full((n_slots,), n_assign, jnp.int32).at[dest].set(order.astype(jnp.int32))
    slot_tok = slot_assign // TOP_K
    slot_gate = jnp.concatenate([gates.reshape(-1), jnp.zeros((1,), jnp.float32)])[slot_assign]
    block_e = jnp.minimum(jnp.searchsorted(jnp.cumsum(padded), jnp.arange(n_blocks) * MOE_BLOCK, side='right'),
                          N_EXPERTS - 1)
    h_pad = jnp.concatenate([h, jnp.zeros((1, D), h.dtype)], axis=0)
    xs = h_pad[slot_tok].reshape(n_blocks, MOE_BLOCK, D)

    def expert_block(args):
        xb, e = args
        u = xb @ w_mlp1[e] + b_mlp1[e]
        x_glu = jnp.minimum(u[..., ::2], SWIGLU_LIMIT)
        x_lin = jnp.clip(u[..., 1::2], -SWIGLU_LIMIT, SWIGLU_LIMIT)
        a = x_glu * jax.nn.sigmoid(SWIGLU_ALPHA * x_glu) * (x_lin + 1)
        return a @ w_mlp2[e] + b_mlp2[e]

    ys = lax.map(expert_block, (xs, block_e)).reshape(n_slots, D)
    out = jnp.zeros((T + 1, D), jnp.float32).at[slot_tok].add(ys.astype(jnp.float32) * slot_gate[:, None])
    return out[:T].astype(h.dtype)


def setup_inputs(seed: int = 0) -> dict:
    key = jax.random.key(seed)
    ks = jax.random.split(key, 24)
    nrm = lambda k, shape, s: jax.random.normal(k, shape, jnp.float32) * s
    L, D, E, F = DEPTH, D_MODEL, N_EXPERTS, D_EXPERT
    return {
        "x": nrm(ks[0], (BATCH, SEQ, D), 1.0),
        "c": nrm(ks[1], (BATCH, D), 1.0),
        "w_ada": nrm(ks[2], (L, D, 6 * D), 0.5 * D ** -0.5),
        "b_ada": nrm(ks[3], (L, 6 * D), 0.02),
        "g_pre_mix": 1.0 + nrm(ks[4], (L, D), 0.02),
        "g_post_mix": 1.0 + nrm(ks[5], (L, D), 0.02),
        "g_pre_ffn": 1.0 + nrm(ks[6], (L, D), 0.02),
        "g_post_ffn": 1.0 + nrm(ks[7], (L, D), 0.02),
        "w_in": nrm(ks[8], (L, D, IN_WIDTH), D ** -0.5),
        "w_gla_gate_up": nrm(ks[9], (L, GLA_GATE_RANK, GLA_K_WIDTH), GLA_GATE_RANK ** -0.5),
        "b_gla_gate": nrm(ks[10], (L, GLA_K_WIDTH), 0.1),
        "g_gla_norm": 1.0 + nrm(ks[11], (L, GLA_DV), 0.02),
        "sinks": nrm(ks[12], (L, ATT_Q_HEADS), 1.0),
        "w_out": nrm(ks[13], (L, MIX_WIDTH, D), MIX_WIDTH ** -0.5),
        "w_router": nrm(ks[14], (L, D, E), D ** -0.5),
        "b_router": nrm(ks[15], (L, E), 0.01),
        "w_mlp1": nrm(ks[16], (L, E, D, 2 * F), D ** -0.5),
        "b_mlp1": nrm(ks[17], (L, E, 2 * F), 0.02),
        "w_mlp2": nrm(ks[18], (L, E, F, D), F ** -0.5),
        "b_mlp2": nrm(ks[19], (L, E, D), 0.02),
    }


def reference(x, c, w_ada, b_ada, g_pre_mix, g_post_mix, g_pre_ffn, g_post_ffn, w_in, w_gla_gate_up,
              b_gla_gate, g_gla_norm, sinks, w_out, w_router, b_router, w_mlp1, b_mlp1, w_mlp2, b_mlp2):
    B, S, D = x.shape
    for l in range(DEPTH):
        mod = jax.nn.silu(c) @ w_ada[l] + b_ada[l]
        sh1, sc1, gt1, sh2, sc2, gt2 = [m[:, None, :] for m in jnp.split(mod, 6, axis=-1)]
        h = rms_norm(x, g_pre_mix[l]) * (1 + sc1) + sh1
        y = hybrid_mixer(h, w_in[l], w_gla_gate_up[l], b_gla_gate[l], g_gla_norm[l], sinks[l], w_out[l])
        x = x + gt1 * rms_norm(y, g_post_mix[l])
        h = rms_norm(x, g_pre_ffn[l]) * (1 + sc2) + sh2
        y = moe_ffn(h.reshape(B * S, D), w_router[l], b_router[l], w_mlp1[l], b_mlp1[l],
                    w_mlp2[l], b_mlp2[l]).reshape(B, S, D)
        x = x + gt2 * rms_norm(y, g_post_ffn[l])
    return x
```

```python
import functools

import numpy as np
import jax
import jax.numpy as jnp
from jax import lax
from jax.experimental import pallas as pl
from jax.experimental.pallas import tpu as pltpu

F32 = jnp.float32
BF16 = jnp.bfloat16
HI = lax.Precision.HIGHEST

ATT_Q_HEADS = 8
ATT_KV_HEADS = 2
ATT_HEAD_DIM = 64
ATT_BLOCK = 128
GLA_HEADS = 4
GLA_DK = 64
GLA_DV = 128
GLA_GATE_RANK = 16
GLA_GATE_NORMALIZER = 16.0
GLA_CHUNK = 64
TOP_K = 4
SWIGLU_LIMIT = 7.0
SWIGLU_ALPHA = 1.702
NORM_EPS = 1e-6

LANES = 128
ATT_WIDTH = ATT_Q_HEADS * ATT_HEAD_DIM
ATT_KV_WIDTH = ATT_KV_HEADS * ATT_HEAD_DIM
GLA_K_WIDTH = GLA_HEADS * GLA_DK
GLA_V_WIDTH = GLA_HEADS * GLA_DV
C_AQ = 0
C_AKV = C_AQ + ATT_WIDTH
C_GQ = C_AKV + 2 * ATT_KV_WIDTH
C_GK = C_GQ + GLA_K_WIDTH
C_GV = C_GK + GLA_K_WIDTH
C_GG = C_GV + GLA_V_WIDTH
C_GA = C_GG + GLA_V_WIDTH
C_END = C_GA + LANES

MOE_ROWS = 256
VMEM_LIMIT = 48 * 1024 * 1024


def _dot(a, b, prec=None):
    return jnp.dot(a, b, preferred_element_type=F32, precision=prec)


def _dot_nt(a, b):
    return lax.dot_general(a, b, (((1,), (1,)), ((), ())), preferred_element_type=F32)


def _rms(t):
    return t * lax.rsqrt(jnp.mean(t * t, axis=-1, keepdims=True) + NORM_EPS)


def _params(sem):
    return pltpu.CompilerParams(dimension_semantics=sem, vmem_limit_bytes=VMEM_LIMIT)


def _ada_kernel(c_ref, w_ref, b_ref, o_ref):
    c = c_ref[...]
    o_ref[...] = _dot(c * jax.nn.sigmoid(c), w_ref[...], HI) + b_ref[...]


def _ada(c, w, b):
    nb, d = c.shape
    n = w.shape[1]
    cp = jnp.zeros((8, d), F32).at[:nb].set(c)
    out = pl.pallas_call(
        _ada_kernel,
        grid=(n // d,),
        in_specs=[pl.BlockSpec((8, d), lambda j: (0, 0)),
                  pl.BlockSpec((d, d), lambda j: (0, j)),
                  pl.BlockSpec((1, d), lambda j: (0, j))],
        out_specs=pl.BlockSpec((8, d), lambda j: (0, j)),
        out_shape=jax.ShapeDtypeStruct((8, n), F32),
        compiler_params=_params(("parallel",)),
        name="ada",
    )(cp, w, b.reshape(1, n))
    return out[:nb]


def _inproj_kernel(x_ref, mod_ref, g_ref, w_ref, wup_ref, bup_ref,
                   aq_ref, akv_ref, gq_ref, gk_ref, gv_ref, gg_ref, gl_ref):
    h = _rms(x_ref[...]) * g_ref[...]
    h = h * (1.0 + mod_ref[1:2, :]) + mod_ref[0:1, :]
    p = _dot(h.astype(BF16), w_ref[...])
    aq_ref[...] = (p[:, C_AQ:C_AKV] * (ATT_HEAD_DIM ** -0.5)).astype(BF16)
    akv_ref[...] = p[:, C_AKV:C_GQ].astype(BF16)
    gq_ref[...] = (p[:, C_GQ:C_GK] * (GLA_DK ** -0.5)).astype(BF16)
    gk_ref[...] = p[:, C_GK:C_GV].astype(BF16)
    gv_ref[...] = p[:, C_GV:C_GG].astype(BF16)
    gg_ref[...] = p[:, C_GG:C_GA].astype(BF16)
    z = _dot(p[:, C_GA:C_END], wup_ref[...], HI) + bup_ref[...]
    gl_ref[...] = (jnp.minimum(z, 0.0) - jnp.log(1.0 + jnp.exp(-jnp.abs(z)))) * (1.0 / GLA_GATE_NORMALIZER)


def _inproj(x2, mod3, g, w_r, wup, bup, seq, tm):
    t, d = x2.shape
    per = seq // tm
    row = lambda i: (i, 0)
    fixed = lambda i: (0, 0)
    widths = (ATT_WIDTH, 2 * ATT_KV_WIDTH, GLA_K_WIDTH, GLA_K_WIDTH, GLA_V_WIDTH, GLA_V_WIDTH)
    out_shape = [jax.ShapeDtypeStruct((t, w), BF16) for w in widths] + [jax.ShapeDtypeStruct((t, GLA_K_WIDTH), F32)]
    out_specs = [pl.BlockSpec((tm, w), row) for w in widths] + [pl.BlockSpec((tm, GLA_K_WIDTH), row)]
    return pl.pallas_call(
        _inproj_kernel,
        grid=(t // tm,),
        in_specs=[pl.BlockSpec((tm, d), row),
                  pl.BlockSpec((None, 8, d), lambda i: (i // per, 0, 0)),
                  pl.BlockSpec((1, d), fixed),
                  pl.BlockSpec((d, C_END), fixed),
                  pl.BlockSpec((LANES, GLA_K_WIDTH), fixed),
                  pl.BlockSpec((1, GLA_K_WIDTH), fixed)],
        out_specs=out_specs,
        out_shape=out_shape,
        compiler_params=_params(("parallel",)),
        name="inproj",
    )(x2, mod3, g, w_r, wup, bup)


def _att_kernel(sinks_ref, q_ref, kv_ref, kvp_ref, o_ref, *, nblk):
    i = pl.program_id(1)
    blk = ATT_BLOCK
    lo = lax.broadcasted_iota(jnp.int32, (blk, LANES), 1) < ATT_HEAD_DIM
    qi = lax.broadcasted_iota(jnp.int32, (2 * blk, 2 * blk), 0) % blk
    kj = lax.broadcasted_iota(jnp.int32, (2 * blk, 2 * blk), 1)
    cur_ok = (kj >= blk) & ((kj - blk) <= qi)
    prev_ok = (kj < blk) & (kj > qi)
    first_off = jnp.where(i > 0, 0, blk)
    top = lax.broadcasted_iota(jnp.int32, (2 * blk, 1), 0) < blk
    for jb in range(nblk):
        r0 = jb * blk
        kvc = kv_ref[r0:r0 + blk, :]
        if jb == 0:
            kvp = kvp_ref[...]
            mask = cur_ok | (prev_ok & (kj >= first_off))
        else:
            kvp = kv_ref[r0 - blk:r0, :]
            mask = cur_ok | prev_ok
        kcat = jnp.concatenate([kvp[:, 0:ATT_KV_WIDTH], kvc[:, 0:ATT_KV_WIDTH]], axis=0)
        vcat = jnp.concatenate([kvp[:, ATT_KV_WIDTH:], kvc[:, ATT_KV_WIDTH:]], axis=0)
        for j in range(ATT_Q_HEADS // 2):
            qp = q_ref[r0:r0 + blk, j * LANES:(j + 1) * LANES]
            zero = jnp.zeros_like(qp)
            q2 = jnp.concatenate([jnp.where(lo, qp, zero), jnp.where(lo, zero, qp)], axis=0)
            s = jnp.where(mask, _dot_nt(q2, kcat), -jnp.inf)
            sink = jnp.where(top, sinks_ref[j], sinks_ref[ATT_Q_HEADS // 2 + j])
            m = jnp.maximum(jnp.max(s, axis=-1, keepdims=True), sink)
            p = jnp.exp(s - m)
            den = jnp.sum(p, axis=-1, keepdims=True) + jnp.exp(sink - m)
            o2 = _dot(p.astype(BF16), vcat) / den
            o = jnp.where(lo, o2[0:blk], o2[blk:2 * blk])
            o_ref[r0:r0 + blk, j * LANES:(j + 1) * LANES] = o.astype(BF16)


def _att(aq, akv, sinks, nbatch, seq, ta):
    t = aq.shape[0]
    nblk = ta // ATT_BLOCK
    per = seq // ta
    perb = seq // ATT_BLOCK
    return pl.pallas_call(
        functools.partial(_att_kernel, nblk=nblk),
        grid=(nbatch, per),
        in_specs=[pl.BlockSpec(memory_space=pltpu.SMEM),
                  pl.BlockSpec((ta, ATT_WIDTH), lambda b, i: (b * per + i, 0)),
                  pl.BlockSpec((ta, 2 * ATT_KV_WIDTH), lambda b, i: (b * per + i, 0)),
                  pl.BlockSpec((ATT_BLOCK, 2 * ATT_KV_WIDTH),
                               lambda b, i: (b * perb + jnp.maximum(i * nblk - 1, 0), 0))],
        out_specs=pl.BlockSpec((ta, ATT_WIDTH), lambda b, i: (b * per + i, 0)),
        out_shape=jax.ShapeDtypeStruct((t, ATT_WIDTH), BF16),
        compiler_params=_params(("parallel", "parallel")),
        name="att",
    )(sinks, aq, akv, akv)


def _gla_kernel(gq_ref, gk_ref, gv_ref, gg_ref, gl_ref, gn_ref, o_ref, st_ref, *, nchunk):
    ch = GLA_CHUNK
    tg = nchunk * ch

    @pl.when(pl.program_id(1) == 0)
    def _():
        st_ref[...] = jnp.zeros_like(st_ref)

    ri = lax.broadcasted_iota(jnp.int32, (tg, tg), 0)
    ci = lax.broadcasted_iota(jnp.int32, (tg, tg), 1)
    tri = ((ri // ch == ci // ch) & (ci <= ri)).astype(F32)
    bt = _dot(tri, gl_ref[...], HI)
    lo = lax.broadcasted_iota(jnp.int32, (ch, LANES), 1) < GLA_DK
    causal = lax.broadcasted_iota(jnp.int32, (ch, ch), 0) >= lax.broadcasted_iota(jnp.int32, (ch, ch), 1)
    eye = (lax.broadcasted_iota(jnp.int32, (LANES, LANES), 0)
           == lax.broadcasted_iota(jnp.int32, (LANES, LANES), 1)).astype(BF16)
    gn = gn_ref[...]
    zero = jnp.zeros((ch, LANES), BF16)
    for c in range(nchunk):
        r0 = c * ch
        b = bt[r0:r0 + ch]
        bm = b[ch // 2 - 1:ch // 2]
        bl = b[ch - 1:ch]
        q = gq_ref[r0:r0 + ch, :].astype(F32)
        k = gk_ref[r0:r0 + ch, :].astype(F32)
        qe = (q * jnp.exp(b - bm)).astype(BF16)
        ke = (k * jnp.exp(bm - b)).astype(BF16)
        qs = (q * jnp.exp(b)).astype(BF16)
        kl = (k * jnp.exp(bl - b)).astype(BF16)
        dec = jnp.exp(bl)
        for p in range(GLA_HEADS // 2):
            sl = slice(p * LANES, (p + 1) * LANES)
            st = st_ref[p]
            stb = st.astype(BF16)
            upd = None
            for hh in range(2):
                h = 2 * p + hh
                hs = slice(h * GLA_DV, (h + 1) * GLA_DV)
                msk = lo if hh == 0 else jnp.logical_not(lo)
                a = _dot_nt(jnp.where(msk, qe[:, sl], zero), ke[:, sl])
                a = jnp.where(causal, a, 0.0).astype(BF16)
                v = gv_ref[r0:r0 + ch, hs]
                o = _dot(a, v) + _dot_nt(jnp.where(msk, qs[:, sl], zero), stb)
                vt = _dot_nt(eye, v).astype(BF16)
                u = _dot(vt, jnp.where(msk, kl[:, sl], zero))
                upd = u if upd is None else upd + u
                gg = gg_ref[r0:r0 + ch, hs].astype(F32)
                o_ref[r0:r0 + ch, hs] = (_rms(o) * gn * (gg * jax.nn.sigmoid(gg))).astype(BF16)
            st_ref[p] = st * dec[:, sl] + upd


def _gla(gq, gk, gv, gg, gl, gn, nbatch, seq, tg):
    t = gq.shape[0]
    per = seq // tg
    row = lambda b, i: (b * per + i, 0)
    return pl.pallas_call(
        functools.partial(_gla_kernel, nchunk=tg // GLA_CHUNK),
        grid=(nbatch, per),
        in_specs=[pl.BlockSpec((tg, GLA_K_WIDTH), row),
                  pl.BlockSpec((tg, GLA_K_WIDTH), row),
                  pl.BlockSpec((tg, GLA_V_WIDTH), row),
                  pl.BlockSpec((tg, GLA_V_WIDTH), row),
                  pl.BlockSpec((tg, GLA_K_WIDTH), row),
                  pl.BlockSpec((1, GLA_DV), lambda b, i: (0, 0))],
        out_specs=pl.BlockSpec((tg, GLA_V_WIDTH), row),
        out_shape=jax.ShapeDtypeStruct((t, GLA_V_WIDTH), BF16),
        scratch_shapes=[pltpu.VMEM((GLA_HEADS // 2, GLA_DV, LANES), F32)],
        compiler_params=_params(("parallel", "arbitrary")),
        name="gla",
    )(gq, gk, gv, gg, gl, gn)


def _outproj_kernel(att_ref, gla_ref, x_ref, mod_ref, gpost_ref, gpre_ref, woa_ref, wog_ref, wr_ref, br_ref,
                    x1_ref, h2_ref, route_ref, cnt_ref, base_ref, *, n_exp):
    tm = x_ref.shape[0]

    @pl.when(pl.program_id(0) == 0)
    def _():
        base_ref[...] = jnp.zeros_like(base_ref)

    y = _dot(att_ref[...], woa_ref[...]) + _dot(gla_ref[...], wog_ref[...])
    x1 = x_ref[...] + mod_ref[2:3, :] * (_rms(y) * gpost_ref[...])
    x1_ref[...] = x1
    h2 = _rms(x1) * gpre_ref[...]
    h2 = h2 * (1.0 + mod_ref[4:5, :]) + mod_ref[3:4, :]
    h2_ref[...] = h2

    lane = lax.broadcasted_iota(jnp.int32, (tm, LANES), 1)
    logits = _dot(h2, wr_ref[...], HI) + br_ref[...]
    vals = jnp.where(lane < n_exp, logits, -jnp.inf)
    sels, tops, idxs = [], [], []
    for _ in range(TOP_K):
        m = jnp.max(vals, axis=-1, keepdims=True)
        idx = jnp.min(jnp.where(vals == m, lane, LANES), axis=-1, keepdims=True)
        sel = lane == idx
        vals = jnp.where(sel, -jnp.inf, vals)
        sels.append(sel)
        tops.append(m)
        idxs.append(idx)
    es = [jnp.exp(m - tops[0]) for m in tops]
    tot = es[0] + es[1] + es[2] + es[3]
    onehot = jnp.zeros((tm, LANES), F32)
    for sel in sels:
        onehot = onehot + sel.astype(F32)
    below = (lax.broadcasted_iota(jnp.int32, (tm, tm), 0) > lax.broadcasted_iota(jnp.int32, (tm, tm), 1)).astype(BF16)
    csum = _dot(below, onehot.astype(BF16)) + base_ref[...]
    route = jnp.zeros((tm, LANES), F32)
    for k in range(TOP_K):
        rank = jnp.sum(jnp.where(sels[k], csum, 0.0), axis=-1, keepdims=True)
        route = jnp.where(lane == k, idxs[k].astype(F32), route)
        route = jnp.where(lane == TOP_K + k, es[k] / tot, route)
        route = jnp.where(lane == 2 * TOP_K + k, rank, route)
    route_ref[...] = route
    base = base_ref[...] + jnp.sum(onehot, axis=0, keepdims=True)
    base_ref[...] = base
    cnt_ref[...] = jnp.broadcast_to(base, cnt_ref.shape)


def _outproj(att, gla, x2, mod3, gpost, gpre, woa, wog, wr, br, seq, tm, n_exp):
    t, d = x2.shape
    per = seq // tm
    row = lambda i: (i, 0)
    fixed = lambda i: (0, 0)
    return pl.pallas_call(
        functools.partial(_outproj_kernel, n_exp=n_exp),
        grid=(t // tm,),
        in_specs=[pl.BlockSpec((tm, ATT_WIDTH), row),
                  pl.BlockSpec((tm, GLA_V_WIDTH), row),
                  pl.BlockSpec((tm, d), row),
                  pl.BlockSpec((None, 8, d), lambda i: (i // per, 0, 0)),
                  pl.BlockSpec((1, d), fixed),
                  pl.BlockSpec((1, d), fixed),
                  pl.BlockSpec((ATT_WIDTH, d), fixed),
                  pl.BlockSpec((GLA_V_WIDTH, d), fixed),
                  pl.BlockSpec((d, LANES), fixed),
                  pl.BlockSpec((1, LANES), fixed)],
        out_specs=[pl.BlockSpec((tm, d), row),
                   pl.BlockSpec((tm, d), row),
                   pl.BlockSpec((tm, LANES), row),
                   pl.BlockSpec((8, LANES), fixed)],
        out_shape=[jax.ShapeDtypeStruct((t, d), F32),
                   jax.ShapeDtypeStruct((t, d), F32),
                   jax.ShapeDtypeStruct((t, LANES), F32),
                   jax.ShapeDtypeStruct((8, LANES), F32)],
        scratch_shapes=[pltpu.VMEM((1, LANES), F32)],
        compiler_params=_params(("arbitrary",)),
        name="outproj",
    )(att, gla, x2, mod3, gpost, gpre, woa, wog, wr, br)


def _row_copy(src, dst, sem):
    return pltpu.make_async_copy(src, dst, sem)


def _dispatch_kernel(dest_hbm, h2_ref, xs_in, xs_hbm, dsm, sem_i, sem_r, *, td):
    del xs_in
    i = pl.program_id(0)
    n = td * TOP_K
    idx_cp = pltpu.make_async_copy(dest_hbm.at[pl.ds(pl.multiple_of(i * n, n), n)], dsm, sem_i)
    idx_cp.start()
    idx_cp.wait()

    def issue(r, carry):
        for k in range(TOP_K):
            d = dsm[r * TOP_K + k]
            _row_copy(h2_ref.at[pl.ds(r, 1), :], xs_hbm.at[pl.ds(d, 1), :], sem_r).start()
        return carry

    lax.fori_loop(0, td, issue, 0)

    def drain(r, carry):
        for k in range(TOP_K):
            _row_copy(h2_ref.at[pl.ds(0, 1), :], xs_hbm.at[pl.ds(0, 1), :], sem_r).wait()
        return carry

    lax.fori_loop(0, td, drain, 0)


def _dispatch(dest, h2, xs0, td):
    t, d = h2.shape
    return pl.pallas_call(
        functools.partial(_dispatch_kernel, td=td),
        grid=(t // td,),
        in_specs=[pl.BlockSpec(memory_space=pl.ANY),
                  pl.BlockSpec((td, d), lambda i: (i, 0)),
                  pl.BlockSpec(memory_space=pl.ANY)],
        out_specs=pl.BlockSpec(memory_space=pl.ANY),
        out_shape=jax.ShapeDtypeStruct(xs0.shape, xs0.dtype),
        scratch_shapes=[pltpu.SMEM((td * TOP_K,), jnp.int32),
                        pltpu.SemaphoreType.DMA,
                        pltpu.SemaphoreType.DMA],
        input_output_aliases={2: 0},
        compiler_params=_params(("arbitrary",)),
        name="dispatch",
    )(dest, h2, xs0)


def _moe_kernel(be_ref, nu_ref, xs_ref, w1g_ref, w1l_ref, b1g_ref, b1l_ref, w2_ref, b2_ref, ys_ref):
    del be_ref

    @pl.when(pl.program_id(0) < nu_ref[0])
    def _():
        x = xs_ref[...].astype(BF16)
        glu = jnp.minimum(_dot(x, w1g_ref[...]) + b1g_ref[...], SWIGLU_LIMIT)
        lin = jnp.clip(_dot(x, w1l_ref[...]) + b1l_ref[...], -SWIGLU_LIMIT, SWIGLU_LIMIT)
        a = glu * jax.nn.sigmoid(SWIGLU_ALPHA * glu) * (lin + 1.0)
        ys_ref[...] = _dot(a.astype(BF16), w2_ref[...]) + b2_ref[...]

    @pl.when(pl.program_id(0) >= nu_ref[0])
    def _():
        ys_ref[...] = jnp.zeros_like(ys_ref)


def _moe(block_e, n_used, xs, w1g, w1l, b1g, b1l, w2, b2):
    ns, d = xs.shape
    f = w1g.shape[2]
    rows = lambda i, be, nu: (jnp.minimum(i, nu[0] - 1), 0)
    wsel = lambda i, be, nu: (be[i], 0, 0)
    grid_spec = pltpu.PrefetchScalarGridSpec(
        num_scalar_prefetch=2,
        grid=(ns // MOE_ROWS,),
        in_specs=[pl.BlockSpec((MOE_ROWS, d), rows),
                  pl.BlockSpec((None, d, f), wsel),
                  pl.BlockSpec((None, d, f), wsel),
                  pl.BlockSpec((None, 1, f), wsel),
                  pl.BlockSpec((None, 1, f), wsel),
                  pl.BlockSpec((None, f, d), wsel),
                  pl.BlockSpec((None, 1, d), wsel)],
        out_specs=pl.BlockSpec((MOE_ROWS, d), lambda i, be, nu: (i, 0)),
    )
    return pl.pallas_call(
        _moe_kernel,
        grid_spec=grid_spec,
        out_shape=jax.ShapeDtypeStruct((ns, d), F32),
        compiler_params=_params(("arbitrary",)),
        name="moe",
    )(block_e, n_used, xs, w1g, w1l, b1g, b1l, w2, b2)


def _combine_kernel(dest_hbm, ys_hbm, route_ref, x1_ref, mod_ref, g_ref, o_ref, dsm, buf, sem_i, sem_g, *, tc):
    i = pl.program_id(0)
    nsteps = pl.num_programs(0)
    n = tc * TOP_K

    def fetch(step, slot):
        idx_cp = pltpu.make_async_copy(dest_hbm.at[pl.ds(pl.multiple_of(step * n, n), n)], dsm.at[slot], sem_i)
        idx_cp.start()
        idx_cp.wait()

        def issue(r, carry):
            for k in range(TOP_K):
                d = dsm[slot, r * TOP_K + k]
                _row_copy(ys_hbm.at[pl.ds(d, 1), :], buf.at[slot, k, pl.ds(r, 1), :], sem_g.at[slot]).start()
            return carry

        lax.fori_loop(0, tc, issue, 0)

    @pl.when(i == 0)
    def _():
        fetch(0, 0)

    @pl.when(i + 1 < nsteps)
    def _():
        fetch(i + 1, (i + 1) % 2)

    slot = i % 2

    def drain(r, carry):
        for k in range(TOP_K):
            _row_copy(ys_hbm.at[pl.ds(0, 1), :], buf.at[slot, k, pl.ds(0, 1), :], sem_g.at[slot]).wait()
        return carry

    lax.fori_loop(0, tc, drain, 0)

    y = jnp.zeros(o_ref.shape, F32)
    for k in range(TOP_K):
        y = y + route_ref[:, TOP_K + k:TOP_K + k + 1] * buf[slot, k]
    o_ref[...] = x1_ref[...] + mod_ref[5:6, :] * (_rms(y) * g_ref[...])


def _combine(dest, ys, route, x1, mod3, g, seq, tc):
    t, d = x1.shape
    per = seq // tc
    row = lambda i: (i, 0)
    return pl.pallas_call(
        functools.partial(_combine_kernel, tc=tc),
        grid=(t // tc,),
        in_specs=[pl.BlockSpec(memory_space=pl.ANY),
                  pl.BlockSpec(memory_space=pl.ANY),
                  pl.BlockSpec((tc, LANES), row),
                  pl.BlockSpec((tc, d), row),
                  pl.BlockSpec((None, 8, d), lambda i: (i // per, 0, 0)),
                  pl.BlockSpec((1, d), lambda i: (0, 0))],
        out_specs=pl.BlockSpec((tc, d), row),
        out_shape=jax.ShapeDtypeStruct((t, d), F32),
        scratch_shapes=[pltpu.SMEM((2, tc * TOP_K), jnp.int32),
                        pltpu.VMEM((2, TOP_K, tc, d), F32),
                        pltpu.SemaphoreType.DMA,
                        pltpu.SemaphoreType.DMA((2,))],
        compiler_params=_params(("arbitrary",)),
        name="combine",
    )(dest, ys, route, x1, mod3, g)


def _pair_perm():
    half = ATT_Q_HEADS // 2
    idx = []
    for j in range(half):
        idx += list(range(j * ATT_HEAD_DIM, (j + 1) * ATT_HEAD_DIM))
        idx += list(range((half + j) * ATT_HEAD_DIM, (half + j + 1) * ATT_HEAD_DIM))
    return np.asarray(idx, np.int32)


def _layer(x, mod, g_pre_mix, g_post_mix, g_pre_ffn, g_post_ffn, w_in, w_gla_gate_up, b_gla_gate, g_gla_norm,
           sinks, w_out, w_router, b_router, w_mlp1, b_mlp1, w_mlp2, b_mlp2):
    nbatch, seq, d = x.shape
    t = nbatch * seq
    n_exp = w_router.shape[1]
    f = w_mlp2.shape[1]
    x2 = x.reshape(t, d)
    mod3 = jnp.pad(mod.reshape(nbatch, 6, d), ((0, 0), (0, 2), (0, 0)))

    perm = _pair_perm()
    n_main = C_GA - C_AKV
    w_r = jnp.concatenate([w_in[:, perm], w_in[:, ATT_WIDTH:ATT_WIDTH + n_main],
                           jnp.pad(w_in[:, ATT_WIDTH + n_main:], ((0, 0), (0, LANES - GLA_GATE_RANK)))],
                          axis=1).astype(BF16)
    wup = jnp.pad(w_gla_gate_up, ((0, LANES - GLA_GATE_RANK), (0, 0)))
    woa = w_out[:ATT_WIDTH][perm].astype(BF16)
    wog = w_out[ATT_WIDTH:].astype(BF16)
    wr = jnp.pad(w_router, ((0, 0), (0, LANES - n_exp)))
    br = jnp.pad(b_router, (0, LANES - n_exp)).reshape(1, LANES)
    w1 = w_mlp1.reshape(n_exp, d, f, 2)
    w1g = w1[..., 0].astype(BF16)
    w1l = w1[..., 1].astype(BF16)
    b1 = b_mlp1.reshape(n_exp, 1, f, 2)
    w2 = w_mlp2.astype(BF16)

    tm = min(512, seq)
    aq, akv, gq, gk, gv, gg, gl = _inproj(x2, mod3, g_pre_mix.reshape(1, d), w_r, wup,
                                          b_gla_gate.reshape(1, GLA_K_WIDTH), seq, tm)
    att = _att(aq, akv, sinks, nbatch, seq, min(256, seq))
    gla = _gla(gq, gk, gv, gg, gl, g_gla_norm.reshape(1, GLA_DV), nbatch, seq, min(256, seq))
    x1, h2, route, cnt = _outproj(att, gla, x2, mod3, g_post_mix.reshape(1, d), g_pre_ffn.reshape(1, d),
                                  woa, wog, wr, br, seq, min(256, seq), n_exp)

    counts = cnt[0, :n_exp].astype(jnp.int32)
    padded = (counts + MOE_ROWS - 1) // MOE_ROWS * MOE_ROWS
    pend = jnp.cumsum(padded)
    pstart = pend - padded
    top_i = route[:, 0:TOP_K].astype(jnp.int32)
    rank = route[:, 2 * TOP_K:3 * TOP_K].astype(jnp.int32)
    dest = (pstart[top_i] + rank).reshape(t * TOP_K)
    n_slots = t * TOP_K + n_exp * MOE_ROWS
    n_blocks = n_slots // MOE_ROWS
    n_used = (pend[-1] // MOE_ROWS).astype(jnp.int32)
    blk_ids = jnp.minimum(jnp.arange(n_blocks, dtype=jnp.int32), n_used - 1)
    block_e = jnp.minimum(jnp.searchsorted(pend, blk_ids * MOE_ROWS, side='right'), n_exp - 1).astype(jnp.int32)

    xs = _dispatch(dest, h2, jnp.zeros((n_slots, d), F32), min(256, seq))
    ys = _moe(block_e, n_used.reshape(1), xs, w1g, w1l, b1[..., 0], b1[..., 1], w2, b_mlp2.reshape(n_exp, 1, d))
    out = _combine(dest, ys, route, x1, mod3, g_post_ffn.reshape(1, d), seq, min(256, seq))
    return out.reshape(nbatch, seq, d)


def kernel(x, c, w_ada, b_ada, g_pre_mix, g_post_mix, g_pre_ffn, g_post_ffn, w_in, w_gla_gate_up, b_gla_gate,
           g_gla_norm, sinks, w_out, w_router, b_router, w_mlp1, b_mlp1, w_mlp2, b_mlp2):
    for l in range(w_in.shape[0]):
        mod = _ada(c, w_ada[l], b_ada[l])
        x = _layer(x, mod, g_pre_mix[l], g_post_mix[l], g_pre_ffn[l], g_post_ffn[l], w_in[l], w_gla_gate_up[l],
                   b_gla_gate[l], g_gla_norm[l], sinks[l], w_out[l], w_router[l], b_router[l], w_mlp1[l], b_mlp1[l],
                   w_mlp2[l], b_mlp2[l])
    return x
```

```python
import functools

import numpy as np
import jax
import jax.numpy as jnp
from jax import lax
from jax.experimental import pallas as pl
from jax.experimental.pallas import tpu as pltpu

F32 = jnp.float32
BF16 = jnp.bfloat16
HI = lax.Precision.HIGHEST

ATT_Q_HEADS = 8
ATT_KV_HEADS = 2
ATT_HEAD_DIM = 64
ATT_BLOCK = 128
GLA_HEADS = 4
GLA_DK = 64
GLA_DV = 128
GLA_GATE_RANK = 16
GLA_GATE_NORMALIZER = 16.0
GLA_CHUNK = 64
TOP_K = 4
SWIGLU_LIMIT = 7.0
SWIGLU_ALPHA = 1.702
NORM_EPS = 1e-6

LANES = 128
ATT_WIDTH = ATT_Q_HEADS * ATT_HEAD_DIM
ATT_KV_WIDTH = ATT_KV_HEADS * ATT_HEAD_DIM
GLA_K_WIDTH = GLA_HEADS * GLA_DK
GLA_V_WIDTH = GLA_HEADS * GLA_DV
C_AQ = 0
C_AKV = C_AQ + ATT_WIDTH
C_GQ = C_AKV + 2 * ATT_KV_WIDTH
C_GK = C_GQ + GLA_K_WIDTH
C_GV = C_GK + GLA_K_WIDTH
C_GG = C_GV + GLA_V_WIDTH
C_GA = C_GG + GLA_V_WIDTH
C_END = C_GA + LANES

MOE_ROWS = 256
VMEM_LIMIT = 48 * 1024 * 1024


def _dot(a, b, prec=None):
    return jnp.dot(a, b, preferred_element_type=F32, precision=prec)


def _dot_nt(a, b):
    return lax.dot_general(a, b, (((1,), (1,)), ((), ())), preferred_element_type=F32)


def _rms(t):
    return t * lax.rsqrt(jnp.mean(t * t, axis=-1, keepdims=True) + NORM_EPS)


def _params(sem):
    return pltpu.CompilerParams(dimension_semantics=sem, vmem_limit_bytes=VMEM_LIMIT)


HI16 = -65536


def _pack(a):
    half = a.shape[1] // 2
    lo = lax.bitcast_convert_type(a[:, :half].astype(BF16).astype(F32), jnp.int32)
    hi = lax.bitcast_convert_type(a[:, half:].astype(BF16).astype(F32), jnp.int32)
    return hi | lax.shift_right_logical(lo, 16)


def _unpack(p):
    lo = lax.bitcast_convert_type(lax.shift_left(p, 16), F32)
    hi = lax.bitcast_convert_type(p & HI16, F32)
    return lo, hi


def _store_rows(ref, p):
    n, w = p.shape
    c = w // LANES
    for s in range(c):
        ref[pl.ds(s, n, stride=c), :] = p[:, s * LANES:(s + 1) * LANES]


def _load_rows(ref, c):
    n = ref.shape[0] // c
    return jnp.concatenate([ref[pl.ds(s, n, stride=c), :] for s in range(c)], axis=1)


def _ada_kernel(c_ref, w_ref, b_ref, o_ref):
    c = c_ref[...]
    o_ref[...] = _dot(c * jax.nn.sigmoid(c), w_ref[...], HI) + b_ref[...]


def _ada(c, w, b):
    nb, d = c.shape
    n = w.shape[1]
    cp = jnp.zeros((8, d), F32).at[:nb].set(c)
    out = pl.pallas_call(
        _ada_kernel,
        grid=(n // d,),
        in_specs=[pl.BlockSpec((8, d), lambda j: (0, 0)),
                  pl.BlockSpec((d, d), lambda j: (0, j)),
                  pl.BlockSpec((1, d), lambda j: (0, j))],
        out_specs=pl.BlockSpec((8, d), lambda j: (0, j)),
        out_shape=jax.ShapeDtypeStruct((8, n), F32),
        compiler_params=_params(("parallel",)),
        name="ada",
    )(cp, w, b.reshape(1, n))
    return out[:nb]


def _inproj_kernel(x_ref, mod_ref, g_ref, w_ref, wup_ref, bup_ref,
                   aq_ref, akv_ref, gq_ref, gk_ref, gv_ref, gg_ref, gl_ref):
    h = _rms(x_ref[...]) * g_ref[...]
    h = h * (1.0 + mod_ref[1:2, :]) + mod_ref[0:1, :]
    p = _dot(h.astype(BF16), w_ref[...])
    aq_ref[...] = (p[:, C_AQ:C_AKV] * (ATT_HEAD_DIM ** -0.5)).astype(BF16)
    akv_ref[...] = p[:, C_AKV:C_GQ].astype(BF16)
    gq_ref[...] = (p[:, C_GQ:C_GK] * (GLA_DK ** -0.5)).astype(BF16)
    gk_ref[...] = p[:, C_GK:C_GV].astype(BF16)
    gv_ref[...] = p[:, C_GV:C_GG].astype(BF16)
    gg_ref[...] = p[:, C_GG:C_GA].astype(BF16)
    z = _dot(p[:, C_GA:C_END], wup_ref[...], HI) + bup_ref[...]
    gl_ref[...] = (jnp.minimum(z, 0.0) - jnp.log(1.0 + jnp.exp(-jnp.abs(z)))) * (1.0 / GLA_GATE_NORMALIZER)


def _inproj(x2, mod3, g, w_r, wup, bup, seq, tm):
    t, d = x2.shape
    per = seq // tm
    row = lambda i: (i, 0)
    fixed = lambda i: (0, 0)
    widths = (ATT_WIDTH, 2 * ATT_KV_WIDTH, GLA_K_WIDTH, GLA_K_WIDTH, GLA_V_WIDTH, GLA_V_WIDTH)
    out_shape = [jax.ShapeDtypeStruct((t, w), BF16) for w in widths] + [jax.ShapeDtypeStruct((t, GLA_K_WIDTH), F32)]
    out_specs = [pl.BlockSpec((tm, w), row) for w in widths] + [pl.BlockSpec((tm, GLA_K_WIDTH), row)]
    return pl.pallas_call(
        _inproj_kernel,
        grid=(t // tm,),
        in_specs=[pl.BlockSpec((tm, d), row),
                  pl.BlockSpec((None, 8, d), lambda i: (i // per, 0, 0)),
                  pl.BlockSpec((1, d), fixed),
                  pl.BlockSpec((d, C_END), fixed),
                  pl.BlockSpec((LANES, GLA_K_WIDTH), fixed),
                  pl.BlockSpec((1, GLA_K_WIDTH), fixed)],
        out_specs=out_specs,
        out_shape=out_shape,
        compiler_params=_params(("parallel",)),
        name="inproj",
    )(x2, mod3, g, w_r, wup, bup)


def _att_kernel(sinks_ref, q_ref, kv_ref, kvp_ref, o_ref, *, nblk):
    i = pl.program_id(1)
    blk = ATT_BLOCK
    lo = lax.broadcasted_iota(jnp.int32, (blk, LANES), 1) < ATT_HEAD_DIM
    qi = lax.broadcasted_iota(jnp.int32, (2 * blk, 2 * blk), 0) % blk
    kj = lax.broadcasted_iota(jnp.int32, (2 * blk, 2 * blk), 1)
    cur_ok = (kj >= blk) & ((kj - blk) <= qi)
    prev_ok = (kj < blk) & (kj > qi)
    first_off = jnp.where(i > 0, 0, blk)
    top = lax.broadcasted_iota(jnp.int32, (2 * blk, 1), 0) < blk
    for jb in range(nblk):
        r0 = jb * blk
        kvc = kv_ref[r0:r0 + blk, :]
        if jb == 0:
            kvp = kvp_ref[...]
            mask = cur_ok | (prev_ok & (kj >= first_off))
        else:
            kvp = kv_ref[r0 - blk:r0, :]
            mask = cur_ok | prev_ok
        kcat = jnp.concatenate([kvp[:, 0:ATT_KV_WIDTH], kvc[:, 0:ATT_KV_WIDTH]], axis=0)
        vcat = jnp.concatenate([kvp[:, ATT_KV_WIDTH:], kvc[:, ATT_KV_WIDTH:]], axis=0)
        for j in range(ATT_Q_HEADS // 2):
            qp = q_ref[r0:r0 + blk, j * LANES:(j + 1) * LANES]
            zero = jnp.zeros_like(qp)
            q2 = jnp.concatenate([jnp.where(lo, qp, zero), jnp.where(lo, zero, qp)], axis=0)
            s = jnp.where(mask, _dot_nt(q2, kcat), -jnp.inf)
            sink = jnp.where(top, sinks_ref[j], sinks_ref[ATT_Q_HEADS // 2 + j])
            m = jnp.maximum(jnp.max(s, axis=-1, keepdims=True), sink)
            p = jnp.exp(s - m)
            den = jnp.sum(p, axis=-1, keepdims=True) + jnp.exp(sink - m)
            o2 = _dot(p.astype(BF16), vcat) / den
            o = jnp.where(lo, o2[0:blk], o2[blk:2 * blk])
            o_ref[r0:r0 + blk, j * LANES:(j + 1) * LANES] = o.astype(BF16)


def _att(aq, akv, sinks, nbatch, seq, ta):
    t = aq.shape[0]
    nblk = ta // ATT_BLOCK
    per = seq // ta
    perb = seq // ATT_BLOCK
    return pl.pallas_call(
        functools.partial(_att_kernel, nblk=nblk),
        grid=(nbatch, per),
        in_specs=[pl.BlockSpec(memory_space=pltpu.SMEM),
                  pl.BlockSpec((ta, ATT_WIDTH), lambda b, i: (b * per + i, 0)),
                  pl.BlockSpec((ta, 2 * ATT_KV_WIDTH), lambda b, i: (b * per + i, 0)),
                  pl.BlockSpec((ATT_BLOCK, 2 * ATT_KV_WIDTH),
                               lambda b, i: (b * perb + jnp.maximum(i * nblk - 1, 0), 0))],
        out_specs=pl.BlockSpec((ta, ATT_WIDTH), lambda b, i: (b * per + i, 0)),
        out_shape=jax.ShapeDtypeStruct((t, ATT_WIDTH), BF16),
        compiler_params=_params(("parallel", "parallel")),
        name="att",
    )(sinks, aq, akv, akv)


def _gla_kernel(gq_ref, gk_ref, gv_ref, gg_ref, gl_ref, gn_ref, o_ref, st_ref, *, nchunk):
    ch = GLA_CHUNK
    tg = nchunk * ch

    @pl.when(pl.program_id(1) == 0)
    def _():
        st_ref[...] = jnp.zeros_like(st_ref)

    ri = lax.broadcasted_iota(jnp.int32, (tg, tg), 0)
    ci = lax.broadcasted_iota(jnp.int32, (tg, tg), 1)
    tri = ((ri // ch == ci // ch) & (ci <= ri)).astype(F32)
    bt = _dot(tri, gl_ref[...], HI)
    lo = lax.broadcasted_iota(jnp.int32, (ch, LANES), 1) < GLA_DK
    causal = lax.broadcasted_iota(jnp.int32, (ch, ch), 0) >= lax.broadcasted_iota(jnp.int32, (ch, ch), 1)
    eye = (lax.broadcasted_iota(jnp.int32, (LANES, LANES), 0)
           == lax.broadcasted_iota(jnp.int32, (LANES, LANES), 1)).astype(BF16)
    gn = gn_ref[...]
    zero = jnp.zeros((ch, LANES), BF16)
    for c in range(nchunk):
        r0 = c * ch
        b = bt[r0:r0 + ch]
        bm = b[ch // 2 - 1:ch // 2]
        bl = b[ch - 1:ch]
        q = gq_ref[r0:r0 + ch, :].astype(F32)
        k = gk_ref[r0:r0 + ch, :].astype(F32)
        qe = (q * jnp.exp(b - bm)).astype(BF16)
        ke = (k * jnp.exp(bm - b)).astype(BF16)
        qs = (q * jnp.exp(b)).astype(BF16)
        kl = (k * jnp.exp(bl - b)).astype(BF16)
        dec = jnp.exp(bl)
        for p in range(GLA_HEADS // 2):
            sl = slice(p * LANES, (p + 1) * LANES)
            st = st_ref[p]
            stb = st.astype(BF16)
            upd = None
            for hh in range(2):
                h = 2 * p + hh
                hs = slice(h * GLA_DV, (h + 1) * GLA_DV)
                msk = lo if hh == 0 else jnp.logical_not(lo)
                a = _dot_nt(jnp.where(msk, qe[:, sl], zero), ke[:, sl])
                a = jnp.where(causal, a, 0.0).astype(BF16)
                v = gv_ref[r0:r0 + ch, hs]
                o = _dot(a, v) + _dot_nt(jnp.where(msk, qs[:, sl], zero), stb)
                vt = _dot_nt(eye, v).astype(BF16)
                u = _dot(vt, jnp.where(msk, kl[:, sl], zero))
                upd = u if upd is None else upd + u
                gg = gg_ref[r0:r0 + ch, hs].astype(F32)
                o_ref[r0:r0 + ch, hs] = (_rms(o) * gn * (gg * jax.nn.sigmoid(gg))).astype(BF16)
            st_ref[p] = st * dec[:, sl] + upd


def _gla(gq, gk, gv, gg, gl, gn, nbatch, seq, tg):
    t = gq.shape[0]
    per = seq // tg
    row = lambda b, i: (b * per + i, 0)
    return pl.pallas_call(
        functools.partial(_gla_kernel, nchunk=tg // GLA_CHUNK),
        grid=(nbatch, per),
        in_specs=[pl.BlockSpec((tg, GLA_K_WIDTH), row),
                  pl.BlockSpec((tg, GLA_K_WIDTH), row),
                  pl.BlockSpec((tg, GLA_V_WIDTH), row),
                  pl.BlockSpec((tg, GLA_V_WIDTH), row),
                  pl.BlockSpec((tg, GLA_K_WIDTH), row),
                  pl.BlockSpec((1, GLA_DV), lambda b, i: (0, 0))],
        out_specs=pl.BlockSpec((tg, GLA_V_WIDTH), row),
        out_shape=jax.ShapeDtypeStruct((t, GLA_V_WIDTH), BF16),
        scratch_shapes=[pltpu.VMEM((GLA_HEADS // 2, GLA_DV, LANES), F32)],
        compiler_params=_params(("parallel", "arbitrary")),
        name="gla",
    )(gq, gk, gv, gg, gl, gn)


def _outproj_kernel(att_ref, gla_ref, x_ref, mod_ref, gpost_ref, gpre_ref, woa_ref, wog_ref, wr_ref, br_ref,
                    x1_ref, h2_ref, route_ref, cnt_ref, base_ref, *, n_exp):
    tm = x_ref.shape[0]

    @pl.when(pl.program_id(0) == 0)
    def _():
        base_ref[...] = jnp.zeros_like(base_ref)

    y = _dot(att_ref[...], woa_ref[...]) + _dot(gla_ref[...], wog_ref[...])
    x1 = x_ref[...] + mod_ref[2:3, :] * (_rms(y) * gpost_ref[...])
    x1_ref[...] = x1
    h2 = _rms(x1) * gpre_ref[...]
    h2 = h2 * (1.0 + mod_ref[4:5, :]) + mod_ref[3:4, :]
    _store_rows(h2_ref, _pack(h2))

    lane = lax.broadcasted_iota(jnp.int32, (tm, LANES), 1)
    logits = _dot(h2, wr_ref[...], HI) + br_ref[...]
    vals = jnp.where(lane < n_exp, logits, -jnp.inf)
    sels, tops, idxs = [], [], []
    for _ in range(TOP_K):
        m = jnp.max(vals, axis=-1, keepdims=True)
        idx = jnp.min(jnp.where(vals == m, lane, LANES), axis=-1, keepdims=True)
        sel = lane == idx
        vals = jnp.where(sel, -jnp.inf, vals)
        sels.append(sel)
        tops.append(m)
        idxs.append(idx)
    es = [jnp.exp(m - tops[0]) for m in tops]
    tot = es[0] + es[1] + es[2] + es[3]
    onehot = jnp.zeros((tm, LANES), F32)
    for sel in sels:
        onehot = onehot + sel.astype(F32)
    below = (lax.broadcasted_iota(jnp.int32, (tm, tm), 0) > lax.broadcasted_iota(jnp.int32, (tm, tm), 1)).astype(BF16)
    csum = _dot(below, onehot.astype(BF16)) + base_ref[...]
    route = jnp.zeros((tm, LANES), F32)
    for k in range(TOP_K):
        rank = jnp.sum(jnp.where(sels[k], csum, 0.0), axis=-1, keepdims=True)
        route = jnp.where(lane == k, idxs[k].astype(F32), route)
        route = jnp.where(lane == TOP_K + k, es[k] / tot, route)
        route = jnp.where(lane == 2 * TOP_K + k, rank, route)
    route_ref[...] = route
    base = base_ref[...] + jnp.sum(onehot, axis=0, keepdims=True)
    base_ref[...] = base
    cnt_ref[...] = jnp.broadcast_to(base, cnt_ref.shape)


def _outproj(att, gla, x2, mod3, gpost, gpre, woa, wog, wr, br, seq, tm, n_exp):
    t, d = x2.shape
    per = seq // tm
    row = lambda i: (i, 0)
    fixed = lambda i: (0, 0)
    return pl.pallas_call(
        functools.partial(_outproj_kernel, n_exp=n_exp),
        grid=(t // tm,),
        in_specs=[pl.BlockSpec((tm, ATT_WIDTH), row),
                  pl.BlockSpec((tm, GLA_V_WIDTH), row),
                  pl.BlockSpec((tm, d), row),
                  pl.BlockSpec((None, 8, d), lambda i: (i // per, 0, 0)),
                  pl.BlockSpec((1, d), fixed),
                  pl.BlockSpec((1, d), fixed),
                  pl.BlockSpec((ATT_WIDTH, d), fixed),
                  pl.BlockSpec((GLA_V_WIDTH, d), fixed),
                  pl.BlockSpec((d, LANES), fixed),
                  pl.BlockSpec((1, LANES), fixed)],
        out_specs=[pl.BlockSpec((tm, d), row),
                   pl.BlockSpec((tm * (d // 2 // LANES), LANES), row),
                   pl.BlockSpec((tm, LANES), row),
                   pl.BlockSpec((8, LANES), fixed)],
        out_shape=[jax.ShapeDtypeStruct((t, d), F32),
                   jax.ShapeDtypeStruct((t * (d // 2 // LANES), LANES), jnp.int32),
                   jax.ShapeDtypeStruct((t, LANES), F32),
                   jax.ShapeDtypeStruct((8, LANES), F32)],
        scratch_shapes=[pltpu.VMEM((1, LANES), F32)],
        compiler_params=_params(("arbitrary",)),
        name="outproj",
    )(att, gla, x2, mod3, gpost, gpre, woa, wog, wr, br)


def _w1prep_kernel(w_ref, g_ref, l_ref):
    n = 2 * LANES
    r = lax.broadcasted_iota(jnp.int32, (n, n), 0)
    c = lax.broadcasted_iota(jnp.int32, (n, n), 1)
    perm = (r == jnp.where(c < LANES, 2 * c, 2 * (c - LANES) + 1)).astype(BF16)
    for j in range(g_ref.shape[1] // LANES):
        d = _dot(w_ref[:, j * n:(j + 1) * n].astype(BF16), perm)
        g_ref[:, j * LANES:(j + 1) * LANES] = d[:, :LANES].astype(BF16)
        l_ref[:, j * LANES:(j + 1) * LANES] = d[:, LANES:].astype(BF16)


def _w1prep(w1):
    n_exp, d, f2 = w1.shape
    f = f2 // 2
    out = jax.ShapeDtypeStruct((n_exp, d, f), BF16)
    return pl.pallas_call(
        _w1prep_kernel,
        grid=(n_exp,),
        in_specs=[pl.BlockSpec((None, d, f2), lambda e: (e, 0, 0))],
        out_specs=[pl.BlockSpec((None, d, f), lambda e: (e, 0, 0))] * 2,
        out_shape=[out, out],
        compiler_params=_params(("parallel",)),
        name="w1prep",
    )(w1)


IDX_RING = 4


def _moe_kernel(be_ref, nu_ref, idx_hbm, h2_hbm, w1g_ref, w1l_ref, b1g_ref, b1l_ref, w2_ref, b2_ref, y4_hbm,
                idx_sm, xb0, xb1, yb0, yb1, sem_i, sem_g, sem_s, *, tm):
    del be_ref
    i = pl.program_id(0)
    nu = nu_ref[0]
    last = nu - 1
    c = xb0.shape[0] // tm
    half = c * LANES

    def idx_copy(j):
        slot = j & (IDX_RING - 1)
        return pltpu.make_async_copy(idx_hbm.at[jnp.minimum(j, last)], idx_sm.at[pl.ds(slot, 1)], sem_i.at[slot])

    def issue_gather(j, xb, par):
        slot = j & (IDX_RING - 1)
        for r in range(tm):
            src = pl.multiple_of(idx_sm[slot, r], c)
            pltpu.make_async_copy(h2_hbm.at[pl.ds(src, c), :], xb.at[pl.ds(r * c, c), :],
                                  sem_g.at[par]).start(priority=r % 2)

    def issue_scatter(j, yb, par):
        slot = j & (IDX_RING - 1)
        for r in range(tm):
            dst = pl.multiple_of(idx_sm[slot, tm + r], c)
            pltpu.make_async_copy(yb.at[pl.ds(r * c, c), :], y4_hbm.at[pl.ds(dst, c), :],
                                  sem_s.at[par]).start(priority=r % 2)

    def wait_gather(xb, par):
        pltpu.make_async_copy(h2_hbm.at[pl.ds(0, tm * c), :], xb, sem_g.at[par]).wait()

    def wait_scatter(yb, par):
        pltpu.make_async_copy(yb, y4_hbm.at[pl.ds(0, tm * c), :], sem_s.at[par]).wait()

    def compute(xb, yb):
        lo, hi = _unpack(_load_rows(xb, c))
        xl = lo.astype(BF16)
        xh = hi.astype(BF16)
        glu = _dot(xl, w1g_ref[:half, :]) + _dot(xh, w1g_ref[half:, :]) + b1g_ref[...]
        lin = _dot(xl, w1l_ref[:half, :]) + _dot(xh, w1l_ref[half:, :]) + b1l_ref[...]
        glu = jnp.minimum(glu, SWIGLU_LIMIT)
        lin = jnp.clip(lin, -SWIGLU_LIMIT, SWIGLU_LIMIT)
        a = glu * jax.nn.sigmoid(SWIGLU_ALPHA * glu) * (lin + 1.0)
        _store_rows(yb, _pack(_dot(a.astype(BF16), w2_ref[...]) + b2_ref[...]))

    def step(xb_cur, xb_nxt, yb_cur, yb_prv, par, first):
        idx_copy(i + 1).wait()
        wait_gather(xb_cur, par)
        if not first:
            @pl.when(i >= 2)
            def _():
                wait_scatter(yb_cur, par)
        idx_copy(i + 2).start()
        if not first:
            issue_scatter(i - 1, yb_prv, 1 - par)
        issue_gather(i + 1, xb_nxt, 1 - par)
        compute(xb_cur, yb_cur)

        @pl.when(i == last)
        def _():
            issue_scatter(i, yb_cur, par)
            if not first:
                wait_scatter(yb_prv, 1 - par)
            wait_scatter(yb_cur, par)
            wait_gather(xb_nxt, 1 - par)
            idx_copy(i + 2).wait()

    @pl.when(i == 0)
    def _():
        n_spare = 2 * tm * c
        for par, yb in enumerate((yb0, yb1)):
            yb[...] = jnp.zeros_like(yb)
            spare = y4_hbm.at[pl.ds(y4_hbm.shape[0] - n_spare + par * tm * c, tm * c), :]
            fill = pltpu.make_async_copy(yb, spare, sem_s.at[par])
            fill.start()
            fill.wait()
        first_idx = idx_copy(0)
        first_idx.start()
        first_idx.wait()
        idx_copy(1).start()
        issue_gather(0, xb0, 0)
        step(xb0, xb1, yb0, yb1, 0, True)

    @pl.when((i > 0) & (i < nu) & ((i & 1) == 0))
    def _():
        step(xb0, xb1, yb0, yb1, 0, False)

    @pl.when((i < nu) & ((i & 1) == 1))
    def _():
        step(xb1, xb0, yb1, yb0, 1, False)


def _moe(block_e, n_used, idx, h2p, w1g, w1l, b1g, b1l, w2, b2, n_rows_out):
    nb, _, two_tm = idx.shape
    tm = two_tm // 2
    d, f = w1g.shape[1], w1g.shape[2]
    c = d // 2 // LANES
    wsel = lambda i, be, nu: (be[i], 0, 0)
    grid_spec = pltpu.PrefetchScalarGridSpec(
        num_scalar_prefetch=2,
        grid=(nb,),
        in_specs=[pl.BlockSpec(memory_space=pl.ANY),
                  pl.BlockSpec(memory_space=pl.ANY),
                  pl.BlockSpec((None, d, f), wsel),
                  pl.BlockSpec((None, d, f), wsel),
                  pl.BlockSpec((None, 1, f), wsel),
                  pl.BlockSpec((None, 1, f), wsel),
                  pl.BlockSpec((None, f, d), wsel),
                  pl.BlockSpec((None, 1, d), wsel)],
        out_specs=pl.BlockSpec(memory_space=pl.ANY),
        scratch_shapes=[pltpu.SMEM((IDX_RING, two_tm), jnp.int32),
                        pltpu.VMEM((tm * c, LANES), jnp.int32),
                        pltpu.VMEM((tm * c, LANES), jnp.int32),
                        pltpu.VMEM((tm * c, LANES), jnp.int32),
                        pltpu.VMEM((tm * c, LANES), jnp.int32),
                        pltpu.SemaphoreType.DMA((IDX_RING,)),
                        pltpu.SemaphoreType.DMA((2,)),
                        pltpu.SemaphoreType.DMA((2,))],
    )
    return pl.pallas_call(
        functools.partial(_moe_kernel, tm=tm),
        grid_spec=grid_spec,
        out_shape=jax.ShapeDtypeStruct((n_rows_out * c, LANES), jnp.int32),
        compiler_params=_params(("arbitrary",)),
        name="moe",
    )(block_e, n_used, idx, h2p, w1g, w1l, b1g, b1l, w2, b2)


def _final_kernel(y0_ref, y1_ref, y2_ref, y3_ref, route_ref, x1_ref, mod_ref, g_ref, o_ref):
    d = o_ref.shape[1]
    half = d // 2
    c = half // LANES
    acc_lo = jnp.zeros((o_ref.shape[0], half), F32)
    acc_hi = jnp.zeros((o_ref.shape[0], half), F32)
    for k, y_ref in enumerate((y0_ref, y1_ref, y2_ref, y3_ref)):
        lo, hi = _unpack(_load_rows(y_ref, c))
        gate = route_ref[:, TOP_K + k:TOP_K + k + 1]
        acc_lo = acc_lo + gate * lo
        acc_hi = acc_hi + gate * hi
    ms = (jnp.sum(acc_lo * acc_lo, axis=-1, keepdims=True) + jnp.sum(acc_hi * acc_hi, axis=-1, keepdims=True)) / d
    inv = lax.rsqrt(ms + NORM_EPS)
    o_ref[:, :half] = x1_ref[:, :half] + mod_ref[5:6, :half] * (acc_lo * inv * g_ref[:, :half])
    o_ref[:, half:] = x1_ref[:, half:] + mod_ref[5:6, half:] * (acc_hi * inv * g_ref[:, half:])


def _final(y4, route, x1, mod3, g, seq, tc):
    t, d = x1.shape
    per = seq // tc
    nper = t // tc
    row = lambda i: (i, 0)
    return pl.pallas_call(
        _final_kernel,
        grid=(nper,),
        in_specs=[pl.BlockSpec((tc * (d // 2 // LANES), LANES), functools.partial(lambda k, i: (k * nper + i, 0), k))
                  for k in range(TOP_K)]
                 + [pl.BlockSpec((tc, LANES), row),
                    pl.BlockSpec((tc, d), row),
                    pl.BlockSpec((None, 8, d), lambda i: (i // per, 0, 0)),
                    pl.BlockSpec((1, d), lambda i: (0, 0))],
        out_specs=pl.BlockSpec((tc, d), row),
        out_shape=jax.ShapeDtypeStruct((t, d), F32),
        compiler_params=_params(("parallel",)),
        name="final",
    )(y4, y4, y4, y4, route, x1, mod3, g)


def _pair_perm():
    half = ATT_Q_HEADS // 2
    idx = []
    for j in range(half):
        idx += list(range(j * ATT_HEAD_DIM, (j + 1) * ATT_HEAD_DIM))
        idx += list(range((half + j) * ATT_HEAD_DIM, (half + j + 1) * ATT_HEAD_DIM))
    return np.asarray(idx, np.int32)


def _layer(x, mod, g_pre_mix, g_post_mix, g_pre_ffn, g_post_ffn, w_in, w_gla_gate_up, b_gla_gate, g_gla_norm,
           sinks, w_out, w_router, b_router, w_mlp1, b_mlp1, w_mlp2, b_mlp2):
    nbatch, seq, d = x.shape
    t = nbatch * seq
    n_exp = w_router.shape[1]
    f = w_mlp2.shape[1]
    x2 = x.reshape(t, d)
    mod3 = jnp.pad(mod.reshape(nbatch, 6, d), ((0, 0), (0, 2), (0, 0)))

    perm = _pair_perm()
    n_main = C_GA - C_AKV
    w_r = jnp.concatenate([w_in[:, perm], w_in[:, ATT_WIDTH:ATT_WIDTH + n_main],
                           jnp.pad(w_in[:, ATT_WIDTH + n_main:], ((0, 0), (0, LANES - GLA_GATE_RANK)))],
                          axis=1).astype(BF16)
    wup = jnp.pad(w_gla_gate_up, ((0, LANES - GLA_GATE_RANK), (0, 0)))
    woa = w_out[:ATT_WIDTH][perm].astype(BF16)
    wog = w_out[ATT_WIDTH:].astype(BF16)
    wr = jnp.pad(w_router, ((0, 0), (0, LANES - n_exp)))
    br = jnp.pad(b_router, (0, LANES - n_exp)).reshape(1, LANES)
    w1g, w1l = _w1prep(w_mlp1)
    b1 = b_mlp1.reshape(n_exp, 1, f, 2)
    w2 = w_mlp2.astype(BF16)

    tm = min(512, seq)
    aq, akv, gq, gk, gv, gg, gl = _inproj(x2, mod3, g_pre_mix.reshape(1, d), w_r, wup,
                                          b_gla_gate.reshape(1, GLA_K_WIDTH), seq, tm)
    att = _att(aq, akv, sinks, nbatch, seq, min(256, seq))
    gla = _gla(gq, gk, gv, gg, gl, g_gla_norm.reshape(1, GLA_DV), nbatch, seq, min(256, seq))
    x1, h2p, route, cnt = _outproj(att, gla, x2, mod3, g_post_mix.reshape(1, d), g_pre_ffn.reshape(1, d),
                                   woa, wog, wr, br, seq, tm, n_exp)

    rows = MOE_ROWS
    counts = cnt[0, :n_exp].astype(jnp.int32)
    padded = (counts + rows - 1) // rows * rows
    pend = jnp.cumsum(padded)
    pstart = pend - padded
    top_i = route[:, 0:TOP_K].astype(jnp.int32)
    rank = route[:, 2 * TOP_K:3 * TOP_K].astype(jnp.int32)
    dest = (pstart[top_i] + rank).reshape(t * TOP_K)
    n_slots = t * TOP_K + n_exp * rows
    n_blocks = n_slots // rows
    n_used = (pend[-1] // rows).astype(jnp.int32)
    blk_ids = jnp.minimum(jnp.arange(n_blocks, dtype=jnp.int32), n_used - 1)
    block_e = jnp.minimum(jnp.searchsorted(pend, blk_ids * rows, side='right'), n_exp - 1).astype(jnp.int32)
    assign = jnp.arange(t * TOP_K, dtype=jnp.int32)
    slot = jnp.arange(n_slots, dtype=jnp.int32)
    spare = t * TOP_K + ((slot // rows) % 2) * rows + slot % rows
    src_tok = jnp.zeros((n_slots,), jnp.int32).at[dest].set(assign // TOP_K)
    dst_row = spare.at[dest].set((assign % TOP_K) * t + assign // TOP_K)
    sub = d // 2 // LANES
    idx = jnp.concatenate([src_tok.reshape(n_blocks, 1, rows), dst_row.reshape(n_blocks, 1, rows)], axis=2) * sub

    y4 = _moe(block_e, n_used.reshape(1), idx, h2p, w1g, w1l, b1[..., 0], b1[..., 1], w2,
              b_mlp2.reshape(n_exp, 1, d), t * TOP_K + 2 * rows)
    out = _final(y4, route, x1, mod3, g_post_ffn.reshape(1, d), seq, tm)
    return out.reshape(nbatch, seq, d)


def kernel(x, c, w_ada, b_ada, g_pre_mix, g_post_mix, g_pre_ffn, g_post_ffn, w_in, w_gla_gate_up, b_gla_gate,
           g_gla_norm, sinks, w_out, w_router, b_router, w_mlp1, b_mlp1, w_mlp2, b_mlp2):
    for l in range(w_in.shape[0]):
        mod = _ada(c, w_ada[l], b_ada[l])
        x = _layer(x, mod, g_pre_mix[l], g_post_mix[l], g_pre_ffn[l], g_post_ffn[l], w_in[l], w_gla_gate_up[l],
                   b_gla_gate[l], g_gla_norm[l], sinks[l], w_out[l], w_router[l], b_router[l], w_mlp1[l], b_mlp1[l],
                   w_mlp2[l], b_mlp2[l])
    return x
```

```python
import functools

import numpy as np
import jax
import jax.numpy as jnp
from jax import lax
from jax.experimental import pallas as pl
from jax.experimental.pallas import tpu as pltpu

F32 = jnp.float32
BF16 = jnp.bfloat16
HI = lax.Precision.HIGHEST

ATT_Q_HEADS = 8
ATT_KV_HEADS = 2
ATT_HEAD_DIM = 64
ATT_BLOCK = 128
GLA_HEADS = 4
GLA_DK = 64
GLA_DV = 128
GLA_GATE_RANK = 16
GLA_GATE_NORMALIZER = 16.0
GLA_CHUNK = 64
TOP_K = 4
SWIGLU_LIMIT = 7.0
SWIGLU_ALPHA = 1.702
NORM_EPS = 1e-6

LANES = 128
ATT_WIDTH = ATT_Q_HEADS * ATT_HEAD_DIM
ATT_KV_WIDTH = ATT_KV_HEADS * ATT_HEAD_DIM
GLA_K_WIDTH = GLA_HEADS * GLA_DK
GLA_V_WIDTH = GLA_HEADS * GLA_DV
C_AQ = 0
C_AKV = C_AQ + ATT_WIDTH
C_GQ = C_AKV + 2 * ATT_KV_WIDTH
C_GK = C_GQ + GLA_K_WIDTH
C_GV = C_GK + GLA_K_WIDTH
C_GG = C_GV + GLA_V_WIDTH
C_GA = C_GG + GLA_V_WIDTH
C_END = C_GA + LANES

MOE_ROWS = 256
VMEM_LIMIT = 48 * 1024 * 1024


def _dot(a, b, prec=None):
    return jnp.dot(a, b, preferred_element_type=F32, precision=prec)


def _dot_nt(a, b):
    return lax.dot_general(a, b, (((1,), (1,)), ((), ())), preferred_element_type=F32)


def _rms(t):
    return t * lax.rsqrt(jnp.mean(t * t, axis=-1, keepdims=True) + NORM_EPS)


def _params(sem):
    return pltpu.CompilerParams(dimension_semantics=sem, vmem_limit_bytes=VMEM_LIMIT)


HI16 = -65536


def _pack(a):
    half = a.shape[1] // 2
    lo = lax.bitcast_convert_type(a[:, :half].astype(BF16).astype(F32), jnp.int32)
    hi = lax.bitcast_convert_type(a[:, half:].astype(BF16).astype(F32), jnp.int32)
    return hi | lax.shift_right_logical(lo, 16)


def _unpack(p):
    lo = lax.bitcast_convert_type(lax.shift_left(p, 16), F32)
    hi = lax.bitcast_convert_type(p & HI16, F32)
    return lo, hi


def _store_rows(ref, p):
    n, w = p.shape
    c = w // LANES
    for s in range(c):
        ref[pl.ds(s, n, stride=c), :] = p[:, s * LANES:(s + 1) * LANES]


def _load_rows(ref, c):
    n = ref.shape[0] // c
    return jnp.concatenate([ref[pl.ds(s, n, stride=c), :] for s in range(c)], axis=1)


def _ada_kernel(c_ref, w_ref, b_ref, o_ref):
    c = c_ref[...]
    o_ref[...] = _dot(c * jax.nn.sigmoid(c), w_ref[...], HI) + b_ref[...]


def _ada(c, w, b):
    nb, d = c.shape
    n = w.shape[1]
    cp = jnp.zeros((8, d), F32).at[:nb].set(c)
    out = pl.pallas_call(
        _ada_kernel,
        grid=(n // d,),
        in_specs=[pl.BlockSpec((8, d), lambda j: (0, 0)),
                  pl.BlockSpec((d, d), lambda j: (0, j)),
                  pl.BlockSpec((1, d), lambda j: (0, j))],
        out_specs=pl.BlockSpec((8, d), lambda j: (0, j)),
        out_shape=jax.ShapeDtypeStruct((8, n), F32),
        compiler_params=_params(("parallel",)),
        name="ada",
    )(cp, w, b.reshape(1, n))
    return out[:nb]


def _inproj_kernel(x_ref, mod_ref, g_ref, w_ref, wup_ref, bup_ref,
                   aq_ref, akv_ref, gq_ref, gk_ref, gv_ref, gg_ref, gl_ref):
    h = _rms(x_ref[...]) * g_ref[...]
    h = h * (1.0 + mod_ref[1:2, :]) + mod_ref[0:1, :]
    p = _dot(h.astype(BF16), w_ref[...])
    aq_ref[...] = (p[:, C_AQ:C_AKV] * (ATT_HEAD_DIM ** -0.5)).astype(BF16)
    akv_ref[...] = p[:, C_AKV:C_GQ].astype(BF16)
    gq_ref[...] = (p[:, C_GQ:C_GK] * (GLA_DK ** -0.5)).astype(BF16)
    gk_ref[...] = p[:, C_GK:C_GV].astype(BF16)
    gv_ref[...] = p[:, C_GV:C_GG].astype(BF16)
    gg_ref[...] = p[:, C_GG:C_GA].astype(BF16)
    z = _dot(p[:, C_GA:C_END], wup_ref[...], HI) + bup_ref[...]
    gl_ref[...] = (jnp.minimum(z, 0.0) - jnp.log(1.0 + jnp.exp(-jnp.abs(z)))) * (1.0 / GLA_GATE_NORMALIZER)


def _inproj(x2, mod3, g, w_r, wup, bup, seq, tm):
    t, d = x2.shape
    per = seq // tm
    row = lambda i: (i, 0)
    fixed = lambda i: (0, 0)
    widths = (ATT_WIDTH, 2 * ATT_KV_WIDTH, GLA_K_WIDTH, GLA_K_WIDTH, GLA_V_WIDTH, GLA_V_WIDTH)
    out_shape = [jax.ShapeDtypeStruct((t, w), BF16) for w in widths] + [jax.ShapeDtypeStruct((t, GLA_K_WIDTH), F32)]
    out_specs = [pl.BlockSpec((tm, w), row) for w in widths] + [pl.BlockSpec((tm, GLA_K_WIDTH), row)]
    return pl.pallas_call(
        _inproj_kernel,
        grid=(t // tm,),
        in_specs=[pl.BlockSpec((tm, d), row),
                  pl.BlockSpec((None, 8, d), lambda i: (i // per, 0, 0)),
                  pl.BlockSpec((1, d), fixed),
                  pl.BlockSpec((d, C_END), fixed),
                  pl.BlockSpec((LANES, GLA_K_WIDTH), fixed),
                  pl.BlockSpec((1, GLA_K_WIDTH), fixed)],
        out_specs=out_specs,
        out_shape=out_shape,
        compiler_params=_params(("parallel",)),
        name="inproj",
    )(x2, mod3, g, w_r, wup, bup)


def _att_kernel(sinks_ref, q_ref, kv_ref, kvp_ref, o_ref, *, nblk):
    i = pl.program_id(1)
    blk = ATT_BLOCK
    lo = lax.broadcasted_iota(jnp.int32, (blk, LANES), 1) < ATT_HEAD_DIM
    qi = lax.broadcasted_iota(jnp.int32, (2 * blk, 2 * blk), 0) % blk
    kj = lax.broadcasted_iota(jnp.int32, (2 * blk, 2 * blk), 1)
    cur_ok = (kj >= blk) & ((kj - blk) <= qi)
    prev_ok = (kj < blk) & (kj > qi)
    first_off = jnp.where(i > 0, 0, blk)
    top = lax.broadcasted_iota(jnp.int32, (2 * blk, 1), 0) < blk
    for jb in range(nblk):
        r0 = jb * blk
        kvc = kv_ref[r0:r0 + blk, :]
        if jb == 0:
            kvp = kvp_ref[...]
            mask = cur_ok | (prev_ok & (kj >= first_off))
        else:
            kvp = kv_ref[r0 - blk:r0, :]
            mask = cur_ok | prev_ok
        kcat = jnp.concatenate([kvp[:, 0:ATT_KV_WIDTH], kvc[:, 0:ATT_KV_WIDTH]], axis=0)
        vcat = jnp.concatenate([kvp[:, ATT_KV_WIDTH:], kvc[:, ATT_KV_WIDTH:]], axis=0)
        for j in range(ATT_Q_HEADS // 2):
            qp = q_ref[r0:r0 + blk, j * LANES:(j + 1) * LANES]
            zero = jnp.zeros_like(qp)
            q2 = jnp.concatenate([jnp.where(lo, qp, zero), jnp.where(lo, zero, qp)], axis=0)
            s = jnp.where(mask, _dot_nt(q2, kcat), -jnp.inf)
            sink = jnp.where(top, sinks_ref[j], sinks_ref[ATT_Q_HEADS // 2 + j])
            m = jnp.maximum(jnp.max(s, axis=-1, keepdims=True), sink)
            p = jnp.exp(s - m)
            den = jnp.sum(p, axis=-1, keepdims=True) + jnp.exp(sink - m)
            o2 = _dot(p.astype(BF16), vcat) / den
            o = jnp.where(lo, o2[0:blk], o2[blk:2 * blk])
            o_ref[r0:r0 + blk, j * LANES:(j + 1) * LANES] = o.astype(BF16)


def _att(aq, akv, sinks, nbatch, seq, ta):
    t = aq.shape[0]
    nblk = ta // ATT_BLOCK
    per = seq // ta
    perb = seq // ATT_BLOCK
    return pl.pallas_call(
        functools.partial(_att_kernel, nblk=nblk),
        grid=(nbatch, per),
        in_specs=[pl.BlockSpec(memory_space=pltpu.SMEM),
                  pl.BlockSpec((ta, ATT_WIDTH), lambda b, i: (b * per + i, 0)),
                  pl.BlockSpec((ta, 2 * ATT_KV_WIDTH), lambda b, i: (b * per + i, 0)),
                  pl.BlockSpec((ATT_BLOCK, 2 * ATT_KV_WIDTH),
                               lambda b, i: (b * perb + jnp.maximum(i * nblk - 1, 0), 0))],
        out_specs=pl.BlockSpec((ta, ATT_WIDTH), lambda b, i: (b * per + i, 0)),
        out_shape=jax.ShapeDtypeStruct((t, ATT_WIDTH), BF16),
        compiler_params=_params(("parallel", "parallel")),
        name="att",
    )(sinks, aq, akv, akv)


def _gla_kernel(gq_ref, gk_ref, gv_ref, gg_ref, gl_ref, gn_ref, o_ref, st_ref, *, nchunk):
    ch = GLA_CHUNK
    tg = nchunk * ch

    @pl.when(pl.program_id(1) == 0)
    def _():
        st_ref[...] = jnp.zeros_like(st_ref)

    ri = lax.broadcasted_iota(jnp.int32, (tg, tg), 0)
    ci = lax.broadcasted_iota(jnp.int32, (tg, tg), 1)
    tri = ((ri // ch == ci // ch) & (ci <= ri)).astype(F32)
    bt = _dot(tri, gl_ref[...], HI)
    lo = lax.broadcasted_iota(jnp.int32, (ch, LANES), 1) < GLA_DK
    causal = lax.broadcasted_iota(jnp.int32, (ch, ch), 0) >= lax.broadcasted_iota(jnp.int32, (ch, ch), 1)
    eye = (lax.broadcasted_iota(jnp.int32, (LANES, LANES), 0)
           == lax.broadcasted_iota(jnp.int32, (LANES, LANES), 1)).astype(BF16)
    gn = gn_ref[...]
    zero = jnp.zeros((ch, LANES), BF16)
    for c in range(nchunk):
        r0 = c * ch
        b = bt[r0:r0 + ch]
        bm = b[ch // 2 - 1:ch // 2]
        bl = b[ch - 1:ch]
        q = gq_ref[r0:r0 + ch, :].astype(F32)
        k = gk_ref[r0:r0 + ch, :].astype(F32)
        qe = (q * jnp.exp(b - bm)).astype(BF16)
        ke = (k * jnp.exp(bm - b)).astype(BF16)
        qs = (q * jnp.exp(b)).astype(BF16)
        kl = (k * jnp.exp(bl - b)).astype(BF16)
        dec = jnp.exp(bl)
        for p in range(GLA_HEADS // 2):
            sl = slice(p * LANES, (p + 1) * LANES)
            st = st_ref[p]
            stb = st.astype(BF16)
            upd = None
            for hh in range(2):
                h = 2 * p + hh
                hs = slice(h * GLA_DV, (h + 1) * GLA_DV)
                msk = lo if hh == 0 else jnp.logical_not(lo)
                a = _dot_nt(jnp.where(msk, qe[:, sl], zero), ke[:, sl])
                a = jnp.where(causal, a, 0.0).astype(BF16)
                v = gv_ref[r0:r0 + ch, hs]
                o = _dot(a, v) + _dot_nt(jnp.where(msk, qs[:, sl], zero), stb)
                vt = _dot_nt(eye, v).astype(BF16)
                u = _dot(vt, jnp.where(msk, kl[:, sl], zero))
                upd = u if upd is None else upd + u
                gg = gg_ref[r0:r0 + ch, hs].astype(F32)
                o_ref[r0:r0 + ch, hs] = (_rms(o) * gn * (gg * jax.nn.sigmoid(gg))).astype(BF16)
            st_ref[p] = st * dec[:, sl] + upd


def _gla(gq, gk, gv, gg, gl, gn, nbatch, seq, tg):
    t = gq.shape[0]
    per = seq // tg
    row = lambda b, i: (b * per + i, 0)
    return pl.pallas_call(
        functools.partial(_gla_kernel, nchunk=tg // GLA_CHUNK),
        grid=(nbatch, per),
        in_specs=[pl.BlockSpec((tg, GLA_K_WIDTH), row),
                  pl.BlockSpec((tg, GLA_K_WIDTH), row),
                  pl.BlockSpec((tg, GLA_V_WIDTH), row),
                  pl.BlockSpec((tg, GLA_V_WIDTH), row),
                  pl.BlockSpec((tg, GLA_K_WIDTH), row),
                  pl.BlockSpec((1, GLA_DV), lambda b, i: (0, 0))],
        out_specs=pl.BlockSpec((tg, GLA_V_WIDTH), row),
        out_shape=jax.ShapeDtypeStruct((t, GLA_V_WIDTH), BF16),
        scratch_shapes=[pltpu.VMEM((GLA_HEADS // 2, GLA_DV, LANES), F32)],
        compiler_params=_params(("parallel", "arbitrary")),
        name="gla",
    )(gq, gk, gv, gg, gl, gn)


def _outproj_kernel(att_ref, gla_ref, x_ref, mod_ref, gpost_ref, gpre_ref, woa_ref, wog_ref, wr_ref, br_ref,
                    x1_ref, h2_ref, route_ref, cnt_ref, base_ref, *, n_exp):
    tm = x_ref.shape[0]

    @pl.when(pl.program_id(0) == 0)
    def _():
        base_ref[...] = jnp.zeros_like(base_ref)

    y = _dot(att_ref[...], woa_ref[...]) + _dot(gla_ref[...], wog_ref[...])
    x1 = x_ref[...] + mod_ref[2:3, :] * (_rms(y) * gpost_ref[...])
    x1_ref[...] = x1
    h2 = _rms(x1) * gpre_ref[...]
    h2 = h2 * (1.0 + mod_ref[4:5, :]) + mod_ref[3:4, :]
    _store_rows(h2_ref, _pack(h2))

    lane = lax.broadcasted_iota(jnp.int32, (tm, LANES), 1)
    logits = _dot(h2, wr_ref[...], HI) + br_ref[...]
    vals = jnp.where(lane < n_exp, logits, -jnp.inf)
    sels, tops, idxs = [], [], []
    for _ in range(TOP_K):
        m = jnp.max(vals, axis=-1, keepdims=True)
        idx = jnp.min(jnp.where(vals == m, lane, LANES), axis=-1, keepdims=True)
        sel = lane == idx
        vals = jnp.where(sel, -jnp.inf, vals)
        sels.append(sel)
        tops.append(m)
        idxs.append(idx)
    es = [jnp.exp(m - tops[0]) for m in tops]
    tot = es[0] + es[1] + es[2] + es[3]
    onehot = jnp.zeros((tm, LANES), F32)
    for sel in sels:
        onehot = onehot + sel.astype(F32)
    below = (lax.broadcasted_iota(jnp.int32, (tm, tm), 0) > lax.broadcasted_iota(jnp.int32, (tm, tm), 1)).astype(BF16)
    csum = _dot(below, onehot.astype(BF16)) + base_ref[...]
    route = jnp.zeros((tm, LANES), F32)
    for k in range(TOP_K):
        rank = jnp.sum(jnp.where(sels[k], csum, 0.0), axis=-1, keepdims=True)
        route = jnp.where(lane == k, idxs[k].astype(F32), route)
        route = jnp.where(lane == TOP_K + k, es[k] / tot, route)
        route = jnp.where(lane == 2 * TOP_K + k, rank, route)
    route_ref[...] = route
    base = base_ref[...] + jnp.sum(onehot, axis=0, keepdims=True)
    base_ref[...] = base
    cnt_ref[...] = jnp.broadcast_to(base, cnt_ref.shape)


def _outproj(att, gla, x2, mod3, gpost, gpre, woa, wog, wr, br, seq, tm, n_exp):
    t, d = x2.shape
    per = seq // tm
    row = lambda i: (i, 0)
    fixed = lambda i: (0, 0)
    return pl.pallas_call(
        functools.partial(_outproj_kernel, n_exp=n_exp),
        grid=(t // tm,),
        in_specs=[pl.BlockSpec((tm, ATT_WIDTH), row),
                  pl.BlockSpec((tm, GLA_V_WIDTH), row),
                  pl.BlockSpec((tm, d), row),
                  pl.BlockSpec((None, 8, d), lambda i: (i // per, 0, 0)),
                  pl.BlockSpec((1, d), fixed),
                  pl.BlockSpec((1, d), fixed),
                  pl.BlockSpec((ATT_WIDTH, d), fixed),
                  pl.BlockSpec((GLA_V_WIDTH, d), fixed),
                  pl.BlockSpec((d, LANES), fixed),
                  pl.BlockSpec((1, LANES), fixed)],
        out_specs=[pl.BlockSpec((tm, d), row),
                   pl.BlockSpec((tm * (d // 2 // LANES), LANES), row),
                   pl.BlockSpec((tm, LANES), row),
                   pl.BlockSpec((8, LANES), fixed)],
        out_shape=[jax.ShapeDtypeStruct((t, d), F32),
                   jax.ShapeDtypeStruct((t * (d // 2 // LANES), LANES), jnp.int32),
                   jax.ShapeDtypeStruct((t, LANES), F32),
                   jax.ShapeDtypeStruct((8, LANES), F32)],
        scratch_shapes=[pltpu.VMEM((1, LANES), F32)],
        compiler_params=_params(("arbitrary",)),
        name="outproj",
    )(att, gla, x2, mod3, gpost, gpre, woa, wog, wr, br)


def _w1prep_kernel(w_ref, g_ref, l_ref):
    n = 2 * LANES
    r = lax.broadcasted_iota(jnp.int32, (n, n), 0)
    c = lax.broadcasted_iota(jnp.int32, (n, n), 1)
    perm = (r == jnp.where(c < LANES, 2 * c, 2 * (c - LANES) + 1)).astype(BF16)
    for j in range(g_ref.shape[1] // LANES):
        d = _dot(w_ref[:, j * n:(j + 1) * n].astype(BF16), perm)
        g_ref[:, j * LANES:(j + 1) * LANES] = d[:, :LANES].astype(BF16)
        l_ref[:, j * LANES:(j + 1) * LANES] = d[:, LANES:].astype(BF16)


def _w1prep(w1):
    n_exp, d, f2 = w1.shape
    f = f2 // 2
    out = jax.ShapeDtypeStruct((n_exp, d, f), BF16)
    return pl.pallas_call(
        _w1prep_kernel,
        grid=(n_exp,),
        in_specs=[pl.BlockSpec((None, d, f2), lambda e: (e, 0, 0))],
        out_specs=[pl.BlockSpec((None, d, f), lambda e: (e, 0, 0))] * 2,
        out_shape=[out, out],
        compiler_params=_params(("parallel",)),
        name="w1prep",
    )(w1)


def _dispatch_kernel(pend_ref, nu_ref, idx_hbm, h2_ref, xs_hbm, idx_sm, zbuf, sem_i, sem_r, sem_z,
                     *, tt, rows, n_exp, n_blocks):
    i = pl.program_id(0)
    nsteps = pl.num_programs(0)
    c = h2_ref.shape[0] // tt
    blk_rows = rows * c

    def idx_copy(j):
        slot = j & 1
        return pltpu.make_async_copy(idx_hbm.at[j], idx_sm.at[pl.ds(slot, 1)], sem_i.at[slot])

    def zero_fill(start):
        return pltpu.make_async_copy(zbuf, xs_hbm.at[pl.ds(pl.multiple_of(start, blk_rows), blk_rows), :], sem_z)

    @pl.when(i == 0)
    def _():
        idx_copy(0).start()
        zbuf[...] = jnp.zeros_like(zbuf)
        for phase in range(2):
            for e in range(n_exp):
                hi = pend_ref[e]
                lo = pend_ref[e - 1] if e > 0 else 0
                tail = nu_ref[0] + e
                for pred, start in ((hi > lo, (hi - rows) * c), (tail < n_blocks, tail * blk_rows)):
                    @pl.when(pred)
                    def _():
                        if phase == 0:
                            zero_fill(start).start()
                        else:
                            zero_fill(start).wait()

    idx_copy(i).wait()

    @pl.when(i + 1 < nsteps)
    def _():
        idx_copy(i + 1).start()

    slot = i & 1
    for r in range(tt):
        for k in range(TOP_K):
            n = r * TOP_K + k
            dst = pl.multiple_of(idx_sm[slot, n], c)
            pltpu.make_async_copy(h2_ref.at[pl.ds(r * c, c), :], xs_hbm.at[pl.ds(dst, c), :],
                                  sem_r).start(priority=n % 2)
    for k in range(TOP_K):
        pltpu.make_async_copy(h2_ref, xs_hbm.at[pl.ds(0, tt * c), :], sem_r).wait()


def _dispatch(pend, n_used, idx, h2p, n_slots, rows, tt):
    c = h2p.shape[0] * TOP_K // idx.size
    n_exp = pend.shape[0]
    grid_spec = pltpu.PrefetchScalarGridSpec(
        num_scalar_prefetch=2,
        grid=(idx.shape[0],),
        in_specs=[pl.BlockSpec(memory_space=pl.ANY),
                  pl.BlockSpec((tt * c, LANES), lambda i, pe, nu: (i, 0))],
        out_specs=pl.BlockSpec(memory_space=pl.ANY),
        scratch_shapes=[pltpu.SMEM((2, tt * TOP_K), jnp.int32),
                        pltpu.VMEM((rows * c, LANES), jnp.int32),
                        pltpu.SemaphoreType.DMA((2,)),
                        pltpu.SemaphoreType.DMA,
                        pltpu.SemaphoreType.DMA],
    )
    return pl.pallas_call(
        functools.partial(_dispatch_kernel, tt=tt, rows=rows, n_exp=n_exp, n_blocks=n_slots // rows),
        grid_spec=grid_spec,
        out_shape=jax.ShapeDtypeStruct((n_slots * c, LANES), jnp.int32),
        compiler_params=_params(("arbitrary",)),
        name="dispatch",
    )(pend, n_used, idx, h2p)


def _moe_kernel(be_ref, nu_ref, xs_ref, w1g_ref, w1l_ref, b1g_ref, b1l_ref, w2_ref, b2_ref, ys_ref, *, rows):
    del be_ref
    c = xs_ref.shape[0] // rows
    half = c * LANES

    @pl.when(pl.program_id(0) < nu_ref[0])
    def _():
        lo, hi = _unpack(_load_rows(xs_ref, c))
        xl = lo.astype(BF16)
        xh = hi.astype(BF16)
        glu = _dot(xl, w1g_ref[:half, :]) + _dot(xh, w1g_ref[half:, :]) + b1g_ref[...]
        lin = _dot(xl, w1l_ref[:half, :]) + _dot(xh, w1l_ref[half:, :]) + b1l_ref[...]
        glu = jnp.minimum(glu, SWIGLU_LIMIT)
        lin = jnp.clip(lin, -SWIGLU_LIMIT, SWIGLU_LIMIT)
        a = glu * jax.nn.sigmoid(SWIGLU_ALPHA * glu) * (lin + 1.0)
        _store_rows(ys_ref, _pack(_dot(a.astype(BF16), w2_ref[...]) + b2_ref[...]))

    @pl.when(pl.program_id(0) >= nu_ref[0])
    def _():
        ys_ref[...] = jnp.zeros_like(ys_ref)


def _moe(block_e, n_used, xs, w1g, w1l, b1g, b1l, w2, b2, rows):
    d, f = w1g.shape[1], w1g.shape[2]
    c = d // 2 // LANES
    nb = xs.shape[0] // (rows * c)
    wsel = lambda i, be, nu: (be[i], 0, 0)
    grid_spec = pltpu.PrefetchScalarGridSpec(
        num_scalar_prefetch=2,
        grid=(nb,),
        in_specs=[pl.BlockSpec((rows * c, LANES), lambda i, be, nu: (jnp.minimum(i, nu[0] - 1), 0)),
                  pl.BlockSpec((None, d, f), wsel),
                  pl.BlockSpec((None, d, f), wsel),
                  pl.BlockSpec((None, 1, f), wsel),
                  pl.BlockSpec((None, 1, f), wsel),
                  pl.BlockSpec((None, f, d), wsel),
                  pl.BlockSpec((None, 1, d), wsel)],
        out_specs=pl.BlockSpec((rows * c, LANES), lambda i, be, nu: (i, 0)),
    )
    return pl.pallas_call(
        functools.partial(_moe_kernel, rows=rows),
        grid_spec=grid_spec,
        out_shape=jax.ShapeDtypeStruct(xs.shape, jnp.int32),
        compiler_params=_params(("arbitrary",)),
        name="moe",
    )(block_e, n_used, xs, w1g, w1l, b1g, b1l, w2, b2)


def _combine_kernel(idx_hbm, ys_hbm, route_ref, x1_ref, mod_ref, g_ref, o_ref, idx_sm, buf, sem_i, sem_g, *, tt):
    i = pl.program_id(0)
    nsteps = pl.num_programs(0)
    d = o_ref.shape[1]
    half = d // 2
    c = half // LANES

    def idx_copy(j):
        slot = j & 1
        return pltpu.make_async_copy(idx_hbm.at[j], idx_sm.at[pl.ds(slot, 1)], sem_i.at[slot])

    def issue(j):
        slot = j & 1
        for r in range(tt):
            for k in range(TOP_K):
                n = r * TOP_K + k
                src = pl.multiple_of(idx_sm[slot, n], c)
                pltpu.make_async_copy(ys_hbm.at[pl.ds(src, c), :], buf.at[slot, k, pl.ds(r * c, c), :],
                                      sem_g.at[slot]).start(priority=n % 2)

    @pl.when(i == 0)
    def _():
        first = idx_copy(0)
        first.start()
        first.wait()
        issue(0)

        @pl.when(nsteps > 1)
        def _():
            idx_copy(1).start()

    @pl.when(i + 1 < nsteps)
    def _():
        idx_copy(i + 1).wait()
        issue(i + 1)

    @pl.when(i + 2 < nsteps)
    def _():
        idx_copy(i + 2).start()

    slot = i & 1
    for k in range(TOP_K):
        pltpu.make_async_copy(ys_hbm.at[pl.ds(0, tt * c), :], buf.at[slot, k], sem_g.at[slot]).wait()

    acc_lo = jnp.zeros((tt, half), F32)
    acc_hi = jnp.zeros((tt, half), F32)
    for k in range(TOP_K):
        lo, hi = _unpack(_load_rows(buf.at[slot, k], c))
        gate = route_ref[:, TOP_K + k:TOP_K + k + 1]
        acc_lo = acc_lo + gate * lo
        acc_hi = acc_hi + gate * hi
    ms = (jnp.sum(acc_lo * acc_lo, axis=-1, keepdims=True) + jnp.sum(acc_hi * acc_hi, axis=-1, keepdims=True)) / d
    inv = lax.rsqrt(ms + NORM_EPS)
    o_ref[:, :half] = x1_ref[:, :half] + mod_ref[5:6, :half] * (acc_lo * inv * g_ref[:, :half])
    o_ref[:, half:] = x1_ref[:, half:] + mod_ref[5:6, half:] * (acc_hi * inv * g_ref[:, half:])


def _combine(idx, ys, route, x1, mod3, g, seq, tt):
    t, d = x1.shape
    per = seq // tt
    c = d // 2 // LANES
    row = lambda i: (i, 0)
    return pl.pallas_call(
        functools.partial(_combine_kernel, tt=tt),
        grid=(t // tt,),
        in_specs=[pl.BlockSpec(memory_space=pl.ANY),
                  pl.BlockSpec(memory_space=pl.ANY),
                  pl.BlockSpec((tt, LANES), row),
                  pl.BlockSpec((tt, d), row),
                  pl.BlockSpec((None, 8, d), lambda i: (i // per, 0, 0)),
                  pl.BlockSpec((1, d), lambda i: (0, 0))],
        out_specs=pl.BlockSpec((tt, d), row),
        out_shape=jax.ShapeDtypeStruct((t, d), F32),
        scratch_shapes=[pltpu.SMEM((2, tt * TOP_K), jnp.int32),
                        pltpu.VMEM((2, TOP_K, tt * c, LANES), jnp.int32),
                        pltpu.SemaphoreType.DMA((2,)),
                        pltpu.SemaphoreType.DMA((2,))],
        compiler_params=_params(("arbitrary",)),
        name="combine",
    )(idx, ys, route, x1, mod3, g)


def _pair_perm():
    half = ATT_Q_HEADS // 2
    idx = []
    for j in range(half):
        idx += list(range(j * ATT_HEAD_DIM, (j + 1) * ATT_HEAD_DIM))
        idx += list(range((half + j) * ATT_HEAD_DIM, (half + j + 1) * ATT_HEAD_DIM))
    return np.asarray(idx, np.int32)


def _layer(x, mod, g_pre_mix, g_post_mix, g_pre_ffn, g_post_ffn, w_in, w_gla_gate_up, b_gla_gate, g_gla_norm,
           sinks, w_out, w_router, b_router, w_mlp1, b_mlp1, w_mlp2, b_mlp2):
    nbatch, seq, d = x.shape
    t = nbatch * seq
    n_exp = w_router.shape[1]
    f = w_mlp2.shape[1]
    x2 = x.reshape(t, d)
    mod3 = jnp.pad(mod.reshape(nbatch, 6, d), ((0, 0), (0, 2), (0, 0)))

    perm = _pair_perm()
    n_main = C_GA - C_AKV
    w_r = jnp.concatenate([w_in[:, perm], w_in[:, ATT_WIDTH:ATT_WIDTH + n_main],
                           jnp.pad(w_in[:, ATT_WIDTH + n_main:], ((0, 0), (0, LANES - GLA_GATE_RANK)))],
                          axis=1).astype(BF16)
    wup = jnp.pad(w_gla_gate_up, ((0, LANES - GLA_GATE_RANK), (0, 0)))
    woa = w_out[:ATT_WIDTH][perm].astype(BF16)
    wog = w_out[ATT_WIDTH:].astype(BF16)
    wr = jnp.pad(w_router, ((0, 0), (0, LANES - n_exp)))
    br = jnp.pad(b_router, (0, LANES - n_exp)).reshape(1, LANES)
    w1g, w1l = _w1prep(w_mlp1)
    b1 = b_mlp1.reshape(n_exp, 1, f, 2)
    w2 = w_mlp2.astype(BF16)

    tm = min(512, seq)
    aq, akv, gq, gk, gv, gg, gl = _inproj(x2, mod3, g_pre_mix.reshape(1, d), w_r, wup,
                                          b_gla_gate.reshape(1, GLA_K_WIDTH), seq, tm)
    att = _att(aq, akv, sinks, nbatch, seq, min(256, seq))
    gla = _gla(gq, gk, gv, gg, gl, g_gla_norm.reshape(1, GLA_DV), nbatch, seq, min(256, seq))
    x1, h2p, route, cnt = _outproj(att, gla, x2, mod3, g_post_mix.reshape(1, d), g_pre_ffn.reshape(1, d),
                                   woa, wog, wr, br, seq, tm, n_exp)

    rows = MOE_ROWS
    counts = cnt[0, :n_exp].astype(jnp.int32)
    padded = (counts + rows - 1) // rows * rows
    pend = jnp.cumsum(padded)
    pstart = pend - padded
    top_i = route[:, 0:TOP_K].astype(jnp.int32)
    rank = route[:, 2 * TOP_K:3 * TOP_K].astype(jnp.int32)
    n_slots = t * TOP_K + n_exp * rows
    n_blocks = n_slots // rows
    n_used = (pend[-1] // rows).astype(jnp.int32).reshape(1)
    blk_ids = jnp.minimum(jnp.arange(n_blocks, dtype=jnp.int32), n_used - 1)
    block_e = jnp.minimum(jnp.sum(pend[None, :] <= (blk_ids * rows)[:, None], axis=1), n_exp - 1).astype(jnp.int32)
    tt = min(256, seq)
    sub = d // 2 // LANES
    idx = ((pstart[top_i] + rank) * sub).reshape(t // tt, 1, tt * TOP_K)

    xs = _dispatch(pend, n_used, idx, h2p, n_slots, rows, tt)
    ys = _moe(block_e, n_used, xs, w1g, w1l, b1[..., 0], b1[..., 1], w2, b_mlp2.reshape(n_exp, 1, d), rows)
    out = _combine(idx, ys, route, x1, mod3, g_post_ffn.reshape(1, d), seq, tt)
    return out.reshape(nbatch, seq, d)


def kernel(x, c, w_ada, b_ada, g_pre_mix, g_post_mix, g_pre_ffn, g_post_ffn, w_in, w_gla_gate_up, b_gla_gate,
           g_gla_norm, sinks, w_out, w_router, b_router, w_mlp1, b_mlp1, w_mlp2, b_mlp2):
    for l in range(w_in.shape[0]):
        mod = _ada(c, w_ada[l], b_ada[l])
        x = _layer(x, mod, g_pre_mix[l], g_post_mix[l], g_pre_ffn[l], g_post_ffn[l], w_in[l], w_gla_gate_up[l],
                   b_gla_gate[l], g_gla_norm[l], sinks[l], w_out[l], w_router[l], b_router[l], w_mlp1[l], b_mlp1[l],
                   w_mlp2[l], b_mlp2[l])
    return x
```

```python
import functools

import numpy as np
import jax
import jax.numpy as jnp
from jax import lax
from jax.experimental import pallas as pl
from jax.experimental.pallas import tpu as pltpu

F32 = jnp.float32
BF16 = jnp.bfloat16
HI = lax.Precision.HIGHEST

ATT_Q_HEADS = 8
ATT_KV_HEADS = 2
ATT_HEAD_DIM = 64
ATT_BLOCK = 128
GLA_HEADS = 4
GLA_DK = 64
GLA_DV = 128
GLA_GATE_RANK = 16
GLA_GATE_NORMALIZER = 16.0
GLA_CHUNK = 64
TOP_K = 4
SWIGLU_LIMIT = 7.0
SWIGLU_ALPHA = 1.702
NORM_EPS = 1e-6

LANES = 128
ATT_WIDTH = ATT_Q_HEADS * ATT_HEAD_DIM
ATT_KV_WIDTH = ATT_KV_HEADS * ATT_HEAD_DIM
GLA_K_WIDTH = GLA_HEADS * GLA_DK
GLA_V_WIDTH = GLA_HEADS * GLA_DV
C_AQ = 0
C_AKV = C_AQ + ATT_WIDTH
C_GQ = C_AKV + 2 * ATT_KV_WIDTH
C_GK = C_GQ + GLA_K_WIDTH
C_GV = C_GK + GLA_K_WIDTH
C_GG = C_GV + GLA_V_WIDTH
C_GA = C_GG + GLA_V_WIDTH
C_END = C_GA + LANES

MOE_ROWS = 512
VMEM_LIMIT = 48 * 1024 * 1024


def _dot(a, b, prec=None):
    return jnp.dot(a, b, preferred_element_type=F32, precision=prec)


def _dot_nt(a, b):
    return lax.dot_general(a, b, (((1,), (1,)), ((), ())), preferred_element_type=F32)


def _rms(t):
    return t * lax.rsqrt(jnp.mean(t * t, axis=-1, keepdims=True) + NORM_EPS)


def _params(sem):
    return pltpu.CompilerParams(dimension_semantics=sem, vmem_limit_bytes=VMEM_LIMIT)


HI16 = -65536


def _pack(a):
    half = a.shape[1] // 2
    lo = lax.bitcast_convert_type(a[:, :half].astype(BF16).astype(F32), jnp.int32)
    hi = lax.bitcast_convert_type(a[:, half:].astype(BF16).astype(F32), jnp.int32)
    return hi | lax.shift_right_logical(lo, 16)


def _unpack(p):
    lo = lax.bitcast_convert_type(lax.shift_left(p, 16), F32)
    hi = lax.bitcast_convert_type(p & HI16, F32)
    return lo, hi


def _store_rows(ref, p):
    n, w = p.shape
    c = w // LANES
    for s in range(c):
        ref[pl.ds(s, n, stride=c), :] = p[:, s * LANES:(s + 1) * LANES]


def _load_rows(ref, c):
    n = ref.shape[0] // c
    return jnp.concatenate([ref[pl.ds(s, n, stride=c), :] for s in range(c)], axis=1)


def _ada_kernel(c_ref, w_ref, b_ref, o_ref):
    c = c_ref[...]
    o_ref[...] = _dot(c * jax.nn.sigmoid(c), w_ref[...], HI) + b_ref[...]


def _ada(c, w, b):
    nb, d = c.shape
    n = w.shape[1]
    cp = jnp.zeros((8, d), F32).at[:nb].set(c)
    out = pl.pallas_call(
        _ada_kernel,
        grid=(n // d,),
        in_specs=[pl.BlockSpec((8, d), lambda j: (0, 0)),
                  pl.BlockSpec((d, d), lambda j: (0, j)),
                  pl.BlockSpec((1, d), lambda j: (0, j))],
        out_specs=pl.BlockSpec((8, d), lambda j: (0, j)),
        out_shape=jax.ShapeDtypeStruct((8, n), F32),
        compiler_params=_params(("parallel",)),
        name="ada",
    )(cp, w, b.reshape(1, n))
    return out[:nb]


def _inproj_kernel(x_ref, mod_ref, g_ref, w_ref, wup_ref, bup_ref,
                   aq_ref, akv_ref, gq_ref, gk_ref, gv_ref, gg_ref, gl_ref):
    h = _rms(x_ref[...]) * g_ref[...]
    h = h * (1.0 + mod_ref[1:2, :]) + mod_ref[0:1, :]
    p = _dot(h.astype(BF16), w_ref[...])
    aq_ref[...] = (p[:, C_AQ:C_AKV] * (ATT_HEAD_DIM ** -0.5)).astype(BF16)
    akv_ref[...] = p[:, C_AKV:C_GQ].astype(BF16)
    gq_ref[...] = (p[:, C_GQ:C_GK] * (GLA_DK ** -0.5)).astype(BF16)
    gk_ref[...] = p[:, C_GK:C_GV].astype(BF16)
    gv_ref[...] = p[:, C_GV:C_GG].astype(BF16)
    gg_ref[...] = p[:, C_GG:C_GA].astype(BF16)
    z = _dot(p[:, C_GA:C_END], wup_ref[...], HI) + bup_ref[...]
    gl_ref[...] = (jnp.minimum(z, 0.0) - jnp.log(1.0 + jnp.exp(-jnp.abs(z)))) * (1.0 / GLA_GATE_NORMALIZER)


def _inproj(x2, mod3, g, w_r, wup, bup, seq, tm):
    t, d = x2.shape
    per = seq // tm
    row = lambda i: (i, 0)
    fixed = lambda i: (0, 0)
    widths = (ATT_WIDTH, 2 * ATT_KV_WIDTH, GLA_K_WIDTH, GLA_K_WIDTH, GLA_V_WIDTH, GLA_V_WIDTH)
    out_shape = [jax.ShapeDtypeStruct((t, w), BF16) for w in widths] + [jax.ShapeDtypeStruct((t, GLA_K_WIDTH), F32)]
    out_specs = [pl.BlockSpec((tm, w), row) for w in widths] + [pl.BlockSpec((tm, GLA_K_WIDTH), row)]
    return pl.pallas_call(
        _inproj_kernel,
        grid=(t // tm,),
        in_specs=[pl.BlockSpec((tm, d), row),
                  pl.BlockSpec((None, 8, d), lambda i: (i // per, 0, 0)),
                  pl.BlockSpec((1, d), fixed),
                  pl.BlockSpec((d, C_END), fixed),
                  pl.BlockSpec((LANES, GLA_K_WIDTH), fixed),
                  pl.BlockSpec((1, GLA_K_WIDTH), fixed)],
        out_specs=out_specs,
        out_shape=out_shape,
        compiler_params=_params(("parallel",)),
        name="inproj",
    )(x2, mod3, g, w_r, wup, bup)


def _att_kernel(sinks_ref, q_ref, kv_ref, kvp_ref, o_ref, *, nblk):
    i = pl.program_id(1)
    blk = ATT_BLOCK
    lo = lax.broadcasted_iota(jnp.int32, (blk, LANES), 1) < ATT_HEAD_DIM
    qi = lax.broadcasted_iota(jnp.int32, (2 * blk, 2 * blk), 0) % blk
    kj = lax.broadcasted_iota(jnp.int32, (2 * blk, 2 * blk), 1)
    cur_ok = (kj >= blk) & ((kj - blk) <= qi)
    prev_ok = (kj < blk) & (kj > qi)
    first_off = jnp.where(i > 0, 0, blk)
    top = lax.broadcasted_iota(jnp.int32, (2 * blk, 1), 0) < blk
    for jb in range(nblk):
        r0 = jb * blk
        kvc = kv_ref[r0:r0 + blk, :]
        if jb == 0:
            kvp = kvp_ref[...]
            mask = cur_ok | (prev_ok & (kj >= first_off))
        else:
            kvp = kv_ref[r0 - blk:r0, :]
            mask = cur_ok | prev_ok
        kcat = jnp.concatenate([kvp[:, 0:ATT_KV_WIDTH], kvc[:, 0:ATT_KV_WIDTH]], axis=0)
        vcat = jnp.concatenate([kvp[:, ATT_KV_WIDTH:], kvc[:, ATT_KV_WIDTH:]], axis=0)
        for j in range(ATT_Q_HEADS // 2):
            qp = q_ref[r0:r0 + blk, j * LANES:(j + 1) * LANES]
            zero = jnp.zeros_like(qp)
            q2 = jnp.concatenate([jnp.where(lo, qp, zero), jnp.where(lo, zero, qp)], axis=0)
            s = jnp.where(mask, _dot_nt(q2, kcat), -jnp.inf)
            sink = jnp.where(top, sinks_ref[j], sinks_ref[ATT_Q_HEADS // 2 + j])
            m = jnp.maximum(jnp.max(s, axis=-1, keepdims=True), sink)
            p = jnp.exp(s - m)
            den = jnp.sum(p, axis=-1, keepdims=True) + jnp.exp(sink - m)
            o2 = _dot(p.astype(BF16), vcat) / den
            o = jnp.where(lo, o2[0:blk], o2[blk:2 * blk])
            o_ref[r0:r0 + blk, j * LANES:(j + 1) * LANES] = o.astype(BF16)


def _att(aq, akv, sinks, nbatch, seq, ta):
    t = aq.shape[0]
    nblk = ta // ATT_BLOCK
    per = seq // ta
    perb = seq // ATT_BLOCK
    return pl.pallas_call(
        functools.partial(_att_kernel, nblk=nblk),
        grid=(nbatch, per),
        in_specs=[pl.BlockSpec(memory_space=pltpu.SMEM),
                  pl.BlockSpec((ta, ATT_WIDTH), lambda b, i: (b * per + i, 0)),
                  pl.BlockSpec((ta, 2 * ATT_KV_WIDTH), lambda b, i: (b * per + i, 0)),
                  pl.BlockSpec((ATT_BLOCK, 2 * ATT_KV_WIDTH),
                               lambda b, i: (b * perb + jnp.maximum(i * nblk - 1, 0), 0))],
        out_specs=pl.BlockSpec((ta, ATT_WIDTH), lambda b, i: (b * per + i, 0)),
        out_shape=jax.ShapeDtypeStruct((t, ATT_WIDTH), BF16),
        compiler_params=_params(("parallel", "parallel")),
        name="att",
    )(sinks, aq, akv, akv)


def _gla_kernel(gq_ref, gk_ref, gv_ref, gg_ref, gl_ref, gn_ref, o_ref, st_ref, *, nchunk):
    ch = GLA_CHUNK
    tg = nchunk * ch
    nbatch = gq_ref.shape[0]

    @pl.when(pl.program_id(0) == 0)
    def _():
        st_ref[...] = jnp.zeros_like(st_ref)

    ri = lax.broadcasted_iota(jnp.int32, (tg, tg), 0)
    ci = lax.broadcasted_iota(jnp.int32, (tg, tg), 1)
    tri = ((ri // ch == ci // ch) & (ci <= ri)).astype(F32)
    bts = [_dot(tri, gl_ref[s], HI) for s in range(nbatch)]
    lo = lax.broadcasted_iota(jnp.int32, (ch, LANES), 1) < GLA_DK
    causal = lax.broadcasted_iota(jnp.int32, (ch, ch), 0) >= lax.broadcasted_iota(jnp.int32, (ch, ch), 1)
    eye = (lax.broadcasted_iota(jnp.int32, (LANES, LANES), 0)
           == lax.broadcasted_iota(jnp.int32, (LANES, LANES), 1)).astype(BF16)
    gn = gn_ref[...]
    zero = jnp.zeros((ch, LANES), BF16)
    for c in range(nchunk):
        r0 = c * ch
        for s in range(nbatch):
            b = bts[s][r0:r0 + ch]
            bm = b[ch // 2 - 1:ch // 2]
            bl = b[ch - 1:ch]
            q = gq_ref[s, r0:r0 + ch, :].astype(F32)
            k = gk_ref[s, r0:r0 + ch, :].astype(F32)
            qe = (q * jnp.exp(b - bm)).astype(BF16)
            ke = (k * jnp.exp(bm - b)).astype(BF16)
            qs = (q * jnp.exp(b)).astype(BF16)
            kl = (k * jnp.exp(bl - b)).astype(BF16)
            dec = jnp.exp(bl)
            for p in range(GLA_HEADS // 2):
                sl = slice(p * LANES, (p + 1) * LANES)
                st = st_ref[s, p]
                stb = st.astype(BF16)
                upd = None
                for hh in range(2):
                    h = 2 * p + hh
                    hs = slice(h * GLA_DV, (h + 1) * GLA_DV)
                    msk = lo if hh == 0 else jnp.logical_not(lo)
                    a = _dot_nt(jnp.where(msk, qe[:, sl], zero), ke[:, sl])
                    a = jnp.where(causal, a, 0.0).astype(BF16)
                    v = gv_ref[s, r0:r0 + ch, hs]
                    o = _dot(a, v) + _dot_nt(jnp.where(msk, qs[:, sl], zero), stb)
                    vt = _dot_nt(eye, v).astype(BF16)
                    u = _dot(vt, jnp.where(msk, kl[:, sl], zero))
                    upd = u if upd is None else upd + u
                    gg = gg_ref[s, r0:r0 + ch, hs].astype(F32)
                    o_ref[s, r0:r0 + ch, hs] = (_rms(o) * gn * (gg * jax.nn.sigmoid(gg))).astype(BF16)
                st_ref[s, p] = st * dec[:, sl] + upd


def _gla(gq, gk, gv, gg, gl, gn, nbatch, seq, tg):
    t = gq.shape[0]
    tile = lambda w: pl.BlockSpec((nbatch, tg, w), lambda i: (0, i, 0))
    as3d = lambda a: a.reshape(nbatch, seq, a.shape[1])
    out = pl.pallas_call(
        functools.partial(_gla_kernel, nchunk=tg // GLA_CHUNK),
        grid=(seq // tg,),
        in_specs=[tile(GLA_K_WIDTH), tile(GLA_K_WIDTH), tile(GLA_V_WIDTH), tile(GLA_V_WIDTH), tile(GLA_K_WIDTH),
                  pl.BlockSpec((1, GLA_DV), lambda i: (0, 0))],
        out_specs=tile(GLA_V_WIDTH),
        out_shape=jax.ShapeDtypeStruct((nbatch, seq, GLA_V_WIDTH), BF16),
        scratch_shapes=[pltpu.VMEM((nbatch, GLA_HEADS // 2, GLA_DV, LANES), F32)],
        compiler_params=_params(("arbitrary",)),
        name="gla",
    )(as3d(gq), as3d(gk), as3d(gv), as3d(gg), as3d(gl), gn)
    return out.reshape(t, GLA_V_WIDTH)


def _outproj_kernel(att_ref, gla_ref, x_ref, mod_ref, gpost_ref, gpre_ref, woa_ref, wog_ref, wr_ref, br_ref,
                    x1_ref, h2_ref, route_ref, cnt_ref, base_ref, *, n_exp):
    tm = x_ref.shape[0]

    @pl.when(pl.program_id(0) == 0)
    def _():
        base_ref[...] = jnp.zeros_like(base_ref)

    y = _dot(att_ref[...], woa_ref[...]) + _dot(gla_ref[...], wog_ref[...])
    x1 = x_ref[...] + mod_ref[2:3, :] * (_rms(y) * gpost_ref[...])
    x1_ref[...] = x1
    h2 = _rms(x1) * gpre_ref[...]
    h2 = h2 * (1.0 + mod_ref[4:5, :]) + mod_ref[3:4, :]
    h2_hi = h2.astype(BF16)
    _store_rows(h2_ref, _pack(h2_hi))

    h2_lo = (h2 - h2_hi.astype(F32)).astype(BF16)
    p_hi = _dot(h2_hi, wr_ref[...])
    p_lo = _dot(h2_lo, wr_ref[...])
    logits = ((p_lo[:, LANES:] + p_lo[:, :LANES]) + p_hi[:, LANES:]) + p_hi[:, :LANES] + br_ref[...]
    lane = lax.broadcasted_iota(jnp.int32, (tm, LANES), 1)
    vals = jnp.where(lane < n_exp, logits, -jnp.inf)
    sels, tops, idxs = [], [], []
    for _ in range(TOP_K):
        m = jnp.max(vals, axis=-1, keepdims=True)
        idx = jnp.min(jnp.where(vals == m, lane, LANES), axis=-1, keepdims=True)
        sel = lane == idx
        vals = jnp.where(sel, -jnp.inf, vals)
        sels.append(sel)
        tops.append(m)
        idxs.append(idx)
    es = [jnp.exp(m - tops[0]) for m in tops]
    tot = es[0] + es[1] + es[2] + es[3]
    onehot = jnp.zeros((tm, LANES), F32)
    for sel in sels:
        onehot = onehot + sel.astype(F32)
    below = (lax.broadcasted_iota(jnp.int32, (tm, tm), 0) > lax.broadcasted_iota(jnp.int32, (tm, tm), 1)).astype(BF16)
    csum = _dot(below, onehot.astype(BF16)) + base_ref[...]
    route = jnp.zeros((tm, LANES), F32)
    for k in range(TOP_K):
        rank = jnp.sum(jnp.where(sels[k], csum, 0.0), axis=-1, keepdims=True)
        route = jnp.where(lane == k, idxs[k].astype(F32), route)
        route = jnp.where(lane == TOP_K + k, es[k] / tot, route)
        route = jnp.where(lane == 2 * TOP_K + k, rank, route)
    route_ref[...] = route
    base = base_ref[...] + jnp.sum(onehot, axis=0, keepdims=True)
    base_ref[...] = base
    cnt_ref[...] = jnp.broadcast_to(base, cnt_ref.shape)


def _outproj(att, gla, x2, mod3, gpost, gpre, woa, wog, wr, br, seq, tm, n_exp):
    t, d = x2.shape
    per = seq // tm
    row = lambda i: (i, 0)
    fixed = lambda i: (0, 0)
    return pl.pallas_call(
        functools.partial(_outproj_kernel, n_exp=n_exp),
        grid=(t // tm,),
        in_specs=[pl.BlockSpec((tm, ATT_WIDTH), row),
                  pl.BlockSpec((tm, GLA_V_WIDTH), row),
                  pl.BlockSpec((tm, d), row),
                  pl.BlockSpec((None, 8, d), lambda i: (i // per, 0, 0)),
                  pl.BlockSpec((1, d), fixed),
                  pl.BlockSpec((1, d), fixed),
                  pl.BlockSpec((ATT_WIDTH, d), fixed),
                  pl.BlockSpec((GLA_V_WIDTH, d), fixed),
                  pl.BlockSpec((d, 2 * LANES), fixed),
                  pl.BlockSpec((1, LANES), fixed)],
        out_specs=[pl.BlockSpec((tm, d), row),
                   pl.BlockSpec((tm * (d // 2 // LANES), LANES), row),
                   pl.BlockSpec((tm, LANES), row),
                   pl.BlockSpec((8, LANES), fixed)],
        out_shape=[jax.ShapeDtypeStruct((t, d), F32),
                   jax.ShapeDtypeStruct((t * (d // 2 // LANES), LANES), jnp.int32),
                   jax.ShapeDtypeStruct((t, LANES), F32),
                   jax.ShapeDtypeStruct((8, LANES), F32)],
        scratch_shapes=[pltpu.VMEM((1, LANES), F32)],
        compiler_params=_params(("arbitrary",)),
        name="outproj",
    )(att, gla, x2, mod3, gpost, gpre, woa, wog, wr, br)


def _w1prep_kernel(w_ref, g_ref, l_ref):
    n = 2 * LANES
    r = lax.broadcasted_iota(jnp.int32, (n, n), 0)
    c = lax.broadcasted_iota(jnp.int32, (n, n), 1)
    perm = (r == jnp.where(c < LANES, 2 * c, 2 * (c - LANES) + 1)).astype(BF16)
    for j in range(g_ref.shape[1] // LANES):
        d = _dot(w_ref[:, j * n:(j + 1) * n].astype(BF16), perm)
        g_ref[:, j * LANES:(j + 1) * LANES] = d[:, :LANES].astype(BF16)
        l_ref[:, j * LANES:(j + 1) * LANES] = d[:, LANES:].astype(BF16)


def _w1prep(w1):
    n_exp, d, f2 = w1.shape
    f = f2 // 2
    out = jax.ShapeDtypeStruct((n_exp, d, f), BF16)
    return pl.pallas_call(
        _w1prep_kernel,
        grid=(n_exp,),
        in_specs=[pl.BlockSpec((None, d, f2), lambda e: (e, 0, 0))],
        out_specs=[pl.BlockSpec((None, d, f), lambda e: (e, 0, 0))] * 2,
        out_shape=[out, out],
        compiler_params=_params(("parallel",)),
        name="w1prep",
    )(w1)


def _dispatch_kernel(pend_ref, nu_ref, idx_hbm, h2_ref, xs_hbm, idx_sm, zbuf, sem_i, sem_r, sem_z,
                     *, tt, rows, n_exp, n_blocks):
    i = pl.program_id(0)
    nsteps = pl.num_programs(0)
    c = h2_ref.shape[0] // tt
    blk_rows = rows * c

    def idx_copy(j):
        slot = j & 1
        return pltpu.make_async_copy(idx_hbm.at[j], idx_sm.at[pl.ds(slot, 1)], sem_i.at[slot])

    def zero_fill(start):
        return pltpu.make_async_copy(zbuf, xs_hbm.at[pl.ds(pl.multiple_of(start, blk_rows), blk_rows), :], sem_z)

    @pl.when(i == 0)
    def _():
        idx_copy(0).start()
        zbuf[...] = jnp.zeros_like(zbuf)
        for phase in range(2):
            for e in range(n_exp):
                hi = pend_ref[e]
                lo = pend_ref[e - 1] if e > 0 else 0
                tail = nu_ref[0] + e
                for pred, start in ((hi > lo, (hi - rows) * c), (tail < n_blocks, tail * blk_rows)):
                    @pl.when(pred)
                    def _():
                        if phase == 0:
                            zero_fill(start).start()
                        else:
                            zero_fill(start).wait()

    idx_copy(i).wait()

    @pl.when(i + 1 < nsteps)
    def _():
        idx_copy(i + 1).start()

    slot = i & 1
    for r in range(tt):
        for k in range(TOP_K):
            n = r * TOP_K + k
            dst = pl.multiple_of(idx_sm[slot, n], c)
            pltpu.make_async_copy(h2_ref.at[pl.ds(r * c, c), :], xs_hbm.at[pl.ds(dst, c), :],
                                  sem_r).start(priority=n % 2)
    for k in range(TOP_K):
        pltpu.make_async_copy(h2_ref, xs_hbm.at[pl.ds(0, tt * c), :], sem_r).wait()


def _dispatch(pend, n_used, idx, h2p, n_slots, rows, tt):
    c = h2p.shape[0] * TOP_K // idx.size
    n_exp = pend.shape[0]
    grid_spec = pltpu.PrefetchScalarGridSpec(
        num_scalar_prefetch=2,
        grid=(idx.shape[0],),
        in_specs=[pl.BlockSpec(memory_space=pl.ANY),
                  pl.BlockSpec((tt * c, LANES), lambda i, pe, nu: (i, 0))],
        out_specs=pl.BlockSpec(memory_space=pl.ANY),
        scratch_shapes=[pltpu.SMEM((2, tt * TOP_K), jnp.int32),
                        pltpu.VMEM((rows * c, LANES), jnp.int32),
                        pltpu.SemaphoreType.DMA((2,)),
                        pltpu.SemaphoreType.DMA,
                        pltpu.SemaphoreType.DMA],
    )
    return pl.pallas_call(
        functools.partial(_dispatch_kernel, tt=tt, rows=rows, n_exp=n_exp, n_blocks=n_slots // rows),
        grid_spec=grid_spec,
        out_shape=jax.ShapeDtypeStruct((n_slots * c, LANES), jnp.int32),
        compiler_params=_params(("arbitrary",)),
        name="dispatch",
    )(pend, n_used, idx, h2p)


def _moe_kernel(be_ref, nu_ref, xs_ref, w1g_ref, w1l_ref, b1g_ref, b1l_ref, w2_ref, b2_ref, ys_ref, *, rows):
    del be_ref
    c = xs_ref.shape[0] // rows
    half = c * LANES

    @pl.when(pl.program_id(0) < nu_ref[0])
    def _():
        lo, hi = _unpack(_load_rows(xs_ref, c))
        xl = lo.astype(BF16)
        xh = hi.astype(BF16)
        glu = _dot(xl, w1g_ref[:half, :]) + _dot(xh, w1g_ref[half:, :]) + b1g_ref[...]
        lin = _dot(xl, w1l_ref[:half, :]) + _dot(xh, w1l_ref[half:, :]) + b1l_ref[...]
        glu = jnp.minimum(glu, SWIGLU_LIMIT)
        lin = jnp.clip(lin, -SWIGLU_LIMIT, SWIGLU_LIMIT)
        a = glu * jax.nn.sigmoid(SWIGLU_ALPHA * glu) * (lin + 1.0)
        _store_rows(ys_ref, _pack(_dot(a.astype(BF16), w2_ref[...]) + b2_ref[...]))

    @pl.when(pl.program_id(0) >= nu_ref[0])
    def _():
        ys_ref[...] = jnp.zeros_like(ys_ref)


def _moe(block_e, n_used, xs, w1g, w1l, b1g, b1l, w2, b2, rows):
    d, f = w1g.shape[1], w1g.shape[2]
    c = d // 2 // LANES
    nb = xs.shape[0] // (rows * c)
    wsel = lambda i, be, nu: (be[i], 0, 0)
    grid_spec = pltpu.PrefetchScalarGridSpec(
        num_scalar_prefetch=2,
        grid=(nb,),
        in_specs=[pl.BlockSpec((rows * c, LANES), lambda i, be, nu: (jnp.minimum(i, nu[0] - 1), 0)),
                  pl.BlockSpec((None, d, f), wsel),
                  pl.BlockSpec((None, d, f), wsel),
                  pl.BlockSpec((None, 1, f), wsel),
                  pl.BlockSpec((None, 1, f), wsel),
                  pl.BlockSpec((None, f, d), wsel),
                  pl.BlockSpec((None, 1, d), wsel)],
        out_specs=pl.BlockSpec((rows * c, LANES), lambda i, be, nu: (i, 0)),
    )
    return pl.pallas_call(
        functools.partial(_moe_kernel, rows=rows),
        grid_spec=grid_spec,
        out_shape=jax.ShapeDtypeStruct(xs.shape, jnp.int32),
        compiler_params=_params(("arbitrary",)),
        name="moe",
    )(block_e, n_used, xs, w1g, w1l, b1g, b1l, w2, b2)


def _combine_kernel(idx_hbm, ys_hbm, route_ref, x1_ref, mod_ref, g_ref, o_ref, idx_sm, buf, sem_i, sem_g, *, tt):
    i = pl.program_id(0)
    nsteps = pl.num_programs(0)
    d = o_ref.shape[1]
    half = d // 2
    c = half // LANES

    def idx_copy(j):
        slot = j & 1
        return pltpu.make_async_copy(idx_hbm.at[j], idx_sm.at[pl.ds(slot, 1)], sem_i.at[slot])

    def issue(j):
        slot = j & 1
        for r in range(tt):
            for k in range(TOP_K):
                n = r * TOP_K + k
                src = pl.multiple_of(idx_sm[slot, n], c)
                pltpu.make_async_copy(ys_hbm.at[pl.ds(src, c), :], buf.at[slot, k, pl.ds(r * c, c), :],
                                      sem_g.at[slot]).start(priority=n % 2)

    @pl.when(i == 0)
    def _():
        first = idx_copy(0)
        first.start()
        first.wait()
        issue(0)

        @pl.when(nsteps > 1)
        def _():
            idx_copy(1).start()

    @pl.when(i + 1 < nsteps)
    def _():
        idx_copy(i + 1).wait()
        issue(i + 1)

    @pl.when(i + 2 < nsteps)
    def _():
        idx_copy(i + 2).start()

    slot = i & 1
    for k in range(TOP_K):
        pltpu.make_async_copy(ys_hbm.at[pl.ds(0, tt * c), :], buf.at[slot, k], sem_g.at[slot]).wait()

    acc_lo = jnp.zeros((tt, half), F32)
    acc_hi = jnp.zeros((tt, half), F32)
    for k in range(TOP_K):
        lo, hi = _unpack(_load_rows(buf.at[slot, k], c))
        gate = route_ref[:, TOP_K + k:TOP_K + k + 1]
        acc_lo = acc_lo + gate * lo
        acc_hi = acc_hi + gate * hi
    ms = (jnp.sum(acc_lo * acc_lo, axis=-1, keepdims=True) + jnp.sum(acc_hi * acc_hi, axis=-1, keepdims=True)) / d
    inv = lax.rsqrt(ms + NORM_EPS)
    o_ref[:, :half] = x1_ref[:, :half] + mod_ref[5:6, :half] * (acc_lo * inv * g_ref[:, :half])
    o_ref[:, half:] = x1_ref[:, half:] + mod_ref[5:6, half:] * (acc_hi * inv * g_ref[:, half:])


def _combine(idx, ys, route, x1, mod3, g, seq, tt):
    t, d = x1.shape
    per = seq // tt
    c = d // 2 // LANES
    row = lambda i: (i, 0)
    return pl.pallas_call(
        functools.partial(_combine_kernel, tt=tt),
        grid=(t // tt,),
        in_specs=[pl.BlockSpec(memory_space=pl.ANY),
                  pl.BlockSpec(memory_space=pl.ANY),
                  pl.BlockSpec((tt, LANES), row),
                  pl.BlockSpec((tt, d), row),
                  pl.BlockSpec((None, 8, d), lambda i: (i // per, 0, 0)),
                  pl.BlockSpec((1, d), lambda i: (0, 0))],
        out_specs=pl.BlockSpec((tt, d), row),
        out_shape=jax.ShapeDtypeStruct((t, d), F32),
        scratch_shapes=[pltpu.SMEM((2, tt * TOP_K), jnp.int32),
                        pltpu.VMEM((2, TOP_K, tt * c, LANES), jnp.int32),
                        pltpu.SemaphoreType.DMA((2,)),
                        pltpu.SemaphoreType.DMA((2,))],
        compiler_params=_params(("arbitrary",)),
        name="combine",
    )(idx, ys, route, x1, mod3, g)


def _pair_perm():
    half = ATT_Q_HEADS // 2
    idx = []
    for j in range(half):
        idx += list(range(j * ATT_HEAD_DIM, (j + 1) * ATT_HEAD_DIM))
        idx += list(range((half + j) * ATT_HEAD_DIM, (half + j + 1) * ATT_HEAD_DIM))
    return np.asarray(idx, np.int32)


def _layer(x, mod, g_pre_mix, g_post_mix, g_pre_ffn, g_post_ffn, w_in, w_gla_gate_up, b_gla_gate, g_gla_norm,
           sinks, w_out, w_router, b_router, w_mlp1, b_mlp1, w_mlp2, b_mlp2):
    nbatch, seq, d = x.shape
    t = nbatch * seq
    n_exp = w_router.shape[1]
    f = w_mlp2.shape[1]
    x2 = x.reshape(t, d)
    mod3 = jnp.pad(mod.reshape(nbatch, 6, d), ((0, 0), (0, 2), (0, 0)))

    perm = _pair_perm()
    n_main = C_GA - C_AKV
    w_r = jnp.concatenate([w_in[:, perm], w_in[:, ATT_WIDTH:ATT_WIDTH + n_main],
                           jnp.pad(w_in[:, ATT_WIDTH + n_main:], ((0, 0), (0, LANES - GLA_GATE_RANK)))],
                          axis=1).astype(BF16)
    wup = jnp.pad(w_gla_gate_up, ((0, LANES - GLA_GATE_RANK), (0, 0)))
    woa = w_out[:ATT_WIDTH][perm].astype(BF16)
    wog = w_out[ATT_WIDTH:].astype(BF16)
    wr = jnp.pad(w_router, ((0, 0), (0, LANES - n_exp)))
    wr_hi = wr.astype(BF16)
    wr = jnp.concatenate([wr_hi, (wr - wr_hi.astype(F32)).astype(BF16)], axis=1)
    br = jnp.pad(b_router, (0, LANES - n_exp)).reshape(1, LANES)
    w1g, w1l = _w1prep(w_mlp1)
    b1 = b_mlp1.reshape(n_exp, 1, f, 2)
    w2 = w_mlp2.astype(BF16)

    tm = min(512, seq)
    aq, akv, gq, gk, gv, gg, gl = _inproj(x2, mod3, g_pre_mix.reshape(1, d), w_r, wup,
                                          b_gla_gate.reshape(1, GLA_K_WIDTH), seq, tm)
    att = _att(aq, akv, sinks, nbatch, seq, min(256, seq))
    gla = _gla(gq, gk, gv, gg, gl, g_gla_norm.reshape(1, GLA_DV), nbatch, seq, min(256, seq))
    x1, h2p, route, cnt = _outproj(att, gla, x2, mod3, g_post_mix.reshape(1, d), g_pre_ffn.reshape(1, d),
                                   woa, wog, wr, br, seq, tm, n_exp)

    rows = MOE_ROWS
    counts = cnt[0, :n_exp].astype(jnp.int32)
    padded = (counts + rows - 1) // rows * rows
    pend = jnp.cumsum(padded)
    pstart = pend - padded
    top_i = route[:, 0:TOP_K].astype(jnp.int32)
    rank = route[:, 2 * TOP_K:3 * TOP_K].astype(jnp.int32)
    n_slots = t * TOP_K + n_exp * rows
    n_blocks = n_slots // rows
    n_used = (pend[-1] // rows).astype(jnp.int32).reshape(1)
    blk_ids = jnp.minimum(jnp.arange(n_blocks, dtype=jnp.int32), n_used - 1)
    block_e = jnp.minimum(jnp.sum(pend[None, :] <= (blk_ids * rows)[:, None], axis=1), n_exp - 1).astype(jnp.int32)
    tt = min(256, seq)
    sub = d // 2 // LANES
    idx = ((pstart[top_i] + rank) * sub).reshape(t // tt, 1, tt * TOP_K)

    xs = _dispatch(pend, n_used, idx, h2p, n_slots, rows, tt)
    ys = _moe(block_e, n_used, xs, w1g, w1l, b1[..., 0], b1[..., 1], w2, b_mlp2.reshape(n_exp, 1, d), rows)
    out = _combine(idx, ys, route, x1, mod3, g_post_ffn.reshape(1, d), seq, tt)
    return out.reshape(nbatch, seq, d)


def kernel(x, c, w_ada, b_ada, g_pre_mix, g_post_mix, g_pre_ffn, g_post_ffn, w_in, w_gla_gate_up, b_gla_gate,
           g_gla_norm, sinks, w_out, w_router, b_router, w_mlp1, b_mlp1, w_mlp2, b_mlp2):
    for l in range(w_in.shape[0]):
        mod = _ada(c, w_ada[l], b_ada[l])
        x = _layer(x, mod, g_pre_mix[l], g_post_mix[l], g_pre_ffn[l], g_post_ffn[l], w_in[l], w_gla_gate_up[l],
                   b_gla_gate[l], g_gla_norm[l], sinks[l], w_out[l], w_router[l], b_router[l], w_mlp1[l], b_mlp1[l],
                   w_mlp2[l], b_mlp2[l])
    return x
```

```python
import functools

import numpy as np
import jax
import jax.numpy as jnp
from jax import lax
from jax.experimental import pallas as pl
from jax.experimental.pallas import tpu as pltpu

F32 = jnp.float32
BF16 = jnp.bfloat16
HI = lax.Precision.HIGHEST

ATT_Q_HEADS = 8
ATT_KV_HEADS = 2
ATT_HEAD_DIM = 64
ATT_BLOCK = 128
GLA_HEADS = 4
GLA_DK = 64
GLA_DV = 128
GLA_GATE_RANK = 16
GLA_GATE_NORMALIZER = 16.0
GLA_CHUNK = 64
TOP_K = 4
SWIGLU_LIMIT = 7.0
SWIGLU_ALPHA = 1.702
NORM_EPS = 1e-6

LANES = 128
ATT_WIDTH = ATT_Q_HEADS * ATT_HEAD_DIM
ATT_KV_WIDTH = ATT_KV_HEADS * ATT_HEAD_DIM
GLA_K_WIDTH = GLA_HEADS * GLA_DK
GLA_V_WIDTH = GLA_HEADS * GLA_DV
C_AQ = 0
C_AKV = C_AQ + ATT_WIDTH
C_GQ = C_AKV + 2 * ATT_KV_WIDTH
C_GK = C_GQ + GLA_K_WIDTH
C_GV = C_GK + GLA_K_WIDTH
C_GG = C_GV + GLA_V_WIDTH
C_GA = C_GG + GLA_V_WIDTH
C_END = C_GA + LANES

MOE_ROWS = 512
VMEM_LIMIT = 48 * 1024 * 1024


def _dot(a, b, prec=None):
    return jnp.dot(a, b, preferred_element_type=F32, precision=prec)


def _dot_nt(a, b):
    return lax.dot_general(a, b, (((1,), (1,)), ((), ())), preferred_element_type=F32)


def _rms(t):
    return t * lax.rsqrt(jnp.mean(t * t, axis=-1, keepdims=True) + NORM_EPS)


def _params(sem):
    return pltpu.CompilerParams(dimension_semantics=sem, vmem_limit_bytes=VMEM_LIMIT)


HI16 = -65536


def _pack(a):
    half = a.shape[1] // 2
    lo = lax.bitcast_convert_type(a[:, :half].astype(BF16).astype(F32), jnp.int32)
    hi = lax.bitcast_convert_type(a[:, half:].astype(BF16).astype(F32), jnp.int32)
    return hi | lax.shift_right_logical(lo, 16)


def _unpack(p):
    lo = lax.bitcast_convert_type(lax.shift_left(p, 16), F32)
    hi = lax.bitcast_convert_type(p & HI16, F32)
    return lo, hi


def _store_rows(ref, p):
    n, w = p.shape
    c = w // LANES
    for s in range(c):
        ref[pl.ds(s, n, stride=c), :] = p[:, s * LANES:(s + 1) * LANES]


def _load_rows(ref, c):
    n = ref.shape[0] // c
    return jnp.concatenate([ref[pl.ds(s, n, stride=c), :] for s in range(c)], axis=1)


def _ada_kernel(c_ref, w_ref, b_ref, o_ref):
    c = c_ref[...]
    o_ref[...] = _dot(c * jax.nn.sigmoid(c), w_ref[...], HI) + b_ref[...]


def _ada(c, w, b):
    nb, d = c.shape
    n = w.shape[1]
    cp = jnp.zeros((8, d), F32).at[:nb].set(c)
    out = pl.pallas_call(
        _ada_kernel,
        grid=(n // d,),
        in_specs=[pl.BlockSpec((8, d), lambda j: (0, 0)),
                  pl.BlockSpec((d, d), lambda j: (0, j)),
                  pl.BlockSpec((1, d), lambda j: (0, j))],
        out_specs=pl.BlockSpec((8, d), lambda j: (0, j)),
        out_shape=jax.ShapeDtypeStruct((8, n), F32),
        compiler_params=_params(("parallel",)),
        name="ada",
    )(cp, w, b.reshape(1, n))
    return out[:nb]


def _inproj_kernel(x_ref, mod_ref, g_ref, w_ref, wup_ref, bup_ref,
                   aq_ref, akv_ref, gq_ref, gk_ref, gv_ref, gg_ref, gl_ref):
    h = _rms(x_ref[...]) * g_ref[...]
    h = h * (1.0 + mod_ref[1:2, :]) + mod_ref[0:1, :]
    p = _dot(h.astype(BF16), w_ref[...])
    aq_ref[...] = (p[:, C_AQ:C_AKV] * (ATT_HEAD_DIM ** -0.5)).astype(BF16)
    akv_ref[...] = p[:, C_AKV:C_GQ].astype(BF16)
    gq_ref[...] = (p[:, C_GQ:C_GK] * (GLA_DK ** -0.5)).astype(BF16)
    gk_ref[...] = p[:, C_GK:C_GV].astype(BF16)
    gv_ref[...] = p[:, C_GV:C_GG].astype(BF16)
    gg_ref[...] = p[:, C_GG:C_GA].astype(BF16)
    z = _dot(p[:, C_GA:C_END], wup_ref[...], HI) + bup_ref[...]
    gl_ref[...] = (jnp.minimum(z, 0.0) - jnp.log(1.0 + jnp.exp(-jnp.abs(z)))) * (1.0 / GLA_GATE_NORMALIZER)


def _inproj(x2, mod3, g, w_r, wup, bup, seq, tm):
    t, d = x2.shape
    per = seq // tm
    row = lambda i: (i, 0)
    fixed = lambda i: (0, 0)
    widths = (ATT_WIDTH, 2 * ATT_KV_WIDTH, GLA_K_WIDTH, GLA_K_WIDTH, GLA_V_WIDTH, GLA_V_WIDTH)
    out_shape = [jax.ShapeDtypeStruct((t, w), BF16) for w in widths] + [jax.ShapeDtypeStruct((t, GLA_K_WIDTH), F32)]
    out_specs = [pl.BlockSpec((tm, w), row) for w in widths] + [pl.BlockSpec((tm, GLA_K_WIDTH), row)]
    return pl.pallas_call(
        _inproj_kernel,
        grid=(t // tm,),
        in_specs=[pl.BlockSpec((tm, d), row),
                  pl.BlockSpec((None, 8, d), lambda i: (i // per, 0, 0)),
                  pl.BlockSpec((1, d), fixed),
                  pl.BlockSpec((d, C_END), fixed),
                  pl.BlockSpec((LANES, GLA_K_WIDTH), fixed),
                  pl.BlockSpec((1, GLA_K_WIDTH), fixed)],
        out_specs=out_specs,
        out_shape=out_shape,
        compiler_params=_params(("parallel",)),
        name="inproj",
    )(x2, mod3, g, w_r, wup, bup)


def _att_kernel(sinks_ref, q_ref, kv_ref, kvp_ref, o_ref, *, nblk):
    i = pl.program_id(1)
    blk = ATT_BLOCK
    lo = lax.broadcasted_iota(jnp.int32, (blk, LANES), 1) < ATT_HEAD_DIM
    qi = lax.broadcasted_iota(jnp.int32, (2 * blk, 2 * blk), 0) % blk
    kj = lax.broadcasted_iota(jnp.int32, (2 * blk, 2 * blk), 1)
    cur_ok = (kj >= blk) & ((kj - blk) <= qi)
    prev_ok = (kj < blk) & (kj > qi)
    first_off = jnp.where(i > 0, 0, blk)
    top = lax.broadcasted_iota(jnp.int32, (2 * blk, 1), 0) < blk
    for jb in range(nblk):
        r0 = jb * blk
        kvc = kv_ref[r0:r0 + blk, :]
        if jb == 0:
            kvp = kvp_ref[...]
            mask = cur_ok | (prev_ok & (kj >= first_off))
        else:
            kvp = kv_ref[r0 - blk:r0, :]
            mask = cur_ok | prev_ok
        kcat = jnp.concatenate([kvp[:, 0:ATT_KV_WIDTH], kvc[:, 0:ATT_KV_WIDTH]], axis=0)
        vcat = jnp.concatenate([kvp[:, ATT_KV_WIDTH:], kvc[:, ATT_KV_WIDTH:]], axis=0)
        for j in range(ATT_Q_HEADS // 2):
            qp = q_ref[r0:r0 + blk, j * LANES:(j + 1) * LANES]
            zero = jnp.zeros_like(qp)
            q2 = jnp.concatenate([jnp.where(lo, qp, zero), jnp.where(lo, zero, qp)], axis=0)
            s = jnp.where(mask, _dot_nt(q2, kcat), -jnp.inf)
            sink = jnp.where(top, sinks_ref[j], sinks_ref[ATT_Q_HEADS // 2 + j])
            m = jnp.maximum(jnp.max(s, axis=-1, keepdims=True), sink)
            p = jnp.exp(s - m)
            den = jnp.sum(p, axis=-1, keepdims=True) + jnp.exp(sink - m)
            o2 = _dot(p.astype(BF16), vcat) / den
            o = jnp.where(lo, o2[0:blk], o2[blk:2 * blk])
            o_ref[r0:r0 + blk, j * LANES:(j + 1) * LANES] = o.astype(BF16)


def _att(aq, akv, sinks, nbatch, seq, ta):
    t = aq.shape[0]
    nblk = ta // ATT_BLOCK
    per = seq // ta
    perb = seq // ATT_BLOCK
    return pl.pallas_call(
        functools.partial(_att_kernel, nblk=nblk),
        grid=(nbatch, per),
        in_specs=[pl.BlockSpec(memory_space=pltpu.SMEM),
                  pl.BlockSpec((ta, ATT_WIDTH), lambda b, i: (b * per + i, 0)),
                  pl.BlockSpec((ta, 2 * ATT_KV_WIDTH), lambda b, i: (b * per + i, 0)),
                  pl.BlockSpec((ATT_BLOCK, 2 * ATT_KV_WIDTH),
                               lambda b, i: (b * perb + jnp.maximum(i * nblk - 1, 0), 0))],
        out_specs=pl.BlockSpec((ta, ATT_WIDTH), lambda b, i: (b * per + i, 0)),
        out_shape=jax.ShapeDtypeStruct((t, ATT_WIDTH), BF16),
        compiler_params=_params(("parallel", "parallel")),
        name="att",
    )(sinks, aq, akv, akv)


def _gla_kernel(gq_ref, gk_ref, gv_ref, gg_ref, gl_ref, gn_ref, o_ref, st_ref, *, nchunk):
    ch = GLA_CHUNK
    tg = nchunk * ch
    nbatch = gq_ref.shape[0]

    @pl.when(pl.program_id(0) == 0)
    def _():
        st_ref[...] = jnp.zeros_like(st_ref)

    ri = lax.broadcasted_iota(jnp.int32, (tg, tg), 0)
    ci = lax.broadcasted_iota(jnp.int32, (tg, tg), 1)
    tri = ((ri // ch == ci // ch) & (ci <= ri)).astype(F32)
    bts = [_dot(tri, gl_ref[s], HI) for s in range(nbatch)]
    lo = lax.broadcasted_iota(jnp.int32, (ch, LANES), 1) < GLA_DK
    causal = lax.broadcasted_iota(jnp.int32, (ch, ch), 0) >= lax.broadcasted_iota(jnp.int32, (ch, ch), 1)
    eye = (lax.broadcasted_iota(jnp.int32, (LANES, LANES), 0)
           == lax.broadcasted_iota(jnp.int32, (LANES, LANES), 1)).astype(BF16)
    gn = gn_ref[...]
    zero = jnp.zeros((ch, LANES), BF16)
    for c in range(nchunk):
        r0 = c * ch
        for s in range(nbatch):
            b = bts[s][r0:r0 + ch]
            bm = b[ch // 2 - 1:ch // 2]
            bl = b[ch - 1:ch]
            q = gq_ref[s, r0:r0 + ch, :].astype(F32)
            k = gk_ref[s, r0:r0 + ch, :].astype(F32)
            qe = (q * jnp.exp(b - bm)).astype(BF16)
            ke = (k * jnp.exp(bm - b)).astype(BF16)
            qs = (q * jnp.exp(b)).astype(BF16)
            kl = (k * jnp.exp(bl - b)).astype(BF16)
            dec = jnp.exp(bl)
            for p in range(GLA_HEADS // 2):
                sl = slice(p * LANES, (p + 1) * LANES)
                st = st_ref[s, p]
                stb = st.astype(BF16)
                upd = None
                for hh in range(2):
                    h = 2 * p + hh
                    hs = slice(h * GLA_DV, (h + 1) * GLA_DV)
                    msk = lo if hh == 0 else jnp.logical_not(lo)
                    a = _dot_nt(jnp.where(msk, qe[:, sl], zero), ke[:, sl])
                    a = jnp.where(causal, a, 0.0).astype(BF16)
                    v = gv_ref[s, r0:r0 + ch, hs]
                    o = _dot(a, v) + _dot_nt(jnp.where(msk, qs[:, sl], zero), stb)
                    vt = _dot_nt(eye, v).astype(BF16)
                    u = _dot(vt, jnp.where(msk, kl[:, sl], zero))
                    upd = u if upd is None else upd + u
                    gg = gg_ref[s, r0:r0 + ch, hs].astype(F32)
                    o_ref[s, r0:r0 + ch, hs] = (_rms(o) * gn * (gg * jax.nn.sigmoid(gg))).astype(BF16)
                st_ref[s, p] = st * dec[:, sl] + upd


def _gla(gq, gk, gv, gg, gl, gn, nbatch, seq, tg):
    t = gq.shape[0]
    tile = lambda w: pl.BlockSpec((nbatch, tg, w), lambda i: (0, i, 0))
    as3d = lambda a: a.reshape(nbatch, seq, a.shape[1])
    out = pl.pallas_call(
        functools.partial(_gla_kernel, nchunk=tg // GLA_CHUNK),
        grid=(seq // tg,),
        in_specs=[tile(GLA_K_WIDTH), tile(GLA_K_WIDTH), tile(GLA_V_WIDTH), tile(GLA_V_WIDTH), tile(GLA_K_WIDTH),
                  pl.BlockSpec((1, GLA_DV), lambda i: (0, 0))],
        out_specs=tile(GLA_V_WIDTH),
        out_shape=jax.ShapeDtypeStruct((nbatch, seq, GLA_V_WIDTH), BF16),
        scratch_shapes=[pltpu.VMEM((nbatch, GLA_HEADS // 2, GLA_DV, LANES), F32)],
        compiler_params=_params(("arbitrary",)),
        name="gla",
    )(as3d(gq), as3d(gk), as3d(gv), as3d(gg), as3d(gl), gn)
    return out.reshape(t, GLA_V_WIDTH)


def _outproj_kernel(att_ref, gla_ref, x_ref, mod_ref, gpost_ref, gpre_ref, woa_ref, wog_ref, wr_ref, br_ref,
                    x1_ref, h2_ref, route_ref, cnt_ref, base_ref, *, n_exp):
    tm = x_ref.shape[0]

    @pl.when(pl.program_id(0) == 0)
    def _():
        base_ref[...] = jnp.zeros_like(base_ref)

    y = _dot(att_ref[...], woa_ref[...]) + _dot(gla_ref[...], wog_ref[...])
    x1 = x_ref[...] + mod_ref[2:3, :] * (_rms(y) * gpost_ref[...])
    x1_ref[...] = x1
    h2 = _rms(x1) * gpre_ref[...]
    h2 = h2 * (1.0 + mod_ref[4:5, :]) + mod_ref[3:4, :]
    h2_hi = h2.astype(BF16)
    h2_ref[...] = h2_hi

    h2_lo = (h2 - h2_hi.astype(F32)).astype(BF16)
    p_hi = _dot(h2_hi, wr_ref[...])
    p_lo = _dot(h2_lo, wr_ref[...])
    logits = ((p_lo[:, LANES:] + p_lo[:, :LANES]) + p_hi[:, LANES:]) + p_hi[:, :LANES] + br_ref[...]
    lane = lax.broadcasted_iota(jnp.int32, (tm, LANES), 1)
    vals = jnp.where(lane < n_exp, logits, -jnp.inf)
    sels, tops, idxs = [], [], []
    for _ in range(TOP_K):
        m = jnp.max(vals, axis=-1, keepdims=True)
        idx = jnp.min(jnp.where(vals == m, lane, LANES), axis=-1, keepdims=True)
        sel = lane == idx
        vals = jnp.where(sel, -jnp.inf, vals)
        sels.append(sel)
        tops.append(m)
        idxs.append(idx)
    es = [jnp.exp(m - tops[0]) for m in tops]
    tot = es[0] + es[1] + es[2] + es[3]
    onehot = jnp.zeros((tm, LANES), F32)
    for sel in sels:
        onehot = onehot + sel.astype(F32)
    below = (lax.broadcasted_iota(jnp.int32, (tm, tm), 0) > lax.broadcasted_iota(jnp.int32, (tm, tm), 1)).astype(BF16)
    csum = _dot(below, onehot.astype(BF16)) + base_ref[...]
    route = jnp.zeros((tm, LANES), F32)
    for k in range(TOP_K):
        rank = jnp.sum(jnp.where(sels[k], csum, 0.0), axis=-1, keepdims=True)
        route = jnp.where(lane == k, idxs[k].astype(F32), route)
        route = jnp.where(lane == TOP_K + k, es[k] / tot, route)
        route = jnp.where(lane == 2 * TOP_K + k, rank, route)
    route_ref[...] = route
    base = base_ref[...] + jnp.sum(onehot, axis=0, keepdims=True)
    base_ref[...] = base
    cnt_ref[...] = jnp.broadcast_to(base, cnt_ref.shape)


def _outproj(att, gla, x2, mod3, gpost, gpre, woa, wog, wr, br, seq, tm, n_exp):
    t, d = x2.shape
    per = seq // tm
    row = lambda i: (i, 0)
    fixed = lambda i: (0, 0)
    return pl.pallas_call(
        functools.partial(_outproj_kernel, n_exp=n_exp),
        grid=(t // tm,),
        in_specs=[pl.BlockSpec((tm, ATT_WIDTH), row),
                  pl.BlockSpec((tm, GLA_V_WIDTH), row),
                  pl.BlockSpec((tm, d), row),
                  pl.BlockSpec((None, 8, d), lambda i: (i // per, 0, 0)),
                  pl.BlockSpec((1, d), fixed),
                  pl.BlockSpec((1, d), fixed),
                  pl.BlockSpec((ATT_WIDTH, d), fixed),
                  pl.BlockSpec((GLA_V_WIDTH, d), fixed),
                  pl.BlockSpec((d, 2 * LANES), fixed),
                  pl.BlockSpec((1, LANES), fixed)],
        out_specs=[pl.BlockSpec((tm, d), row),
                   pl.BlockSpec((tm, d), row),
                   pl.BlockSpec((tm, LANES), row),
                   pl.BlockSpec((8, LANES), fixed)],
        out_shape=[jax.ShapeDtypeStruct((t, d), F32),
                   jax.ShapeDtypeStruct((t, d), BF16),
                   jax.ShapeDtypeStruct((t, LANES), F32),
                   jax.ShapeDtypeStruct((8, LANES), F32)],
        scratch_shapes=[pltpu.VMEM((1, LANES), F32)],
        compiler_params=_params(("arbitrary",)),
        name="outproj",
    )(att, gla, x2, mod3, gpost, gpre, woa, wog, wr, br)


def _w1prep_kernel(w_ref, g_ref, l_ref):
    n = 2 * LANES
    r = lax.broadcasted_iota(jnp.int32, (n, n), 0)
    c = lax.broadcasted_iota(jnp.int32, (n, n), 1)
    perm = (r == jnp.where(c < LANES, 2 * c, 2 * (c - LANES) + 1)).astype(BF16)
    for j in range(g_ref.shape[1] // LANES):
        d = _dot(w_ref[:, j * n:(j + 1) * n].astype(BF16), perm)
        g_ref[:, j * LANES:(j + 1) * LANES] = d[:, :LANES].astype(BF16)
        l_ref[:, j * LANES:(j + 1) * LANES] = d[:, LANES:].astype(BF16)


def _w1prep(w1):
    n_exp, d, f2 = w1.shape
    f = f2 // 2
    out = jax.ShapeDtypeStruct((n_exp, d, f), BF16)
    return pl.pallas_call(
        _w1prep_kernel,
        grid=(n_exp,),
        in_specs=[pl.BlockSpec((None, d, f2), lambda e: (e, 0, 0))],
        out_specs=[pl.BlockSpec((None, d, f), lambda e: (e, 0, 0))] * 2,
        out_shape=[out, out],
        compiler_params=_params(("parallel",)),
        name="w1prep",
    )(w1)


def _dispatch_kernel(pend_ref, nu_ref, n_ref, cs_ref, gs_ref, prow_ref, h2_ref, xs_hbm, sb, zbuf, sem_r, sem_z,
                     *, tt, rows, n_exp, n_blocks):
    i = pl.program_id(0)
    nsteps = pl.num_programs(0)
    n_sorted = TOP_K * tt
    c = sb.shape[1] // n_sorted
    blk_rows = rows * c
    slot = i & 1

    def zero_fill(start):
        return pltpu.make_async_copy(zbuf, xs_hbm.at[pl.ds(pl.multiple_of(start, blk_rows), blk_rows), :], sem_z)

    def wait_runs(s):
        pltpu.make_async_copy(sb.at[s], xs_hbm.at[pl.ds(0, n_sorted * c), :], sem_r.at[s]).wait()

    @pl.when(i == 0)
    def _():
        zbuf[...] = jnp.zeros_like(zbuf)
        for phase in range(2):
            for e in range(n_exp):
                hi = pend_ref[e]
                lo = pend_ref[e - 1] if e > 0 else 0
                tail = nu_ref[0] + e
                for pred, start in ((hi > lo, (hi - rows) * c), (tail < n_blocks, tail * blk_rows)):
                    @pl.when(pred)
                    def _():
                        if phase == 0:
                            zero_fill(start).start()
                        else:
                            zero_fill(start).wait()

    @pl.when(i >= 2)
    def _():
        wait_runs(slot)

    j = lax.broadcasted_iota(jnp.int32, (n_sorted, tt), 0)
    perm = jnp.zeros((n_sorted, tt), F32)
    for k in range(TOP_K):
        perm = jnp.where(j == prow_ref[k:k + 1, :], 1.0, perm)
    srt = _dot(perm.astype(BF16), h2_ref[...])
    _store_rows(sb.at[slot], _pack(srt))

    for e in range(n_exp):
        n = n_ref[i * n_exp + e]

        @pl.when(n > 0)
        def _():
            src = pl.multiple_of(cs_ref[i * n_exp + e], c)
            dst = pl.multiple_of(gs_ref[i * n_exp + e], c)
            pltpu.make_async_copy(sb.at[slot, pl.ds(src, n), :], xs_hbm.at[pl.ds(dst, n), :],
                                  sem_r.at[slot]).start(priority=e % 2)

    @pl.when(i == nsteps - 1)
    def _():
        @pl.when(nsteps > 1)
        def _():
            wait_runs(1 - slot)
        wait_runs(slot)


def _dispatch(pend, n_used, n_tab, cs_tab, gs_tab, prow, h2, n_slots, rows, tt):
    t, d = h2.shape
    c = d // 2 // LANES
    n_exp = pend.shape[0]
    grid_spec = pltpu.PrefetchScalarGridSpec(
        num_scalar_prefetch=5,
        grid=(t // tt,),
        in_specs=[pl.BlockSpec((None, 8, tt), lambda i, *_: (i, 0, 0)),
                  pl.BlockSpec((tt, d), lambda i, *_: (i, 0))],
        out_specs=pl.BlockSpec(memory_space=pl.ANY),
        scratch_shapes=[pltpu.VMEM((2, TOP_K * tt * c, LANES), jnp.int32),
                        pltpu.VMEM((rows * c, LANES), jnp.int32),
                        pltpu.SemaphoreType.DMA((2,)),
                        pltpu.SemaphoreType.DMA],
    )
    return pl.pallas_call(
        functools.partial(_dispatch_kernel, tt=tt, rows=rows, n_exp=n_exp, n_blocks=n_slots // rows),
        grid_spec=grid_spec,
        out_shape=jax.ShapeDtypeStruct((n_slots * c, LANES), jnp.int32),
        compiler_params=_params(("arbitrary",)),
        name="dispatch",
    )(pend, n_used, n_tab, cs_tab, gs_tab, prow, h2)


def _moe_kernel(be_ref, nu_ref, xs_ref, w1g_ref, w1l_ref, b1g_ref, b1l_ref, w2_ref, b2_ref, ys_ref, *, rows):
    del be_ref
    c = xs_ref.shape[0] // rows
    half = c * LANES

    @pl.when(pl.program_id(0) < nu_ref[0])
    def _():
        lo, hi = _unpack(_load_rows(xs_ref, c))
        xl = lo.astype(BF16)
        xh = hi.astype(BF16)
        glu = _dot(xl, w1g_ref[:half, :]) + _dot(xh, w1g_ref[half:, :]) + b1g_ref[...]
        lin = _dot(xl, w1l_ref[:half, :]) + _dot(xh, w1l_ref[half:, :]) + b1l_ref[...]
        glu = jnp.minimum(glu, SWIGLU_LIMIT)
        lin = jnp.clip(lin, -SWIGLU_LIMIT, SWIGLU_LIMIT)
        a = glu * jax.nn.sigmoid(SWIGLU_ALPHA * glu) * (lin + 1.0)
        _store_rows(ys_ref, _pack(_dot(a.astype(BF16), w2_ref[...]) + b2_ref[...]))

    @pl.when(pl.program_id(0) >= nu_ref[0])
    def _():
        ys_ref[...] = jnp.zeros_like(ys_ref)


def _moe(block_e, n_used, xs, w1g, w1l, b1g, b1l, w2, b2, rows):
    d, f = w1g.shape[1], w1g.shape[2]
    c = d // 2 // LANES
    nb = xs.shape[0] // (rows * c)
    wsel = lambda i, be, nu: (be[i], 0, 0)
    grid_spec = pltpu.PrefetchScalarGridSpec(
        num_scalar_prefetch=2,
        grid=(nb,),
        in_specs=[pl.BlockSpec((rows * c, LANES), lambda i, be, nu: (jnp.minimum(i, nu[0] - 1), 0)),
                  pl.BlockSpec((None, d, f), wsel),
                  pl.BlockSpec((None, d, f), wsel),
                  pl.BlockSpec((None, 1, f), wsel),
                  pl.BlockSpec((None, 1, f), wsel),
                  pl.BlockSpec((None, f, d), wsel),
                  pl.BlockSpec((None, 1, d), wsel)],
        out_specs=pl.BlockSpec((rows * c, LANES), lambda i, be, nu: (i, 0)),
    )
    return pl.pallas_call(
        functools.partial(_moe_kernel, rows=rows),
        grid_spec=grid_spec,
        out_shape=jax.ShapeDtypeStruct(xs.shape, jnp.int32),
        compiler_params=_params(("arbitrary",)),
        name="moe",
    )(block_e, n_used, xs, w1g, w1l, b1g, b1l, w2, b2)


def _combine_kernel(n_ref, cs_ref, gs_ref, ys_hbm, pg_ref, x1_ref, mod_ref, g_ref, o_ref, yb, sem_g, *, tt, n_exp):
    i = pl.program_id(0)
    nsteps = pl.num_programs(0)
    d = o_ref.shape[1]
    half = d // 2
    c = half // LANES
    n_sorted = TOP_K * tt

    def issue(j):
        s = j & 1
        for e in range(n_exp):
            n = n_ref[j * n_exp + e]

            @pl.when(n > 0)
            def _():
                src = pl.multiple_of(gs_ref[j * n_exp + e], c)
                dst = pl.multiple_of(cs_ref[j * n_exp + e], c)
                pltpu.make_async_copy(ys_hbm.at[pl.ds(src, n), :], yb.at[s, pl.ds(dst, n), :],
                                      sem_g.at[s]).start(priority=e % 2)

    @pl.when(i == 0)
    def _():
        issue(0)

    @pl.when(i + 1 < nsteps)
    def _():
        issue(i + 1)

    slot = i & 1
    pltpu.make_async_copy(ys_hbm.at[pl.ds(0, n_sorted * c), :], yb.at[slot], sem_g.at[slot]).wait()

    lo, hi = _unpack(_load_rows(yb.at[slot], c))
    j = lax.broadcasted_iota(jnp.int32, (tt, n_sorted), 1)
    sel = jnp.zeros((tt, n_sorted), F32)
    for k in range(TOP_K):
        pos = pg_ref[:, k:k + 1].astype(jnp.int32)
        sel = jnp.where(j == pos, pg_ref[:, TOP_K + k:TOP_K + k + 1], sel)
    sel_hi = sel.astype(BF16)
    sel_lo = (sel - sel_hi.astype(F32)).astype(BF16)
    lo_b = lo.astype(BF16)
    hi_b = hi.astype(BF16)
    acc_lo = _dot(sel_lo, lo_b) + _dot(sel_hi, lo_b)
    acc_hi = _dot(sel_lo, hi_b) + _dot(sel_hi, hi_b)
    ms = (jnp.sum(acc_lo * acc_lo, axis=-1, keepdims=True) + jnp.sum(acc_hi * acc_hi, axis=-1, keepdims=True)) / d
    inv = lax.rsqrt(ms + NORM_EPS)
    o_ref[:, :half] = x1_ref[:, :half] + mod_ref[5:6, :half] * (acc_lo * inv * g_ref[:, :half])
    o_ref[:, half:] = x1_ref[:, half:] + mod_ref[5:6, half:] * (acc_hi * inv * g_ref[:, half:])


def _combine(n_tab, cs_tab, gs_tab, ys, pg, x1, mod3, g, seq, tt, n_exp):
    t, d = x1.shape
    per = seq // tt
    c = d // 2 // LANES
    row = lambda i, *_: (i, 0)
    grid_spec = pltpu.PrefetchScalarGridSpec(
        num_scalar_prefetch=3,
        grid=(t // tt,),
        in_specs=[pl.BlockSpec(memory_space=pl.ANY),
                  pl.BlockSpec((tt, 2 * TOP_K), row),
                  pl.BlockSpec((tt, d), row),
                  pl.BlockSpec((None, 8, d), lambda i, *_: (i // per, 0, 0)),
                  pl.BlockSpec((1, d), lambda i, *_: (0, 0))],
        out_specs=pl.BlockSpec((tt, d), row),
        scratch_shapes=[pltpu.VMEM((2, TOP_K * tt * c, LANES), jnp.int32),
                        pltpu.SemaphoreType.DMA((2,))],
    )
    return pl.pallas_call(
        functools.partial(_combine_kernel, tt=tt, n_exp=n_exp),
        grid_spec=grid_spec,
        out_shape=jax.ShapeDtypeStruct((t, d), F32),
        compiler_params=_params(("arbitrary",)),
        name="combine",
    )(n_tab, cs_tab, gs_tab, ys, pg, x1, mod3, g)


def _pair_perm():
    half = ATT_Q_HEADS // 2
    idx = []
    for j in range(half):
        idx += list(range(j * ATT_HEAD_DIM, (j + 1) * ATT_HEAD_DIM))
        idx += list(range((half + j) * ATT_HEAD_DIM, (half + j + 1) * ATT_HEAD_DIM))
    return np.asarray(idx, np.int32)


def _layer(x, mod, g_pre_mix, g_post_mix, g_pre_ffn, g_post_ffn, w_in, w_gla_gate_up, b_gla_gate, g_gla_norm,
           sinks, w_out, w_router, b_router, w_mlp1, b_mlp1, w_mlp2, b_mlp2):
    nbatch, seq, d = x.shape
    t = nbatch * seq
    n_exp = w_router.shape[1]
    f = w_mlp2.shape[1]
    x2 = x.reshape(t, d)
    mod3 = jnp.pad(mod.reshape(nbatch, 6, d), ((0, 0), (0, 2), (0, 0)))

    perm = _pair_perm()
    n_main = C_GA - C_AKV
    w_r = jnp.concatenate([w_in[:, perm], w_in[:, ATT_WIDTH:ATT_WIDTH + n_main],
                           jnp.pad(w_in[:, ATT_WIDTH + n_main:], ((0, 0), (0, LANES - GLA_GATE_RANK)))],
                          axis=1).astype(BF16)
    wup = jnp.pad(w_gla_gate_up, ((0, LANES - GLA_GATE_RANK), (0, 0)))
    woa = w_out[:ATT_WIDTH][perm].astype(BF16)
    wog = w_out[ATT_WIDTH:].astype(BF16)
    wr = jnp.pad(w_router, ((0, 0), (0, LANES - n_exp)))
    wr_hi = wr.astype(BF16)
    wr = jnp.concatenate([wr_hi, (wr - wr_hi.astype(F32)).astype(BF16)], axis=1)
    br = jnp.pad(b_router, (0, LANES - n_exp)).reshape(1, LANES)
    w1g, w1l = _w1prep(w_mlp1)
    b1 = b_mlp1.reshape(n_exp, 1, f, 2)
    w2 = w_mlp2.astype(BF16)

    tm = min(512, seq)
    aq, akv, gq, gk, gv, gg, gl = _inproj(x2, mod3, g_pre_mix.reshape(1, d), w_r, wup,
                                          b_gla_gate.reshape(1, GLA_K_WIDTH), seq, tm)
    att = _att(aq, akv, sinks, nbatch, seq, min(256, seq))
    gla = _gla(gq, gk, gv, gg, gl, g_gla_norm.reshape(1, GLA_DV), nbatch, seq, min(256, seq))
    x1, h2, route, cnt = _outproj(att, gla, x2, mod3, g_post_mix.reshape(1, d), g_pre_ffn.reshape(1, d),
                                   woa, wog, wr, br, seq, tm, n_exp)

    rows = MOE_ROWS
    counts = cnt[0, :n_exp].astype(jnp.int32)
    padded = (counts + rows - 1) // rows * rows
    pend = jnp.cumsum(padded)
    pstart = pend - padded
    top_i = route[:, 0:TOP_K].astype(jnp.int32)
    rank = route[:, 2 * TOP_K:3 * TOP_K].astype(jnp.int32)
    n_slots = t * TOP_K + n_exp * rows
    n_blocks = n_slots // rows
    n_used = (pend[-1] // rows).astype(jnp.int32).reshape(1)
    blk_ids = jnp.minimum(jnp.arange(n_blocks, dtype=jnp.int32), n_used - 1)
    block_e = jnp.minimum(jnp.sum(pend[None, :] <= (blk_ids * rows)[:, None], axis=1), n_exp - 1).astype(jnp.int32)
    tt = min(256, seq)
    n_tiles = t // tt
    sub = d // 2 // LANES
    experts = jnp.arange(n_exp, dtype=jnp.int32)
    tile_cnt = jnp.sum(top_i.reshape(n_tiles, tt * TOP_K, 1) == experts, axis=1, dtype=jnp.int32)
    run_start = jnp.cumsum(tile_cnt, axis=1) - tile_cnt
    before = jnp.cumsum(tile_cnt, axis=0) - tile_cnt
    n_tab = (tile_cnt * sub).reshape(-1)
    cs_tab = (run_start * sub).reshape(-1)
    gs_tab = ((pstart[None, :] + before) * sub).reshape(-1)
    tile_of = (jnp.arange(t, dtype=jnp.int32) // tt)[:, None]
    pos = rank + (run_start - before).reshape(-1)[tile_of * n_exp + top_i]
    prow = jnp.pad(pos.reshape(n_tiles, tt, TOP_K).transpose(0, 2, 1), ((0, 0), (0, 8 - TOP_K), (0, 0)),
                   constant_values=-1)
    pg = jnp.concatenate([pos.astype(F32), route[:, TOP_K:2 * TOP_K]], axis=1)

    xs = _dispatch(pend, n_used, n_tab, cs_tab, gs_tab, prow, h2, n_slots, rows, tt)
    ys = _moe(block_e, n_used, xs, w1g, w1l, b1[..., 0], b1[..., 1], w2, b_mlp2.reshape(n_exp, 1, d), rows)
    out = _combine(n_tab, cs_tab, gs_tab, ys, pg, x1, mod3, g_post_ffn.reshape(1, d), seq, tt, n_exp)
    return out.reshape(nbatch, seq, d)


def kernel(x, c, w_ada, b_ada, g_pre_mix, g_post_mix, g_pre_ffn, g_post_ffn, w_in, w_gla_gate_up, b_gla_gate,
           g_gla_norm, sinks, w_out, w_router, b_router, w_mlp1, b_mlp1, w_mlp2, b_mlp2):
    for l in range(w_in.shape[0]):
        mod = _ada(c, w_ada[l], b_ada[l])
        x = _layer(x, mod, g_pre_mix[l], g_post_mix[l], g_pre_ffn[l], g_post_ffn[l], w_in[l], w_gla_gate_up[l],
                   b_gla_gate[l], g_gla_norm[l], sinks[l], w_out[l], w_router[l], b_router[l], w_mlp1[l], b_mlp1[l],
                   w_mlp2[l], b_mlp2[l])
    return x
```

```python
import functools

import numpy as np
import jax
import jax.numpy as jnp
from jax import lax
from jax.experimental import pallas as pl
from jax.experimental.pallas import tpu as pltpu

F32 = jnp.float32
BF16 = jnp.bfloat16
HI = lax.Precision.HIGHEST

ATT_Q_HEADS = 8
ATT_KV_HEADS = 2
ATT_HEAD_DIM = 64
ATT_BLOCK = 128
GLA_HEADS = 4
GLA_DK = 64
GLA_DV = 128
GLA_GATE_RANK = 16
GLA_GATE_NORMALIZER = 16.0
GLA_CHUNK = 64
TOP_K = 4
SWIGLU_LIMIT = 7.0
SWIGLU_ALPHA = 1.702
NORM_EPS = 1e-6

LANES = 128
ATT_WIDTH = ATT_Q_HEADS * ATT_HEAD_DIM
ATT_KV_WIDTH = ATT_KV_HEADS * ATT_HEAD_DIM
GLA_K_WIDTH = GLA_HEADS * GLA_DK
GLA_V_WIDTH = GLA_HEADS * GLA_DV
C_AQ = 0
C_AKV = C_AQ + ATT_WIDTH
C_GQ = C_AKV + 2 * ATT_KV_WIDTH
C_GK = C_GQ + GLA_K_WIDTH
C_GV = C_GK + GLA_K_WIDTH
C_GG = C_GV + GLA_V_WIDTH
C_GA = C_GG + GLA_V_WIDTH
C_END = C_GA + LANES

MOE_ROWS = 512
VMEM_LIMIT = 48 * 1024 * 1024


def _dot(a, b, prec=None):
    return jnp.dot(a, b, preferred_element_type=F32, precision=prec)


def _dot_nt(a, b):
    return lax.dot_general(a, b, (((1,), (1,)), ((), ())), preferred_element_type=F32)


def _rms(t):
    return t * lax.rsqrt(jnp.mean(t * t, axis=-1, keepdims=True) + NORM_EPS)


def _params(sem):
    return pltpu.CompilerParams(dimension_semantics=sem, vmem_limit_bytes=VMEM_LIMIT)


HI16 = -65536


def _pack(a):
    half = a.shape[1] // 2
    lo = lax.bitcast_convert_type(a[:, :half].astype(BF16).astype(F32), jnp.int32)
    hi = lax.bitcast_convert_type(a[:, half:].astype(BF16).astype(F32), jnp.int32)
    return hi | lax.shift_right_logical(lo, 16)


def _unpack(p):
    lo = lax.bitcast_convert_type(lax.shift_left(p, 16), F32)
    hi = lax.bitcast_convert_type(p & HI16, F32)
    return lo, hi


def _store_rows(ref, p):
    n, w = p.shape
    c = w // LANES
    for s in range(c):
        ref[pl.ds(s, n, stride=c), :] = p[:, s * LANES:(s + 1) * LANES]


def _load_rows(ref, c):
    n = ref.shape[0] // c
    return jnp.concatenate([ref[pl.ds(s, n, stride=c), :] for s in range(c)], axis=1)


def _ada_kernel(c_ref, w_ref, b_ref, o_ref):
    c = c_ref[...]
    o_ref[...] = _dot(c * jax.nn.sigmoid(c), w_ref[...], HI) + b_ref[...]


def _ada(c, w, b):
    nb, d = c.shape
    n = w.shape[1]
    cp = jnp.zeros((8, d), F32).at[:nb].set(c)
    out = pl.pallas_call(
        _ada_kernel,
        grid=(n // d,),
        in_specs=[pl.BlockSpec((8, d), lambda j: (0, 0)),
                  pl.BlockSpec((d, d), lambda j: (0, j)),
                  pl.BlockSpec((1, d), lambda j: (0, j))],
        out_specs=pl.BlockSpec((8, d), lambda j: (0, j)),
        out_shape=jax.ShapeDtypeStruct((8, n), F32),
        compiler_params=_params(("parallel",)),
        name="ada",
    )(cp, w, b.reshape(1, n))
    return out[:nb]


def _inproj_kernel(x_ref, mod_ref, g_ref, w_ref, wup_ref, bup_ref,
                   aq_ref, akv_ref, gq_ref, gk_ref, gv_ref, gg_ref, gl_ref):
    half = x_ref.shape[0] // 2
    for r in (slice(0, half), slice(half, 2 * half)):
        h = _rms(x_ref[r, :]) * g_ref[...]
        h = h * (1.0 + mod_ref[1:2, :]) + mod_ref[0:1, :]
        p = _dot(h.astype(BF16), w_ref[...])
        aq_ref[r, :] = (p[:, C_AQ:C_AKV] * (ATT_HEAD_DIM ** -0.5)).astype(BF16)
        akv_ref[r, :] = p[:, C_AKV:C_GQ].astype(BF16)
        gq_ref[r, :] = (p[:, C_GQ:C_GK] * (GLA_DK ** -0.5)).astype(BF16)
        gk_ref[r, :] = p[:, C_GK:C_GV].astype(BF16)
        gv_ref[r, :] = p[:, C_GV:C_GG].astype(BF16)
        gg_ref[r, :] = p[:, C_GG:C_GA].astype(BF16)
        z = _dot(p[:, C_GA:C_END], wup_ref[...], HI) + bup_ref[...]
        gl_ref[r, :] = (jnp.minimum(z, 0.0) - jnp.log(1.0 + jnp.exp(-jnp.abs(z)))) * (1.0 / GLA_GATE_NORMALIZER)


def _inproj(x2, mod3, g, w_r, wup, bup, seq, tm):
    t, d = x2.shape
    per = seq // tm
    row = lambda i: (i, 0)
    fixed = lambda i: (0, 0)
    widths = (ATT_WIDTH, 2 * ATT_KV_WIDTH, GLA_K_WIDTH, GLA_K_WIDTH, GLA_V_WIDTH, GLA_V_WIDTH)
    out_shape = [jax.ShapeDtypeStruct((t, w), BF16) for w in widths] + [jax.ShapeDtypeStruct((t, GLA_K_WIDTH), F32)]
    out_specs = [pl.BlockSpec((tm, w), row) for w in widths] + [pl.BlockSpec((tm, GLA_K_WIDTH), row)]
    return pl.pallas_call(
        _inproj_kernel,
        grid=(t // tm,),
        in_specs=[pl.BlockSpec((tm, d), row),
                  pl.BlockSpec((None, 8, d), lambda i: (i // per, 0, 0)),
                  pl.BlockSpec((1, d), fixed),
                  pl.BlockSpec((d, C_END), fixed),
                  pl.BlockSpec((LANES, GLA_K_WIDTH), fixed),
                  pl.BlockSpec((1, GLA_K_WIDTH), fixed)],
        out_specs=out_specs,
        out_shape=out_shape,
        compiler_params=_params(("parallel",)),
        name="inproj",
    )(x2, mod3, g, w_r, wup, bup)


def _att_kernel(sinks_ref, q_ref, kv_ref, kvp_ref, o_ref, *, nblk):
    i = pl.program_id(1)
    blk = ATT_BLOCK
    lo = lax.broadcasted_iota(jnp.int32, (blk, LANES), 1) < ATT_HEAD_DIM
    qi = lax.broadcasted_iota(jnp.int32, (2 * blk, 2 * blk), 0) % blk
    kj = lax.broadcasted_iota(jnp.int32, (2 * blk, 2 * blk), 1)
    cur_ok = (kj >= blk) & ((kj - blk) <= qi)
    prev_ok = (kj < blk) & (kj > qi)
    first_off = jnp.where(i > 0, 0, blk)
    top = lax.broadcasted_iota(jnp.int32, (2 * blk, 1), 0) < blk
    for jb in range(nblk):
        r0 = jb * blk
        kvc = kv_ref[r0:r0 + blk, :]
        if jb == 0:
            kvp = kvp_ref[...]
            mask = cur_ok | (prev_ok & (kj >= first_off))
        else:
            kvp = kv_ref[r0 - blk:r0, :]
            mask = cur_ok | prev_ok
        kcat = jnp.concatenate([kvp[:, 0:ATT_KV_WIDTH], kvc[:, 0:ATT_KV_WIDTH]], axis=0)
        vcat = jnp.concatenate([kvp[:, ATT_KV_WIDTH:], kvc[:, ATT_KV_WIDTH:]], axis=0)
        for j in range(ATT_Q_HEADS // 2):
            qp = q_ref[r0:r0 + blk, j * LANES:(j + 1) * LANES]
            zero = jnp.zeros_like(qp)
            q2 = jnp.concatenate([jnp.where(lo, qp, zero), jnp.where(lo, zero, qp)], axis=0)
            s = jnp.where(mask, _dot_nt(q2, kcat), -jnp.inf)
            sink = jnp.where(top, sinks_ref[j], sinks_ref[ATT_Q_HEADS // 2 + j])
            m = jnp.maximum(jnp.max(s, axis=-1, keepdims=True), sink)
            p = jnp.exp(s - m)
            den = jnp.sum(p, axis=-1, keepdims=True) + jnp.exp(sink - m)
            o2 = _dot(p.astype(BF16), vcat) / den
            o = jnp.where(lo, o2[0:blk], o2[blk:2 * blk])
            o_ref[r0:r0 + blk, j * LANES:(j + 1) * LANES] = o.astype(BF16)


def _att(aq, akv, sinks, nbatch, seq, ta):
    t = aq.shape[0]
    nblk = ta // ATT_BLOCK
    per = seq // ta
    perb = seq // ATT_BLOCK
    return pl.pallas_call(
        functools.partial(_att_kernel, nblk=nblk),
        grid=(nbatch, per),
        in_specs=[pl.BlockSpec(memory_space=pltpu.SMEM),
                  pl.BlockSpec((ta, ATT_WIDTH), lambda b, i: (b * per + i, 0)),
                  pl.BlockSpec((ta, 2 * ATT_KV_WIDTH), lambda b, i: (b * per + i, 0)),
                  pl.BlockSpec((ATT_BLOCK, 2 * ATT_KV_WIDTH),
                               lambda b, i: (b * perb + jnp.maximum(i * nblk - 1, 0), 0))],
        out_specs=pl.BlockSpec((ta, ATT_WIDTH), lambda b, i: (b * per + i, 0)),
        out_shape=jax.ShapeDtypeStruct((t, ATT_WIDTH), BF16),
        compiler_params=_params(("parallel", "parallel")),
        name="att",
    )(sinks, aq, akv, akv)


def _gla_kernel(gq_ref, gk_ref, gv_ref, gg_ref, gl_ref, gn_ref, o_ref, st_ref, *, nchunk):
    ch = GLA_CHUNK
    tg = nchunk * ch
    nbatch = gq_ref.shape[0]

    @pl.when(pl.program_id(0) == 0)
    def _():
        st_ref[...] = jnp.zeros_like(st_ref)

    ri = lax.broadcasted_iota(jnp.int32, (tg, tg), 0)
    ci = lax.broadcasted_iota(jnp.int32, (tg, tg), 1)
    tri = ((ri // ch == ci // ch) & (ci <= ri)).astype(F32)
    bts = [_dot(tri, gl_ref[s], HI) for s in range(nbatch)]
    lo = lax.broadcasted_iota(jnp.int32, (ch, LANES), 1) < GLA_DK
    causal = lax.broadcasted_iota(jnp.int32, (ch, ch), 0) >= lax.broadcasted_iota(jnp.int32, (ch, ch), 1)
    eye = (lax.broadcasted_iota(jnp.int32, (LANES, LANES), 0)
           == lax.broadcasted_iota(jnp.int32, (LANES, LANES), 1)).astype(BF16)
    gn = gn_ref[...]
    zero = jnp.zeros((ch, LANES), BF16)
    for c in range(nchunk):
        r0 = c * ch
        for s in range(nbatch):
            b = bts[s][r0:r0 + ch]
            bm = b[ch // 2 - 1:ch // 2]
            bl = b[ch - 1:ch]
            q = gq_ref[s, r0:r0 + ch, :].astype(F32)
            k = gk_ref[s, r0:r0 + ch, :].astype(F32)
            qe = (q * jnp.exp(b - bm)).astype(BF16)
            ke = (k * jnp.exp(bm - b)).astype(BF16)
            qs = (q * jnp.exp(b)).astype(BF16)
            kl = (k * jnp.exp(bl - b)).astype(BF16)
            dec = jnp.exp(bl)
            for p in range(GLA_HEADS // 2):
                sl = slice(p * LANES, (p + 1) * LANES)
                st = st_ref[s, p]
                stb = st.astype(BF16)
                upd = None
                for hh in range(2):
                    h = 2 * p + hh
                    hs = slice(h * GLA_DV, (h + 1) * GLA_DV)
                    msk = lo if hh == 0 else jnp.logical_not(lo)
                    a = _dot_nt(jnp.where(msk, qe[:, sl], zero), ke[:, sl])
                    a = jnp.where(causal, a, 0.0).astype(BF16)
                    v = gv_ref[s, r0:r0 + ch, hs]
                    o = _dot(a, v) + _dot_nt(jnp.where(msk, qs[:, sl], zero), stb)
                    vt = _dot_nt(eye, v).astype(BF16)
                    u = _dot(vt, jnp.where(msk, kl[:, sl], zero))
                    upd = u if upd is None else upd + u
                    gg = gg_ref[s, r0:r0 + ch, hs].astype(F32)
                    o_ref[s, r0:r0 + ch, hs] = (_rms(o) * gn * (gg * jax.nn.sigmoid(gg))).astype(BF16)
                st_ref[s, p] = st * dec[:, sl] + upd


def _gla(gq, gk, gv, gg, gl, gn, nbatch, seq, tg):
    t = gq.shape[0]
    tile = lambda w: pl.BlockSpec((nbatch, tg, w), lambda i: (0, i, 0))
    as3d = lambda a: a.reshape(nbatch, seq, a.shape[1])
    out = pl.pallas_call(
        functools.partial(_gla_kernel, nchunk=tg // GLA_CHUNK),
        grid=(seq // tg,),
        in_specs=[tile(GLA_K_WIDTH), tile(GLA_K_WIDTH), tile(GLA_V_WIDTH), tile(GLA_V_WIDTH), tile(GLA_K_WIDTH),
                  pl.BlockSpec((1, GLA_DV), lambda i: (0, 0))],
        out_specs=tile(GLA_V_WIDTH),
        out_shape=jax.ShapeDtypeStruct((nbatch, seq, GLA_V_WIDTH), BF16),
        scratch_shapes=[pltpu.VMEM((nbatch, GLA_HEADS // 2, GLA_DV, LANES), F32)],
        compiler_params=_params(("arbitrary",)),
        name="gla",
    )(as3d(gq), as3d(gk), as3d(gv), as3d(gg), as3d(gl), gn)
    return out.reshape(t, GLA_V_WIDTH)


def _outproj_kernel(att_ref, gla_ref, x_ref, mod_ref, gpost_ref, gpre_ref, woa_ref, wog_ref, wr_ref, br_ref,
                    x1_ref, h2_ref, route_ref, cnt_ref, tcnt_ref, base_ref, *, n_exp, tt):
    tm = x_ref.shape[0]

    @pl.when(pl.program_id(0) == 0)
    def _():
        base_ref[...] = jnp.zeros_like(base_ref)

    y = _dot(att_ref[...], woa_ref[...]) + _dot(gla_ref[...], wog_ref[...])
    x1 = x_ref[...] + mod_ref[2:3, :] * (_rms(y) * gpost_ref[...])
    x1_ref[...] = x1
    h2 = _rms(x1) * gpre_ref[...]
    h2 = h2 * (1.0 + mod_ref[4:5, :]) + mod_ref[3:4, :]
    h2_hi = h2.astype(BF16)
    h2_ref[...] = h2_hi

    h2_lo = (h2 - h2_hi.astype(F32)).astype(BF16)
    p_hi = _dot(h2_hi, wr_ref[...])
    p_lo = _dot(h2_lo, wr_ref[...])
    logits = ((p_lo[:, LANES:] + p_lo[:, :LANES]) + p_hi[:, LANES:]) + p_hi[:, :LANES] + br_ref[...]
    lane = lax.broadcasted_iota(jnp.int32, (tm, LANES), 1)
    vals = jnp.where(lane < n_exp, logits, -jnp.inf)
    sels, tops, idxs = [], [], []
    for _ in range(TOP_K):
        m = jnp.max(vals, axis=-1, keepdims=True)
        idx = jnp.min(jnp.where(vals == m, lane, LANES), axis=-1, keepdims=True)
        sel = lane == idx
        vals = jnp.where(sel, -jnp.inf, vals)
        sels.append(sel)
        tops.append(m)
        idxs.append(idx)
    es = [jnp.exp(m - tops[0]) for m in tops]
    tot = es[0] + es[1] + es[2] + es[3]
    onehot = jnp.zeros((tm, LANES), F32)
    for sel in sels:
        onehot = onehot + sel.astype(F32)
    nsub = tm // tt
    ri = lax.broadcasted_iota(jnp.int32, (tm, tm), 0)
    ci = lax.broadcasted_iota(jnp.int32, (tm, tm), 1)
    below = ((ci < ri) & (ri // tt == ci // tt)).astype(BF16)
    earlier = _dot(below, onehot.astype(BF16))
    sub_row = lax.broadcasted_iota(jnp.int32, (8, LANES), 0)
    sub_cnt = jnp.zeros((8, LANES), F32)
    for h in range(nsub):
        sub_cnt = jnp.where(sub_row == h, jnp.sum(onehot[h * tt:(h + 1) * tt], axis=0, keepdims=True), sub_cnt)
    lower = (lax.broadcasted_iota(jnp.int32, (LANES, LANES), 0)
             < lax.broadcasted_iota(jnp.int32, (LANES, LANES), 1)).astype(BF16)
    run_start = _dot(sub_cnt.astype(BF16), lower)
    tok_sub = lax.broadcasted_iota(jnp.int32, (tm, LANES), 0) // tt
    pos_all = earlier
    for h in range(nsub):
        pos_all = pos_all + jnp.where(tok_sub == h, run_start[h:h + 1, :], 0.0)
    route = jnp.zeros((tm, LANES), F32)
    for k in range(TOP_K):
        pos = jnp.sum(jnp.where(sels[k], pos_all, 0.0), axis=-1, keepdims=True)
        route = jnp.where(lane == k, idxs[k].astype(F32), route)
        route = jnp.where(lane == TOP_K + k, es[k] / tot, route)
        route = jnp.where(lane == 2 * TOP_K + k, pos, route)
    route_ref[...] = route
    tcnt_ref[...] = sub_cnt
    base = base_ref[...] + jnp.sum(onehot, axis=0, keepdims=True)
    base_ref[...] = base
    cnt_ref[...] = jnp.broadcast_to(base, cnt_ref.shape)


def _outproj(att, gla, x2, mod3, gpost, gpre, woa, wog, wr, br, seq, tm, n_exp, tt):
    t, d = x2.shape
    per = seq // tm
    row = lambda i: (i, 0)
    fixed = lambda i: (0, 0)
    return pl.pallas_call(
        functools.partial(_outproj_kernel, n_exp=n_exp, tt=tt),
        grid=(t // tm,),
        in_specs=[pl.BlockSpec((tm, ATT_WIDTH), row),
                  pl.BlockSpec((tm, GLA_V_WIDTH), row),
                  pl.BlockSpec((tm, d), row),
                  pl.BlockSpec((None, 8, d), lambda i: (i // per, 0, 0)),
                  pl.BlockSpec((1, d), fixed),
                  pl.BlockSpec((1, d), fixed),
                  pl.BlockSpec((ATT_WIDTH, d), fixed),
                  pl.BlockSpec((GLA_V_WIDTH, d), fixed),
                  pl.BlockSpec((d, 2 * LANES), fixed),
                  pl.BlockSpec((1, LANES), fixed)],
        out_specs=[pl.BlockSpec((tm, d), row),
                   pl.BlockSpec((tm, d), row),
                   pl.BlockSpec((tm, LANES), row),
                   pl.BlockSpec((8, LANES), fixed),
                   pl.BlockSpec((None, 8, LANES), lambda i: (i, 0, 0))],
        out_shape=[jax.ShapeDtypeStruct((t, d), F32),
                   jax.ShapeDtypeStruct((t, d), BF16),
                   jax.ShapeDtypeStruct((t, LANES), F32),
                   jax.ShapeDtypeStruct((8, LANES), F32),
                   jax.ShapeDtypeStruct((t // tm, 8, LANES), F32)],
        scratch_shapes=[pltpu.VMEM((1, LANES), F32)],
        compiler_params=_params(("arbitrary",)),
        name="outproj",
    )(att, gla, x2, mod3, gpost, gpre, woa, wog, wr, br)


def _w1prep_kernel(w_ref, g_ref, l_ref):
    n = 2 * LANES
    r = lax.broadcasted_iota(jnp.int32, (n, n), 0)
    c = lax.broadcasted_iota(jnp.int32, (n, n), 1)
    perm = (r == jnp.where(c < LANES, 2 * c, 2 * (c - LANES) + 1)).astype(BF16)
    for j in range(g_ref.shape[1] // LANES):
        d = _dot(w_ref[:, j * n:(j + 1) * n].astype(BF16), perm)
        g_ref[:, j * LANES:(j + 1) * LANES] = d[:, :LANES].astype(BF16)
        l_ref[:, j * LANES:(j + 1) * LANES] = d[:, LANES:].astype(BF16)


def _w1prep(w1):
    n_exp, d, f2 = w1.shape
    f = f2 // 2
    out = jax.ShapeDtypeStruct((n_exp, d, f), BF16)
    return pl.pallas_call(
        _w1prep_kernel,
        grid=(n_exp,),
        in_specs=[pl.BlockSpec((None, d, f2), lambda e: (e, 0, 0))],
        out_specs=[pl.BlockSpec((None, d, f), lambda e: (e, 0, 0))] * 2,
        out_shape=[out, out],
        compiler_params=_params(("parallel",)),
        name="w1prep",
    )(w1)


def _dispatch_kernel(pend_ref, nu_ref, n_ref, cs_ref, gs_ref, prow_ref, h2_ref, xs_hbm, sb, zbuf, sem_r, sem_z,
                     *, tt, rows, n_exp, n_blocks):
    i = pl.program_id(0)
    nsteps = pl.num_programs(0)
    n_sorted = TOP_K * tt
    c = sb.shape[1] // n_sorted
    blk_rows = rows * c
    slot = i & 1

    def zero_fill(start):
        return pltpu.make_async_copy(zbuf, xs_hbm.at[pl.ds(pl.multiple_of(start, blk_rows), blk_rows), :], sem_z)

    def wait_runs(s):
        pltpu.make_async_copy(sb.at[s], xs_hbm.at[pl.ds(0, n_sorted * c), :], sem_r.at[s]).wait()

    @pl.when(i == 0)
    def _():
        zbuf[...] = jnp.zeros_like(zbuf)
        for phase in range(2):
            for e in range(n_exp):
                hi = pend_ref[e]
                lo = pend_ref[e - 1] if e > 0 else 0
                tail = nu_ref[0] + e
                for pred, start in ((hi > lo, (hi - rows) * c), (tail < n_blocks, tail * blk_rows)):
                    @pl.when(pred)
                    def _():
                        if phase == 0:
                            zero_fill(start).start()
                        else:
                            zero_fill(start).wait()

    @pl.when(i >= 2)
    def _():
        wait_runs(slot)

    j = lax.broadcasted_iota(jnp.int32, (n_sorted, tt), 0)
    perm = jnp.zeros((n_sorted, tt), F32)
    for k in range(TOP_K):
        perm = jnp.where(j == prow_ref[k:k + 1, :], 1.0, perm)
    srt = _dot(perm.astype(BF16), h2_ref[...])
    _store_rows(sb.at[slot], _pack(srt))

    for e in range(n_exp):
        n = n_ref[i * n_exp + e]

        @pl.when(n > 0)
        def _():
            src = pl.multiple_of(cs_ref[i * n_exp + e], c)
            dst = pl.multiple_of(gs_ref[i * n_exp + e], c)
            pltpu.make_async_copy(sb.at[slot, pl.ds(src, n), :], xs_hbm.at[pl.ds(dst, n), :],
                                  sem_r.at[slot]).start(priority=e % 2)

    @pl.when(i == nsteps - 1)
    def _():
        @pl.when(nsteps > 1)
        def _():
            wait_runs(1 - slot)
        wait_runs(slot)


def _dispatch(pend, n_used, n_tab, cs_tab, gs_tab, prow, h2, n_slots, rows, tt):
    t, d = h2.shape
    c = d // 2 // LANES
    n_exp = pend.shape[0]
    grid_spec = pltpu.PrefetchScalarGridSpec(
        num_scalar_prefetch=5,
        grid=(t // tt,),
        in_specs=[pl.BlockSpec((None, 8, tt), lambda i, *_: (i, 0, 0)),
                  pl.BlockSpec((tt, d), lambda i, *_: (i, 0))],
        out_specs=pl.BlockSpec(memory_space=pl.ANY),
        scratch_shapes=[pltpu.VMEM((2, TOP_K * tt * c, LANES), jnp.int32),
                        pltpu.VMEM((rows * c, LANES), jnp.int32),
                        pltpu.SemaphoreType.DMA((2,)),
                        pltpu.SemaphoreType.DMA],
    )
    return pl.pallas_call(
        functools.partial(_dispatch_kernel, tt=tt, rows=rows, n_exp=n_exp, n_blocks=n_slots // rows),
        grid_spec=grid_spec,
        out_shape=jax.ShapeDtypeStruct((n_slots * c, LANES), jnp.int32),
        compiler_params=_params(("arbitrary",)),
        name="dispatch",
    )(pend, n_used, n_tab, cs_tab, gs_tab, prow, h2)


def _moe_kernel(be_ref, nu_ref, xs_ref, w1g_ref, w1l_ref, b1g_ref, b1l_ref, w2_ref, b2_ref, ys_ref, *, rows):
    del be_ref
    c = xs_ref.shape[0] // rows
    half = c * LANES

    @pl.when(pl.program_id(0) < nu_ref[0])
    def _():
        lo, hi = _unpack(_load_rows(xs_ref, c))
        xl = lo.astype(BF16)
        xh = hi.astype(BF16)
        glu = _dot(xl, w1g_ref[:half, :]) + _dot(xh, w1g_ref[half:, :]) + b1g_ref[...]
        lin = _dot(xl, w1l_ref[:half, :]) + _dot(xh, w1l_ref[half:, :]) + b1l_ref[...]
        glu = jnp.minimum(glu, SWIGLU_LIMIT)
        lin = jnp.clip(lin, -SWIGLU_LIMIT, SWIGLU_LIMIT)
        a = glu * jax.nn.sigmoid(SWIGLU_ALPHA * glu) * (lin + 1.0)
        _store_rows(ys_ref, _pack(_dot(a.astype(BF16), w2_ref[...].astype(BF16)) + b2_ref[...]))

    @pl.when(pl.program_id(0) >= nu_ref[0])
    def _():
        ys_ref[...] = jnp.zeros_like(ys_ref)


def _moe(block_e, n_used, xs, w1g, w1l, b1g, b1l, w2, b2, rows):
    d, f = w1g.shape[1], w1g.shape[2]
    c = d // 2 // LANES
    nb = xs.shape[0] // (rows * c)
    wsel = lambda i, be, nu: (be[i], 0, 0)
    grid_spec = pltpu.PrefetchScalarGridSpec(
        num_scalar_prefetch=2,
        grid=(nb,),
        in_specs=[pl.BlockSpec((rows * c, LANES), lambda i, be, nu: (jnp.minimum(i, nu[0] - 1), 0)),
                  pl.BlockSpec((None, d, f), wsel),
                  pl.BlockSpec((None, d, f), wsel),
                  pl.BlockSpec((None, 1, f), wsel),
                  pl.BlockSpec((None, 1, f), wsel),
                  pl.BlockSpec((None, f, d), wsel),
                  pl.BlockSpec((None, 1, d), wsel)],
        out_specs=pl.BlockSpec((rows * c, LANES), lambda i, be, nu: (i, 0)),
    )
    return pl.pallas_call(
        functools.partial(_moe_kernel, rows=rows),
        grid_spec=grid_spec,
        out_shape=jax.ShapeDtypeStruct(xs.shape, jnp.int32),
        compiler_params=_params(("arbitrary",)),
        name="moe",
    )(block_e, n_used, xs, w1g, w1l, b1g, b1l, w2, b2)


def _combine_kernel(n_ref, cs_ref, gs_ref, ys_hbm, pg_ref, x1_ref, mod_ref, g_ref, o_ref, yb, sem_g, *, tt, n_exp):
    i = pl.program_id(0)
    nsteps = pl.num_programs(0)
    d = o_ref.shape[1]
    half = d // 2
    c = half // LANES
    n_sorted = TOP_K * tt

    def issue(j):
        s = j & 1
        for e in range(n_exp):
            n = n_ref[j * n_exp + e]

            @pl.when(n > 0)
            def _():
                src = pl.multiple_of(gs_ref[j * n_exp + e], c)
                dst = pl.multiple_of(cs_ref[j * n_exp + e], c)
                pltpu.make_async_copy(ys_hbm.at[pl.ds(src, n), :], yb.at[s, pl.ds(dst, n), :],
                                      sem_g.at[s]).start(priority=e % 2)

    @pl.when(i == 0)
    def _():
        issue(0)

    @pl.when(i + 1 < nsteps)
    def _():
        issue(i + 1)

    slot = i & 1
    pltpu.make_async_copy(ys_hbm.at[pl.ds(0, n_sorted * c), :], yb.at[slot], sem_g.at[slot]).wait()

    lo, hi = _unpack(_load_rows(yb.at[slot], c))
    j = lax.broadcasted_iota(jnp.int32, (tt, n_sorted), 1)
    sel = jnp.zeros((tt, n_sorted), F32)
    for k in range(TOP_K):
        pos = pg_ref[:, 2 * TOP_K + k:2 * TOP_K + k + 1].astype(jnp.int32)
        sel = jnp.where(j == pos, pg_ref[:, TOP_K + k:TOP_K + k + 1], sel)
    sel_hi = sel.astype(BF16)
    sel_lo = (sel - sel_hi.astype(F32)).astype(BF16)
    lo_b = lo.astype(BF16)
    hi_b = hi.astype(BF16)
    acc_lo = _dot(sel_lo, lo_b) + _dot(sel_hi, lo_b)
    acc_hi = _dot(sel_lo, hi_b) + _dot(sel_hi, hi_b)
    ms = (jnp.sum(acc_lo * acc_lo, axis=-1, keepdims=True) + jnp.sum(acc_hi * acc_hi, axis=-1, keepdims=True)) / d
    inv = lax.rsqrt(ms + NORM_EPS)
    o_ref[:, :half] = x1_ref[:, :half] + mod_ref[5:6, :half] * (acc_lo * inv * g_ref[:, :half])
    o_ref[:, half:] = x1_ref[:, half:] + mod_ref[5:6, half:] * (acc_hi * inv * g_ref[:, half:])


def _combine(n_tab, cs_tab, gs_tab, ys, pg, x1, mod3, g, seq, tt, n_exp):
    t, d = x1.shape
    per = seq // tt
    c = d // 2 // LANES
    row = lambda i, *_: (i, 0)
    grid_spec = pltpu.PrefetchScalarGridSpec(
        num_scalar_prefetch=3,
        grid=(t // tt,),
        in_specs=[pl.BlockSpec(memory_space=pl.ANY),
                  pl.BlockSpec((tt, LANES), row),
                  pl.BlockSpec((tt, d), row),
                  pl.BlockSpec((None, 8, d), lambda i, *_: (i // per, 0, 0)),
                  pl.BlockSpec((1, d), lambda i, *_: (0, 0))],
        out_specs=pl.BlockSpec((tt, d), row),
        scratch_shapes=[pltpu.VMEM((2, TOP_K * tt * c, LANES), jnp.int32),
                        pltpu.SemaphoreType.DMA((2,))],
    )
    return pl.pallas_call(
        functools.partial(_combine_kernel, tt=tt, n_exp=n_exp),
        grid_spec=grid_spec,
        out_shape=jax.ShapeDtypeStruct((t, d), F32),
        compiler_params=_params(("arbitrary",)),
        name="combine",
    )(n_tab, cs_tab, gs_tab, ys, pg, x1, mod3, g)


def _pair_perm():
    half = ATT_Q_HEADS // 2
    idx = []
    for j in range(half):
        idx += list(range(j * ATT_HEAD_DIM, (j + 1) * ATT_HEAD_DIM))
        idx += list(range((half + j) * ATT_HEAD_DIM, (half + j + 1) * ATT_HEAD_DIM))
    return np.asarray(idx, np.int32)


def _layer(x, mod, g_pre_mix, g_post_mix, g_pre_ffn, g_post_ffn, w_in, w_gla_gate_up, b_gla_gate, g_gla_norm,
           sinks, w_out, w_router, b_router, w_mlp1, b_mlp1, w_mlp2, b_mlp2):
    nbatch, seq, d = x.shape
    t = nbatch * seq
    n_exp = w_router.shape[1]
    f = w_mlp2.shape[1]
    x2 = x.reshape(t, d)
    mod3 = jnp.pad(mod.reshape(nbatch, 6, d), ((0, 0), (0, 2), (0, 0)))

    perm = _pair_perm()
    n_main = C_GA - C_AKV
    w_r = jnp.concatenate([w_in[:, perm], w_in[:, ATT_WIDTH:ATT_WIDTH + n_main],
                           jnp.pad(w_in[:, ATT_WIDTH + n_main:], ((0, 0), (0, LANES - GLA_GATE_RANK)))],
                          axis=1).astype(BF16)
    wup = jnp.pad(w_gla_gate_up, ((0, LANES - GLA_GATE_RANK), (0, 0)))
    woa = w_out[:ATT_WIDTH][perm].astype(BF16)
    wog = w_out[ATT_WIDTH:].astype(BF16)
    wr = jnp.pad(w_router, ((0, 0), (0, LANES - n_exp)))
    wr_hi = wr.astype(BF16)
    wr = jnp.concatenate([wr_hi, (wr - wr_hi.astype(F32)).astype(BF16)], axis=1)
    br = jnp.pad(b_router, (0, LANES - n_exp)).reshape(1, LANES)
    w1g, w1l = _w1prep(w_mlp1)
    b1 = b_mlp1.reshape(n_exp, 1, f, 2)
    w2 = w_mlp2

    tm = min(512, seq)
    tt = min(256, seq)
    aq, akv, gq, gk, gv, gg, gl = _inproj(x2, mod3, g_pre_mix.reshape(1, d), w_r, wup,
                                          b_gla_gate.reshape(1, GLA_K_WIDTH), seq, min(1024, seq))
    att = _att(aq, akv, sinks, nbatch, seq, min(512, seq))
    gla = _gla(gq, gk, gv, gg, gl, g_gla_norm.reshape(1, GLA_DV), nbatch, seq, min(256, seq))
    x1, h2, route, cnt, tcnt = _outproj(att, gla, x2, mod3, g_post_mix.reshape(1, d), g_pre_ffn.reshape(1, d),
                                        woa, wog, wr, br, seq, tm, n_exp, tt)

    rows = MOE_ROWS
    counts = cnt[0, :n_exp].astype(jnp.int32)
    padded = (counts + rows - 1) // rows * rows
    pend = jnp.cumsum(padded)
    pstart = pend - padded
    n_slots = t * TOP_K + n_exp * rows
    n_blocks = n_slots // rows
    n_used = (pend[-1] // rows).astype(jnp.int32).reshape(1)
    blk_ids = jnp.minimum(jnp.arange(n_blocks, dtype=jnp.int32), n_used - 1)
    block_e = jnp.minimum(jnp.sum(pend[None, :] <= (blk_ids * rows)[:, None], axis=1), n_exp - 1).astype(jnp.int32)
    n_tiles = t // tt
    sub = d // 2 // LANES
    tile_cnt = tcnt[:, :tm // tt, :n_exp].reshape(n_tiles, n_exp).astype(jnp.int32)
    run_start = jnp.cumsum(tile_cnt, axis=1) - tile_cnt
    before = jnp.cumsum(tile_cnt, axis=0) - tile_cnt
    n_tab = (tile_cnt * sub).reshape(-1)
    cs_tab = (run_start * sub).reshape(-1)
    gs_tab = ((pstart[None, :] + before) * sub).reshape(-1)
    pos = route[:, 2 * TOP_K:3 * TOP_K].astype(jnp.int32)
    prow = jnp.pad(pos.reshape(n_tiles, tt, TOP_K).transpose(0, 2, 1), ((0, 0), (0, 8 - TOP_K), (0, 0)),
                   constant_values=-1)

    xs = _dispatch(pend, n_used, n_tab, cs_tab, gs_tab, prow, h2, n_slots, rows, tt)
    ys = _moe(block_e, n_used, xs, w1g, w1l, b1[..., 0], b1[..., 1], w2, b_mlp2.reshape(n_exp, 1, d), rows)
    out = _combine(n_tab, cs_tab, gs_tab, ys, route, x1, mod3, g_post_ffn.reshape(1, d), seq, tt, n_exp)
    return out.reshape(nbatch, seq, d)


def kernel(x, c, w_ada, b_ada, g_pre_mix, g_post_mix, g_pre_ffn, g_post_ffn, w_in, w_gla_gate_up, b_gla_gate,
           g_gla_norm, sinks, w_out, w_router, b_router, w_mlp1, b_mlp1, w_mlp2, b_mlp2):
    for l in range(w_in.shape[0]):
        mod = _ada(c, w_ada[l], b_ada[l])
        x = _layer(x, mod, g_pre_mix[l], g_post_mix[l], g_pre_ffn[l], g_post_ffn[l], w_in[l], w_gla_gate_up[l],
                   b_gla_gate[l], g_gla_norm[l], sinks[l], w_out[l], w_router[l], b_router[l], w_mlp1[l], b_mlp1[l],
                   w_mlp2[l], b_mlp2[l])
    return x
```

```python
import functools

import numpy as np
import jax
import jax.numpy as jnp
from jax import lax
from jax.experimental import pallas as pl
from jax.experimental.pallas import tpu as pltpu

F32 = jnp.float32
BF16 = jnp.bfloat16
HI = lax.Precision.HIGHEST

ATT_Q_HEADS = 8
ATT_KV_HEADS = 2
ATT_HEAD_DIM = 64
ATT_BLOCK = 128
GLA_HEADS = 4
GLA_DK = 64
GLA_DV = 128
GLA_GATE_RANK = 16
GLA_GATE_NORMALIZER = 16.0
GLA_CHUNK = 64
TOP_K = 4
SWIGLU_LIMIT = 7.0
SWIGLU_ALPHA = 1.702
NORM_EPS = 1e-6

LANES = 128
ATT_WIDTH = ATT_Q_HEADS * ATT_HEAD_DIM
ATT_KV_WIDTH = ATT_KV_HEADS * ATT_HEAD_DIM
GLA_K_WIDTH = GLA_HEADS * GLA_DK
GLA_V_WIDTH = GLA_HEADS * GLA_DV
C_AQ = 0
C_AKV = C_AQ + ATT_WIDTH
C_GQ = C_AKV + 2 * ATT_KV_WIDTH
C_GK = C_GQ + GLA_K_WIDTH
C_GV = C_GK + GLA_K_WIDTH
C_GG = C_GV + GLA_V_WIDTH
C_GA = C_GG + GLA_V_WIDTH
C_END = C_GA + LANES

MOE_ROWS = 512
VMEM_LIMIT = 48 * 1024 * 1024


def _dot(a, b, prec=None):
    return jnp.dot(a, b, preferred_element_type=F32, precision=prec)


def _dot_nt(a, b):
    return lax.dot_general(a, b, (((1,), (1,)), ((), ())), preferred_element_type=F32)


def _rms(t):
    return t * lax.rsqrt(jnp.mean(t * t, axis=-1, keepdims=True) + NORM_EPS)


def _params(sem):
    return pltpu.CompilerParams(dimension_semantics=sem, vmem_limit_bytes=VMEM_LIMIT)


HI16 = -65536


def _pack(a):
    half = a.shape[1] // 2
    lo = lax.bitcast_convert_type(a[:, :half].astype(BF16).astype(F32), jnp.int32)
    hi = lax.bitcast_convert_type(a[:, half:].astype(BF16).astype(F32), jnp.int32)
    return hi | lax.shift_right_logical(lo, 16)


def _unpack(p):
    lo = lax.bitcast_convert_type(lax.shift_left(p, 16), F32)
    hi = lax.bitcast_convert_type(p & HI16, F32)
    return lo, hi


def _store_rows(ref, p):
    n, w = p.shape
    c = w // LANES
    for s in range(c):
        ref[pl.ds(s, n, stride=c), :] = p[:, s * LANES:(s + 1) * LANES]


def _load_rows(ref, c):
    n = ref.shape[0] // c
    return jnp.concatenate([ref[pl.ds(s, n, stride=c), :] for s in range(c)], axis=1)


def _ada_kernel(c_ref, w_ref, b_ref, o_ref):
    c = c_ref[...]
    o_ref[...] = _dot(c * jax.nn.sigmoid(c), w_ref[...], HI) + b_ref[...]


def _ada(c, w, b):
    nb, d = c.shape
    n = w.shape[1]
    cp = jnp.zeros((8, d), F32).at[:nb].set(c)
    out = pl.pallas_call(
        _ada_kernel,
        grid=(n // d,),
        in_specs=[pl.BlockSpec((8, d), lambda j: (0, 0)),
                  pl.BlockSpec((d, d), lambda j: (0, j)),
                  pl.BlockSpec((1, d), lambda j: (0, j))],
        out_specs=pl.BlockSpec((8, d), lambda j: (0, j)),
        out_shape=jax.ShapeDtypeStruct((8, n), F32),
        compiler_params=_params(("parallel",)),
        name="ada",
    )(cp, w, b.reshape(1, n))
    return out[:nb]


def _inproj_kernel(x_ref, mod_ref, g_ref, w_ref, wup_ref, bup_ref,
                   aq_ref, akv_ref, gq_ref, gk_ref, gv_ref, gg_ref, gl_ref):
    half = x_ref.shape[0] // 2
    for r in (slice(0, half), slice(half, 2 * half)):
        h = _rms(x_ref[r, :]) * g_ref[...]
        h = h * (1.0 + mod_ref[1:2, :]) + mod_ref[0:1, :]
        p = _dot(h.astype(BF16), w_ref[...])
        aq_ref[r, :] = (p[:, C_AQ:C_AKV] * (ATT_HEAD_DIM ** -0.5)).astype(BF16)
        akv_ref[r, :] = p[:, C_AKV:C_GQ].astype(BF16)
        gq_ref[r, :] = (p[:, C_GQ:C_GK] * (GLA_DK ** -0.5)).astype(BF16)
        gk_ref[r, :] = p[:, C_GK:C_GV].astype(BF16)
        gv_ref[r, :] = p[:, C_GV:C_GG].astype(BF16)
        gg_ref[r, :] = p[:, C_GG:C_GA].astype(BF16)
        z = _dot(p[:, C_GA:C_END], wup_ref[...], HI) + bup_ref[...]
        gl_ref[r, :] = (jnp.minimum(z, 0.0) - jnp.log(1.0 + jnp.exp(-jnp.abs(z)))) * (1.0 / GLA_GATE_NORMALIZER)


def _inproj(x2, mod3, g, w_r, wup, bup, seq, tm):
    t, d = x2.shape
    per = seq // tm
    row = lambda i: (i, 0)
    fixed = lambda i: (0, 0)
    widths = (ATT_WIDTH, 2 * ATT_KV_WIDTH, GLA_K_WIDTH, GLA_K_WIDTH, GLA_V_WIDTH, GLA_V_WIDTH)
    out_shape = [jax.ShapeDtypeStruct((t, w), BF16) for w in widths] + [jax.ShapeDtypeStruct((t, GLA_K_WIDTH), F32)]
    out_specs = [pl.BlockSpec((tm, w), row) for w in widths] + [pl.BlockSpec((tm, GLA_K_WIDTH), row)]
    return pl.pallas_call(
        _inproj_kernel,
        grid=(t // tm,),
        in_specs=[pl.BlockSpec((tm, d), row),
                  pl.BlockSpec((None, 8, d), lambda i: (i // per, 0, 0)),
                  pl.BlockSpec((1, d), fixed),
                  pl.BlockSpec((d, C_END), fixed),
                  pl.BlockSpec((LANES, GLA_K_WIDTH), fixed),
                  pl.BlockSpec((1, GLA_K_WIDTH), fixed)],
        out_specs=out_specs,
        out_shape=out_shape,
        compiler_params=_params(("parallel",)),
        name="inproj",
    )(x2, mod3, g, w_r, wup, bup)


def _att_kernel(sinks_ref, q_ref, kv_ref, kvp_ref, o_ref, *, nblk):
    i = pl.program_id(1)
    blk = ATT_BLOCK
    lo = lax.broadcasted_iota(jnp.int32, (blk, LANES), 1) < ATT_HEAD_DIM
    qi = lax.broadcasted_iota(jnp.int32, (2 * blk, 2 * blk), 0) % blk
    kj = lax.broadcasted_iota(jnp.int32, (2 * blk, 2 * blk), 1)
    cur_ok = (kj >= blk) & ((kj - blk) <= qi)
    prev_ok = (kj < blk) & (kj > qi)
    first_off = jnp.where(i > 0, 0, blk)
    top = lax.broadcasted_iota(jnp.int32, (2 * blk, 1), 0) < blk
    for jb in range(nblk):
        r0 = jb * blk
        kvc = kv_ref[r0:r0 + blk, :]
        if jb == 0:
            kvp = kvp_ref[...]
            mask = cur_ok | (prev_ok & (kj >= first_off))
        else:
            kvp = kv_ref[r0 - blk:r0, :]
            mask = cur_ok | prev_ok
        kcat = jnp.concatenate([kvp[:, 0:ATT_KV_WIDTH], kvc[:, 0:ATT_KV_WIDTH]], axis=0)
        vcat = jnp.concatenate([kvp[:, ATT_KV_WIDTH:], kvc[:, ATT_KV_WIDTH:]], axis=0)
        for j in range(ATT_Q_HEADS // 2):
            qp = q_ref[r0:r0 + blk, j * LANES:(j + 1) * LANES]
            zero = jnp.zeros_like(qp)
            q2 = jnp.concatenate([jnp.where(lo, qp, zero), jnp.where(lo, zero, qp)], axis=0)
            s = jnp.where(mask, _dot_nt(q2, kcat), -jnp.inf)
            sink = jnp.where(top, sinks_ref[j], sinks_ref[ATT_Q_HEADS // 2 + j])
            m = jnp.maximum(jnp.max(s, axis=-1, keepdims=True), sink)
            p = jnp.exp(s - m)
            den = jnp.sum(p, axis=-1, keepdims=True) + jnp.exp(sink - m)
            o2 = _dot(p.astype(BF16), vcat) / den
            o = jnp.where(lo, o2[0:blk], o2[blk:2 * blk])
            o_ref[r0:r0 + blk, j * LANES:(j + 1) * LANES] = o.astype(BF16)


def _att(aq, akv, sinks, nbatch, seq, ta):
    t = aq.shape[0]
    nblk = ta // ATT_BLOCK
    per = seq // ta
    perb = seq // ATT_BLOCK
    return pl.pallas_call(
        functools.partial(_att_kernel, nblk=nblk),
        grid=(nbatch, per),
        in_specs=[pl.BlockSpec(memory_space=pltpu.SMEM),
                  pl.BlockSpec((ta, ATT_WIDTH), lambda b, i: (b * per + i, 0)),
                  pl.BlockSpec((ta, 2 * ATT_KV_WIDTH), lambda b, i: (b * per + i, 0)),
                  pl.BlockSpec((ATT_BLOCK, 2 * ATT_KV_WIDTH),
                               lambda b, i: (b * perb + jnp.maximum(i * nblk - 1, 0), 0))],
        out_specs=pl.BlockSpec((ta, ATT_WIDTH), lambda b, i: (b * per + i, 0)),
        out_shape=jax.ShapeDtypeStruct((t, ATT_WIDTH), BF16),
        compiler_params=_params(("parallel", "parallel")),
        name="att",
    )(sinks, aq, akv, akv)


def _gla_kernel(gq_ref, gk_ref, gv_ref, gg_ref, gl_ref, gn_ref, o_ref, st_ref, *, nchunk):
    ch = GLA_CHUNK
    tg = nchunk * ch
    nbatch = gq_ref.shape[0]

    @pl.when(pl.program_id(0) == 0)
    def _():
        st_ref[...] = jnp.zeros_like(st_ref)

    ri = lax.broadcasted_iota(jnp.int32, (tg, tg), 0)
    ci = lax.broadcasted_iota(jnp.int32, (tg, tg), 1)
    tri = ((ri // ch == ci // ch) & (ci <= ri)).astype(BF16)
    bts = []
    for s in range(nbatch):
        g0 = gl_ref[s]
        g1 = g0 - g0.astype(BF16).astype(F32)
        g2 = g1 - g1.astype(BF16).astype(F32)
        parts = _dot(tri, jnp.concatenate([g0.astype(BF16), g1.astype(BF16), g2.astype(BF16)], axis=1))
        bts.append((parts[:, 2 * GLA_K_WIDTH:] + parts[:, GLA_K_WIDTH:2 * GLA_K_WIDTH]) + parts[:, :GLA_K_WIDTH])
    row2 = lax.broadcasted_iota(jnp.int32, (2 * ch, LANES), 0)
    lane2 = lax.broadcasted_iota(jnp.int32, (2 * ch, LANES), 1)
    own = row2 // ch == lane2 // GLA_DK
    causal = (row2 // ch == lane2 // ch) & (lane2 % ch <= row2 % ch)
    gn = gn_ref[...]
    zero = jnp.zeros((2 * ch, LANES), BF16)

    def both(x):
        return jnp.where(own, jnp.concatenate([x, x], axis=0), zero)

    for c in range(nchunk):
        r0 = c * ch
        for s in range(nbatch):
            b = bts[s][r0:r0 + ch]
            bm = b[ch // 2 - 1:ch // 2]
            bl = b[ch - 1:ch]
            q = gq_ref[s, r0:r0 + ch, :].astype(F32)
            k = gk_ref[s, r0:r0 + ch, :].astype(F32)
            qe = (q * jnp.exp(b - bm)).astype(BF16)
            ke = (k * jnp.exp(bm - b)).astype(BF16)
            qs = (q * jnp.exp(b)).astype(BF16)
            kl = (k * jnp.exp(bl - b)).astype(BF16)
            dec = jnp.exp(bl)
            for p in range(GLA_HEADS // 2):
                sl = slice(p * LANES, (p + 1) * LANES)
                h0 = slice(2 * p * GLA_DV, (2 * p + 1) * GLA_DV)
                h1 = slice((2 * p + 1) * GLA_DV, (2 * p + 2) * GLA_DV)
                st = st_ref[s, p]
                a = _dot_nt(both(qe[:, sl]), both(ke[:, sl]))
                a = jnp.where(causal, a, 0.0).astype(BF16)
                v2 = jnp.concatenate([gv_ref[s, r0:r0 + ch, h0], gv_ref[s, r0:r0 + ch, h1]], axis=0)
                o = _dot(a, v2) + _dot_nt(both(qs[:, sl]), st.astype(BF16))
                upd = lax.dot_general(v2, both(kl[:, sl]), (((0,), (0,)), ((), ())),
                                      preferred_element_type=F32)
                gg = jnp.concatenate([gg_ref[s, r0:r0 + ch, h0], gg_ref[s, r0:r0 + ch, h1]], axis=0).astype(F32)
                res = (_rms(o) * gn * (gg * jax.nn.sigmoid(gg))).astype(BF16)
                o_ref[s, r0:r0 + ch, h0] = res[:ch]
                o_ref[s, r0:r0 + ch, h1] = res[ch:]
                st_ref[s, p] = st * dec[:, sl] + upd


def _gla(gq, gk, gv, gg, gl, gn, nbatch, seq, tg):
    t = gq.shape[0]
    tile = lambda w: pl.BlockSpec((nbatch, tg, w), lambda i: (0, i, 0))
    as3d = lambda a: a.reshape(nbatch, seq, a.shape[1])
    out = pl.pallas_call(
        functools.partial(_gla_kernel, nchunk=tg // GLA_CHUNK),
        grid=(seq // tg,),
        in_specs=[tile(GLA_K_WIDTH), tile(GLA_K_WIDTH), tile(GLA_V_WIDTH), tile(GLA_V_WIDTH), tile(GLA_K_WIDTH),
                  pl.BlockSpec((1, GLA_DV), lambda i: (0, 0))],
        out_specs=tile(GLA_V_WIDTH),
        out_shape=jax.ShapeDtypeStruct((nbatch, seq, GLA_V_WIDTH), BF16),
        scratch_shapes=[pltpu.VMEM((nbatch, GLA_HEADS // 2, GLA_DV, LANES), F32)],
        compiler_params=_params(("arbitrary",)),
        name="gla",
    )(as3d(gq), as3d(gk), as3d(gv), as3d(gg), as3d(gl), gn)
    return out.reshape(t, GLA_V_WIDTH)


def _outproj_kernel(att_ref, gla_ref, x_ref, mod_ref, gpost_ref, gpre_ref, woa_ref, wog_ref, wr_ref, br_ref,
                    x1_ref, h2_ref, route_ref, cnt_ref, tcnt_ref, base_ref, *, n_exp, tt, tc):
    tm = x_ref.shape[0]

    @pl.when(pl.program_id(0) == 0)
    def _():
        base_ref[...] = jnp.zeros_like(base_ref)

    nsub = tc // tt
    lane = lax.broadcasted_iota(jnp.int32, (tc, LANES), 1)
    ri = lax.broadcasted_iota(jnp.int32, (tc, tc), 0)
    ci = lax.broadcasted_iota(jnp.int32, (tc, tc), 1)
    below = ((ci < ri) & (ri // tt == ci // tt)).astype(BF16)
    lower = (lax.broadcasted_iota(jnp.int32, (LANES, LANES), 0)
             < lax.broadcasted_iota(jnp.int32, (LANES, LANES), 1)).astype(BF16)
    sub_row = lax.broadcasted_iota(jnp.int32, (8, LANES), 0)
    tok_sub = lax.broadcasted_iota(jnp.int32, (tc, LANES), 0) // tt
    all_cnt = jnp.zeros((8, LANES), F32)
    for ch in range(tm // tc):
        r = slice(ch * tc, (ch + 1) * tc)
        y = _dot(att_ref[r, :], woa_ref[...]) + _dot(gla_ref[r, :], wog_ref[...])
        x1 = x_ref[r, :] + mod_ref[2:3, :] * (_rms(y) * gpost_ref[...])
        x1_ref[r, :] = x1
        h2 = _rms(x1) * gpre_ref[...]
        h2 = h2 * (1.0 + mod_ref[4:5, :]) + mod_ref[3:4, :]
        h2_hi = h2.astype(BF16)
        h2_ref[r, :] = h2_hi

        h2_lo = (h2 - h2_hi.astype(F32)).astype(BF16)
        p_hi = _dot(h2_hi, wr_ref[...])
        p_lo = _dot(h2_lo, wr_ref[...])
        logits = ((p_lo[:, LANES:] + p_lo[:, :LANES]) + p_hi[:, LANES:]) + p_hi[:, :LANES] + br_ref[...]
        vals = jnp.where(lane < n_exp, logits, -jnp.inf)
        sels, tops, idxs = [], [], []
        for _ in range(TOP_K):
            m = jnp.max(vals, axis=-1, keepdims=True)
            idx = jnp.min(jnp.where(vals == m, lane, LANES), axis=-1, keepdims=True)
            sel = lane == idx
            vals = jnp.where(sel, -jnp.inf, vals)
            sels.append(sel)
            tops.append(m)
            idxs.append(idx)
        es = [jnp.exp(m - tops[0]) for m in tops]
        tot = es[0] + es[1] + es[2] + es[3]
        onehot = jnp.zeros((tc, LANES), F32)
        for sel in sels:
            onehot = onehot + sel.astype(F32)
        earlier = _dot(below, onehot.astype(BF16))
        sub_cnt = jnp.zeros((8, LANES), F32)
        for h in range(nsub):
            sub_cnt = jnp.where(sub_row == h, jnp.sum(onehot[h * tt:(h + 1) * tt], axis=0, keepdims=True), sub_cnt)
        run_start = _dot(sub_cnt.astype(BF16), lower)
        pos_all = earlier
        for h in range(nsub):
            pos_all = pos_all + jnp.where(tok_sub == h, run_start[h:h + 1, :], 0.0)
            all_cnt = jnp.where(sub_row == ch * nsub + h, sub_cnt[h:h + 1, :], all_cnt)
        route = jnp.zeros((tc, LANES), F32)
        for k in range(TOP_K):
            pos = jnp.sum(jnp.where(sels[k], pos_all, 0.0), axis=-1, keepdims=True)
            route = jnp.where(lane == k, idxs[k].astype(F32), route)
            route = jnp.where(lane == TOP_K + k, es[k] / tot, route)
            route = jnp.where(lane == 2 * TOP_K + k, pos, route)
        route_ref[r, :] = route
    tcnt_ref[...] = all_cnt
    base = base_ref[...] + jnp.sum(all_cnt, axis=0, keepdims=True)
    base_ref[...] = base
    cnt_ref[...] = jnp.broadcast_to(base, cnt_ref.shape)


def _outproj(att, gla, x2, mod3, gpost, gpre, woa, wog, wr, br, seq, tm, n_exp, tt):
    t, d = x2.shape
    per = seq // tm
    row = lambda i: (i, 0)
    fixed = lambda i: (0, 0)
    return pl.pallas_call(
        functools.partial(_outproj_kernel, n_exp=n_exp, tt=tt, tc=min(512, tm)),
        grid=(t // tm,),
        in_specs=[pl.BlockSpec((tm, ATT_WIDTH), row),
                  pl.BlockSpec((tm, GLA_V_WIDTH), row),
                  pl.BlockSpec((tm, d), row),
                  pl.BlockSpec((None, 8, d), lambda i: (i // per, 0, 0)),
                  pl.BlockSpec((1, d), fixed),
                  pl.BlockSpec((1, d), fixed),
                  pl.BlockSpec((ATT_WIDTH, d), fixed),
                  pl.BlockSpec((GLA_V_WIDTH, d), fixed),
                  pl.BlockSpec((d, 2 * LANES), fixed),
                  pl.BlockSpec((1, LANES), fixed)],
        out_specs=[pl.BlockSpec((tm, d), row),
                   pl.BlockSpec((tm, d), row),
                   pl.BlockSpec((tm, LANES), row),
                   pl.BlockSpec((8, LANES), fixed),
                   pl.BlockSpec((None, 8, LANES), lambda i: (i, 0, 0))],
        out_shape=[jax.ShapeDtypeStruct((t, d), F32),
                   jax.ShapeDtypeStruct((t, d), BF16),
                   jax.ShapeDtypeStruct((t, LANES), F32),
                   jax.ShapeDtypeStruct((8, LANES), F32),
                   jax.ShapeDtypeStruct((t // tm, 8, LANES), F32)],
        scratch_shapes=[pltpu.VMEM((1, LANES), F32)],
        compiler_params=_params(("arbitrary",)),
        name="outproj",
    )(att, gla, x2, mod3, gpost, gpre, woa, wog, wr, br)


def _w1prep_kernel(w_ref, g_ref, l_ref):
    n = 2 * LANES
    r = lax.broadcasted_iota(jnp.int32, (n, n), 0)
    c = lax.broadcasted_iota(jnp.int32, (n, n), 1)
    perm = (r == jnp.where(c < LANES, 2 * c, 2 * (c - LANES) + 1)).astype(BF16)
    for j in range(g_ref.shape[1] // LANES):
        d = _dot(w_ref[:, j * n:(j + 1) * n].astype(BF16), perm)
        g_ref[:, j * LANES:(j + 1) * LANES] = d[:, :LANES].astype(BF16)
        l_ref[:, j * LANES:(j + 1) * LANES] = d[:, LANES:].astype(BF16)


def _w1prep(w1):
    n_exp, d, f2 = w1.shape
    f = f2 // 2
    out = jax.ShapeDtypeStruct((n_exp, d, f), BF16)
    return pl.pallas_call(
        _w1prep_kernel,
        grid=(n_exp,),
        in_specs=[pl.BlockSpec((None, d, f2), lambda e: (e, 0, 0))],
        out_specs=[pl.BlockSpec((None, d, f), lambda e: (e, 0, 0))] * 2,
        out_shape=[out, out],
        compiler_params=_params(("parallel",)),
        name="w1prep",
    )(w1)


def _dispatch_kernel(pend_ref, nu_ref, n_ref, cs_ref, gs_ref, prow_ref, h2_ref, xs_hbm, sb, zbuf, sem_r, sem_z,
                     *, tt, rows, n_exp, n_blocks):
    i = pl.program_id(0)
    nsteps = pl.num_programs(0)
    n_sorted = TOP_K * tt
    c = sb.shape[1] // n_sorted
    blk_rows = rows * c
    slot = i & 1

    def zero_fill(start):
        return pltpu.make_async_copy(zbuf, xs_hbm.at[pl.ds(pl.multiple_of(start, blk_rows), blk_rows), :], sem_z)

    def wait_runs(s):
        pltpu.make_async_copy(sb.at[s], xs_hbm.at[pl.ds(0, n_sorted * c), :], sem_r.at[s]).wait()

    @pl.when(i == 0)
    def _():
        zbuf[...] = jnp.zeros_like(zbuf)
        for phase in range(2):
            for e in range(n_exp):
                hi = pend_ref[e]
                lo = pend_ref[e - 1] if e > 0 else 0
                tail = nu_ref[0] + e
                for pred, start in ((hi > lo, (hi - rows) * c), (tail < n_blocks, tail * blk_rows)):
                    @pl.when(pred)
                    def _():
                        if phase == 0:
                            zero_fill(start).start()
                        else:
                            zero_fill(start).wait()

    @pl.when(i >= 2)
    def _():
        wait_runs(slot)

    j = lax.broadcasted_iota(jnp.int32, (n_sorted, tt), 0)
    perm = jnp.zeros((n_sorted, tt), F32)
    for k in range(TOP_K):
        perm = jnp.where(j == prow_ref[k:k + 1, :], 1.0, perm)
    srt = _dot(perm.astype(BF16), h2_ref[...])
    _store_rows(sb.at[slot], _pack(srt))

    for e in range(n_exp):
        n = n_ref[i * n_exp + e]

        @pl.when(n > 0)
        def _():
            src = pl.multiple_of(cs_ref[i * n_exp + e], c)
            dst = pl.multiple_of(gs_ref[i * n_exp + e], c)
            pltpu.make_async_copy(sb.at[slot, pl.ds(src, n), :], xs_hbm.at[pl.ds(dst, n), :],
                                  sem_r.at[slot]).start(priority=e % 2)

    @pl.when(i == nsteps - 1)
    def _():
        @pl.when(nsteps > 1)
        def _():
            wait_runs(1 - slot)
        wait_runs(slot)


def _dispatch(pend, n_used, n_tab, cs_tab, gs_tab, prow, h2, n_slots, rows, tt):
    t, d = h2.shape
    c = d // 2 // LANES
    n_exp = pend.shape[0]
    grid_spec = pltpu.PrefetchScalarGridSpec(
        num_scalar_prefetch=5,
        grid=(t // tt,),
        in_specs=[pl.BlockSpec((None, 8, tt), lambda i, *_: (i, 0, 0)),
                  pl.BlockSpec((tt, d), lambda i, *_: (i, 0))],
        out_specs=pl.BlockSpec(memory_space=pl.ANY),
        scratch_shapes=[pltpu.VMEM((2, TOP_K * tt * c, LANES), jnp.int32),
                        pltpu.VMEM((rows * c, LANES), jnp.int32),
                        pltpu.SemaphoreType.DMA((2,)),
                        pltpu.SemaphoreType.DMA],
    )
    return pl.pallas_call(
        functools.partial(_dispatch_kernel, tt=tt, rows=rows, n_exp=n_exp, n_blocks=n_slots // rows),
        grid_spec=grid_spec,
        out_shape=jax.ShapeDtypeStruct((n_slots * c, LANES), jnp.int32),
        compiler_params=_params(("arbitrary",)),
        name="dispatch",
    )(pend, n_used, n_tab, cs_tab, gs_tab, prow, h2)


def _moe_kernel(be_ref, bv_ref, nu_ref, xs_ref, w1g_ref, w1l_ref, b1g_ref, b1l_ref, w2_ref, b2_ref, ys_ref, *, rows):
    del be_ref, nu_ref
    c = xs_ref.shape[0] // rows
    half = c * LANES
    valid = bv_ref[pl.program_id(0)]

    def mlp(n):
        xs = xs_ref if n == rows else xs_ref.at[pl.ds(0, n * c)]
        ys = ys_ref if n == rows else ys_ref.at[pl.ds(0, n * c)]
        lo, hi = _unpack(_load_rows(xs, c))
        xl = lo.astype(BF16)
        xh = hi.astype(BF16)
        glu = _dot(xl, w1g_ref[:half, :]) + _dot(xh, w1g_ref[half:, :]) + b1g_ref[...]
        lin = _dot(xl, w1l_ref[:half, :]) + _dot(xh, w1l_ref[half:, :]) + b1l_ref[...]
        glu = jnp.minimum(glu, SWIGLU_LIMIT)
        lin = jnp.clip(lin, -SWIGLU_LIMIT, SWIGLU_LIMIT)
        a = glu * jax.nn.sigmoid(SWIGLU_ALPHA * glu) * (lin + 1.0)
        _store_rows(ys, _pack(_dot(a.astype(BF16), w2_ref[...].astype(BF16)) + b2_ref[...]))
        if n < rows:
            ys_ref[n * c:, :] = jnp.zeros((rows * c - n * c, LANES), jnp.int32)

    @pl.when(valid > rows // 2)
    def _():
        mlp(rows)

    @pl.when((valid > 0) & (valid <= rows // 2))
    def _():
        mlp(rows // 2)

    @pl.when(valid == 0)
    def _():
        ys_ref[...] = jnp.zeros_like(ys_ref)


def _moe(block_e, block_valid, n_used, xs, w1g, w1l, b1g, b1l, w2, b2, rows):
    d, f = w1g.shape[1], w1g.shape[2]
    c = d // 2 // LANES
    nb = xs.shape[0] // (rows * c)
    wsel = lambda i, be, bv, nu: (be[i], 0, 0)
    grid_spec = pltpu.PrefetchScalarGridSpec(
        num_scalar_prefetch=3,
        grid=(nb,),
        in_specs=[pl.BlockSpec((rows * c, LANES), lambda i, be, bv, nu: (jnp.minimum(i, nu[0] - 1), 0)),
                  pl.BlockSpec((None, d, f), wsel),
                  pl.BlockSpec((None, d, f), wsel),
                  pl.BlockSpec((None, 1, f), wsel),
                  pl.BlockSpec((None, 1, f), wsel),
                  pl.BlockSpec((None, f, d), wsel),
                  pl.BlockSpec((None, 1, d), wsel)],
        out_specs=pl.BlockSpec((rows * c, LANES), lambda i, be, bv, nu: (i, 0)),
    )
    return pl.pallas_call(
        functools.partial(_moe_kernel, rows=rows),
        grid_spec=grid_spec,
        out_shape=jax.ShapeDtypeStruct(xs.shape, jnp.int32),
        compiler_params=_params(("arbitrary",)),
        name="moe",
    )(block_e, block_valid, n_used, xs, w1g, w1l, b1g, b1l, w2, b2)


def _combine_kernel(n_ref, cs_ref, gs_ref, ys_hbm, pg_ref, x1_ref, mod_ref, g_ref, o_ref, yb, sem_g, *, tt, n_exp):
    i = pl.program_id(0)
    nsteps = pl.num_programs(0)
    d = o_ref.shape[1]
    half = d // 2
    c = half // LANES
    n_sorted = TOP_K * tt

    def issue(j):
        s = j & 1
        for e in range(n_exp):
            n = n_ref[j * n_exp + e]

            @pl.when(n > 0)
            def _():
                src = pl.multiple_of(gs_ref[j * n_exp + e], c)
                dst = pl.multiple_of(cs_ref[j * n_exp + e], c)
                pltpu.make_async_copy(ys_hbm.at[pl.ds(src, n), :], yb.at[s, pl.ds(dst, n), :],
                                      sem_g.at[s]).start(priority=e % 2)

    @pl.when(i == 0)
    def _():
        issue(0)

    @pl.when(i + 1 < nsteps)
    def _():
        issue(i + 1)

    slot = i & 1
    pltpu.make_async_copy(ys_hbm.at[pl.ds(0, n_sorted * c), :], yb.at[slot], sem_g.at[slot]).wait()

    lo, hi = _unpack(_load_rows(yb.at[slot], c))
    j = lax.broadcasted_iota(jnp.int32, (tt, n_sorted), 1)
    sel = jnp.zeros((tt, n_sorted), F32)
    for k in range(TOP_K):
        pos = pg_ref[:, 2 * TOP_K + k:2 * TOP_K + k + 1].astype(jnp.int32)
        sel = jnp.where(j == pos, pg_ref[:, TOP_K + k:TOP_K + k + 1], sel)
    sel_hi = sel.astype(BF16)
    sel_lo = (sel - sel_hi.astype(F32)).astype(BF16)
    lo_b = lo.astype(BF16)
    hi_b = hi.astype(BF16)
    acc_lo = _dot(sel_lo, lo_b) + _dot(sel_hi, lo_b)
    acc_hi = _dot(sel_lo, hi_b) + _dot(sel_hi, hi_b)
    ms = (jnp.sum(acc_lo * acc_lo, axis=-1, keepdims=True) + jnp.sum(acc_hi * acc_hi, axis=-1, keepdims=True)) / d
    inv = lax.rsqrt(ms + NORM_EPS)
    o_ref[:, :half] = x1_ref[:, :half] + mod_ref[5:6, :half] * (acc_lo * inv * g_ref[:, :half])
    o_ref[:, half:] = x1_ref[:, half:] + mod_ref[5:6, half:] * (acc_hi * inv * g_ref[:, half:])


def _combine(n_tab, cs_tab, gs_tab, ys, pg, x1, mod3, g, seq, tt, n_exp):
    t, d = x1.shape
    per = seq // tt
    c = d // 2 // LANES
    row = lambda i, *_: (i, 0)
    grid_spec = pltpu.PrefetchScalarGridSpec(
        num_scalar_prefetch=3,
        grid=(t // tt,),
        in_specs=[pl.BlockSpec(memory_space=pl.ANY),
                  pl.BlockSpec((tt, LANES), row),
                  pl.BlockSpec((tt, d), row),
                  pl.BlockSpec((None, 8, d), lambda i, *_: (i // per, 0, 0)),
                  pl.BlockSpec((1, d), lambda i, *_: (0, 0))],
        out_specs=pl.BlockSpec((tt, d), row),
        scratch_shapes=[pltpu.VMEM((2, TOP_K * tt * c, LANES), jnp.int32),
                        pltpu.SemaphoreType.DMA((2,))],
    )
    return pl.pallas_call(
        functools.partial(_combine_kernel, tt=tt, n_exp=n_exp),
        grid_spec=grid_spec,
        out_shape=jax.ShapeDtypeStruct((t, d), F32),
        compiler_params=_params(("arbitrary",)),
        name="combine",
    )(n_tab, cs_tab, gs_tab, ys, pg, x1, mod3, g)


def _pair_perm():
    half = ATT_Q_HEADS // 2
    idx = []
    for j in range(half):
        idx += list(range(j * ATT_HEAD_DIM, (j + 1) * ATT_HEAD_DIM))
        idx += list(range((half + j) * ATT_HEAD_DIM, (half + j + 1) * ATT_HEAD_DIM))
    return np.asarray(idx, np.int32)


def _layer(x, mod, g_pre_mix, g_post_mix, g_pre_ffn, g_post_ffn, w_in, w_gla_gate_up, b_gla_gate, g_gla_norm,
           sinks, w_out, w_router, b_router, w_mlp1, b_mlp1, w_mlp2, b_mlp2):
    nbatch, seq, d = x.shape
    t = nbatch * seq
    n_exp = w_router.shape[1]
    f = w_mlp2.shape[1]
    x2 = x.reshape(t, d)
    mod3 = jnp.pad(mod.reshape(nbatch, 6, d), ((0, 0), (0, 2), (0, 0)))

    perm = _pair_perm()
    n_main = C_GA - C_AKV
    w_r = jnp.concatenate([w_in[:, perm], w_in[:, ATT_WIDTH:ATT_WIDTH + n_main],
                           jnp.pad(w_in[:, ATT_WIDTH + n_main:], ((0, 0), (0, LANES - GLA_GATE_RANK)))],
                          axis=1).astype(BF16)
    wup = jnp.pad(w_gla_gate_up, ((0, LANES - GLA_GATE_RANK), (0, 0)))
    woa = w_out[:ATT_WIDTH][perm].astype(BF16)
    wog = w_out[ATT_WIDTH:].astype(BF16)
    wr = jnp.pad(w_router, ((0, 0), (0, LANES - n_exp)))
    wr_hi = wr.astype(BF16)
    wr = jnp.concatenate([wr_hi, (wr - wr_hi.astype(F32)).astype(BF16)], axis=1)
    br = jnp.pad(b_router, (0, LANES - n_exp)).reshape(1, LANES)
    w1g, w1l = _w1prep(w_mlp1)
    b1 = b_mlp1.reshape(n_exp, 1, f, 2)
    w2 = w_mlp2

    tm = min(512, seq)
    tt = min(256, seq)
    to = min(1024, seq)
    aq, akv, gq, gk, gv, gg, gl = _inproj(x2, mod3, g_pre_mix.reshape(1, d), w_r, wup,
                                          b_gla_gate.reshape(1, GLA_K_WIDTH), seq, min(1024, seq))
    att = _att(aq, akv, sinks, nbatch, seq, min(512, seq))
    gla = _gla(gq, gk, gv, gg, gl, g_gla_norm.reshape(1, GLA_DV), nbatch, seq, min(256, seq))
    x1, h2, route, cnt, tcnt = _outproj(att, gla, x2, mod3, g_post_mix.reshape(1, d), g_pre_ffn.reshape(1, d),
                                        woa, wog, wr, br, seq, to, n_exp, tt)

    rows = MOE_ROWS
    counts = cnt[0, :n_exp].astype(jnp.int32)
    padded = (counts + rows - 1) // rows * rows
    pend = jnp.cumsum(padded)
    pstart = pend - padded
    n_slots = t * TOP_K + n_exp * rows
    n_blocks = n_slots // rows
    n_used = (pend[-1] // rows).astype(jnp.int32).reshape(1)
    blk_ids = jnp.minimum(jnp.arange(n_blocks, dtype=jnp.int32), n_used - 1)
    block_e = jnp.minimum(jnp.sum(pend[None, :] <= (blk_ids * rows)[:, None], axis=1), n_exp - 1).astype(jnp.int32)
    block_valid = jnp.where(jnp.arange(n_blocks) < n_used,
                            jnp.clip(counts[block_e] - (blk_ids * rows - pstart[block_e]), 0, rows), 0).astype(jnp.int32)
    n_tiles = t // tt
    sub = d // 2 // LANES
    tile_cnt = tcnt[:, :to // tt, :n_exp].reshape(n_tiles, n_exp).astype(jnp.int32)
    run_start = jnp.cumsum(tile_cnt, axis=1) - tile_cnt
    before = jnp.cumsum(tile_cnt, axis=0) - tile_cnt
    n_tab = (tile_cnt * sub).reshape(-1)
    cs_tab = (run_start * sub).reshape(-1)
    gs_tab = ((pstart[None, :] + before) * sub).reshape(-1)
    pos = route[:, 2 * TOP_K:3 * TOP_K].astype(jnp.int32)
    prow = jnp.pad(pos.reshape(n_tiles, tt, TOP_K).transpose(0, 2, 1), ((0, 0), (0, 8 - TOP_K), (0, 0)),
                   constant_values=-1)

    xs = _dispatch(pend, n_used, n_tab, cs_tab, gs_tab, prow, h2, n_slots, rows, tt)
    ys = _moe(block_e, block_valid, n_used, xs, w1g, w1l, b1[..., 0], b1[..., 1], w2, b_mlp2.reshape(n_exp, 1, d), rows)
    out = _combine(n_tab, cs_tab, gs_tab, ys, route, x1, mod3, g_post_ffn.reshape(1, d), seq, tt, n_exp)
    return out.reshape(nbatch, seq, d)


def kernel(x, c, w_ada, b_ada, g_pre_mix, g_post_mix, g_pre_ffn, g_post_ffn, w_in, w_gla_gate_up, b_gla_gate,
           g_gla_norm, sinks, w_out, w_router, b_router, w_mlp1, b_mlp1, w_mlp2, b_mlp2):
    for l in range(w_in.shape[0]):
        mod = _ada(c, w_ada[l], b_ada[l])
        x = _layer(x, mod, g_pre_mix[l], g_post_mix[l], g_pre_ffn[l], g_post_ffn[l], w_in[l], w_gla_gate_up[l],
                   b_gla_gate[l], g_gla_norm[l], sinks[l], w_out[l], w_router[l], b_router[l], w_mlp1[l], b_mlp1[l],
                   w_mlp2[l], b_mlp2[l])
    return x
```

```python
import functools

import numpy as np
import jax
import jax.numpy as jnp
from jax import lax
from jax.experimental import pallas as pl
from jax.experimental.pallas import tpu as pltpu

F32 = jnp.float32
BF16 = jnp.bfloat16
HI = lax.Precision.HIGHEST

ATT_Q_HEADS = 8
ATT_KV_HEADS = 2
ATT_HEAD_DIM = 64
ATT_BLOCK = 128
GLA_HEADS = 4
GLA_DK = 64
GLA_DV = 128
GLA_GATE_RANK = 16
GLA_GATE_NORMALIZER = 16.0
GLA_CHUNK = 64
TOP_K = 4
SWIGLU_LIMIT = 7.0
SWIGLU_ALPHA = 1.702
NORM_EPS = 1e-6

LANES = 128
ATT_WIDTH = ATT_Q_HEADS * ATT_HEAD_DIM
ATT_KV_WIDTH = ATT_KV_HEADS * ATT_HEAD_DIM
GLA_K_WIDTH = GLA_HEADS * GLA_DK
GLA_V_WIDTH = GLA_HEADS * GLA_DV
C_AQ = 0
C_AKV = C_AQ + ATT_WIDTH
C_GQ = C_AKV + 2 * ATT_KV_WIDTH
C_GK = C_GQ + GLA_K_WIDTH
C_GV = C_GK + GLA_K_WIDTH
C_GG = C_GV + GLA_V_WIDTH
C_GA = C_GG + GLA_V_WIDTH
C_END = C_GA + LANES

MOE_ROWS = 512
VMEM_LIMIT = 48 * 1024 * 1024
MOE_VMEM_LIMIT = 56 * 1024 * 1024


def _dot(a, b, prec=None):
    return jnp.dot(a, b, preferred_element_type=F32, precision=prec)


def _dot_nt(a, b):
    return lax.dot_general(a, b, (((1,), (1,)), ((), ())), preferred_element_type=F32)


def _rms(t):
    return t * lax.rsqrt(jnp.mean(t * t, axis=-1, keepdims=True) + NORM_EPS)


def _params(sem):
    return pltpu.CompilerParams(dimension_semantics=sem, vmem_limit_bytes=VMEM_LIMIT)


HI16 = -65536


def _pack(a):
    half = a.shape[1] // 2
    lo = lax.bitcast_convert_type(a[:, :half].astype(BF16).astype(F32), jnp.int32)
    hi = lax.bitcast_convert_type(a[:, half:].astype(BF16).astype(F32), jnp.int32)
    return hi | lax.shift_right_logical(lo, 16)


def _pack_bf16_valued(a):
    half = a.shape[1] // 2
    lo = lax.bitcast_convert_type(a[:, :half], jnp.int32)
    hi = lax.bitcast_convert_type(a[:, half:], jnp.int32)
    return hi | lax.shift_right_logical(lo, 16)


def _unpack(p):
    lo = lax.bitcast_convert_type(lax.shift_left(p, 16), F32)
    hi = lax.bitcast_convert_type(p & HI16, F32)
    return lo, hi


def _store_rows(ref, p):
    n, w = p.shape
    c = w // LANES
    for s in range(c):
        ref[pl.ds(s, n, stride=c), :] = p[:, s * LANES:(s + 1) * LANES]


def _load_rows(ref, c):
    n = ref.shape[0] // c
    return jnp.concatenate([ref[pl.ds(s, n, stride=c), :] for s in range(c)], axis=1)


def _ada_kernel(c_ref, w_ref, b_ref, o_ref):
    c = c_ref[...]
    o_ref[...] = _dot(c * jax.nn.sigmoid(c), w_ref[...], HI) + b_ref[...]


def _ada(c, w, b):
    nb, d = c.shape
    n = w.shape[1]
    cp = jnp.zeros((8, d), F32).at[:nb].set(c)
    out = pl.pallas_call(
        _ada_kernel,
        grid=(n // d,),
        in_specs=[pl.BlockSpec((8, d), lambda j: (0, 0)),
                  pl.BlockSpec((d, d), lambda j: (0, j)),
                  pl.BlockSpec((1, d), lambda j: (0, j))],
        out_specs=pl.BlockSpec((8, d), lambda j: (0, j)),
        out_shape=jax.ShapeDtypeStruct((8, n), F32),
        compiler_params=_params(("parallel",)),
        name="ada",
    )(cp, w, b.reshape(1, n))
    return out[:nb]


def _inproj_kernel(x_ref, mod_ref, g_ref, w_ref, wup_ref, bup_ref,
                   aq_ref, akv_ref, gq_ref, gk_ref, gv_ref, gg_ref, gl_ref):
    half = x_ref.shape[0] // 2
    for r in (slice(0, half), slice(half, 2 * half)):
        h = _rms(x_ref[r, :]) * g_ref[...]
        h = h * (1.0 + mod_ref[1:2, :]) + mod_ref[0:1, :]
        p = _dot(h.astype(BF16), w_ref[...])
        aq_ref[r, :] = (p[:, C_AQ:C_AKV] * (ATT_HEAD_DIM ** -0.5)).astype(BF16)
        akv_ref[r, :] = p[:, C_AKV:C_GQ].astype(BF16)
        gq_ref[r, :] = (p[:, C_GQ:C_GK] * (GLA_DK ** -0.5)).astype(BF16)
        gk_ref[r, :] = p[:, C_GK:C_GV].astype(BF16)
        gv_ref[r, :] = p[:, C_GV:C_GG].astype(BF16)
        gg_ref[r, :] = p[:, C_GG:C_GA].astype(BF16)
        z = _dot(p[:, C_GA:C_END], wup_ref[...], HI) + bup_ref[...]
        gl_ref[r, :] = (jnp.minimum(z, 0.0) - jnp.log(1.0 + jnp.exp(-jnp.abs(z)))) * (1.0 / GLA_GATE_NORMALIZER)


def _inproj(x2, mod3, g, w_r, wup, bup, seq, tm):
    t, d = x2.shape
    per = seq // tm
    row = lambda i: (i, 0)
    fixed = lambda i: (0, 0)
    widths = (ATT_WIDTH, 2 * ATT_KV_WIDTH, GLA_K_WIDTH, GLA_K_WIDTH, GLA_V_WIDTH, GLA_V_WIDTH)
    out_shape = [jax.ShapeDtypeStruct((t, w), BF16) for w in widths] + [jax.ShapeDtypeStruct((t, GLA_K_WIDTH), F32)]
    out_specs = [pl.BlockSpec((tm, w), row) for w in widths] + [pl.BlockSpec((tm, GLA_K_WIDTH), row)]
    return pl.pallas_call(
        _inproj_kernel,
        grid=(t // tm,),
        in_specs=[pl.BlockSpec((tm, d), row),
                  pl.BlockSpec((None, 8, d), lambda i: (i // per, 0, 0)),
                  pl.BlockSpec((1, d), fixed),
                  pl.BlockSpec((d, C_END), fixed),
                  pl.BlockSpec((LANES, GLA_K_WIDTH), fixed),
                  pl.BlockSpec((1, GLA_K_WIDTH), fixed)],
        out_specs=out_specs,
        out_shape=out_shape,
        compiler_params=_params(("parallel",)),
        name="inproj",
    )(x2, mod3, g, w_r, wup, bup)


def _att_kernel(sinks_ref, q_ref, kv_ref, kvp_ref, o_ref, *, nblk):
    i = pl.program_id(1)
    blk = ATT_BLOCK
    lo = lax.broadcasted_iota(jnp.int32, (blk, LANES), 1) < ATT_HEAD_DIM
    qi = lax.broadcasted_iota(jnp.int32, (2 * blk, 2 * blk), 0) % blk
    kj = lax.broadcasted_iota(jnp.int32, (2 * blk, 2 * blk), 1)
    cur_ok = (kj >= blk) & ((kj - blk) <= qi)
    prev_ok = (kj < blk) & (kj > qi)
    first_off = jnp.where(i > 0, 0, blk)
    top = lax.broadcasted_iota(jnp.int32, (2 * blk, 1), 0) < blk
    for jb in range(nblk):
        r0 = jb * blk
        kvc = kv_ref[r0:r0 + blk, :]
        if jb == 0:
            kvp = kvp_ref[...]
            mask = cur_ok | (prev_ok & (kj >= first_off))
        else:
            kvp = kv_ref[r0 - blk:r0, :]
            mask = cur_ok | prev_ok
        kcat = jnp.concatenate([kvp[:, 0:ATT_KV_WIDTH], kvc[:, 0:ATT_KV_WIDTH]], axis=0)
        vcat = jnp.concatenate([kvp[:, ATT_KV_WIDTH:], kvc[:, ATT_KV_WIDTH:]], axis=0)
        for j in range(ATT_Q_HEADS // 2):
            qp = q_ref[r0:r0 + blk, j * LANES:(j + 1) * LANES]
            zero = jnp.zeros_like(qp)
            q2 = jnp.concatenate([jnp.where(lo, qp, zero), jnp.where(lo, zero, qp)], axis=0)
            s = jnp.where(mask, _dot_nt(q2, kcat), -jnp.inf)
            sink = jnp.where(top, sinks_ref[j], sinks_ref[ATT_Q_HEADS // 2 + j])
            m = jnp.maximum(jnp.max(s, axis=-1, keepdims=True), sink)
            p = jnp.exp(s - m)
            den = jnp.sum(p, axis=-1, keepdims=True) + jnp.exp(sink - m)
            o2 = _dot(p.astype(BF16), vcat) / den
            o = jnp.where(lo, o2[0:blk], o2[blk:2 * blk])
            o_ref[r0:r0 + blk, j * LANES:(j + 1) * LANES] = o.astype(BF16)


def _att(aq, akv, sinks, nbatch, seq, ta):
    t = aq.shape[0]
    nblk = ta // ATT_BLOCK
    per = seq // ta
    perb = seq // ATT_BLOCK
    return pl.pallas_call(
        functools.partial(_att_kernel, nblk=nblk),
        grid=(nbatch, per),
        in_specs=[pl.BlockSpec(memory_space=pltpu.SMEM),
                  pl.BlockSpec((ta, ATT_WIDTH), lambda b, i: (b * per + i, 0)),
                  pl.BlockSpec((ta, 2 * ATT_KV_WIDTH), lambda b, i: (b * per + i, 0)),
                  pl.BlockSpec((ATT_BLOCK, 2 * ATT_KV_WIDTH),
                               lambda b, i: (b * perb + jnp.maximum(i * nblk - 1, 0), 0))],
        out_specs=pl.BlockSpec((ta, ATT_WIDTH), lambda b, i: (b * per + i, 0)),
        out_shape=jax.ShapeDtypeStruct((t, ATT_WIDTH), BF16),
        compiler_params=_params(("parallel", "parallel")),
        name="att",
    )(sinks, aq, akv, akv)


def _gla_kernel(gq_ref, gk_ref, gv_ref, gg_ref, gl_ref, gn_ref, o_ref, st_ref, *, nchunk):
    ch = GLA_CHUNK
    tg = nchunk * ch
    nbatch = gq_ref.shape[0]

    @pl.when(pl.program_id(0) == 0)
    def _():
        st_ref[...] = jnp.zeros_like(st_ref)

    ri = lax.broadcasted_iota(jnp.int32, (tg, tg), 0)
    ci = lax.broadcasted_iota(jnp.int32, (tg, tg), 1)
    tri = ((ri // ch == ci // ch) & (ci <= ri)).astype(BF16)
    bts = []
    for s in range(nbatch):
        g0 = gl_ref[s]
        g1 = g0 - g0.astype(BF16).astype(F32)
        g2 = g1 - g1.astype(BF16).astype(F32)
        parts = _dot(tri, jnp.concatenate([g0.astype(BF16), g1.astype(BF16), g2.astype(BF16)], axis=1))
        bts.append((parts[:, 2 * GLA_K_WIDTH:] + parts[:, GLA_K_WIDTH:2 * GLA_K_WIDTH]) + parts[:, :GLA_K_WIDTH])
    row2 = lax.broadcasted_iota(jnp.int32, (2 * ch, LANES), 0)
    lane2 = lax.broadcasted_iota(jnp.int32, (2 * ch, LANES), 1)
    own = row2 // ch == lane2 // GLA_DK
    causal = (row2 // ch == lane2 // ch) & (lane2 % ch <= row2 % ch)
    gn = gn_ref[...]
    zero = jnp.zeros((2 * ch, LANES), BF16)

    def both(x):
        return jnp.where(own, jnp.concatenate([x, x], axis=0), zero)

    for c in range(nchunk):
        r0 = c * ch
        for s in range(nbatch):
            b = bts[s][r0:r0 + ch]
            bm = b[ch // 2 - 1:ch // 2]
            bl = b[ch - 1:ch]
            q = gq_ref[s, r0:r0 + ch, :].astype(F32)
            k = gk_ref[s, r0:r0 + ch, :].astype(F32)
            qe = (q * jnp.exp(b - bm)).astype(BF16)
            ke = (k * jnp.exp(bm - b)).astype(BF16)
            qs = (q * jnp.exp(b)).astype(BF16)
            kl = (k * jnp.exp(bl - b)).astype(BF16)
            dec = jnp.exp(bl)
            for p in range(GLA_HEADS // 2):
                sl = slice(p * LANES, (p + 1) * LANES)
                h0 = slice(2 * p * GLA_DV, (2 * p + 1) * GLA_DV)
                h1 = slice((2 * p + 1) * GLA_DV, (2 * p + 2) * GLA_DV)
                st = st_ref[s, p]
                a = _dot_nt(both(qe[:, sl]), both(ke[:, sl]))
                a = jnp.where(causal, a, 0.0).astype(BF16)
                v2 = jnp.concatenate([gv_ref[s, r0:r0 + ch, h0], gv_ref[s, r0:r0 + ch, h1]], axis=0)
                o = _dot(a, v2) + _dot_nt(both(qs[:, sl]), st.astype(BF16))
                upd = lax.dot_general(v2, both(kl[:, sl]), (((0,), (0,)), ((), ())),
                                      preferred_element_type=F32)
                gg = jnp.concatenate([gg_ref[s, r0:r0 + ch, h0], gg_ref[s, r0:r0 + ch, h1]], axis=0).astype(F32)
                res = (_rms(o) * gn * (gg * jax.nn.sigmoid(gg))).astype(BF16)
                o_ref[s, r0:r0 + ch, h0] = res[:ch]
                o_ref[s, r0:r0 + ch, h1] = res[ch:]
                st_ref[s, p] = st * dec[:, sl] + upd


def _gla(gq, gk, gv, gg, gl, gn, nbatch, seq, tg):
    t = gq.shape[0]
    tile = lambda w: pl.BlockSpec((nbatch, tg, w), lambda i: (0, i, 0))
    as3d = lambda a: a.reshape(nbatch, seq, a.shape[1])
    out = pl.pallas_call(
        functools.partial(_gla_kernel, nchunk=tg // GLA_CHUNK),
        grid=(seq // tg,),
        in_specs=[tile(GLA_K_WIDTH), tile(GLA_K_WIDTH), tile(GLA_V_WIDTH), tile(GLA_V_WIDTH), tile(GLA_K_WIDTH),
                  pl.BlockSpec((1, GLA_DV), lambda i: (0, 0))],
        out_specs=tile(GLA_V_WIDTH),
        out_shape=jax.ShapeDtypeStruct((nbatch, seq, GLA_V_WIDTH), BF16),
        scratch_shapes=[pltpu.VMEM((nbatch, GLA_HEADS // 2, GLA_DV, LANES), F32)],
        compiler_params=_params(("arbitrary",)),
        name="gla",
    )(as3d(gq), as3d(gk), as3d(gv), as3d(gg), as3d(gl), gn)
    return out.reshape(t, GLA_V_WIDTH)


def _outproj_kernel(att_ref, gla_ref, x_ref, mod_ref, gpost_ref, gpre_ref, woa_ref, wog_ref, wr_ref, br_ref,
                    x1_ref, h2_ref, route_ref, cnt_ref, tcnt_ref, base_ref, *, n_exp, tt, tc):
    tm = x_ref.shape[0]

    @pl.when(pl.program_id(0) == 0)
    def _():
        base_ref[...] = jnp.zeros_like(base_ref)

    nsub = tc // tt
    lane = lax.broadcasted_iota(jnp.int32, (tc, LANES), 1)
    ri = lax.broadcasted_iota(jnp.int32, (tc, tc), 0)
    ci = lax.broadcasted_iota(jnp.int32, (tc, tc), 1)
    below = ((ci < ri) & (ri // tt == ci // tt)).astype(BF16)
    lower = (lax.broadcasted_iota(jnp.int32, (LANES, LANES), 0)
             < lax.broadcasted_iota(jnp.int32, (LANES, LANES), 1)).astype(BF16)
    sub_row = lax.broadcasted_iota(jnp.int32, (8, LANES), 0)
    tok_sub = lax.broadcasted_iota(jnp.int32, (tc, LANES), 0) // tt
    all_cnt = jnp.zeros((8, LANES), F32)
    for ch in range(tm // tc):
        r = slice(ch * tc, (ch + 1) * tc)
        y = _dot(att_ref[r, :], woa_ref[...]) + _dot(gla_ref[r, :], wog_ref[...])
        x1 = x_ref[r, :] + mod_ref[2:3, :] * (_rms(y) * gpost_ref[...])
        x1_ref[r, :] = x1
        h2 = _rms(x1) * gpre_ref[...]
        h2 = h2 * (1.0 + mod_ref[4:5, :]) + mod_ref[3:4, :]
        h2_hi = h2.astype(BF16)
        h2_ref[r, :] = h2_hi

        h2_lo = (h2 - h2_hi.astype(F32)).astype(BF16)
        p_hi = _dot(h2_hi, wr_ref[...])
        p_lo = _dot(h2_lo, wr_ref[...])
        logits = ((p_lo[:, LANES:] + p_lo[:, :LANES]) + p_hi[:, LANES:]) + p_hi[:, :LANES] + br_ref[...]
        vals = jnp.where(lane < n_exp, logits, -jnp.inf)
        sels, tops, idxs = [], [], []
        for _ in range(TOP_K):
            m = jnp.max(vals, axis=-1, keepdims=True)
            idx = jnp.min(jnp.where(vals == m, lane, LANES), axis=-1, keepdims=True)
            sel = lane == idx
            vals = jnp.where(sel, -jnp.inf, vals)
            sels.append(sel)
            tops.append(m)
            idxs.append(idx)
        es = [jnp.exp(m - tops[0]) for m in tops]
        tot = es[0] + es[1] + es[2] + es[3]
        onehot = jnp.zeros((tc, LANES), F32)
        for sel in sels:
            onehot = onehot + sel.astype(F32)
        earlier = _dot(below, onehot.astype(BF16))
        sub_cnt = jnp.zeros((8, LANES), F32)
        for h in range(nsub):
            sub_cnt = jnp.where(sub_row == h, jnp.sum(onehot[h * tt:(h + 1) * tt], axis=0, keepdims=True), sub_cnt)
        run_start = _dot(sub_cnt.astype(BF16), lower)
        pos_all = earlier
        for h in range(nsub):
            pos_all = pos_all + jnp.where(tok_sub == h, run_start[h:h + 1, :], 0.0)
            all_cnt = jnp.where(sub_row == ch * nsub + h, sub_cnt[h:h + 1, :], all_cnt)
        route = jnp.zeros((tc, LANES), F32)
        for k in range(TOP_K):
            pos = jnp.sum(jnp.where(sels[k], pos_all, 0.0), axis=-1, keepdims=True)
            route = jnp.where(lane == k, idxs[k].astype(F32), route)
            route = jnp.where(lane == TOP_K + k, es[k] / tot, route)
            route = jnp.where(lane == 2 * TOP_K + k, pos, route)
        route_ref[r, :] = route
    tcnt_ref[...] = all_cnt
    base = base_ref[...] + jnp.sum(all_cnt, axis=0, keepdims=True)
    base_ref[...] = base
    cnt_ref[...] = jnp.broadcast_to(base, cnt_ref.shape)


def _outproj(att, gla, x2, mod3, gpost, gpre, woa, wog, wr, br, seq, tm, n_exp, tt):
    t, d = x2.shape
    per = seq // tm
    row = lambda i: (i, 0)
    fixed = lambda i: (0, 0)
    return pl.pallas_call(
        functools.partial(_outproj_kernel, n_exp=n_exp, tt=tt, tc=min(512, tm)),
        grid=(t // tm,),
        in_specs=[pl.BlockSpec((tm, ATT_WIDTH), row),
                  pl.BlockSpec((tm, GLA_V_WIDTH), row),
                  pl.BlockSpec((tm, d), row),
                  pl.BlockSpec((None, 8, d), lambda i: (i // per, 0, 0)),
                  pl.BlockSpec((1, d), fixed),
                  pl.BlockSpec((1, d), fixed),
                  pl.BlockSpec((ATT_WIDTH, d), fixed),
                  pl.BlockSpec((GLA_V_WIDTH, d), fixed),
                  pl.BlockSpec((d, 2 * LANES), fixed),
                  pl.BlockSpec((1, LANES), fixed)],
        out_specs=[pl.BlockSpec((tm, d), row),
                   pl.BlockSpec((tm, d), row),
                   pl.BlockSpec((tm, LANES), row),
                   pl.BlockSpec((8, LANES), fixed),
                   pl.BlockSpec((None, 8, LANES), lambda i: (i, 0, 0))],
        out_shape=[jax.ShapeDtypeStruct((t, d), F32),
                   jax.ShapeDtypeStruct((t, d), BF16),
                   jax.ShapeDtypeStruct((t, LANES), F32),
                   jax.ShapeDtypeStruct((8, LANES), F32),
                   jax.ShapeDtypeStruct((t // tm, 8, LANES), F32)],
        scratch_shapes=[pltpu.VMEM((1, LANES), F32)],
        compiler_params=_params(("arbitrary",)),
        name="outproj",
    )(att, gla, x2, mod3, gpost, gpre, woa, wog, wr, br)


def _dispatch_kernel(pend_ref, nu_ref, n_ref, cs_ref, gs_ref, prow_ref, h2_ref, xs_hbm, sb, zbuf, sem_r, sem_z,
                     *, tt, rows, n_exp, n_blocks):
    i = pl.program_id(0)
    nsteps = pl.num_programs(0)
    n_sorted = TOP_K * tt
    c = sb.shape[1] // n_sorted
    blk_rows = rows * c
    slot = i & 1

    def zero_fill(start):
        return pltpu.make_async_copy(zbuf, xs_hbm.at[pl.ds(pl.multiple_of(start, blk_rows), blk_rows), :], sem_z)

    def wait_runs(s):
        pltpu.make_async_copy(sb.at[s], xs_hbm.at[pl.ds(0, n_sorted * c), :], sem_r.at[s]).wait()

    @pl.when(i == 0)
    def _():
        zbuf[...] = jnp.zeros_like(zbuf)
        for phase in range(2):
            for e in range(n_exp):
                hi = pend_ref[e]
                lo = pend_ref[e - 1] if e > 0 else 0
                tail = nu_ref[0] + e
                for pred, start in ((hi > lo, (hi - rows) * c), (tail < n_blocks, tail * blk_rows)):
                    @pl.when(pred)
                    def _():
                        if phase == 0:
                            zero_fill(start).start()
                        else:
                            zero_fill(start).wait()

    @pl.when(i >= 2)
    def _():
        wait_runs(slot)

    j = lax.broadcasted_iota(jnp.int32, (n_sorted, tt), 0)
    perm = jnp.zeros((n_sorted, tt), F32)
    for k in range(TOP_K):
        perm = jnp.where(j == prow_ref[k:k + 1, :], 1.0, perm)
    srt = _dot(perm.astype(BF16), h2_ref[...])
    _store_rows(sb.at[slot], _pack_bf16_valued(srt))

    for e in range(n_exp):
        n = n_ref[i * n_exp + e]

        @pl.when(n > 0)
        def _():
            src = pl.multiple_of(cs_ref[i * n_exp + e], c)
            dst = pl.multiple_of(gs_ref[i * n_exp + e], c)
            pltpu.make_async_copy(sb.at[slot, pl.ds(src, n), :], xs_hbm.at[pl.ds(dst, n), :],
                                  sem_r.at[slot]).start(priority=e % 2)

    @pl.when(i == nsteps - 1)
    def _():
        @pl.when(nsteps > 1)
        def _():
            wait_runs(1 - slot)
        wait_runs(slot)


def _dispatch(pend, n_used, n_tab, cs_tab, gs_tab, prow, h2, n_slots, rows, tt):
    t, d = h2.shape
    c = d // 2 // LANES
    n_exp = pend.shape[0]
    grid_spec = pltpu.PrefetchScalarGridSpec(
        num_scalar_prefetch=5,
        grid=(t // tt,),
        in_specs=[pl.BlockSpec((None, 8, tt), lambda i, *_: (i, 0, 0)),
                  pl.BlockSpec((tt, d), lambda i, *_: (i, 0))],
        out_specs=pl.BlockSpec(memory_space=pl.ANY),
        scratch_shapes=[pltpu.VMEM((2, TOP_K * tt * c, LANES), jnp.int32),
                        pltpu.VMEM((rows * c, LANES), jnp.int32),
                        pltpu.SemaphoreType.DMA((2,)),
                        pltpu.SemaphoreType.DMA],
    )
    return pl.pallas_call(
        functools.partial(_dispatch_kernel, tt=tt, rows=rows, n_exp=n_exp, n_blocks=n_slots // rows),
        grid_spec=grid_spec,
        out_shape=jax.ShapeDtypeStruct((n_slots * c, LANES), jnp.int32),
        compiler_params=_params(("arbitrary",)),
        name="dispatch",
    )(pend, n_used, n_tab, cs_tab, gs_tab, prow, h2)


def _moe_kernel(be_ref, nu_ref, xs_ref, w1_ref, b1g_ref, b1l_ref, w2_ref, b2_ref, ys_ref, w1g, w1l, *, rows):
    i = pl.program_id(0)
    c = xs_ref.shape[0] // rows
    half = c * LANES

    @pl.when((i < nu_ref[0]) & ((i == 0) | (be_ref[i] != be_ref[jnp.maximum(i - 1, 0)])))
    def _():
        n = 2 * LANES
        r = lax.broadcasted_iota(jnp.int32, (n, n), 0)
        col = lax.broadcasted_iota(jnp.int32, (n, n), 1)
        perm = (r == jnp.where(col < LANES, 2 * col, 2 * (col - LANES) + 1)).astype(BF16)
        for j in range(w1g.shape[1] // LANES):
            d = _dot(w1_ref[:, j * n:(j + 1) * n].astype(BF16), perm)
            w1g[:, j * LANES:(j + 1) * LANES] = d[:, :LANES].astype(BF16)
            w1l[:, j * LANES:(j + 1) * LANES] = d[:, LANES:].astype(BF16)

    @pl.when(i < nu_ref[0])
    def _():
        lo, hi = _unpack(_load_rows(xs_ref, c))
        xl = lo.astype(BF16)
        xh = hi.astype(BF16)
        glu = _dot(xl, w1g[:half, :]) + _dot(xh, w1g[half:, :]) + b1g_ref[...]
        lin = _dot(xl, w1l[:half, :]) + _dot(xh, w1l[half:, :]) + b1l_ref[...]
        glu = jnp.minimum(glu, SWIGLU_LIMIT)
        lin = jnp.clip(lin, -SWIGLU_LIMIT, SWIGLU_LIMIT)
        a = glu * jax.nn.sigmoid(SWIGLU_ALPHA * glu) * (lin + 1.0)
        _store_rows(ys_ref, _pack(_dot(a.astype(BF16), w2_ref[...].astype(BF16)) + b2_ref[...]))

    @pl.when(i >= nu_ref[0])
    def _():
        ys_ref[...] = jnp.zeros_like(ys_ref)


def _moe(block_e, n_used, xs, w1, b1g, b1l, w2, b2, rows):
    d, f2 = w1.shape[1], w1.shape[2]
    f = f2 // 2
    c = d // 2 // LANES
    nb = xs.shape[0] // (rows * c)
    wsel = lambda i, be, nu: (be[i], 0, 0)
    grid_spec = pltpu.PrefetchScalarGridSpec(
        num_scalar_prefetch=2,
        grid=(nb,),
        in_specs=[pl.BlockSpec((rows * c, LANES), lambda i, be, nu: (jnp.minimum(i, nu[0] - 1), 0)),
                  pl.BlockSpec((None, d, f2), wsel),
                  pl.BlockSpec((None, 1, f), wsel),
                  pl.BlockSpec((None, 1, f), wsel),
                  pl.BlockSpec((None, f, d), wsel),
                  pl.BlockSpec((None, 1, d), wsel)],
        out_specs=pl.BlockSpec((rows * c, LANES), lambda i, be, nu: (i, 0)),
        scratch_shapes=[pltpu.VMEM((d, f), BF16), pltpu.VMEM((d, f), BF16)],
    )
    return pl.pallas_call(
        functools.partial(_moe_kernel, rows=rows),
        grid_spec=grid_spec,
        out_shape=jax.ShapeDtypeStruct(xs.shape, jnp.int32),
        compiler_params=pltpu.CompilerParams(dimension_semantics=("arbitrary",), vmem_limit_bytes=MOE_VMEM_LIMIT),
        name="moe",
    )(block_e, n_used, xs, w1, b1g, b1l, w2, b2)


def _combine_kernel(n_ref, cs_ref, gs_ref, ys_hbm, pg_ref, x1_ref, mod_ref, g_ref, o_ref, yb, sem_g, *, tt, n_exp):
    i = pl.program_id(0)
    nsteps = pl.num_programs(0)
    d = o_ref.shape[1]
    half = d // 2
    c = half // LANES
    n_sorted = TOP_K * tt

    def issue(j):
        s = j & 1
        for e in range(n_exp):
            n = n_ref[j * n_exp + e]

            @pl.when(n > 0)
            def _():
                src = pl.multiple_of(gs_ref[j * n_exp + e], c)
                dst = pl.multiple_of(cs_ref[j * n_exp + e], c)
                pltpu.make_async_copy(ys_hbm.at[pl.ds(src, n), :], yb.at[s, pl.ds(dst, n), :],
                                      sem_g.at[s]).start(priority=e % 2)

    @pl.when(i == 0)
    def _():
        issue(0)

    @pl.when(i + 1 < nsteps)
    def _():
        issue(i + 1)

    slot = i & 1
    pltpu.make_async_copy(ys_hbm.at[pl.ds(0, n_sorted * c), :], yb.at[slot], sem_g.at[slot]).wait()

    lo, hi = _unpack(_load_rows(yb.at[slot], c))
    j = lax.broadcasted_iota(jnp.int32, (tt, n_sorted), 1)
    sel = jnp.zeros((tt, n_sorted), F32)
    for k in range(TOP_K):
        pos = pg_ref[:, 2 * TOP_K + k:2 * TOP_K + k + 1].astype(jnp.int32)
        sel = jnp.where(j == pos, pg_ref[:, TOP_K + k:TOP_K + k + 1], sel)
    sel_hi = sel.astype(BF16)
    sel_lo = (sel - sel_hi.astype(F32)).astype(BF16)
    lo_b = lo.astype(BF16)
    hi_b = hi.astype(BF16)
    acc_lo = _dot(sel_lo, lo_b) + _dot(sel_hi, lo_b)
    acc_hi = _dot(sel_lo, hi_b) + _dot(sel_hi, hi_b)
    ms = (jnp.sum(acc_lo * acc_lo, axis=-1, keepdims=True) + jnp.sum(acc_hi * acc_hi, axis=-1, keepdims=True)) / d
    inv = lax.rsqrt(ms + NORM_EPS)
    o_ref[:, :half] = x1_ref[:, :half] + mod_ref[5:6, :half] * (acc_lo * inv * g_ref[:, :half])
    o_ref[:, half:] = x1_ref[:, half:] + mod_ref[5:6, half:] * (acc_hi * inv * g_ref[:, half:])


def _combine(n_tab, cs_tab, gs_tab, ys, pg, x1, mod3, g, seq, tt, n_exp):
    t, d = x1.shape
    per = seq // tt
    c = d // 2 // LANES
    row = lambda i, *_: (i, 0)
    grid_spec = pltpu.PrefetchScalarGridSpec(
        num_scalar_prefetch=3,
        grid=(t // tt,),
        in_specs=[pl.BlockSpec(memory_space=pl.ANY),
                  pl.BlockSpec((tt, LANES), row),
                  pl.BlockSpec((tt, d), row),
                  pl.BlockSpec((None, 8, d), lambda i, *_: (i // per, 0, 0)),
                  pl.BlockSpec((1, d), lambda i, *_: (0, 0))],
        out_specs=pl.BlockSpec((tt, d), row),
        scratch_shapes=[pltpu.VMEM((2, TOP_K * tt * c, LANES), jnp.int32),
                        pltpu.SemaphoreType.DMA((2,))],
    )
    return pl.pallas_call(
        functools.partial(_combine_kernel, tt=tt, n_exp=n_exp),
        grid_spec=grid_spec,
        out_shape=jax.ShapeDtypeStruct((t, d), F32),
        compiler_params=_params(("arbitrary",)),
        name="combine",
    )(n_tab, cs_tab, gs_tab, ys, pg, x1, mod3, g)


def _pair_perm():
    half = ATT_Q_HEADS // 2
    idx = []
    for j in range(half):
        idx += list(range(j * ATT_HEAD_DIM, (j + 1) * ATT_HEAD_DIM))
        idx += list(range((half + j) * ATT_HEAD_DIM, (half + j + 1) * ATT_HEAD_DIM))
    return np.asarray(idx, np.int32)


def _layer(x, mod, g_pre_mix, g_post_mix, g_pre_ffn, g_post_ffn, w_in, w_gla_gate_up, b_gla_gate, g_gla_norm,
           sinks, w_out, w_router, b_router, w_mlp1, b_mlp1, w_mlp2, b_mlp2):
    nbatch, seq, d = x.shape
    t = nbatch * seq
    n_exp = w_router.shape[1]
    f = w_mlp2.shape[1]
    x2 = x.reshape(t, d)
    mod3 = jnp.pad(mod.reshape(nbatch, 6, d), ((0, 0), (0, 2), (0, 0)))

    perm = _pair_perm()
    n_main = C_GA - C_AKV
    w_r = jnp.concatenate([w_in[:, perm], w_in[:, ATT_WIDTH:ATT_WIDTH + n_main],
                           jnp.pad(w_in[:, ATT_WIDTH + n_main:], ((0, 0), (0, LANES - GLA_GATE_RANK)))],
                          axis=1).astype(BF16)
    wup = jnp.pad(w_gla_gate_up, ((0, LANES - GLA_GATE_RANK), (0, 0)))
    woa = w_out[:ATT_WIDTH][perm].astype(BF16)
    wog = w_out[ATT_WIDTH:].astype(BF16)
    wr = jnp.pad(w_router, ((0, 0), (0, LANES - n_exp)))
    wr_hi = wr.astype(BF16)
    wr = jnp.concatenate([wr_hi, (wr - wr_hi.astype(F32)).astype(BF16)], axis=1)
    br = jnp.pad(b_router, (0, LANES - n_exp)).reshape(1, LANES)
    b1 = b_mlp1.reshape(n_exp, 1, f, 2)

    tm = min(512, seq)
    tt = min(256, seq)
    to = min(1024, seq)
    aq, akv, gq, gk, gv, gg, gl = _inproj(x2, mod3, g_pre_mix.reshape(1, d), w_r, wup,
                                          b_gla_gate.reshape(1, GLA_K_WIDTH), seq, min(1024, seq))
    att = _att(aq, akv, sinks, nbatch, seq, min(512, seq))
    gla = _gla(gq, gk, gv, gg, gl, g_gla_norm.reshape(1, GLA_DV), nbatch, seq, min(256, seq))
    x1, h2, route, cnt, tcnt = _outproj(att, gla, x2, mod3, g_post_mix.reshape(1, d), g_pre_ffn.reshape(1, d),
                                        woa, wog, wr, br, seq, to, n_exp, tt)

    rows = MOE_ROWS
    counts = cnt[0, :n_exp].astype(jnp.int32)
    padded = (counts + rows - 1) // rows * rows
    pend = jnp.cumsum(padded)
    pstart = pend - padded
    n_slots = t * TOP_K + n_exp * rows
    n_blocks = n_slots // rows
    n_used = (pend[-1] // rows).astype(jnp.int32).reshape(1)
    blk_ids = jnp.minimum(jnp.arange(n_blocks, dtype=jnp.int32), n_used - 1)
    block_e = jnp.minimum(jnp.sum(pend[None, :] <= (blk_ids * rows)[:, None], axis=1), n_exp - 1).astype(jnp.int32)
    n_tiles = t // tt
    sub = d // 2 // LANES
    tile_cnt = tcnt[:, :to // tt, :n_exp].reshape(n_tiles, n_exp).astype(jnp.int32)
    run_start = jnp.cumsum(tile_cnt, axis=1) - tile_cnt
    before = jnp.cumsum(tile_cnt, axis=0) - tile_cnt
    n_tab = (tile_cnt * sub).reshape(-1)
    cs_tab = (run_start * sub).reshape(-1)
    gs_tab = ((pstart[None, :] + before) * sub).reshape(-1)
    pos = route[:, 2 * TOP_K:3 * TOP_K].astype(jnp.int32)
    prow = jnp.pad(pos.reshape(n_tiles, tt, TOP_K).transpose(0, 2, 1), ((0, 0), (0, 8 - TOP_K), (0, 0)),
                   constant_values=-1)

    xs = _dispatch(pend, n_used, n_tab, cs_tab, gs_tab, prow, h2, n_slots, rows, tt)
    ys = _moe(block_e, n_used, xs, w_mlp1, b1[..., 0], b1[..., 1], w_mlp2, b_mlp2.reshape(n_exp, 1, d), rows)
    out = _combine(n_tab, cs_tab, gs_tab, ys, route, x1, mod3, g_post_ffn.reshape(1, d), seq, tt, n_exp)
    return out.reshape(nbatch, seq, d)


def kernel(x, c, w_ada, b_ada, g_pre_mix, g_post_mix, g_pre_ffn, g_post_ffn, w_in, w_gla_gate_up, b_gla_gate,
           g_gla_norm, sinks, w_out, w_router, b_router, w_mlp1, b_mlp1, w_mlp2, b_mlp2):
    for l in range(w_in.shape[0]):
        mod = _ada(c, w_ada[l], b_ada[l])
        x = _layer(x, mod, g_pre_mix[l], g_post_mix[l], g_pre_ffn[l], g_post_ffn[l], w_in[l], w_gla_gate_up[l],
                   b_gla_gate[l], g_gla_norm[l], sinks[l], w_out[l], w_router[l], b_router[l], w_mlp1[l], b_mlp1[l],
                   w_mlp2[l], b_mlp2[l])
    return x
```

```python
import functools

import numpy as np
import jax
import jax.numpy as jnp
from jax import lax
from jax.experimental import pallas as pl
from jax.experimental.pallas import tpu as pltpu

F32 = jnp.float32
BF16 = jnp.bfloat16
HI = lax.Precision.HIGHEST

ATT_Q_HEADS = 8
ATT_KV_HEADS = 2
ATT_HEAD_DIM = 64
ATT_BLOCK = 128
GLA_HEADS = 4
GLA_DK = 64
GLA_DV = 128
GLA_GATE_RANK = 16
GLA_GATE_NORMALIZER = 16.0
GLA_CHUNK = 64
TOP_K = 4
SWIGLU_LIMIT = 7.0
SWIGLU_ALPHA = 1.702
NORM_EPS = 1e-6

LANES = 128
ATT_WIDTH = ATT_Q_HEADS * ATT_HEAD_DIM
ATT_KV_WIDTH = ATT_KV_HEADS * ATT_HEAD_DIM
GLA_K_WIDTH = GLA_HEADS * GLA_DK
GLA_V_WIDTH = GLA_HEADS * GLA_DV
C_AQ = 0
C_AKV = C_AQ + ATT_WIDTH
C_GQ = C_AKV + 2 * ATT_KV_WIDTH
C_GK = C_GQ + GLA_K_WIDTH
C_GV = C_GK + GLA_K_WIDTH
C_GG = C_GV + GLA_V_WIDTH
C_GA = C_GG + GLA_V_WIDTH
C_END = C_GA + LANES

MOE_ROWS = 512
VMEM_LIMIT = 48 * 1024 * 1024
MOE_VMEM_LIMIT = 56 * 1024 * 1024


def _dot(a, b, prec=None):
    return jnp.dot(a, b, preferred_element_type=F32, precision=prec)


def _dot_nt(a, b):
    return lax.dot_general(a, b, (((1,), (1,)), ((), ())), preferred_element_type=F32)


def _rms(t):
    return t * lax.rsqrt(jnp.mean(t * t, axis=-1, keepdims=True) + NORM_EPS)


def _params(sem):
    return pltpu.CompilerParams(dimension_semantics=sem, vmem_limit_bytes=VMEM_LIMIT)


HI16 = -65536


def _pack(a):
    half = a.shape[1] // 2
    lo = lax.bitcast_convert_type(a[:, :half].astype(BF16).astype(F32), jnp.int32)
    hi = lax.bitcast_convert_type(a[:, half:].astype(BF16).astype(F32), jnp.int32)
    return hi | lax.shift_right_logical(lo, 16)


def _pack_bf16_valued(a):
    half = a.shape[1] // 2
    lo = lax.bitcast_convert_type(a[:, :half], jnp.int32)
    hi = lax.bitcast_convert_type(a[:, half:], jnp.int32)
    return hi | lax.shift_right_logical(lo, 16)


def _unpack(p):
    lo = lax.bitcast_convert_type(lax.shift_left(p, 16), F32)
    hi = lax.bitcast_convert_type(p & HI16, F32)
    return lo, hi


def _store_rows(ref, p):
    n, w = p.shape
    c = w // LANES
    for s in range(c):
        ref[pl.ds(s, n, stride=c), :] = p[:, s * LANES:(s + 1) * LANES]


def _load_rows(ref, c):
    n = ref.shape[0] // c
    return jnp.concatenate([ref[pl.ds(s, n, stride=c), :] for s in range(c)], axis=1)


def _ada_kernel(c_ref, w_ref, b_ref, o_ref):
    c = c_ref[...]
    o_ref[...] = _dot(c * jax.nn.sigmoid(c), w_ref[...], HI) + b_ref[...]


def _ada(c, w, b):
    nb, d = c.shape
    n = w.shape[1]
    cp = jnp.zeros((8, d), F32).at[:nb].set(c)
    out = pl.pallas_call(
        _ada_kernel,
        grid=(n // d,),
        in_specs=[pl.BlockSpec((8, d), lambda j: (0, 0)),
                  pl.BlockSpec((d, d), lambda j: (0, j)),
                  pl.BlockSpec((1, d), lambda j: (0, j))],
        out_specs=pl.BlockSpec((8, d), lambda j: (0, j)),
        out_shape=jax.ShapeDtypeStruct((8, n), F32),
        compiler_params=_params(("parallel",)),
        name="ada",
    )(cp, w, b.reshape(1, n))
    return out[:nb]


def _inproj_kernel(x_ref, mod_ref, g_ref, w_ref, wup_ref, bup_ref,
                   aq_ref, akv_ref, gq_ref, gk_ref, gv_ref, gg_ref, gl_ref):
    half = x_ref.shape[0] // 2
    for r in (slice(0, half), slice(half, 2 * half)):
        h = _rms(x_ref[r, :]) * g_ref[...]
        h = h * (1.0 + mod_ref[1:2, :]) + mod_ref[0:1, :]
        p = _dot(h.astype(BF16), w_ref[...])
        aq_ref[r, :] = (p[:, C_AQ:C_AKV] * (ATT_HEAD_DIM ** -0.5)).astype(BF16)
        akv_ref[r, :] = p[:, C_AKV:C_GQ].astype(BF16)
        gq_ref[r, :] = (p[:, C_GQ:C_GK] * (GLA_DK ** -0.5)).astype(BF16)
        gk_ref[r, :] = p[:, C_GK:C_GV].astype(BF16)
        gv_ref[r, :] = p[:, C_GV:C_GG].astype(BF16)
        gg_ref[r, :] = p[:, C_GG:C_GA].astype(BF16)
        z = _dot(p[:, C_GA:C_END], wup_ref[...], HI) + bup_ref[...]
        gl_ref[r, :] = (jnp.minimum(z, 0.0) - jnp.log(1.0 + jnp.exp(-jnp.abs(z)))) * (1.0 / GLA_GATE_NORMALIZER)


def _inproj(x2, mod3, g, w_r, wup, bup, seq, tm):
    t, d = x2.shape
    per = seq // tm
    row = lambda i: (i, 0)
    fixed = lambda i: (0, 0)
    widths = (ATT_WIDTH, 2 * ATT_KV_WIDTH, GLA_K_WIDTH, GLA_K_WIDTH, GLA_V_WIDTH, GLA_V_WIDTH)
    out_shape = [jax.ShapeDtypeStruct((t, w), BF16) for w in widths] + [jax.ShapeDtypeStruct((t, GLA_K_WIDTH), F32)]
    out_specs = [pl.BlockSpec((tm, w), row) for w in widths] + [pl.BlockSpec((tm, GLA_K_WIDTH), row)]
    return pl.pallas_call(
        _inproj_kernel,
        grid=(t // tm,),
        in_specs=[pl.BlockSpec((tm, d), row),
                  pl.BlockSpec((None, 8, d), lambda i: (i // per, 0, 0)),
                  pl.BlockSpec((1, d), fixed),
                  pl.BlockSpec((d, C_END), fixed),
                  pl.BlockSpec((LANES, GLA_K_WIDTH), fixed),
                  pl.BlockSpec((1, GLA_K_WIDTH), fixed)],
        out_specs=out_specs,
        out_shape=out_shape,
        compiler_params=_params(("parallel",)),
        name="inproj",
    )(x2, mod3, g, w_r, wup, bup)


def _att_kernel(sinks_ref, q_ref, kv_ref, kvp_ref, o_ref, *, nblk):
    i = pl.program_id(1)
    blk = ATT_BLOCK
    lo = lax.broadcasted_iota(jnp.int32, (blk, LANES), 1) < ATT_HEAD_DIM
    qi = lax.broadcasted_iota(jnp.int32, (2 * blk, 2 * blk), 0) % blk
    kj = lax.broadcasted_iota(jnp.int32, (2 * blk, 2 * blk), 1)
    cur_ok = (kj >= blk) & ((kj - blk) <= qi)
    prev_ok = (kj < blk) & (kj > qi)
    first_off = jnp.where(i > 0, 0, blk)
    top = lax.broadcasted_iota(jnp.int32, (2 * blk, 1), 0) < blk
    for jb in range(nblk):
        r0 = jb * blk
        kvc = kv_ref[r0:r0 + blk, :]
        if jb == 0:
            kvp = kvp_ref[...]
            mask = cur_ok | (prev_ok & (kj >= first_off))
        else:
            kvp = kv_ref[r0 - blk:r0, :]
            mask = cur_ok | prev_ok
        kcat = jnp.concatenate([kvp[:, 0:ATT_KV_WIDTH], kvc[:, 0:ATT_KV_WIDTH]], axis=0)
        vcat = jnp.concatenate([kvp[:, ATT_KV_WIDTH:], kvc[:, ATT_KV_WIDTH:]], axis=0)
        for j in range(ATT_Q_HEADS // 2):
            qp = q_ref[r0:r0 + blk, j * LANES:(j + 1) * LANES]
            zero = jnp.zeros_like(qp)
            q2 = jnp.concatenate([jnp.where(lo, qp, zero), jnp.where(lo, zero, qp)], axis=0)
            s = jnp.where(mask, _dot_nt(q2, kcat), -jnp.inf)
            sink = jnp.where(top, sinks_ref[j], sinks_ref[ATT_Q_HEADS // 2 + j])
            m = jnp.maximum(jnp.max(s, axis=-1, keepdims=True), sink)
            p = jnp.exp(s - m)
            den = jnp.sum(p, axis=-1, keepdims=True) + jnp.exp(sink - m)
            o2 = _dot(p.astype(BF16), vcat) / den
            o = jnp.where(lo, o2[0:blk], o2[blk:2 * blk])
            o_ref[r0:r0 + blk, j * LANES:(j + 1) * LANES] = o.astype(BF16)


def _att(aq, akv, sinks, nbatch, seq, ta):
    t = aq.shape[0]
    nblk = ta // ATT_BLOCK
    per = seq // ta
    perb = seq // ATT_BLOCK
    return pl.pallas_call(
        functools.partial(_att_kernel, nblk=nblk),
        grid=(nbatch, per),
        in_specs=[pl.BlockSpec(memory_space=pltpu.SMEM),
                  pl.BlockSpec((ta, ATT_WIDTH), lambda b, i: (b * per + i, 0)),
                  pl.BlockSpec((ta, 2 * ATT_KV_WIDTH), lambda b, i: (b * per + i, 0)),
                  pl.BlockSpec((ATT_BLOCK, 2 * ATT_KV_WIDTH),
                               lambda b, i: (b * perb + jnp.maximum(i * nblk - 1, 0), 0))],
        out_specs=pl.BlockSpec((ta, ATT_WIDTH), lambda b, i: (b * per + i, 0)),
        out_shape=jax.ShapeDtypeStruct((t, ATT_WIDTH), BF16),
        compiler_params=_params(("parallel", "parallel")),
        name="att",
    )(sinks, aq, akv, akv)


def _gla_kernel(gq_ref, gk_ref, gv_ref, gg_ref, gl_ref, gn_ref, o_ref, st_ref, *, nchunk):
    ch = GLA_CHUNK
    tg = nchunk * ch
    nbatch = gq_ref.shape[0]

    @pl.when(pl.program_id(0) == 0)
    def _():
        st_ref[...] = jnp.zeros_like(st_ref)

    ri = lax.broadcasted_iota(jnp.int32, (tg, tg), 0)
    ci = lax.broadcasted_iota(jnp.int32, (tg, tg), 1)
    tri = ((ri // ch == ci // ch) & (ci <= ri)).astype(BF16)
    bts = []
    for s in range(nbatch):
        g0 = gl_ref[s]
        g1 = g0 - g0.astype(BF16).astype(F32)
        g2 = g1 - g1.astype(BF16).astype(F32)
        parts = _dot(tri, jnp.concatenate([g0.astype(BF16), g1.astype(BF16), g2.astype(BF16)], axis=1))
        bts.append((parts[:, 2 * GLA_K_WIDTH:] + parts[:, GLA_K_WIDTH:2 * GLA_K_WIDTH]) + parts[:, :GLA_K_WIDTH])
    row2 = lax.broadcasted_iota(jnp.int32, (2 * ch, LANES), 0)
    lane2 = lax.broadcasted_iota(jnp.int32, (2 * ch, LANES), 1)
    own = row2 // ch == lane2 // GLA_DK
    causal = (row2 // ch == lane2 // ch) & (lane2 % ch <= row2 % ch)
    gn = gn_ref[...]
    zero = jnp.zeros((2 * ch, LANES), BF16)

    def both(x):
        return jnp.where(own, jnp.concatenate([x, x], axis=0), zero)

    for c in range(nchunk):
        r0 = c * ch
        for s in range(nbatch):
            b = bts[s][r0:r0 + ch]
            bm = b[ch // 2 - 1:ch // 2]
            bl = b[ch - 1:ch]
            q = gq_ref[s, r0:r0 + ch, :].astype(F32)
            k = gk_ref[s, r0:r0 + ch, :].astype(F32)
            qe = (q * jnp.exp(b - bm)).astype(BF16)
            ke = (k * jnp.exp(bm - b)).astype(BF16)
            qs = (q * jnp.exp(b)).astype(BF16)
            kl = (k * jnp.exp(bl - b)).astype(BF16)
            dec = jnp.exp(bl)
            for p in range(GLA_HEADS // 2):
                sl = slice(p * LANES, (p + 1) * LANES)
                h0 = slice(2 * p * GLA_DV, (2 * p + 1) * GLA_DV)
                h1 = slice((2 * p + 1) * GLA_DV, (2 * p + 2) * GLA_DV)
                st = st_ref[s, p]
                a = _dot_nt(both(qe[:, sl]), both(ke[:, sl]))
                a = jnp.where(causal, a, 0.0).astype(BF16)
                v2 = jnp.concatenate([gv_ref[s, r0:r0 + ch, h0], gv_ref[s, r0:r0 + ch, h1]], axis=0)
                o = _dot(a, v2) + _dot_nt(both(qs[:, sl]), st.astype(BF16))
                upd = lax.dot_general(v2, both(kl[:, sl]), (((0,), (0,)), ((), ())),
                                      preferred_element_type=F32)
                gg = jnp.concatenate([gg_ref[s, r0:r0 + ch, h0], gg_ref[s, r0:r0 + ch, h1]], axis=0).astype(F32)
                res = (_rms(o) * gn * (gg * jax.nn.sigmoid(gg))).astype(BF16)
                o_ref[s, r0:r0 + ch, h0] = res[:ch]
                o_ref[s, r0:r0 + ch, h1] = res[ch:]
                st_ref[s, p] = st * dec[:, sl] + upd


def _gla(gq, gk, gv, gg, gl, gn, nbatch, seq, tg):
    t = gq.shape[0]
    tile = lambda w: pl.BlockSpec((nbatch, tg, w), lambda i: (0, i, 0))
    as3d = lambda a: a.reshape(nbatch, seq, a.shape[1])
    out = pl.pallas_call(
        functools.partial(_gla_kernel, nchunk=tg // GLA_CHUNK),
        grid=(seq // tg,),
        in_specs=[tile(GLA_K_WIDTH), tile(GLA_K_WIDTH), tile(GLA_V_WIDTH), tile(GLA_V_WIDTH), tile(GLA_K_WIDTH),
                  pl.BlockSpec((1, GLA_DV), lambda i: (0, 0))],
        out_specs=tile(GLA_V_WIDTH),
        out_shape=jax.ShapeDtypeStruct((nbatch, seq, GLA_V_WIDTH), BF16),
        scratch_shapes=[pltpu.VMEM((nbatch, GLA_HEADS // 2, GLA_DV, LANES), F32)],
        compiler_params=_params(("arbitrary",)),
        name="gla",
    )(as3d(gq), as3d(gk), as3d(gv), as3d(gg), as3d(gl), gn)
    return out.reshape(t, GLA_V_WIDTH)


def _outproj_kernel(att_ref, gla_ref, x_ref, mod_ref, gpost_ref, gpre_ref, woa_ref, wog_ref, wr_ref, br_ref,
                    x1_ref, h2_ref, route_ref, cnt_ref, tcnt_ref, base_ref, *, n_exp, tt, tc):
    tm = x_ref.shape[0]

    @pl.when(pl.program_id(0) == 0)
    def _():
        base_ref[...] = jnp.zeros_like(base_ref)

    nsub = tc // tt
    lane = lax.broadcasted_iota(jnp.int32, (tc, LANES), 1)
    ri = lax.broadcasted_iota(jnp.int32, (tc, tc), 0)
    ci = lax.broadcasted_iota(jnp.int32, (tc, tc), 1)
    below = ((ci < ri) & (ri // tt == ci // tt)).astype(BF16)
    lower = (lax.broadcasted_iota(jnp.int32, (LANES, LANES), 0)
             < lax.broadcasted_iota(jnp.int32, (LANES, LANES), 1)).astype(BF16)
    sub_row = lax.broadcasted_iota(jnp.int32, (8, LANES), 0)
    tok_sub = lax.broadcasted_iota(jnp.int32, (tc, LANES), 0) // tt
    all_cnt = jnp.zeros((8, LANES), F32)
    for ch in range(tm // tc):
        r = slice(ch * tc, (ch + 1) * tc)
        y = _dot(att_ref[r, :], woa_ref[...]) + _dot(gla_ref[r, :], wog_ref[...])
        x1 = x_ref[r, :] + mod_ref[2:3, :] * (_rms(y) * gpost_ref[...])
        x1_ref[r, :] = x1
        h2 = _rms(x1) * gpre_ref[...]
        h2 = h2 * (1.0 + mod_ref[4:5, :]) + mod_ref[3:4, :]
        h2_hi = h2.astype(BF16)
        h2_ref[r, :] = h2_hi

        h2_lo = (h2 - h2_hi.astype(F32)).astype(BF16)
        p_hi = _dot(h2_hi, wr_ref[...])
        p_lo = _dot(h2_lo, wr_ref[...])
        logits = ((p_lo[:, LANES:] + p_lo[:, :LANES]) + p_hi[:, LANES:]) + p_hi[:, :LANES] + br_ref[...]
        vals = jnp.where(lane < n_exp, logits, -jnp.inf)
        sels, tops, idxs = [], [], []
        for _ in range(TOP_K):
            m = jnp.max(vals, axis=-1, keepdims=True)
            idx = jnp.min(jnp.where(vals == m, lane, LANES), axis=-1, keepdims=True)
            sel = lane == idx
            vals = jnp.where(sel, -jnp.inf, vals)
            sels.append(sel)
            tops.append(m)
            idxs.append(idx)
        es = [jnp.exp(m - tops[0]) for m in tops]
        tot = es[0] + es[1] + es[2] + es[3]
        onehot = jnp.zeros((tc, LANES), F32)
        for sel in sels:
            onehot = onehot + sel.astype(F32)
        earlier = _dot(below, onehot.astype(BF16))
        sub_cnt = jnp.zeros((8, LANES), F32)
        for h in range(nsub):
            sub_cnt = jnp.where(sub_row == h, jnp.sum(onehot[h * tt:(h + 1) * tt], axis=0, keepdims=True), sub_cnt)
        run_start = _dot(sub_cnt.astype(BF16), lower)
        pos_all = earlier
        for h in range(nsub):
            pos_all = pos_all + jnp.where(tok_sub == h, run_start[h:h + 1, :], 0.0)
            all_cnt = jnp.where(sub_row == ch * nsub + h, sub_cnt[h:h + 1, :], all_cnt)
        route = jnp.zeros((tc, LANES), F32)
        for k in range(TOP_K):
            pos = jnp.sum(jnp.where(sels[k], pos_all, 0.0), axis=-1, keepdims=True)
            route = jnp.where(lane == k, idxs[k].astype(F32), route)
            route = jnp.where(lane == TOP_K + k, es[k] / tot, route)
            route = jnp.where(lane == 2 * TOP_K + k, pos, route)
        route_ref[r, :] = route
    tcnt_ref[...] = all_cnt
    base = base_ref[...] + jnp.sum(all_cnt, axis=0, keepdims=True)
    base_ref[...] = base
    cnt_ref[...] = jnp.broadcast_to(base, cnt_ref.shape)


def _outproj(att, gla, x2, mod3, gpost, gpre, woa, wog, wr, br, seq, tm, n_exp, tt):
    t, d = x2.shape
    per = seq // tm
    row = lambda i: (i, 0)
    fixed = lambda i: (0, 0)
    return pl.pallas_call(
        functools.partial(_outproj_kernel, n_exp=n_exp, tt=tt, tc=min(512, tm)),
        grid=(t // tm,),
        in_specs=[pl.BlockSpec((tm, ATT_WIDTH), row),
                  pl.BlockSpec((tm, GLA_V_WIDTH), row),
                  pl.BlockSpec((tm, d), row),
                  pl.BlockSpec((None, 8, d), lambda i: (i // per, 0, 0)),
                  pl.BlockSpec((1, d), fixed),
                  pl.BlockSpec((1, d), fixed),
                  pl.BlockSpec((ATT_WIDTH, d), fixed),
                  pl.BlockSpec((GLA_V_WIDTH, d), fixed),
                  pl.BlockSpec((d, 2 * LANES), fixed),
                  pl.BlockSpec((1, LANES), fixed)],
        out_specs=[pl.BlockSpec((tm, d), row),
                   pl.BlockSpec((tm, d), row),
                   pl.BlockSpec((tm, LANES), row),
                   pl.BlockSpec((8, LANES), fixed),
                   pl.BlockSpec((None, 8, LANES), lambda i: (i, 0, 0))],
        out_shape=[jax.ShapeDtypeStruct((t, d), F32),
                   jax.ShapeDtypeStruct((t, d), BF16),
                   jax.ShapeDtypeStruct((t, LANES), F32),
                   jax.ShapeDtypeStruct((8, LANES), F32),
                   jax.ShapeDtypeStruct((t // tm, 8, LANES), F32)],
        scratch_shapes=[pltpu.VMEM((1, LANES), F32)],
        compiler_params=_params(("arbitrary",)),
        name="outproj",
    )(att, gla, x2, mod3, gpost, gpre, woa, wog, wr, br)


def _dispatch_kernel(pend_ref, nu_ref, n_ref, cs_ref, gs_ref, prow_ref, h2_ref, xs_hbm, sb, zbuf, sem_r, sem_z,
                     *, tt, nsub, rows, n_exp, n_blocks):
    i = pl.program_id(0)
    nsteps = pl.num_programs(0)
    n_sorted = TOP_K * tt
    c = sb.shape[1] // n_sorted
    blk_rows = rows * c
    slot = i & 1

    def zero_fill(start):
        return pltpu.make_async_copy(zbuf, xs_hbm.at[pl.ds(pl.multiple_of(start, blk_rows), blk_rows), :], sem_z)

    def wait_runs(s):
        pltpu.make_async_copy(sb.at[s], xs_hbm.at[pl.ds(0, n_sorted * c), :], sem_r.at[s]).wait()

    @pl.when(i == 0)
    def _():
        zbuf[...] = jnp.zeros_like(zbuf)
        for phase in range(2):
            for e in range(n_exp):
                hi = pend_ref[e]
                lo = pend_ref[e - 1] if e > 0 else 0
                tail = nu_ref[0] + e
                for pred, start in ((hi > lo, (hi - rows) * c), (tail < n_blocks, tail * blk_rows)):
                    @pl.when(pred)
                    def _():
                        if phase == 0:
                            zero_fill(start).start()
                        else:
                            zero_fill(start).wait()

    @pl.when(i >= 2)
    def _():
        for h in range(nsub):
            wait_runs(slot * nsub + h)

    j = lax.broadcasted_iota(jnp.int32, (n_sorted, tt), 0)
    for h in range(nsub):
        perm = jnp.zeros((n_sorted, tt), F32)
        for k in range(TOP_K):
            perm = jnp.where(j == prow_ref[h, k:k + 1, :], 1.0, perm)
        srt = _dot(perm.astype(BF16), h2_ref[h * tt:(h + 1) * tt, :])
        _store_rows(sb.at[slot * nsub + h], _pack_bf16_valued(srt))

    for h in range(nsub):
        tile = i * nsub + h
        for e in range(n_exp):
            n = n_ref[tile * n_exp + e]

            @pl.when(n > 0)
            def _():
                src = pl.multiple_of(cs_ref[tile * n_exp + e], c)
                dst = pl.multiple_of(gs_ref[tile * n_exp + e], c)
                pltpu.make_async_copy(sb.at[slot * nsub + h, pl.ds(src, n), :], xs_hbm.at[pl.ds(dst, n), :],
                                      sem_r.at[slot * nsub + h]).start(priority=e % 2)

    @pl.when(i == nsteps - 1)
    def _():
        for h in range(nsub):
            @pl.when(nsteps > 1)
            def _():
                wait_runs((1 - slot) * nsub + h)
            wait_runs(slot * nsub + h)


def _dispatch(pend, n_used, n_tab, cs_tab, gs_tab, prow, h2, n_slots, rows, tt, nsub):
    t, d = h2.shape
    c = d // 2 // LANES
    n_exp = pend.shape[0]
    grid_spec = pltpu.PrefetchScalarGridSpec(
        num_scalar_prefetch=5,
        grid=(t // (nsub * tt),),
        in_specs=[pl.BlockSpec((nsub, 8, tt), lambda i, *_: (i, 0, 0)),
                  pl.BlockSpec((nsub * tt, d), lambda i, *_: (i, 0))],
        out_specs=pl.BlockSpec(memory_space=pl.ANY),
        scratch_shapes=[pltpu.VMEM((2 * nsub, TOP_K * tt * c, LANES), jnp.int32),
                        pltpu.VMEM((rows * c, LANES), jnp.int32),
                        pltpu.SemaphoreType.DMA((2 * nsub,)),
                        pltpu.SemaphoreType.DMA],
    )
    return pl.pallas_call(
        functools.partial(_dispatch_kernel, tt=tt, nsub=nsub, rows=rows, n_exp=n_exp, n_blocks=n_slots // rows),
        grid_spec=grid_spec,
        out_shape=jax.ShapeDtypeStruct((n_slots * c, LANES), jnp.int32),
        compiler_params=_params(("arbitrary",)),
        name="dispatch",
    )(pend, n_used, n_tab, cs_tab, gs_tab, prow, h2)


def _moe_kernel(be_ref, nu_ref, xs_ref, w1_ref, b1g_ref, b1l_ref, w2_ref, b2_ref, ys_ref, w1g, w1l, *, rows):
    i = pl.program_id(0)
    c = xs_ref.shape[0] // rows
    half = c * LANES

    @pl.when((i < nu_ref[0]) & ((i == 0) | (be_ref[i] != be_ref[jnp.maximum(i - 1, 0)])))
    def _():
        n = 2 * LANES
        r = lax.broadcasted_iota(jnp.int32, (n, n), 0)
        col = lax.broadcasted_iota(jnp.int32, (n, n), 1)
        perm = (r == jnp.where(col < LANES, 2 * col, 2 * (col - LANES) + 1)).astype(BF16)
        for j in range(w1g.shape[1] // LANES):
            d = _dot(w1_ref[:, j * n:(j + 1) * n].astype(BF16), perm)
            w1g[:, j * LANES:(j + 1) * LANES] = d[:, :LANES].astype(BF16)
            w1l[:, j * LANES:(j + 1) * LANES] = d[:, LANES:].astype(BF16)

    @pl.when(i < nu_ref[0])
    def _():
        lo, hi = _unpack(_load_rows(xs_ref, c))
        xl = lo.astype(BF16)
        xh = hi.astype(BF16)
        glu = _dot(xl, w1g[:half, :]) + _dot(xh, w1g[half:, :]) + b1g_ref[...]
        lin = _dot(xl, w1l[:half, :]) + _dot(xh, w1l[half:, :]) + b1l_ref[...]
        glu = jnp.minimum(glu, SWIGLU_LIMIT)
        lin = jnp.clip(lin, -SWIGLU_LIMIT, SWIGLU_LIMIT)
        a = glu * jax.nn.sigmoid(SWIGLU_ALPHA * glu) * (lin + 1.0)
        _store_rows(ys_ref, _pack(_dot(a.astype(BF16), w2_ref[...].astype(BF16)) + b2_ref[...]))

    @pl.when(i >= nu_ref[0])
    def _():
        ys_ref[...] = jnp.zeros_like(ys_ref)


def _moe(block_e, n_used, xs, w1, b1g, b1l, w2, b2, rows):
    d, f2 = w1.shape[1], w1.shape[2]
    f = f2 // 2
    c = d // 2 // LANES
    nb = xs.shape[0] // (rows * c)
    wsel = lambda i, be, nu: (be[i], 0, 0)
    grid_spec = pltpu.PrefetchScalarGridSpec(
        num_scalar_prefetch=2,
        grid=(nb,),
        in_specs=[pl.BlockSpec((rows * c, LANES), lambda i, be, nu: (jnp.minimum(i, nu[0] - 1), 0)),
                  pl.BlockSpec((None, d, f2), wsel),
                  pl.BlockSpec((None, 1, f), wsel),
                  pl.BlockSpec((None, 1, f), wsel),
                  pl.BlockSpec((None, f, d), wsel),
                  pl.BlockSpec((None, 1, d), wsel)],
        out_specs=pl.BlockSpec((rows * c, LANES), lambda i, be, nu: (i, 0)),
        scratch_shapes=[pltpu.VMEM((d, f), BF16), pltpu.VMEM((d, f), BF16)],
    )
    return pl.pallas_call(
        functools.partial(_moe_kernel, rows=rows),
        grid_spec=grid_spec,
        out_shape=jax.ShapeDtypeStruct(xs.shape, jnp.int32),
        compiler_params=pltpu.CompilerParams(dimension_semantics=("arbitrary",), vmem_limit_bytes=MOE_VMEM_LIMIT),
        name="moe",
    )(block_e, n_used, xs, w1, b1g, b1l, w2, b2)


def _combine_kernel(n_ref, cs_ref, gs_ref, ys_hbm, pg_ref, x1_ref, mod_ref, g_ref, o_ref, yb, sem_g,
                    *, tt, nsub, n_exp):
    i = pl.program_id(0)
    nsteps = pl.num_programs(0)
    d = o_ref.shape[1]
    half = d // 2
    c = half // LANES
    n_sorted = TOP_K * tt

    def issue(step):
        for h in range(nsub):
            tile = step * nsub + h
            buf = (step & 1) * nsub + h
            for e in range(n_exp):
                n = n_ref[tile * n_exp + e]

                @pl.when(n > 0)
                def _():
                    src = pl.multiple_of(gs_ref[tile * n_exp + e], c)
                    dst = pl.multiple_of(cs_ref[tile * n_exp + e], c)
                    pltpu.make_async_copy(ys_hbm.at[pl.ds(src, n), :], yb.at[buf, pl.ds(dst, n), :],
                                          sem_g.at[buf]).start(priority=e % 2)

    @pl.when(i == 0)
    def _():
        issue(0)

    @pl.when(i + 1 < nsteps)
    def _():
        issue(i + 1)

    slot = i & 1
    for h in range(nsub):
        buf = slot * nsub + h
        pltpu.make_async_copy(ys_hbm.at[pl.ds(0, n_sorted * c), :], yb.at[buf], sem_g.at[buf]).wait()

    j = lax.broadcasted_iota(jnp.int32, (tt, n_sorted), 1)
    for h in range(nsub):
        r = slice(h * tt, (h + 1) * tt)
        lo, hi = _unpack(_load_rows(yb.at[slot * nsub + h], c))
        sel = jnp.zeros((tt, n_sorted), F32)
        for k in range(TOP_K):
            pos = pg_ref[r, 2 * TOP_K + k:2 * TOP_K + k + 1].astype(jnp.int32)
            sel = jnp.where(j == pos, pg_ref[r, TOP_K + k:TOP_K + k + 1], sel)
        sel_hi = sel.astype(BF16)
        sel_lo = (sel - sel_hi.astype(F32)).astype(BF16)
        lo_b = lo.astype(BF16)
        hi_b = hi.astype(BF16)
        acc_lo = _dot(sel_lo, lo_b) + _dot(sel_hi, lo_b)
        acc_hi = _dot(sel_lo, hi_b) + _dot(sel_hi, hi_b)
        ms = (jnp.sum(acc_lo * acc_lo, axis=-1, keepdims=True)
              + jnp.sum(acc_hi * acc_hi, axis=-1, keepdims=True)) / d
        inv = lax.rsqrt(ms + NORM_EPS)
        o_ref[r, :half] = x1_ref[r, :half] + mod_ref[5:6, :half] * (acc_lo * inv * g_ref[:, :half])
        o_ref[r, half:] = x1_ref[r, half:] + mod_ref[5:6, half:] * (acc_hi * inv * g_ref[:, half:])


def _combine(n_tab, cs_tab, gs_tab, ys, pg, x1, mod3, g, seq, tt, nsub, n_exp):
    t, d = x1.shape
    tp = tt * nsub
    per = seq // tp
    c = d // 2 // LANES
    row = lambda i, *_: (i, 0)
    grid_spec = pltpu.PrefetchScalarGridSpec(
        num_scalar_prefetch=3,
        grid=(t // tp,),
        in_specs=[pl.BlockSpec(memory_space=pl.ANY),
                  pl.BlockSpec((tp, LANES), row),
                  pl.BlockSpec((tp, d), row),
                  pl.BlockSpec((None, 8, d), lambda i, *_: (i // per, 0, 0)),
                  pl.BlockSpec((1, d), lambda i, *_: (0, 0))],
        out_specs=pl.BlockSpec((tp, d), row),
        scratch_shapes=[pltpu.VMEM((2 * nsub, TOP_K * tt * c, LANES), jnp.int32),
                        pltpu.SemaphoreType.DMA((2 * nsub,))],
    )
    return pl.pallas_call(
        functools.partial(_combine_kernel, tt=tt, nsub=nsub, n_exp=n_exp),
        grid_spec=grid_spec,
        out_shape=jax.ShapeDtypeStruct((t, d), F32),
        compiler_params=_params(("arbitrary",)),
        name="combine",
    )(n_tab, cs_tab, gs_tab, ys, pg, x1, mod3, g)


def _pair_perm():
    half = ATT_Q_HEADS // 2
    idx = []
    for j in range(half):
        idx += list(range(j * ATT_HEAD_DIM, (j + 1) * ATT_HEAD_DIM))
        idx += list(range((half + j) * ATT_HEAD_DIM, (half + j + 1) * ATT_HEAD_DIM))
    return np.asarray(idx, np.int32)


def _layer(x, mod, g_pre_mix, g_post_mix, g_pre_ffn, g_post_ffn, w_in, w_gla_gate_up, b_gla_gate, g_gla_norm,
           sinks, w_out, w_router, b_router, w_mlp1, b_mlp1, w_mlp2, b_mlp2):
    nbatch, seq, d = x.shape
    t = nbatch * seq
    n_exp = w_router.shape[1]
    f = w_mlp2.shape[1]
    x2 = x.reshape(t, d)
    mod3 = jnp.pad(mod.reshape(nbatch, 6, d), ((0, 0), (0, 2), (0, 0)))

    perm = _pair_perm()
    n_main = C_GA - C_AKV
    w_r = jnp.concatenate([w_in[:, perm], w_in[:, ATT_WIDTH:ATT_WIDTH + n_main],
                           jnp.pad(w_in[:, ATT_WIDTH + n_main:], ((0, 0), (0, LANES - GLA_GATE_RANK)))],
                          axis=1).astype(BF16)
    wup = jnp.pad(w_gla_gate_up, ((0, LANES - GLA_GATE_RANK), (0, 0)))
    woa = w_out[:ATT_WIDTH][perm].astype(BF16)
    wog = w_out[ATT_WIDTH:].astype(BF16)
    wr = jnp.pad(w_router, ((0, 0), (0, LANES - n_exp)))
    wr_hi = wr.astype(BF16)
    wr = jnp.concatenate([wr_hi, (wr - wr_hi.astype(F32)).astype(BF16)], axis=1)
    br = jnp.pad(b_router, (0, LANES - n_exp)).reshape(1, LANES)
    b1 = b_mlp1.reshape(n_exp, 1, f, 2)

    tm = min(512, seq)
    tt = min(256, seq)
    to = min(1024, seq)
    aq, akv, gq, gk, gv, gg, gl = _inproj(x2, mod3, g_pre_mix.reshape(1, d), w_r, wup,
                                          b_gla_gate.reshape(1, GLA_K_WIDTH), seq, min(1024, seq))
    att = _att(aq, akv, sinks, nbatch, seq, min(512, seq))
    gla = _gla(gq, gk, gv, gg, gl, g_gla_norm.reshape(1, GLA_DV), nbatch, seq, min(256, seq))
    x1, h2, route, cnt, tcnt = _outproj(att, gla, x2, mod3, g_post_mix.reshape(1, d), g_pre_ffn.reshape(1, d),
                                        woa, wog, wr, br, seq, to, n_exp, tt)

    rows = MOE_ROWS
    counts = cnt[0, :n_exp].astype(jnp.int32)
    padded = (counts + rows - 1) // rows * rows
    pend = jnp.cumsum(padded)
    pstart = pend - padded
    n_slots = t * TOP_K + n_exp * rows
    n_blocks = n_slots // rows
    n_used = (pend[-1] // rows).astype(jnp.int32).reshape(1)
    blk_ids = jnp.minimum(jnp.arange(n_blocks, dtype=jnp.int32), n_used - 1)
    block_e = jnp.minimum(jnp.sum(pend[None, :] <= (blk_ids * rows)[:, None], axis=1), n_exp - 1).astype(jnp.int32)
    n_tiles = t // tt
    sub = d // 2 // LANES
    tile_cnt = tcnt[:, :to // tt, :n_exp].reshape(n_tiles, n_exp).astype(jnp.int32)
    run_start = jnp.cumsum(tile_cnt, axis=1) - tile_cnt
    before = jnp.cumsum(tile_cnt, axis=0) - tile_cnt
    n_tab = (tile_cnt * sub).reshape(-1)
    cs_tab = (run_start * sub).reshape(-1)
    gs_tab = ((pstart[None, :] + before) * sub).reshape(-1)
    pos = route[:, 2 * TOP_K:3 * TOP_K].astype(jnp.int32)
    prow = jnp.pad(pos.reshape(n_tiles, tt, TOP_K).transpose(0, 2, 1), ((0, 0), (0, 8 - TOP_K), (0, 0)),
                   constant_values=-1)

    nsub = max(1, min(512, seq) // tt)
    xs = _dispatch(pend, n_used, n_tab, cs_tab, gs_tab, prow, h2, n_slots, rows, tt, nsub)
    ys = _moe(block_e, n_used, xs, w_mlp1, b1[..., 0], b1[..., 1], w_mlp2, b_mlp2.reshape(n_exp, 1, d), rows)
    out = _combine(n_tab, cs_tab, gs_tab, ys, route, x1, mod3, g_post_ffn.reshape(1, d), seq, tt, nsub, n_exp)
    return out.reshape(nbatch, seq, d)


def kernel(x, c, w_ada, b_ada, g_pre_mix, g_post_mix, g_pre_ffn, g_post_ffn, w_in, w_gla_gate_up, b_gla_gate,
           g_gla_norm, sinks, w_out, w_router, b_router, w_mlp1, b_mlp1, w_mlp2, b_mlp2):
    for l in range(w_in.shape[0]):
        mod = _ada(c, w_ada[l], b_ada[l])
        x = _layer(x, mod, g_pre_mix[l], g_post_mix[l], g_pre_ffn[l], g_post_ffn[l], w_in[l], w_gla_gate_up[l],
                   b_gla_gate[l], g_gla_norm[l], sinks[l], w_out[l], w_router[l], b_router[l], w_mlp1[l], b_mlp1[l],
                   w_mlp2[l], b_mlp2[l])
    return x
```

```python
import functools

import numpy as np
import jax
import jax.numpy as jnp
from jax import lax
from jax.experimental import pallas as pl
from jax.experimental.pallas import tpu as pltpu

F32 = jnp.float32
BF16 = jnp.bfloat16
HI = lax.Precision.HIGHEST

ATT_Q_HEADS = 8
ATT_KV_HEADS = 2
ATT_HEAD_DIM = 64
ATT_BLOCK = 128
GLA_HEADS = 4
GLA_DK = 64
GLA_DV = 128
GLA_GATE_RANK = 16
GLA_GATE_NORMALIZER = 16.0
GLA_CHUNK = 64
TOP_K = 4
SWIGLU_LIMIT = 7.0
SWIGLU_ALPHA = 1.702
NORM_EPS = 1e-6

LANES = 128
ATT_WIDTH = ATT_Q_HEADS * ATT_HEAD_DIM
ATT_KV_WIDTH = ATT_KV_HEADS * ATT_HEAD_DIM
GLA_K_WIDTH = GLA_HEADS * GLA_DK
GLA_V_WIDTH = GLA_HEADS * GLA_DV
C_AQ = 0
C_AKV = C_AQ + ATT_WIDTH
C_GQ = C_AKV + 2 * ATT_KV_WIDTH
C_GK = C_GQ + GLA_K_WIDTH
C_GV = C_GK + GLA_K_WIDTH
C_GG = C_GV + GLA_V_WIDTH
C_GA = C_GG + GLA_V_WIDTH
C_END = C_GA + LANES

MOE_ROWS = 512
VMEM_LIMIT = 48 * 1024 * 1024
MOE_VMEM_LIMIT = 56 * 1024 * 1024


def _dot(a, b, prec=None):
    return jnp.dot(a, b, preferred_element_type=F32, precision=prec)


def _dot_nt(a, b):
    return lax.dot_general(a, b, (((1,), (1,)), ((), ())), preferred_element_type=F32)


def _rms(t):
    return t * lax.rsqrt(jnp.mean(t * t, axis=-1, keepdims=True) + NORM_EPS)


def _params(sem):
    return pltpu.CompilerParams(dimension_semantics=sem, vmem_limit_bytes=VMEM_LIMIT)


HI16 = -65536


def _pack(a):
    half = a.shape[1] // 2
    lo = lax.bitcast_convert_type(a[:, :half].astype(BF16).astype(F32), jnp.int32)
    hi = lax.bitcast_convert_type(a[:, half:].astype(BF16).astype(F32), jnp.int32)
    return hi | lax.shift_right_logical(lo, 16)


def _pack_bf16_valued(a):
    half = a.shape[1] // 2
    lo = lax.bitcast_convert_type(a[:, :half], jnp.int32)
    hi = lax.bitcast_convert_type(a[:, half:], jnp.int32)
    return hi | lax.shift_right_logical(lo, 16)


def _unpack(p):
    lo = lax.bitcast_convert_type(lax.shift_left(p, 16), F32)
    hi = lax.bitcast_convert_type(p & HI16, F32)
    return lo, hi


def _store_rows(ref, p):
    n, w = p.shape
    c = w // LANES
    for s in range(c):
        ref[pl.ds(s, n, stride=c), :] = p[:, s * LANES:(s + 1) * LANES]


def _load_rows(ref, c):
    n = ref.shape[0] // c
    return jnp.concatenate([ref[pl.ds(s, n, stride=c), :] for s in range(c)], axis=1)


def _ada_kernel(c_ref, w_ref, b_ref, o_ref):
    c = c_ref[...]
    o_ref[...] = _dot(c * jax.nn.sigmoid(c), w_ref[...], HI) + b_ref[...]


def _ada(c, w, b):
    nb, d = c.shape
    n = w.shape[1]
    cp = jnp.zeros((8, d), F32).at[:nb].set(c)
    out = pl.pallas_call(
        _ada_kernel,
        grid=(n // d,),
        in_specs=[pl.BlockSpec((8, d), lambda j: (0, 0)),
                  pl.BlockSpec((d, d), lambda j: (0, j)),
                  pl.BlockSpec((1, d), lambda j: (0, j))],
        out_specs=pl.BlockSpec((8, d), lambda j: (0, j)),
        out_shape=jax.ShapeDtypeStruct((8, n), F32),
        compiler_params=_params(("parallel",)),
        name="ada",
    )(cp, w, b.reshape(1, n))
    return out[:nb]


def _inproj_kernel(x_ref, mod_ref, g_ref, w_ref, wup_ref, bup_ref,
                   aq_ref, akv_ref, gq_ref, gk_ref, gv_ref, gg_ref, gl_ref):
    half = x_ref.shape[0] // 2
    for r in (slice(0, half), slice(half, 2 * half)):
        h = _rms(x_ref[r, :]) * g_ref[...]
        h = h * (1.0 + mod_ref[1:2, :]) + mod_ref[0:1, :]
        p = _dot(h.astype(BF16), w_ref[...])
        aq_ref[r, :] = (p[:, C_AQ:C_AKV] * (ATT_HEAD_DIM ** -0.5)).astype(BF16)
        akv_ref[r, :] = p[:, C_AKV:C_GQ].astype(BF16)
        gq_ref[r, :] = (p[:, C_GQ:C_GK] * (GLA_DK ** -0.5)).astype(BF16)
        gk_ref[r, :] = p[:, C_GK:C_GV].astype(BF16)
        gv_ref[r, :] = p[:, C_GV:C_GG].astype(BF16)
        gg_ref[r, :] = p[:, C_GG:C_GA].astype(BF16)
        z = _dot(p[:, C_GA:C_END], wup_ref[...], HI) + bup_ref[...]
        gl_ref[r, :] = (jnp.minimum(z, 0.0) - jnp.log(1.0 + jnp.exp(-jnp.abs(z)))) * (1.0 / GLA_GATE_NORMALIZER)


def _inproj(x2, mod3, g, w_r, wup, bup, seq, tm):
    t, d = x2.shape
    per = seq // tm
    row = lambda i: (i, 0)
    fixed = lambda i: (0, 0)
    widths = (ATT_WIDTH, 2 * ATT_KV_WIDTH, GLA_K_WIDTH, GLA_K_WIDTH, GLA_V_WIDTH, GLA_V_WIDTH)
    out_shape = [jax.ShapeDtypeStruct((t, w), BF16) for w in widths] + [jax.ShapeDtypeStruct((t, GLA_K_WIDTH), F32)]
    out_specs = [pl.BlockSpec((tm, w), row) for w in widths] + [pl.BlockSpec((tm, GLA_K_WIDTH), row)]
    return pl.pallas_call(
        _inproj_kernel,
        grid=(t // tm,),
        in_specs=[pl.BlockSpec((tm, d), row),
                  pl.BlockSpec((None, 8, d), lambda i: (i // per, 0, 0)),
                  pl.BlockSpec((1, d), fixed),
                  pl.BlockSpec((d, C_END), fixed),
                  pl.BlockSpec((LANES, GLA_K_WIDTH), fixed),
                  pl.BlockSpec((1, GLA_K_WIDTH), fixed)],
        out_specs=out_specs,
        out_shape=out_shape,
        compiler_params=_params(("parallel",)),
        name="inproj",
    )(x2, mod3, g, w_r, wup, bup)


def _att_kernel(sinks_ref, q_ref, kv_ref, kvp_ref, o_ref, *, nblk):
    i = pl.program_id(1)
    blk = ATT_BLOCK
    lo = lax.broadcasted_iota(jnp.int32, (blk, LANES), 1) < ATT_HEAD_DIM
    qi = lax.broadcasted_iota(jnp.int32, (2 * blk, 2 * blk), 0) % blk
    kj = lax.broadcasted_iota(jnp.int32, (2 * blk, 2 * blk), 1)
    cur_ok = (kj >= blk) & ((kj - blk) <= qi)
    prev_ok = (kj < blk) & (kj > qi)
    first_off = jnp.where(i > 0, 0, blk)
    top = lax.broadcasted_iota(jnp.int32, (2 * blk, 1), 0) < blk
    for jb in range(nblk):
        r0 = jb * blk
        kvc = kv_ref[r0:r0 + blk, :]
        if jb == 0:
            kvp = kvp_ref[...]
            mask = cur_ok | (prev_ok & (kj >= first_off))
        else:
            kvp = kv_ref[r0 - blk:r0, :]
            mask = cur_ok | prev_ok
        kcat = jnp.concatenate([kvp[:, 0:ATT_KV_WIDTH], kvc[:, 0:ATT_KV_WIDTH]], axis=0)
        vcat = jnp.concatenate([kvp[:, ATT_KV_WIDTH:], kvc[:, ATT_KV_WIDTH:]], axis=0)
        for j in range(ATT_Q_HEADS // 2):
            qp = q_ref[r0:r0 + blk, j * LANES:(j + 1) * LANES]
            zero = jnp.zeros_like(qp)
            q2 = jnp.concatenate([jnp.where(lo, qp, zero), jnp.where(lo, zero, qp)], axis=0)
            s = jnp.where(mask, _dot_nt(q2, kcat), -jnp.inf)
            sink = jnp.where(top, sinks_ref[j], sinks_ref[ATT_Q_HEADS // 2 + j])
            m = jnp.maximum(jnp.max(s, axis=-1, keepdims=True), sink)
            p = jnp.exp(s - m)
            den = jnp.sum(p, axis=-1, keepdims=True) + jnp.exp(sink - m)
            o2 = _dot(p.astype(BF16), vcat) / den
            o = jnp.where(lo, o2[0:blk], o2[blk:2 * blk])
            o_ref[r0:r0 + blk, j * LANES:(j + 1) * LANES] = o.astype(BF16)


def _att(aq, akv, sinks, nbatch, seq, ta):
    t = aq.shape[0]
    nblk = ta // ATT_BLOCK
    per = seq // ta
    perb = seq // ATT_BLOCK
    return pl.pallas_call(
        functools.partial(_att_kernel, nblk=nblk),
        grid=(nbatch, per),
        in_specs=[pl.BlockSpec(memory_space=pltpu.SMEM),
                  pl.BlockSpec((ta, ATT_WIDTH), lambda b, i: (b * per + i, 0)),
                  pl.BlockSpec((ta, 2 * ATT_KV_WIDTH), lambda b, i: (b * per + i, 0)),
                  pl.BlockSpec((ATT_BLOCK, 2 * ATT_KV_WIDTH),
                               lambda b, i: (b * perb + jnp.maximum(i * nblk - 1, 0), 0))],
        out_specs=pl.BlockSpec((ta, ATT_WIDTH), lambda b, i: (b * per + i, 0)),
        out_shape=jax.ShapeDtypeStruct((t, ATT_WIDTH), BF16),
        compiler_params=_params(("parallel", "parallel")),
        name="att",
    )(sinks, aq, akv, akv)


def _gla_kernel(gq_ref, gk_ref, gv_ref, gg_ref, gl_ref, gn_ref, o_ref, st_ref, *, nchunk):
    ch = GLA_CHUNK
    tg = nchunk * ch
    nbatch = gq_ref.shape[0]

    @pl.when(pl.program_id(0) == 0)
    def _():
        st_ref[...] = jnp.zeros_like(st_ref)

    ri = lax.broadcasted_iota(jnp.int32, (tg, tg), 0)
    ci = lax.broadcasted_iota(jnp.int32, (tg, tg), 1)
    tri = ((ri // ch == ci // ch) & (ci <= ri)).astype(BF16)
    bts = []
    for s in range(nbatch):
        g0 = gl_ref[s]
        g1 = g0 - g0.astype(BF16).astype(F32)
        g2 = g1 - g1.astype(BF16).astype(F32)
        parts = _dot(tri, jnp.concatenate([g0.astype(BF16), g1.astype(BF16), g2.astype(BF16)], axis=1))
        bts.append((parts[:, 2 * GLA_K_WIDTH:] + parts[:, GLA_K_WIDTH:2 * GLA_K_WIDTH]) + parts[:, :GLA_K_WIDTH])
    row2 = lax.broadcasted_iota(jnp.int32, (2 * ch, LANES), 0)
    lane2 = lax.broadcasted_iota(jnp.int32, (2 * ch, LANES), 1)
    own = row2 // ch == lane2 // GLA_DK
    causal = (row2 // ch == lane2 // ch) & (lane2 % ch <= row2 % ch)
    gn = gn_ref[...]
    zero = jnp.zeros((2 * ch, LANES), BF16)

    def both(x):
        return jnp.where(own, jnp.concatenate([x, x], axis=0), zero)

    for c in range(nchunk):
        r0 = c * ch
        for s in range(nbatch):
            b = bts[s][r0:r0 + ch]
            bm = b[ch // 2 - 1:ch // 2]
            bl = b[ch - 1:ch]
            q = gq_ref[s, r0:r0 + ch, :].astype(F32)
            k = gk_ref[s, r0:r0 + ch, :].astype(F32)
            qe = (q * jnp.exp(b - bm)).astype(BF16)
            ke = (k * jnp.exp(bm - b)).astype(BF16)
            qs = (q * jnp.exp(b)).astype(BF16)
            kl = (k * jnp.exp(bl - b)).astype(BF16)
            dec = jnp.exp(bl)
            for p in range(GLA_HEADS // 2):
                sl = slice(p * LANES, (p + 1) * LANES)
                h0 = slice(2 * p * GLA_DV, (2 * p + 1) * GLA_DV)
                h1 = slice((2 * p + 1) * GLA_DV, (2 * p + 2) * GLA_DV)
                st = st_ref[s, p]
                a = _dot_nt(both(qe[:, sl]), both(ke[:, sl]))
                a = jnp.where(causal, a, 0.0).astype(BF16)
                v2 = jnp.concatenate([gv_ref[s, r0:r0 + ch, h0], gv_ref[s, r0:r0 + ch, h1]], axis=0)
                o = _dot(a, v2) + _dot_nt(both(qs[:, sl]), st.astype(BF16))
                upd = lax.dot_general(v2, both(kl[:, sl]), (((0,), (0,)), ((), ())),
                                      preferred_element_type=F32)
                gg = jnp.concatenate([gg_ref[s, r0:r0 + ch, h0], gg_ref[s, r0:r0 + ch, h1]], axis=0).astype(F32)
                res = (_rms(o) * gn * (gg * jax.nn.sigmoid(gg))).astype(BF16)
                o_ref[s, r0:r0 + ch, h0] = res[:ch]
                o_ref[s, r0:r0 + ch, h1] = res[ch:]
                st_ref[s, p] = st * dec[:, sl] + upd


def _gla(gq, gk, gv, gg, gl, gn, nbatch, seq, tg):
    t = gq.shape[0]
    tile = lambda w: pl.BlockSpec((nbatch, tg, w), lambda i: (0, i, 0))
    as3d = lambda a: a.reshape(nbatch, seq, a.shape[1])
    out = pl.pallas_call(
        functools.partial(_gla_kernel, nchunk=tg // GLA_CHUNK),
        grid=(seq // tg,),
        in_specs=[tile(GLA_K_WIDTH), tile(GLA_K_WIDTH), tile(GLA_V_WIDTH), tile(GLA_V_WIDTH), tile(GLA_K_WIDTH),
                  pl.BlockSpec((1, GLA_DV), lambda i: (0, 0))],
        out_specs=tile(GLA_V_WIDTH),
        out_shape=jax.ShapeDtypeStruct((nbatch, seq, GLA_V_WIDTH), BF16),
        scratch_shapes=[pltpu.VMEM((nbatch, GLA_HEADS // 2, GLA_DV, LANES), F32)],
        compiler_params=_params(("arbitrary",)),
        name="gla",
    )(as3d(gq), as3d(gk), as3d(gv), as3d(gg), as3d(gl), gn)
    return out.reshape(t, GLA_V_WIDTH)


def _outproj_kernel(att_ref, gla_ref, x_ref, mod_ref, gpost_ref, gpre_ref, woa_ref, wog_ref, wr_ref, br_ref,
                    x1_ref, h2_ref, route_ref, cnt_ref, tcnt_ref, base_ref, *, n_exp, tt, tc):
    tm = x_ref.shape[0]

    @pl.when(pl.program_id(0) == 0)
    def _():
        base_ref[...] = jnp.zeros_like(base_ref)

    nsub = tc // tt
    lane = lax.broadcasted_iota(jnp.int32, (tc, LANES), 1)
    ri = lax.broadcasted_iota(jnp.int32, (tc, tc), 0)
    ci = lax.broadcasted_iota(jnp.int32, (tc, tc), 1)
    below = ((ci < ri) & (ri // tt == ci // tt)).astype(BF16)
    lower = (lax.broadcasted_iota(jnp.int32, (LANES, LANES), 0)
             < lax.broadcasted_iota(jnp.int32, (LANES, LANES), 1)).astype(BF16)
    sub_row = lax.broadcasted_iota(jnp.int32, (8, LANES), 0)
    tok_sub = lax.broadcasted_iota(jnp.int32, (tc, LANES), 0) // tt
    all_cnt = jnp.zeros((8, LANES), F32)
    for ch in range(tm // tc):
        r = slice(ch * tc, (ch + 1) * tc)
        y = _dot(att_ref[r, :], woa_ref[...]) + _dot(gla_ref[r, :], wog_ref[...])
        x1 = x_ref[r, :] + mod_ref[2:3, :] * (_rms(y) * gpost_ref[...])
        x1_ref[r, :] = x1
        h2 = _rms(x1) * gpre_ref[...]
        h2 = h2 * (1.0 + mod_ref[4:5, :]) + mod_ref[3:4, :]
        h2_hi = h2.astype(BF16)
        h2_ref[r, :] = h2_hi

        h2_lo = (h2 - h2_hi.astype(F32)).astype(BF16)
        p_hi = _dot(h2_hi, wr_ref[...])
        p_lo = _dot(h2_lo, wr_ref[...])
        logits = ((p_lo[:, LANES:] + p_lo[:, :LANES]) + p_hi[:, LANES:]) + p_hi[:, :LANES] + br_ref[...]
        vals = jnp.where(lane < n_exp, logits, -jnp.inf)
        sels, tops, idxs = [], [], []
        for _ in range(TOP_K):
            m = jnp.max(vals, axis=-1, keepdims=True)
            idx = jnp.min(jnp.where(vals == m, lane, LANES), axis=-1, keepdims=True)
            sel = lane == idx
            vals = jnp.where(sel, -jnp.inf, vals)
            sels.append(sel)
            tops.append(m)
            idxs.append(idx)
        es = [jnp.exp(m - tops[0]) for m in tops]
        tot = es[0] + es[1] + es[2] + es[3]
        onehot = jnp.zeros((tc, LANES), F32)
        for sel in sels:
            onehot = onehot + sel.astype(F32)
        earlier = _dot(below, onehot.astype(BF16))
        sub_cnt = jnp.zeros((8, LANES), F32)
        for h in range(nsub):
            sub_cnt = jnp.where(sub_row == h, jnp.sum(onehot[h * tt:(h + 1) * tt], axis=0, keepdims=True), sub_cnt)
        run_start = _dot(sub_cnt.astype(BF16), lower)
        pos_all = earlier
        for h in range(nsub):
            pos_all = pos_all + jnp.where(tok_sub == h, run_start[h:h + 1, :], 0.0)
            all_cnt = jnp.where(sub_row == ch * nsub + h, sub_cnt[h:h + 1, :], all_cnt)
        route = jnp.zeros((tc, LANES), F32)
        for k in range(TOP_K):
            pos = jnp.sum(jnp.where(sels[k], pos_all, 0.0), axis=-1, keepdims=True)
            route = jnp.where(lane == k, idxs[k].astype(F32), route)
            route = jnp.where(lane == TOP_K + k, es[k] / tot, route)
            route = jnp.where(lane == 2 * TOP_K + k, pos, route)
        route_ref[r, :] = route
    tcnt_ref[...] = all_cnt
    base = base_ref[...] + jnp.sum(all_cnt, axis=0, keepdims=True)
    base_ref[...] = base
    cnt_ref[...] = jnp.broadcast_to(base, cnt_ref.shape)


def _outproj(att, gla, x2, mod3, gpost, gpre, woa, wog, wr, br, seq, tm, n_exp, tt):
    t, d = x2.shape
    per = seq // tm
    row = lambda i: (i, 0)
    fixed = lambda i: (0, 0)
    return pl.pallas_call(
        functools.partial(_outproj_kernel, n_exp=n_exp, tt=tt, tc=min(512, tm)),
        grid=(t // tm,),
        in_specs=[pl.BlockSpec((tm, ATT_WIDTH), row),
                  pl.BlockSpec((tm, GLA_V_WIDTH), row),
                  pl.BlockSpec((tm, d), row),
                  pl.BlockSpec((None, 8, d), lambda i: (i // per, 0, 0)),
                  pl.BlockSpec((1, d), fixed),
                  pl.BlockSpec((1, d), fixed),
                  pl.BlockSpec((ATT_WIDTH, d), fixed),
                  pl.BlockSpec((GLA_V_WIDTH, d), fixed),
                  pl.BlockSpec((d, 2 * LANES), fixed),
                  pl.BlockSpec((1, LANES), fixed)],
        out_specs=[pl.BlockSpec((tm, d), row),
                   pl.BlockSpec((tm, d), row),
                   pl.BlockSpec((tm, LANES), row),
                   pl.BlockSpec((8, LANES), fixed),
                   pl.BlockSpec((None, 8, LANES), lambda i: (i, 0, 0))],
        out_shape=[jax.ShapeDtypeStruct((t, d), F32),
                   jax.ShapeDtypeStruct((t, d), BF16),
                   jax.ShapeDtypeStruct((t, LANES), F32),
                   jax.ShapeDtypeStruct((8, LANES), F32),
                   jax.ShapeDtypeStruct((t // tm, 8, LANES), F32)],
        scratch_shapes=[pltpu.VMEM((1, LANES), F32)],
        compiler_params=_params(("arbitrary",)),
        name="outproj",
    )(att, gla, x2, mod3, gpost, gpre, woa, wog, wr, br)


def _dispatch_kernel(pend_ref, nu_ref, n_ref, cs_ref, gs_ref, prow_ref, h2_ref, xs_hbm, sb, zbuf, sem_r, sem_z,
                     *, tt, nsub, rows, n_exp, n_blocks):
    i = pl.program_id(0)
    nsteps = pl.num_programs(0)
    n_sorted = TOP_K * tt
    c = sb.shape[1] // n_sorted
    blk_rows = rows * c
    slot = i & 1

    def zero_fill(start):
        return pltpu.make_async_copy(zbuf, xs_hbm.at[pl.ds(pl.multiple_of(start, blk_rows), blk_rows), :], sem_z)

    def wait_runs(s):
        pltpu.make_async_copy(sb.at[s], xs_hbm.at[pl.ds(0, n_sorted * c), :], sem_r.at[s]).wait()

    @pl.when(i == 0)
    def _():
        zbuf[...] = jnp.zeros_like(zbuf)
        for phase in range(2):
            for e in range(n_exp):
                hi = pend_ref[e]
                lo = pend_ref[e - 1] if e > 0 else 0
                tail = nu_ref[0] + e
                for pred, start in ((hi > lo, (hi - rows) * c), (tail < n_blocks, tail * blk_rows)):
                    @pl.when(pred)
                    def _():
                        if phase == 0:
                            zero_fill(start).start()
                        else:
                            zero_fill(start).wait()

    @pl.when(i >= 2)
    def _():
        for h in range(nsub):
            wait_runs(slot * nsub + h)

    j = lax.broadcasted_iota(jnp.int32, (n_sorted, tt), 0)
    for h in range(nsub):
        perm = jnp.zeros((n_sorted, tt), F32)
        for k in range(TOP_K):
            perm = jnp.where(j == prow_ref[h, k:k + 1, :], 1.0, perm)
        srt = _dot(perm.astype(BF16), h2_ref[h * tt:(h + 1) * tt, :])
        _store_rows(sb.at[slot * nsub + h], _pack_bf16_valued(srt))

    for h in range(nsub):
        tile = i * nsub + h
        for e in range(n_exp):
            n = n_ref[tile * n_exp + e]

            @pl.when(n > 0)
            def _():
                src = pl.multiple_of(cs_ref[tile * n_exp + e], c)
                dst = pl.multiple_of(gs_ref[tile * n_exp + e], c)
                pltpu.make_async_copy(sb.at[slot * nsub + h, pl.ds(src, n), :], xs_hbm.at[pl.ds(dst, n), :],
                                      sem_r.at[slot * nsub + h]).start(priority=e % 2)

    @pl.when(i == nsteps - 1)
    def _():
        for h in range(nsub):
            @pl.when(nsteps > 1)
            def _():
                wait_runs((1 - slot) * nsub + h)
            wait_runs(slot * nsub + h)


def _dispatch(pend, n_used, n_tab, cs_tab, gs_tab, prow, h2, n_slots, rows, tt, nsub):
    t, d = h2.shape
    c = d // 2 // LANES
    n_exp = pend.shape[0]
    grid_spec = pltpu.PrefetchScalarGridSpec(
        num_scalar_prefetch=5,
        grid=(t // (nsub * tt),),
        in_specs=[pl.BlockSpec((nsub, 8, tt), lambda i, *_: (i, 0, 0)),
                  pl.BlockSpec((nsub * tt, d), lambda i, *_: (i, 0))],
        out_specs=pl.BlockSpec(memory_space=pl.ANY),
        scratch_shapes=[pltpu.VMEM((2 * nsub, TOP_K * tt * c, LANES), jnp.int32),
                        pltpu.VMEM((rows * c, LANES), jnp.int32),
                        pltpu.SemaphoreType.DMA((2 * nsub,)),
                        pltpu.SemaphoreType.DMA],
    )
    return pl.pallas_call(
        functools.partial(_dispatch_kernel, tt=tt, nsub=nsub, rows=rows, n_exp=n_exp, n_blocks=n_slots // rows),
        grid_spec=grid_spec,
        out_shape=jax.ShapeDtypeStruct((n_slots * c, LANES), jnp.int32),
        compiler_params=_params(("arbitrary",)),
        name="dispatch",
    )(pend, n_used, n_tab, cs_tab, gs_tab, prow, h2)


def _moe_kernel(nblk_ref, first_ref, nu_ref, xs_hbm, w1_ref, b1g_ref, b1l_ref, w2_ref, b2_ref, ys_hbm,
                w1g, w1l, w2b, xb, yb, sem_x, sem_y, *, rows, n_blocks, n_tail):
    e = pl.program_id(0)
    n = nblk_ref[e]
    c = xb.shape[1] // rows
    half = c * LANES
    blk = rows * c

    def rows_of(j):
        return pl.ds(pl.multiple_of((first_ref[e] + j) * blk, blk), blk)

    def x_copy(j, slot):
        return pltpu.make_async_copy(xs_hbm.at[rows_of(j), :], xb.at[slot], sem_x.at[slot])

    def y_copy(j, slot):
        return pltpu.make_async_copy(yb.at[slot], ys_hbm.at[rows_of(j), :], sem_y.at[slot])

    @pl.when(n > 0)
    def _():
        x_copy(0, 0).start()
        w2b[...] = w2_ref[...].astype(BF16)
        g = 2 * LANES
        r = lax.broadcasted_iota(jnp.int32, (g, g), 0)
        col = lax.broadcasted_iota(jnp.int32, (g, g), 1)
        perm = (r == jnp.where(col < LANES, 2 * col, 2 * (col - LANES) + 1)).astype(BF16)
        for j in range(w1g.shape[1] // LANES):
            d = _dot(w1_ref[:, j * g:(j + 1) * g].astype(BF16), perm)
            w1g[:, j * LANES:(j + 1) * LANES] = d[:, :LANES].astype(BF16)
            w1l[:, j * LANES:(j + 1) * LANES] = d[:, LANES:].astype(BF16)

        def block(j, carry):
            slot = j & 1
            x_copy(j, slot).wait()

            @pl.when(j + 1 < n)
            def _():
                x_copy(j + 1, 1 - slot).start()

            @pl.when(j >= 2)
            def _():
                y_copy(j - 2, slot).wait()

            lo, hi = _unpack(_load_rows(xb.at[slot], c))
            xl = lo.astype(BF16)
            xh = hi.astype(BF16)
            glu = _dot(xl, w1g[:half, :]) + _dot(xh, w1g[half:, :]) + b1g_ref[...]
            lin = _dot(xl, w1l[:half, :]) + _dot(xh, w1l[half:, :]) + b1l_ref[...]
            glu = jnp.minimum(glu, SWIGLU_LIMIT)
            lin = jnp.clip(lin, -SWIGLU_LIMIT, SWIGLU_LIMIT)
            a = glu * jax.nn.sigmoid(SWIGLU_ALPHA * glu) * (lin + 1.0)
            _store_rows(yb.at[slot], _pack(_dot(a.astype(BF16), w2b[...]) + b2_ref[...]))
            y_copy(j, slot).start()
            return carry

        lax.fori_loop(0, n, block, 0)

        @pl.when(n >= 2)
        def _():
            y_copy(n - 2, n & 1).wait()
        y_copy(n - 1, (n - 1) & 1).wait()

    @pl.when(e == pl.num_programs(0) - 1)
    def _():
        yb[0] = jnp.zeros((blk, LANES), jnp.int32)
        for phase in range(2):
            for t in range(n_tail):
                tail = nu_ref[0] + t

                @pl.when(tail < n_blocks)
                def _():
                    fill = pltpu.make_async_copy(
                        yb.at[0], ys_hbm.at[pl.ds(pl.multiple_of(tail * blk, blk), blk), :], sem_y.at[0])
                    if phase == 0:
                        fill.start()
                    else:
                        fill.wait()


def _moe(n_blk, first_blk, n_used, xs, w1, b1g, b1l, w2, b2, rows):
    n_exp, d, f2 = w1.shape
    f = f2 // 2
    c = d // 2 // LANES
    wsel = lambda e, *_: (e, 0, 0)
    grid_spec = pltpu.PrefetchScalarGridSpec(
        num_scalar_prefetch=3,
        grid=(n_exp,),
        in_specs=[pl.BlockSpec(memory_space=pl.ANY),
                  pl.BlockSpec((None, d, f2), wsel),
                  pl.BlockSpec((None, 1, f), wsel),
                  pl.BlockSpec((None, 1, f), wsel),
                  pl.BlockSpec((None, f, d), wsel),
                  pl.BlockSpec((None, 1, d), wsel)],
        out_specs=pl.BlockSpec(memory_space=pl.ANY),
        scratch_shapes=[pltpu.VMEM((d, f), BF16), pltpu.VMEM((d, f), BF16), pltpu.VMEM((f, d), BF16),
                        pltpu.VMEM((2, rows * c, LANES), jnp.int32),
                        pltpu.VMEM((2, rows * c, LANES), jnp.int32),
                        pltpu.SemaphoreType.DMA((2,)),
                        pltpu.SemaphoreType.DMA((2,))],
    )
    return pl.pallas_call(
        functools.partial(_moe_kernel, rows=rows, n_blocks=xs.shape[0] // (rows * c), n_tail=n_exp),
        grid_spec=grid_spec,
        out_shape=jax.ShapeDtypeStruct(xs.shape, jnp.int32),
        compiler_params=pltpu.CompilerParams(dimension_semantics=("arbitrary",), vmem_limit_bytes=MOE_VMEM_LIMIT),
        name="moe",
    )(n_blk, first_blk, n_used, xs, w1, b1g, b1l, w2, b2)


def _combine_kernel(n_ref, cs_ref, gs_ref, ys_hbm, pg_ref, x1_ref, mod_ref, g_ref, o_ref, yb, sem_g,
                    *, tt, nsub, n_exp):
    i = pl.program_id(0)
    nsteps = pl.num_programs(0)
    d = o_ref.shape[1]
    half = d // 2
    c = half // LANES
    n_sorted = TOP_K * tt

    def issue(step):
        for h in range(nsub):
            tile = step * nsub + h
            buf = (step & 1) * nsub + h
            for e in range(n_exp):
                n = n_ref[tile * n_exp + e]

                @pl.when(n > 0)
                def _():
                    src = pl.multiple_of(gs_ref[tile * n_exp + e], c)
                    dst = pl.multiple_of(cs_ref[tile * n_exp + e], c)
                    pltpu.make_async_copy(ys_hbm.at[pl.ds(src, n), :], yb.at[buf, pl.ds(dst, n), :],
                                          sem_g.at[buf]).start(priority=e % 2)

    @pl.when(i == 0)
    def _():
        issue(0)

    @pl.when(i + 1 < nsteps)
    def _():
        issue(i + 1)

    slot = i & 1
    for h in range(nsub):
        buf = slot * nsub + h
        pltpu.make_async_copy(ys_hbm.at[pl.ds(0, n_sorted * c), :], yb.at[buf], sem_g.at[buf]).wait()

    j = lax.broadcasted_iota(jnp.int32, (tt, n_sorted), 1)
    for h in range(nsub):
        r = slice(h * tt, (h + 1) * tt)
        lo, hi = _unpack(_load_rows(yb.at[slot * nsub + h], c))
        sel = jnp.zeros((tt, n_sorted), F32)
        for k in range(TOP_K):
            pos = pg_ref[r, 2 * TOP_K + k:2 * TOP_K + k + 1].astype(jnp.int32)
            sel = jnp.where(j == pos, pg_ref[r, TOP_K + k:TOP_K + k + 1], sel)
        sel_hi = sel.astype(BF16)
        sel_lo = (sel - sel_hi.astype(F32)).astype(BF16)
        lo_b = lo.astype(BF16)
        hi_b = hi.astype(BF16)
        acc_lo = _dot(sel_lo, lo_b) + _dot(sel_hi, lo_b)
        acc_hi = _dot(sel_lo, hi_b) + _dot(sel_hi, hi_b)
        ms = (jnp.sum(acc_lo * acc_lo, axis=-1, keepdims=True)
              + jnp.sum(acc_hi * acc_hi, axis=-1, keepdims=True)) / d
        inv = lax.rsqrt(ms + NORM_EPS)
        o_ref[r, :half] = x1_ref[r, :half] + mod_ref[5:6, :half] * (acc_lo * inv * g_ref[:, :half])
        o_ref[r, half:] = x1_ref[r, half:] + mod_ref[5:6, half:] * (acc_hi * inv * g_ref[:, half:])


def _combine(n_tab, cs_tab, gs_tab, ys, pg, x1, mod3, g, seq, tt, nsub, n_exp):
    t, d = x1.shape
    tp = tt * nsub
    per = seq // tp
    c = d // 2 // LANES
    row = lambda i, *_: (i, 0)
    grid_spec = pltpu.PrefetchScalarGridSpec(
        num_scalar_prefetch=3,
        grid=(t // tp,),
        in_specs=[pl.BlockSpec(memory_space=pl.ANY),
                  pl.BlockSpec((tp, LANES), row),
                  pl.BlockSpec((tp, d), row),
                  pl.BlockSpec((None, 8, d), lambda i, *_: (i // per, 0, 0)),
                  pl.BlockSpec((1, d), lambda i, *_: (0, 0))],
        out_specs=pl.BlockSpec((tp, d), row),
        scratch_shapes=[pltpu.VMEM((2 * nsub, TOP_K * tt * c, LANES), jnp.int32),
                        pltpu.SemaphoreType.DMA((2 * nsub,))],
    )
    return pl.pallas_call(
        functools.partial(_combine_kernel, tt=tt, nsub=nsub, n_exp=n_exp),
        grid_spec=grid_spec,
        out_shape=jax.ShapeDtypeStruct((t, d), F32),
        compiler_params=_params(("arbitrary",)),
        name="combine",
    )(n_tab, cs_tab, gs_tab, ys, pg, x1, mod3, g)


def _pair_perm():
    half = ATT_Q_HEADS // 2
    idx = []
    for j in range(half):
        idx += list(range(j * ATT_HEAD_DIM, (j + 1) * ATT_HEAD_DIM))
        idx += list(range((half + j) * ATT_HEAD_DIM, (half + j + 1) * ATT_HEAD_DIM))
    return np.asarray(idx, np.int32)


def _layer(x, mod, g_pre_mix, g_post_mix, g_pre_ffn, g_post_ffn, w_in, w_gla_gate_up, b_gla_gate, g_gla_norm,
           sinks, w_out, w_router, b_router, w_mlp1, b_mlp1, w_mlp2, b_mlp2):
    nbatch, seq, d = x.shape
    t = nbatch * seq
    n_exp = w_router.shape[1]
    f = w_mlp2.shape[1]
    x2 = x.reshape(t, d)
    mod3 = jnp.pad(mod.reshape(nbatch, 6, d), ((0, 0), (0, 2), (0, 0)))

    perm = _pair_perm()
    n_main = C_GA - C_AKV
    w_r = jnp.concatenate([w_in[:, perm], w_in[:, ATT_WIDTH:ATT_WIDTH + n_main],
                           jnp.pad(w_in[:, ATT_WIDTH + n_main:], ((0, 0), (0, LANES - GLA_GATE_RANK)))],
                          axis=1).astype(BF16)
    wup = jnp.pad(w_gla_gate_up, ((0, LANES - GLA_GATE_RANK), (0, 0)))
    woa = w_out[:ATT_WIDTH][perm].astype(BF16)
    wog = w_out[ATT_WIDTH:].astype(BF16)
    wr = jnp.pad(w_router, ((0, 0), (0, LANES - n_exp)))
    wr_hi = wr.astype(BF16)
    wr = jnp.concatenate([wr_hi, (wr - wr_hi.astype(F32)).astype(BF16)], axis=1)
    br = jnp.pad(b_router, (0, LANES - n_exp)).reshape(1, LANES)
    b1 = b_mlp1.reshape(n_exp, 1, f, 2)

    tm = min(512, seq)
    tt = min(256, seq)
    to = min(1024, seq)
    aq, akv, gq, gk, gv, gg, gl = _inproj(x2, mod3, g_pre_mix.reshape(1, d), w_r, wup,
                                          b_gla_gate.reshape(1, GLA_K_WIDTH), seq, min(1024, seq))
    att = _att(aq, akv, sinks, nbatch, seq, min(512, seq))
    gla = _gla(gq, gk, gv, gg, gl, g_gla_norm.reshape(1, GLA_DV), nbatch, seq, min(256, seq))
    x1, h2, route, cnt, tcnt = _outproj(att, gla, x2, mod3, g_post_mix.reshape(1, d), g_pre_ffn.reshape(1, d),
                                        woa, wog, wr, br, seq, to, n_exp, tt)

    rows = MOE_ROWS
    counts = cnt[0, :n_exp].astype(jnp.int32)
    padded = (counts + rows - 1) // rows * rows
    pend = jnp.cumsum(padded)
    pstart = pend - padded
    n_slots = t * TOP_K + n_exp * rows
    n_blocks = n_slots // rows
    n_used = (pend[-1] // rows).astype(jnp.int32).reshape(1)
    n_tiles = t // tt
    sub = d // 2 // LANES
    tile_cnt = tcnt[:, :to // tt, :n_exp].reshape(n_tiles, n_exp).astype(jnp.int32)
    run_start = jnp.cumsum(tile_cnt, axis=1) - tile_cnt
    before = jnp.cumsum(tile_cnt, axis=0) - tile_cnt
    n_tab = (tile_cnt * sub).reshape(-1)
    cs_tab = (run_start * sub).reshape(-1)
    gs_tab = ((pstart[None, :] + before) * sub).reshape(-1)
    pos = route[:, 2 * TOP_K:3 * TOP_K].astype(jnp.int32)
    prow = jnp.pad(pos.reshape(n_tiles, tt, TOP_K).transpose(0, 2, 1), ((0, 0), (0, 8 - TOP_K), (0, 0)),
                   constant_values=-1)

    nsub = max(1, min(512, seq) // tt)
    xs = _dispatch(pend, n_used, n_tab, cs_tab, gs_tab, prow, h2, n_slots, rows, tt, nsub)
    ys = _moe(padded // rows, pstart // rows, n_used, xs, w_mlp1, b1[..., 0], b1[..., 1], w_mlp2, b_mlp2.reshape(n_exp, 1, d), rows)
    out = _combine(n_tab, cs_tab, gs_tab, ys, route, x1, mod3, g_post_ffn.reshape(1, d), seq, tt, nsub, n_exp)
    return out.reshape(nbatch, seq, d)


def kernel(x, c, w_ada, b_ada, g_pre_mix, g_post_mix, g_pre_ffn, g_post_ffn, w_in, w_gla_gate_up, b_gla_gate,
           g_gla_norm, sinks, w_out, w_router, b_router, w_mlp1, b_mlp1, w_mlp2, b_mlp2):
    for l in range(w_in.shape[0]):
        mod = _ada(c, w_ada[l], b_ada[l])
        x = _layer(x, mod, g_pre_mix[l], g_post_mix[l], g_pre_ffn[l], g_post_ffn[l], w_in[l], w_gla_gate_up[l],
                   b_gla_gate[l], g_gla_norm[l], sinks[l], w_out[l], w_router[l], b_router[l], w_mlp1[l], b_mlp1[l],
                   w_mlp2[l], b_mlp2[l])
    return x
```

```python
import functools

import numpy as np
import jax
import jax.numpy as jnp
from jax import lax
from jax.experimental import pallas as pl
from jax.experimental.pallas import tpu as pltpu

F32 = jnp.float32
BF16 = jnp.bfloat16
HI = lax.Precision.HIGHEST

ATT_Q_HEADS = 8
ATT_KV_HEADS = 2
ATT_HEAD_DIM = 64
ATT_BLOCK = 128
GLA_HEADS = 4
GLA_DK = 64
GLA_DV = 128
GLA_GATE_RANK = 16
GLA_GATE_NORMALIZER = 16.0
GLA_CHUNK = 64
TOP_K = 4
SWIGLU_LIMIT = 7.0
SWIGLU_ALPHA = 1.702
NORM_EPS = 1e-6

LANES = 128
ATT_WIDTH = ATT_Q_HEADS * ATT_HEAD_DIM
ATT_KV_WIDTH = ATT_KV_HEADS * ATT_HEAD_DIM
GLA_K_WIDTH = GLA_HEADS * GLA_DK
GLA_V_WIDTH = GLA_HEADS * GLA_DV
C_AQ = 0
C_AKV = C_AQ + ATT_WIDTH
C_GQ = C_AKV + 2 * ATT_KV_WIDTH
C_GK = C_GQ + GLA_K_WIDTH
C_GV = C_GK + GLA_K_WIDTH
C_GG = C_GV + GLA_V_WIDTH
C_GA = C_GG + GLA_V_WIDTH
C_END = C_GA + LANES

MOE_ROWS = 512
VMEM_LIMIT = 48 * 1024 * 1024
MOE_VMEM_LIMIT = 56 * 1024 * 1024


def _dot(a, b, prec=None):
    return jnp.dot(a, b, preferred_element_type=F32, precision=prec)


def _dot_nt(a, b):
    return lax.dot_general(a, b, (((1,), (1,)), ((), ())), preferred_element_type=F32)


def _rms(t):
    return t * lax.rsqrt(jnp.mean(t * t, axis=-1, keepdims=True) + NORM_EPS)


def _params(sem):
    return pltpu.CompilerParams(dimension_semantics=sem, vmem_limit_bytes=VMEM_LIMIT)


HI16 = -65536


def _pack(a):
    half = a.shape[1] // 2
    lo = lax.bitcast_convert_type(a[:, :half].astype(BF16).astype(F32), jnp.int32)
    hi = lax.bitcast_convert_type(a[:, half:].astype(BF16).astype(F32), jnp.int32)
    return hi | lax.shift_right_logical(lo, 16)


def _pack_bf16_valued(a):
    half = a.shape[1] // 2
    lo = lax.bitcast_convert_type(a[:, :half], jnp.int32)
    hi = lax.bitcast_convert_type(a[:, half:], jnp.int32)
    return hi | lax.shift_right_logical(lo, 16)


def _unpack(p):
    lo = lax.bitcast_convert_type(lax.shift_left(p, 16), F32)
    hi = lax.bitcast_convert_type(p & HI16, F32)
    return lo, hi


def _store_rows(ref, p):
    n, w = p.shape
    c = w // LANES
    for s in range(c):
        ref[pl.ds(s, n, stride=c), :] = p[:, s * LANES:(s + 1) * LANES]


def _load_rows(ref, c):
    n = ref.shape[0] // c
    return jnp.concatenate([ref[pl.ds(s, n, stride=c), :] for s in range(c)], axis=1)


def _ada_kernel(c_ref, w_ref, b_ref, o_ref):
    c = c_ref[...]
    o_ref[...] = _dot(c * jax.nn.sigmoid(c), w_ref[...], HI) + b_ref[...]


def _ada(c, w, b):
    nb, d = c.shape
    n = w.shape[1]
    cp = jnp.zeros((8, d), F32).at[:nb].set(c)
    out = pl.pallas_call(
        _ada_kernel,
        grid=(n // d,),
        in_specs=[pl.BlockSpec((8, d), lambda j: (0, 0)),
                  pl.BlockSpec((d, d), lambda j: (0, j)),
                  pl.BlockSpec((1, d), lambda j: (0, j))],
        out_specs=pl.BlockSpec((8, d), lambda j: (0, j)),
        out_shape=jax.ShapeDtypeStruct((8, n), F32),
        compiler_params=_params(("parallel",)),
        name="ada",
    )(cp, w, b.reshape(1, n))
    return out[:nb]


def _inproj_kernel(x_ref, mod_ref, g_ref, w_ref, wup_ref, bup_ref,
                   aq_ref, akv_ref, gq_ref, gk_ref, gv_ref, gg_ref, gl_ref):
    half = x_ref.shape[0] // 2
    for r in (slice(0, half), slice(half, 2 * half)):
        h = _rms(x_ref[r, :]) * g_ref[...]
        h = h * (1.0 + mod_ref[1:2, :]) + mod_ref[0:1, :]
        p = _dot(h.astype(BF16), w_ref[...])
        aq_ref[r, :] = (p[:, C_AQ:C_AKV] * (ATT_HEAD_DIM ** -0.5)).astype(BF16)
        akv_ref[r, :] = p[:, C_AKV:C_GQ].astype(BF16)
        gq_ref[r, :] = (p[:, C_GQ:C_GK] * (GLA_DK ** -0.5)).astype(BF16)
        gk_ref[r, :] = p[:, C_GK:C_GV].astype(BF16)
        gv_ref[r, :] = p[:, C_GV:C_GG].astype(BF16)
        gg_ref[r, :] = p[:, C_GG:C_GA].astype(BF16)
        z = _dot(p[:, C_GA:C_END], wup_ref[...], HI) + bup_ref[...]
        gl_ref[r, :] = (jnp.minimum(z, 0.0) - jnp.log(1.0 + jnp.exp(-jnp.abs(z)))) * (1.0 / GLA_GATE_NORMALIZER)


def _inproj(x2, mod3, g, w_r, wup, bup, seq, tm):
    t, d = x2.shape
    per = seq // tm
    row = lambda i: (i, 0)
    fixed = lambda i: (0, 0)
    widths = (ATT_WIDTH, 2 * ATT_KV_WIDTH, GLA_K_WIDTH, GLA_K_WIDTH, GLA_V_WIDTH, GLA_V_WIDTH)
    out_shape = [jax.ShapeDtypeStruct((t, w), BF16) for w in widths] + [jax.ShapeDtypeStruct((t, GLA_K_WIDTH), F32)]
    out_specs = [pl.BlockSpec((tm, w), row) for w in widths] + [pl.BlockSpec((tm, GLA_K_WIDTH), row)]
    return pl.pallas_call(
        _inproj_kernel,
        grid=(t // tm,),
        in_specs=[pl.BlockSpec((tm, d), row),
                  pl.BlockSpec((None, 8, d), lambda i: (i // per, 0, 0)),
                  pl.BlockSpec((1, d), fixed),
                  pl.BlockSpec((d, C_END), fixed),
                  pl.BlockSpec((LANES, GLA_K_WIDTH), fixed),
                  pl.BlockSpec((1, GLA_K_WIDTH), fixed)],
        out_specs=out_specs,
        out_shape=out_shape,
        compiler_params=_params(("parallel",)),
        name="inproj",
    )(x2, mod3, g, w_r, wup, bup)


def _att_kernel(sinks_ref, q_ref, kv_ref, kvp_ref, o_ref, *, nblk):
    i = pl.program_id(1)
    blk = ATT_BLOCK
    lo = lax.broadcasted_iota(jnp.int32, (blk, LANES), 1) < ATT_HEAD_DIM
    qi = lax.broadcasted_iota(jnp.int32, (2 * blk, 2 * blk), 0) % blk
    kj = lax.broadcasted_iota(jnp.int32, (2 * blk, 2 * blk), 1)
    cur_ok = (kj >= blk) & ((kj - blk) <= qi)
    prev_ok = (kj < blk) & (kj > qi)
    first_off = jnp.where(i > 0, 0, blk)
    top = lax.broadcasted_iota(jnp.int32, (2 * blk, 1), 0) < blk
    for jb in range(nblk):
        r0 = jb * blk
        kvc = kv_ref[r0:r0 + blk, :]
        if jb == 0:
            kvp = kvp_ref[...]
            mask = cur_ok | (prev_ok & (kj >= first_off))
        else:
            kvp = kv_ref[r0 - blk:r0, :]
            mask = cur_ok | prev_ok
        kcat = jnp.concatenate([kvp[:, 0:ATT_KV_WIDTH], kvc[:, 0:ATT_KV_WIDTH]], axis=0)
        vcat = jnp.concatenate([kvp[:, ATT_KV_WIDTH:], kvc[:, ATT_KV_WIDTH:]], axis=0)
        for j in range(ATT_Q_HEADS // 2):
            qp = q_ref[r0:r0 + blk, j * LANES:(j + 1) * LANES]
            zero = jnp.zeros_like(qp)
            q2 = jnp.concatenate([jnp.where(lo, qp, zero), jnp.where(lo, zero, qp)], axis=0)
            s = jnp.where(mask, _dot_nt(q2, kcat), -jnp.inf)
            sink = jnp.where(top, sinks_ref[j], sinks_ref[ATT_Q_HEADS // 2 + j])
            m = jnp.maximum(jnp.max(s, axis=-1, keepdims=True), sink)
            p = jnp.exp(s - m)
            den = jnp.sum(p, axis=-1, keepdims=True) + jnp.exp(sink - m)
            o2 = _dot(p.astype(BF16), vcat) / den
            o = jnp.where(lo, o2[0:blk], o2[blk:2 * blk])
            o_ref[r0:r0 + blk, j * LANES:(j + 1) * LANES] = o.astype(BF16)


def _att(aq, akv, sinks, nbatch, seq, ta):
    t = aq.shape[0]
    nblk = ta // ATT_BLOCK
    per = seq // ta
    perb = seq // ATT_BLOCK
    return pl.pallas_call(
        functools.partial(_att_kernel, nblk=nblk),
        grid=(nbatch, per),
        in_specs=[pl.BlockSpec(memory_space=pltpu.SMEM),
                  pl.BlockSpec((ta, ATT_WIDTH), lambda b, i: (b * per + i, 0)),
                  pl.BlockSpec((ta, 2 * ATT_KV_WIDTH), lambda b, i: (b * per + i, 0)),
                  pl.BlockSpec((ATT_BLOCK, 2 * ATT_KV_WIDTH),
                               lambda b, i: (b * perb + jnp.maximum(i * nblk - 1, 0), 0))],
        out_specs=pl.BlockSpec((ta, ATT_WIDTH), lambda b, i: (b * per + i, 0)),
        out_shape=jax.ShapeDtypeStruct((t, ATT_WIDTH), BF16),
        compiler_params=_params(("parallel", "parallel")),
        name="att",
    )(sinks, aq, akv, akv)


def _gla_kernel(gq_ref, gk_ref, gv_ref, gg_ref, gl_ref, gn_ref, o_ref, st_ref, *, nchunk):
    ch = GLA_CHUNK
    tg = nchunk * ch
    nbatch = gq_ref.shape[0]

    @pl.when(pl.program_id(0) == 0)
    def _():
        st_ref[...] = jnp.zeros_like(st_ref)

    ri = lax.broadcasted_iota(jnp.int32, (tg, tg), 0)
    ci = lax.broadcasted_iota(jnp.int32, (tg, tg), 1)
    tri = ((ri // ch == ci // ch) & (ci <= ri)).astype(BF16)
    bts = []
    for s in range(nbatch):
        g0 = gl_ref[s]
        g1 = g0 - g0.astype(BF16).astype(F32)
        g2 = g1 - g1.astype(BF16).astype(F32)
        parts = _dot(tri, jnp.concatenate([g0.astype(BF16), g1.astype(BF16), g2.astype(BF16)], axis=1))
        bts.append((parts[:, 2 * GLA_K_WIDTH:] + parts[:, GLA_K_WIDTH:2 * GLA_K_WIDTH]) + parts[:, :GLA_K_WIDTH])
    row2 = lax.broadcasted_iota(jnp.int32, (2 * ch, LANES), 0)
    lane2 = lax.broadcasted_iota(jnp.int32, (2 * ch, LANES), 1)
    own = row2 // ch == lane2 // GLA_DK
    causal = (row2 // ch == lane2 // ch) & (lane2 % ch <= row2 % ch)
    gn = gn_ref[...]
    zero = jnp.zeros((2 * ch, LANES), BF16)

    def both(x):
        return jnp.where(own, jnp.concatenate([x, x], axis=0), zero)

    for c in range(nchunk):
        r0 = c * ch
        for s in range(nbatch):
            b = bts[s][r0:r0 + ch]
            bm = b[ch // 2 - 1:ch // 2]
            bl = b[ch - 1:ch]
            q = gq_ref[s, r0:r0 + ch, :].astype(F32)
            k = gk_ref[s, r0:r0 + ch, :].astype(F32)
            qe = (q * jnp.exp(b - bm)).astype(BF16)
            ke = (k * jnp.exp(bm - b)).astype(BF16)
            qs = (q * jnp.exp(b)).astype(BF16)
            kl = (k * jnp.exp(bl - b)).astype(BF16)
            dec = jnp.exp(bl)
            for p in range(GLA_HEADS // 2):
                sl = slice(p * LANES, (p + 1) * LANES)
                h0 = slice(2 * p * GLA_DV, (2 * p + 1) * GLA_DV)
                h1 = slice((2 * p + 1) * GLA_DV, (2 * p + 2) * GLA_DV)
                st = st_ref[s, p]
                a = _dot_nt(both(qe[:, sl]), both(ke[:, sl]))
                a = jnp.where(causal, a, 0.0).astype(BF16)
                v2 = jnp.concatenate([gv_ref[s, r0:r0 + ch, h0], gv_ref[s, r0:r0 + ch, h1]], axis=0)
                o = _dot(a, v2) + _dot_nt(both(qs[:, sl]), st.astype(BF16))
                upd = lax.dot_general(v2, both(kl[:, sl]), (((0,), (0,)), ((), ())),
                                      preferred_element_type=F32)
                gg = jnp.concatenate([gg_ref[s, r0:r0 + ch, h0], gg_ref[s, r0:r0 + ch, h1]], axis=0).astype(F32)
                res = (_rms(o) * gn * (gg * jax.nn.sigmoid(gg))).astype(BF16)
                o_ref[s, r0:r0 + ch, h0] = res[:ch]
                o_ref[s, r0:r0 + ch, h1] = res[ch:]
                st_ref[s, p] = st * dec[:, sl] + upd


def _gla(gq, gk, gv, gg, gl, gn, nbatch, seq, tg):
    t = gq.shape[0]
    tile = lambda w: pl.BlockSpec((nbatch, tg, w), lambda i: (0, i, 0))
    as3d = lambda a: a.reshape(nbatch, seq, a.shape[1])
    out = pl.pallas_call(
        functools.partial(_gla_kernel, nchunk=tg // GLA_CHUNK),
        grid=(seq // tg,),
        in_specs=[tile(GLA_K_WIDTH), tile(GLA_K_WIDTH), tile(GLA_V_WIDTH), tile(GLA_V_WIDTH), tile(GLA_K_WIDTH),
                  pl.BlockSpec((1, GLA_DV), lambda i: (0, 0))],
        out_specs=tile(GLA_V_WIDTH),
        out_shape=jax.ShapeDtypeStruct((nbatch, seq, GLA_V_WIDTH), BF16),
        scratch_shapes=[pltpu.VMEM((nbatch, GLA_HEADS // 2, GLA_DV, LANES), F32)],
        compiler_params=_params(("arbitrary",)),
        name="gla",
    )(as3d(gq), as3d(gk), as3d(gv), as3d(gg), as3d(gl), gn)
    return out.reshape(t, GLA_V_WIDTH)


def _outproj_kernel(att_ref, gla_ref, x_ref, mod_ref, gpost_ref, gpre_ref, woa_ref, wog_ref, wr_ref, br_ref,
                    x1_ref, h2_ref, route_ref, cnt_ref, tcnt_ref, base_ref, *, n_exp, tt, tc):
    tm = x_ref.shape[0]

    @pl.when(pl.program_id(0) == 0)
    def _():
        base_ref[...] = jnp.zeros_like(base_ref)

    nsub = tc // tt
    lane = lax.broadcasted_iota(jnp.int32, (tc, LANES), 1)
    ri = lax.broadcasted_iota(jnp.int32, (tc, tc), 0)
    ci = lax.broadcasted_iota(jnp.int32, (tc, tc), 1)
    below = ((ci < ri) & (ri // tt == ci // tt)).astype(BF16)
    lower = (lax.broadcasted_iota(jnp.int32, (LANES, LANES), 0)
             < lax.broadcasted_iota(jnp.int32, (LANES, LANES), 1)).astype(BF16)
    sub_row = lax.broadcasted_iota(jnp.int32, (8, LANES), 0)
    tok_sub = lax.broadcasted_iota(jnp.int32, (tc, LANES), 0) // tt
    all_cnt = jnp.zeros((8, LANES), F32)
    for ch in range(tm // tc):
        r = slice(ch * tc, (ch + 1) * tc)
        y = _dot(att_ref[r, :], woa_ref[...]) + _dot(gla_ref[r, :], wog_ref[...])
        x1 = x_ref[r, :] + mod_ref[2:3, :] * (_rms(y) * gpost_ref[...])
        x1_ref[r, :] = x1
        h2 = _rms(x1) * gpre_ref[...]
        h2 = h2 * (1.0 + mod_ref[4:5, :]) + mod_ref[3:4, :]
        h2_hi = h2.astype(BF16)
        h2_ref[r, :] = h2_hi

        h2_lo = (h2 - h2_hi.astype(F32)).astype(BF16)
        p_hi = _dot(h2_hi, wr_ref[...])
        p_lo = _dot(h2_lo, wr_ref[...])
        logits = ((p_lo[:, LANES:] + p_lo[:, :LANES]) + p_hi[:, LANES:]) + p_hi[:, :LANES] + br_ref[...]
        vals = jnp.where(lane < n_exp, logits, -jnp.inf)
        sels, tops, idxs = [], [], []
        for _ in range(TOP_K):
            m = jnp.max(vals, axis=-1, keepdims=True)
            idx = jnp.min(jnp.where(vals == m, lane, LANES), axis=-1, keepdims=True)
            sel = lane == idx
            vals = jnp.where(sel, -jnp.inf, vals)
            sels.append(sel)
            tops.append(m)
            idxs.append(idx)
        es = [jnp.exp(m - tops[0]) for m in tops]
        tot = es[0] + es[1] + es[2] + es[3]
        onehot = jnp.zeros((tc, LANES), F32)
        for sel in sels:
            onehot = onehot + sel.astype(F32)
        earlier = _dot(below, onehot.astype(BF16))
        sub_cnt = jnp.zeros((8, LANES), F32)
        for h in range(nsub):
            sub_cnt = jnp.where(sub_row == h, jnp.sum(onehot[h * tt:(h + 1) * tt], axis=0, keepdims=True), sub_cnt)
        run_start = _dot(sub_cnt.astype(BF16), lower)
        pos_all = earlier
        for h in range(nsub):
            pos_all = pos_all + jnp.where(tok_sub == h, run_start[h:h + 1, :], 0.0)
            all_cnt = jnp.where(sub_row == ch * nsub + h, sub_cnt[h:h + 1, :], all_cnt)
        route = jnp.zeros((tc, LANES), F32)
        for k in range(TOP_K):
            pos = jnp.sum(jnp.where(sels[k], pos_all, 0.0), axis=-1, keepdims=True)
            route = jnp.where(lane == k, idxs[k].astype(F32), route)
            route = jnp.where(lane == TOP_K + k, es[k] / tot, route)
            route = jnp.where(lane == 2 * TOP_K + k, pos, route)
        route_ref[r, :] = route
    tcnt_ref[...] = all_cnt
    base = base_ref[...] + jnp.sum(all_cnt, axis=0, keepdims=True)
    base_ref[...] = base
    cnt_ref[...] = jnp.broadcast_to(base, cnt_ref.shape)


def _outproj(att, gla, x2, mod3, gpost, gpre, woa, wog, wr, br, seq, tm, n_exp, tt):
    t, d = x2.shape
    per = seq // tm
    row = lambda i: (i, 0)
    fixed = lambda i: (0, 0)
    return pl.pallas_call(
        functools.partial(_outproj_kernel, n_exp=n_exp, tt=tt, tc=min(512, tm)),
        grid=(t // tm,),
        in_specs=[pl.BlockSpec((tm, ATT_WIDTH), row),
                  pl.BlockSpec((tm, GLA_V_WIDTH), row),
                  pl.BlockSpec((tm, d), row),
                  pl.BlockSpec((None, 8, d), lambda i: (i // per, 0, 0)),
                  pl.BlockSpec((1, d), fixed),
                  pl.BlockSpec((1, d), fixed),
                  pl.BlockSpec((ATT_WIDTH, d), fixed),
                  pl.BlockSpec((GLA_V_WIDTH, d), fixed),
                  pl.BlockSpec((d, 2 * LANES), fixed),
                  pl.BlockSpec((1, LANES), fixed)],
        out_specs=[pl.BlockSpec((tm, d), row),
                   pl.BlockSpec((tm, d), row),
                   pl.BlockSpec((tm, LANES), row),
                   pl.BlockSpec((8, LANES), fixed),
                   pl.BlockSpec((None, 8, LANES), lambda i: (i, 0, 0))],
        out_shape=[jax.ShapeDtypeStruct((t, d), F32),
                   jax.ShapeDtypeStruct((t, d), BF16),
                   jax.ShapeDtypeStruct((t, LANES), F32),
                   jax.ShapeDtypeStruct((8, LANES), F32),
                   jax.ShapeDtypeStruct((t // tm, 8, LANES), F32)],
        scratch_shapes=[pltpu.VMEM((1, LANES), F32)],
        compiler_params=_params(("arbitrary",)),
        name="outproj",
    )(att, gla, x2, mod3, gpost, gpre, woa, wog, wr, br)


def _dispatch_kernel(pend_ref, nu_ref, n_ref, cs_ref, gs_ref, prow_ref, h2_ref, xs_hbm, sb, zbuf, sem_r, sem_z,
                     *, tt, nsub, rows, n_exp, n_blocks):
    i = pl.program_id(0)
    nsteps = pl.num_programs(0)
    n_sorted = TOP_K * tt
    c = sb.shape[1] // n_sorted
    blk_rows = rows * c
    slot = i & 1

    def zero_fill(start):
        return pltpu.make_async_copy(zbuf, xs_hbm.at[pl.ds(pl.multiple_of(start, blk_rows), blk_rows), :], sem_z)

    def wait_runs(s):
        pltpu.make_async_copy(sb.at[s], xs_hbm.at[pl.ds(0, n_sorted * c), :], sem_r.at[s]).wait()

    @pl.when(i == 0)
    def _():
        zbuf[...] = jnp.zeros_like(zbuf)
        for phase in range(2):
            for e in range(n_exp):
                hi = pend_ref[e]
                lo = pend_ref[e - 1] if e > 0 else 0
                tail = nu_ref[0] + e
                for pred, start in ((hi > lo, (hi - rows) * c), (tail < n_blocks, tail * blk_rows)):
                    @pl.when(pred)
                    def _():
                        if phase == 0:
                            zero_fill(start).start()
                        else:
                            zero_fill(start).wait()

    @pl.when(i >= 2)
    def _():
        for h in range(nsub):
            wait_runs(slot * nsub + h)

    j = lax.broadcasted_iota(jnp.int32, (n_sorted, tt), 0)
    for h in range(nsub):
        perm = jnp.zeros((n_sorted, tt), F32)
        for k in range(TOP_K):
            perm = jnp.where(j == prow_ref[h, k:k + 1, :], 1.0, perm)
        srt = _dot(perm.astype(BF16), h2_ref[h * tt:(h + 1) * tt, :])
        _store_rows(sb.at[slot * nsub + h], _pack_bf16_valued(srt))

    for h in range(nsub):
        tile = i * nsub + h
        for e in range(n_exp):
            n = n_ref[tile * n_exp + e]

            @pl.when(n > 0)
            def _():
                src = pl.multiple_of(cs_ref[tile * n_exp + e], c)
                dst = pl.multiple_of(gs_ref[tile * n_exp + e], c)
                pltpu.make_async_copy(sb.at[slot * nsub + h, pl.ds(src, n), :], xs_hbm.at[pl.ds(dst, n), :],
                                      sem_r.at[slot * nsub + h]).start(priority=e % 2)

    @pl.when(i == nsteps - 1)
    def _():
        for h in range(nsub):
            @pl.when(nsteps > 1)
            def _():
                wait_runs((1 - slot) * nsub + h)
            wait_runs(slot * nsub + h)


def _dispatch(pend, n_used, n_tab, cs_tab, gs_tab, prow, h2, n_slots, rows, tt, nsub):
    t, d = h2.shape
    c = d // 2 // LANES
    n_exp = pend.shape[0]
    grid_spec = pltpu.PrefetchScalarGridSpec(
        num_scalar_prefetch=5,
        grid=(t // (nsub * tt),),
        in_specs=[pl.BlockSpec((nsub, 8, tt), lambda i, *_: (i, 0, 0)),
                  pl.BlockSpec((nsub * tt, d), lambda i, *_: (i, 0))],
        out_specs=pl.BlockSpec(memory_space=pl.ANY),
        scratch_shapes=[pltpu.VMEM((2 * nsub, TOP_K * tt * c, LANES), jnp.int32),
                        pltpu.VMEM((rows * c, LANES), jnp.int32),
                        pltpu.SemaphoreType.DMA((2 * nsub,)),
                        pltpu.SemaphoreType.DMA],
    )
    return pl.pallas_call(
        functools.partial(_dispatch_kernel, tt=tt, nsub=nsub, rows=rows, n_exp=n_exp, n_blocks=n_slots // rows),
        grid_spec=grid_spec,
        out_shape=jax.ShapeDtypeStruct((n_slots * c, LANES), jnp.int32),
        compiler_params=_params(("arbitrary",)),
        name="dispatch",
    )(pend, n_used, n_tab, cs_tab, gs_tab, prow, h2)


def _moe_kernel(nblk_ref, first_ref, nu_ref, xs_hbm, w1_ref, b1g_ref, b1l_ref, w2_ref, b2_ref, ys_hbm,
                w1g, w1l, w2b, xb, yb, sem_x, sem_y, *, rows, n_blocks, n_tail):
    e = pl.program_id(0)
    n = nblk_ref[e]
    c = xb.shape[1] // rows
    half = c * LANES
    blk = rows * c

    def rows_of(j):
        return pl.ds(pl.multiple_of((first_ref[e] + j) * blk, blk), blk)

    def x_copy(j, slot):
        return pltpu.make_async_copy(xs_hbm.at[rows_of(j), :], xb.at[slot], sem_x.at[slot])

    def y_copy(j, slot):
        return pltpu.make_async_copy(yb.at[slot], ys_hbm.at[rows_of(j), :], sem_y.at[slot])

    @pl.when(n > 0)
    def _():
        x_copy(0, 0).start(priority=1)
        w2b[...] = w2_ref[...].astype(BF16)
        g = 2 * LANES
        r = lax.broadcasted_iota(jnp.int32, (g, g), 0)
        col = lax.broadcasted_iota(jnp.int32, (g, g), 1)
        perm = (r == jnp.where(col < LANES, 2 * col, 2 * (col - LANES) + 1)).astype(BF16)
        for j in range(w1g.shape[1] // LANES):
            d = _dot(w1_ref[:, j * g:(j + 1) * g].astype(BF16), perm)
            w1g[:, j * LANES:(j + 1) * LANES] = d[:, :LANES].astype(BF16)
            w1l[:, j * LANES:(j + 1) * LANES] = d[:, LANES:].astype(BF16)

        def block(j, carry):
            slot = j & 1
            x_copy(j, slot).wait()

            @pl.when(j + 1 < n)
            def _():
                x_copy(j + 1, 1 - slot).start(priority=1)

            @pl.when(j >= 2)
            def _():
                y_copy(j - 2, slot).wait()

            lo, hi = _unpack(_load_rows(xb.at[slot], c))
            xl = lo.astype(BF16)
            xh = hi.astype(BF16)
            glu = _dot(xl, w1g[:half, :]) + _dot(xh, w1g[half:, :]) + b1g_ref[...]
            lin = _dot(xl, w1l[:half, :]) + _dot(xh, w1l[half:, :]) + b1l_ref[...]
            glu = jnp.minimum(glu, SWIGLU_LIMIT)
            lin = jnp.clip(lin, -SWIGLU_LIMIT, SWIGLU_LIMIT)
            a = glu * jax.nn.sigmoid(SWIGLU_ALPHA * glu) * (lin + 1.0)
            _store_rows(yb.at[slot], _pack(_dot(a.astype(BF16), w2b[...]) + b2_ref[...]))
            y_copy(j, slot).start(priority=1)
            return carry

        lax.fori_loop(0, n, block, 0)

        @pl.when(n >= 2)
        def _():
            y_copy(n - 2, n & 1).wait()
        y_copy(n - 1, (n - 1) & 1).wait()

    @pl.when(e == pl.num_programs(0) - 1)
    def _():
        yb[0] = jnp.zeros((blk, LANES), jnp.int32)
        for phase in range(2):
            for t in range(n_tail):
                tail = nu_ref[0] + t

                @pl.when(tail < n_blocks)
                def _():
                    fill = pltpu.make_async_copy(
                        yb.at[0], ys_hbm.at[pl.ds(pl.multiple_of(tail * blk, blk), blk), :], sem_y.at[0])
                    if phase == 0:
                        fill.start()
                    else:
                        fill.wait()


def _moe(n_blk, first_blk, n_used, xs, w1, b1g, b1l, w2, b2, rows):
    n_exp, d, f2 = w1.shape
    f = f2 // 2
    c = d // 2 // LANES
    wsel = lambda e, *_: (e, 0, 0)
    grid_spec = pltpu.PrefetchScalarGridSpec(
        num_scalar_prefetch=3,
        grid=(n_exp,),
        in_specs=[pl.BlockSpec(memory_space=pl.ANY),
                  pl.BlockSpec((None, d, f2), wsel),
                  pl.BlockSpec((None, 1, f), wsel),
                  pl.BlockSpec((None, 1, f), wsel),
                  pl.BlockSpec((None, f, d), wsel),
                  pl.BlockSpec((None, 1, d), wsel)],
        out_specs=pl.BlockSpec(memory_space=pl.ANY),
        scratch_shapes=[pltpu.VMEM((d, f), BF16), pltpu.VMEM((d, f), BF16), pltpu.VMEM((f, d), BF16),
                        pltpu.VMEM((2, rows * c, LANES), jnp.int32),
                        pltpu.VMEM((2, rows * c, LANES), jnp.int32),
                        pltpu.SemaphoreType.DMA((2,)),
                        pltpu.SemaphoreType.DMA((2,))],
    )
    return pl.pallas_call(
        functools.partial(_moe_kernel, rows=rows, n_blocks=xs.shape[0] // (rows * c), n_tail=n_exp),
        grid_spec=grid_spec,
        out_shape=jax.ShapeDtypeStruct(xs.shape, jnp.int32),
        compiler_params=pltpu.CompilerParams(dimension_semantics=("arbitrary",), vmem_limit_bytes=MOE_VMEM_LIMIT),
        name="moe",
    )(n_blk, first_blk, n_used, xs, w1, b1g, b1l, w2, b2)


def _combine_kernel(n_ref, cs_ref, gs_ref, ys_hbm, pg_ref, x1_ref, mod_ref, g_ref, o_ref, yb, sem_g,
                    *, tt, nsub, n_exp):
    i = pl.program_id(0)
    nsteps = pl.num_programs(0)
    d = o_ref.shape[1]
    half = d // 2
    c = half // LANES
    n_sorted = TOP_K * tt

    def issue(step):
        for h in range(nsub):
            tile = step * nsub + h
            buf = (step & 1) * nsub + h
            for e in range(n_exp):
                n = n_ref[tile * n_exp + e]

                @pl.when(n > 0)
                def _():
                    src = pl.multiple_of(gs_ref[tile * n_exp + e], c)
                    dst = pl.multiple_of(cs_ref[tile * n_exp + e], c)
                    pltpu.make_async_copy(ys_hbm.at[pl.ds(src, n), :], yb.at[buf, pl.ds(dst, n), :],
                                          sem_g.at[buf]).start(priority=e % 2)

    @pl.when(i == 0)
    def _():
        issue(0)

    @pl.when(i + 1 < nsteps)
    def _():
        issue(i + 1)

    slot = i & 1
    for h in range(nsub):
        buf = slot * nsub + h
        pltpu.make_async_copy(ys_hbm.at[pl.ds(0, n_sorted * c), :], yb.at[buf], sem_g.at[buf]).wait()

    j = lax.broadcasted_iota(jnp.int32, (tt, n_sorted), 1)
    for h in range(nsub):
        r = slice(h * tt, (h + 1) * tt)
        lo, hi = _unpack(_load_rows(yb.at[slot * nsub + h], c))
        sel = jnp.zeros((tt, n_sorted), F32)
        for k in range(TOP_K):
            pos = pg_ref[r, 2 * TOP_K + k:2 * TOP_K + k + 1].astype(jnp.int32)
            sel = jnp.where(j == pos, pg_ref[r, TOP_K + k:TOP_K + k + 1], sel)
        sel_hi = sel.astype(BF16)
        sel_lo = (sel - sel_hi.astype(F32)).astype(BF16)
        lo_b = lo.astype(BF16)
        hi_b = hi.astype(BF16)
        acc_lo = _dot(sel_lo, lo_b) + _dot(sel_hi, lo_b)
        acc_hi = _dot(sel_lo, hi_b) + _dot(sel_hi, hi_b)
        ms = (jnp.sum(acc_lo * acc_lo, axis=-1, keepdims=True)
              + jnp.sum(acc_hi * acc_hi, axis=-1, keepdims=True)) / d
        inv = lax.rsqrt(ms + NORM_EPS)
        o_ref[r, :half] = x1_ref[r, :half] + mod_ref[5:6, :half] * (acc_lo * inv * g_ref[:, :half])
        o_ref[r, half:] = x1_ref[r, half:] + mod_ref[5:6, half:] * (acc_hi * inv * g_ref[:, half:])


def _combine(n_tab, cs_tab, gs_tab, ys, pg, x1, mod3, g, seq, tt, nsub, n_exp):
    t, d = x1.shape
    tp = tt * nsub
    per = seq // tp
    c = d // 2 // LANES
    row = lambda i, *_: (i, 0)
    grid_spec = pltpu.PrefetchScalarGridSpec(
        num_scalar_prefetch=3,
        grid=(t // tp,),
        in_specs=[pl.BlockSpec(memory_space=pl.ANY),
                  pl.BlockSpec((tp, LANES), row),
                  pl.BlockSpec((tp, d), row),
                  pl.BlockSpec((None, 8, d), lambda i, *_: (i // per, 0, 0)),
                  pl.BlockSpec((1, d), lambda i, *_: (0, 0))],
        out_specs=pl.BlockSpec((tp, d), row),
        scratch_shapes=[pltpu.VMEM((2 * nsub, TOP_K * tt * c, LANES), jnp.int32),
                        pltpu.SemaphoreType.DMA((2 * nsub,))],
    )
    return pl.pallas_call(
        functools.partial(_combine_kernel, tt=tt, nsub=nsub, n_exp=n_exp),
        grid_spec=grid_spec,
        out_shape=jax.ShapeDtypeStruct((t, d), F32),
        compiler_params=_params(("arbitrary",)),
        name="combine",
    )(n_tab, cs_tab, gs_tab, ys, pg, x1, mod3, g)


def _pair_perm():
    half = ATT_Q_HEADS // 2
    idx = []
    for j in range(half):
        idx += list(range(j * ATT_HEAD_DIM, (j + 1) * ATT_HEAD_DIM))
        idx += list(range((half + j) * ATT_HEAD_DIM, (half + j + 1) * ATT_HEAD_DIM))
    return np.asarray(idx, np.int32)


def _layer(x, mod, g_pre_mix, g_post_mix, g_pre_ffn, g_post_ffn, w_in, w_gla_gate_up, b_gla_gate, g_gla_norm,
           sinks, w_out, w_router, b_router, w_mlp1, b_mlp1, w_mlp2, b_mlp2):
    nbatch, seq, d = x.shape
    t = nbatch * seq
    n_exp = w_router.shape[1]
    f = w_mlp2.shape[1]
    x2 = x.reshape(t, d)
    mod3 = jnp.pad(mod.reshape(nbatch, 6, d), ((0, 0), (0, 2), (0, 0)))

    perm = _pair_perm()
    n_main = C_GA - C_AKV
    w_r = jnp.concatenate([w_in[:, perm], w_in[:, ATT_WIDTH:ATT_WIDTH + n_main],
                           jnp.pad(w_in[:, ATT_WIDTH + n_main:], ((0, 0), (0, LANES - GLA_GATE_RANK)))],
                          axis=1).astype(BF16)
    wup = jnp.pad(w_gla_gate_up, ((0, LANES - GLA_GATE_RANK), (0, 0)))
    woa = w_out[:ATT_WIDTH][perm].astype(BF16)
    wog = w_out[ATT_WIDTH:].astype(BF16)
    wr = jnp.pad(w_router, ((0, 0), (0, LANES - n_exp)))
    wr_hi = wr.astype(BF16)
    wr = jnp.concatenate([wr_hi, (wr - wr_hi.astype(F32)).astype(BF16)], axis=1)
    br = jnp.pad(b_router, (0, LANES - n_exp)).reshape(1, LANES)
    b1 = b_mlp1.reshape(n_exp, 1, f, 2)

    tm = min(512, seq)
    tt = min(256, seq)
    to = min(1024, seq)
    aq, akv, gq, gk, gv, gg, gl = _inproj(x2, mod3, g_pre_mix.reshape(1, d), w_r, wup,
                                          b_gla_gate.reshape(1, GLA_K_WIDTH), seq, min(1024, seq))
    att = _att(aq, akv, sinks, nbatch, seq, min(512, seq))
    gla = _gla(gq, gk, gv, gg, gl, g_gla_norm.reshape(1, GLA_DV), nbatch, seq, min(256, seq))
    x1, h2, route, cnt, tcnt = _outproj(att, gla, x2, mod3, g_post_mix.reshape(1, d), g_pre_ffn.reshape(1, d),
                                        woa, wog, wr, br, seq, to, n_exp, tt)

    rows = MOE_ROWS
    counts = cnt[0, :n_exp].astype(jnp.int32)
    padded = (counts + rows - 1) // rows * rows
    pend = jnp.cumsum(padded)
    pstart = pend - padded
    n_slots = t * TOP_K + n_exp * rows
    n_blocks = n_slots // rows
    n_used = (pend[-1] // rows).astype(jnp.int32).reshape(1)
    n_tiles = t // tt
    sub = d // 2 // LANES
    tile_cnt = tcnt[:, :to // tt, :n_exp].reshape(n_tiles, n_exp).astype(jnp.int32)
    run_start = jnp.cumsum(tile_cnt, axis=1) - tile_cnt
    before = jnp.cumsum(tile_cnt, axis=0) - tile_cnt
    n_tab = (tile_cnt * sub).reshape(-1)
    cs_tab = (run_start * sub).reshape(-1)
    gs_tab = ((pstart[None, :] + before) * sub).reshape(-1)
    pos = route[:, 2 * TOP_K:3 * TOP_K].astype(jnp.int32)
    prow = jnp.pad(pos.reshape(n_tiles, tt, TOP_K).transpose(0, 2, 1), ((0, 0), (0, 8 - TOP_K), (0, 0)),
                   constant_values=-1)

    nsub = max(1, min(512, seq) // tt)
    xs = _dispatch(pend, n_used, n_tab, cs_tab, gs_tab, prow, h2, n_slots, rows, tt, nsub)
    ys = _moe(padded // rows, pstart // rows, n_used, xs, w_mlp1, b1[..., 0], b1[..., 1], w_mlp2, b_mlp2.reshape(n_exp, 1, d), rows)
    out = _combine(n_tab, cs_tab, gs_tab, ys, route, x1, mod3, g_post_ffn.reshape(1, d), seq, tt, nsub, n_exp)
    return out.reshape(nbatch, seq, d)


def kernel(x, c, w_ada, b_ada, g_pre_mix, g_post_mix, g_pre_ffn, g_post_ffn, w_in, w_gla_gate_up, b_gla_gate,
           g_gla_norm, sinks, w_out, w_router, b_router, w_mlp1, b_mlp1, w_mlp2, b_mlp2):
    for l in range(w_in.shape[0]):
        mod = _ada(c, w_ada[l], b_ada[l])
        x = _layer(x, mod, g_pre_mix[l], g_post_mix[l], g_pre_ffn[l], g_post_ffn[l], w_in[l], w_gla_gate_up[l],
                   b_gla_gate[l], g_gla_norm[l], sinks[l], w_out[l], w_router[l], b_router[l], w_mlp1[l], b_mlp1[l],
                   w_mlp2[l], b_mlp2[l])
    return x
```

```python
import functools

import numpy as np
import jax
import jax.numpy as jnp
from jax import lax
from jax.experimental import pallas as pl
from jax.experimental.pallas import tpu as pltpu

F32 = jnp.float32
BF16 = jnp.bfloat16
HI = lax.Precision.HIGHEST

ATT_Q_HEADS = 8
ATT_KV_HEADS = 2
ATT_HEAD_DIM = 64
ATT_BLOCK = 128
GLA_HEADS = 4
GLA_DK = 64
GLA_DV = 128
GLA_GATE_RANK = 16
GLA_GATE_NORMALIZER = 16.0
GLA_CHUNK = 64
TOP_K = 4
SWIGLU_LIMIT = 7.0
SWIGLU_ALPHA = 1.702
NORM_EPS = 1e-6

LANES = 128
ATT_WIDTH = ATT_Q_HEADS * ATT_HEAD_DIM
ATT_KV_WIDTH = ATT_KV_HEADS * ATT_HEAD_DIM
GLA_K_WIDTH = GLA_HEADS * GLA_DK
GLA_V_WIDTH = GLA_HEADS * GLA_DV
C_AQ = 0
C_AKV = C_AQ + ATT_WIDTH
C_GQ = C_AKV + 2 * ATT_KV_WIDTH
C_GK = C_GQ + GLA_K_WIDTH
C_GV = C_GK + GLA_K_WIDTH
C_GG = C_GV + GLA_V_WIDTH
C_GA = C_GG + GLA_V_WIDTH
C_END = C_GA + LANES

MOE_ROWS = 512
VMEM_LIMIT = 48 * 1024 * 1024
MOE_VMEM_LIMIT = 56 * 1024 * 1024


def _dot(a, b, prec=None):
    return jnp.dot(a, b, preferred_element_type=F32, precision=prec)


def _dot_nt(a, b):
    return lax.dot_general(a, b, (((1,), (1,)), ((), ())), preferred_element_type=F32)


def _rms(t):
    return t * lax.rsqrt(jnp.mean(t * t, axis=-1, keepdims=True) + NORM_EPS)


def _params(sem):
    return pltpu.CompilerParams(dimension_semantics=sem, vmem_limit_bytes=VMEM_LIMIT)


HI16 = -65536


def _pack(a):
    half = a.shape[1] // 2
    lo = lax.bitcast_convert_type(a[:, :half].astype(BF16).astype(F32), jnp.int32)
    hi = lax.bitcast_convert_type(a[:, half:].astype(BF16).astype(F32), jnp.int32)
    return hi | lax.shift_right_logical(lo, 16)


def _pack_bf16_valued(a):
    half = a.shape[1] // 2
    lo = lax.bitcast_convert_type(a[:, :half], jnp.int32)
    hi = lax.bitcast_convert_type(a[:, half:], jnp.int32)
    return hi | lax.shift_right_logical(lo, 16)


def _unpack(p):
    lo = lax.bitcast_convert_type(lax.shift_left(p, 16), F32)
    hi = lax.bitcast_convert_type(p & HI16, F32)
    return lo, hi


def _store_rows(ref, p):
    n, w = p.shape
    c = w // LANES
    for s in range(c):
        ref[pl.ds(s, n, stride=c), :] = p[:, s * LANES:(s + 1) * LANES]


def _load_rows(ref, c):
    n = ref.shape[0] // c
    return jnp.concatenate([ref[pl.ds(s, n, stride=c), :] for s in range(c)], axis=1)


def _ada_kernel(c_ref, w_ref, b_ref, o_ref):
    c = c_ref[...]
    o_ref[...] = _dot(c * jax.nn.sigmoid(c), w_ref[...], HI) + b_ref[...]


def _ada(c, w, b):
    nb, d = c.shape
    n = w.shape[1]
    cp = jnp.zeros((8, d), F32).at[:nb].set(c)
    out = pl.pallas_call(
        _ada_kernel,
        grid=(n // d,),
        in_specs=[pl.BlockSpec((8, d), lambda j: (0, 0)),
                  pl.BlockSpec((d, d), lambda j: (0, j)),
                  pl.BlockSpec((1, d), lambda j: (0, j))],
        out_specs=pl.BlockSpec((8, d), lambda j: (0, j)),
        out_shape=jax.ShapeDtypeStruct((8, n), F32),
        compiler_params=_params(("parallel",)),
        name="ada",
    )(cp, w, b.reshape(1, n))
    return out[:nb]


def _inproj_kernel(x_ref, mod_ref, g_ref, w_ref, wup_ref, bup_ref,
                   aq_ref, akv_ref, gq_ref, gk_ref, gv_ref, gg_ref, gl_ref):
    half = x_ref.shape[0] // 2
    for r in (slice(0, half), slice(half, 2 * half)):
        h = _rms(x_ref[r, :]) * g_ref[...]
        h = h * (1.0 + mod_ref[1:2, :]) + mod_ref[0:1, :]
        p = _dot(h.astype(BF16), w_ref[...])
        aq_ref[r, :] = (p[:, C_AQ:C_AKV] * (ATT_HEAD_DIM ** -0.5)).astype(BF16)
        akv_ref[r, :] = p[:, C_AKV:C_GQ].astype(BF16)
        gq_ref[r, :] = (p[:, C_GQ:C_GK] * (GLA_DK ** -0.5)).astype(BF16)
        gk_ref[r, :] = p[:, C_GK:C_GV].astype(BF16)
        gv_ref[r, :] = p[:, C_GV:C_GG].astype(BF16)
        gg_ref[r, :] = p[:, C_GG:C_GA].astype(BF16)
        z = _dot(p[:, C_GA:C_END], wup_ref[...], HI) + bup_ref[...]
        gl_ref[r, :] = (jnp.minimum(z, 0.0) - jnp.log(1.0 + jnp.exp(-jnp.abs(z)))) * (1.0 / GLA_GATE_NORMALIZER)


def _inproj(x2, mod3, g, w_r, wup, bup, seq, tm):
    t, d = x2.shape
    per = seq // tm
    row = lambda i: (i, 0)
    fixed = lambda i: (0, 0)
    widths = (ATT_WIDTH, 2 * ATT_KV_WIDTH, GLA_K_WIDTH, GLA_K_WIDTH, GLA_V_WIDTH, GLA_V_WIDTH)
    out_shape = [jax.ShapeDtypeStruct((t, w), BF16) for w in widths] + [jax.ShapeDtypeStruct((t, GLA_K_WIDTH), F32)]
    out_specs = [pl.BlockSpec((tm, w), row) for w in widths] + [pl.BlockSpec((tm, GLA_K_WIDTH), row)]
    return pl.pallas_call(
        _inproj_kernel,
        grid=(t // tm,),
        in_specs=[pl.BlockSpec((tm, d), row),
                  pl.BlockSpec((None, 8, d), lambda i: (i // per, 0, 0)),
                  pl.BlockSpec((1, d), fixed),
                  pl.BlockSpec((d, C_END), fixed),
                  pl.BlockSpec((LANES, GLA_K_WIDTH), fixed),
                  pl.BlockSpec((1, GLA_K_WIDTH), fixed)],
        out_specs=out_specs,
        out_shape=out_shape,
        compiler_params=_params(("parallel",)),
        name="inproj",
    )(x2, mod3, g, w_r, wup, bup)


def _att_kernel(sinks_ref, q_ref, kv_ref, kvp_ref, o_ref, *, nblk):
    i = pl.program_id(1)
    blk = ATT_BLOCK
    lo = lax.broadcasted_iota(jnp.int32, (blk, LANES), 1) < ATT_HEAD_DIM
    qi = lax.broadcasted_iota(jnp.int32, (2 * blk, 2 * blk), 0) % blk
    kj = lax.broadcasted_iota(jnp.int32, (2 * blk, 2 * blk), 1)
    cur_ok = (kj >= blk) & ((kj - blk) <= qi)
    prev_ok = (kj < blk) & (kj > qi)
    first_off = jnp.where(i > 0, 0, blk)
    top = lax.broadcasted_iota(jnp.int32, (2 * blk, 1), 0) < blk
    for jb in range(nblk):
        r0 = jb * blk
        kvc = kv_ref[r0:r0 + blk, :]
        if jb == 0:
            kvp = kvp_ref[...]
            mask = cur_ok | (prev_ok & (kj >= first_off))
        else:
            kvp = kv_ref[r0 - blk:r0, :]
            mask = cur_ok | prev_ok
        kcat = jnp.concatenate([kvp[:, 0:ATT_KV_WIDTH], kvc[:, 0:ATT_KV_WIDTH]], axis=0)
        vcat = jnp.concatenate([kvp[:, ATT_KV_WIDTH:], kvc[:, ATT_KV_WIDTH:]], axis=0)
        for j in range(ATT_Q_HEADS // 2):
            qp = q_ref[r0:r0 + blk, j * LANES:(j + 1) * LANES]
            zero = jnp.zeros_like(qp)
            q2 = jnp.concatenate([jnp.where(lo, qp, zero), jnp.where(lo, zero, qp)], axis=0)
            s = jnp.where(mask, _dot_nt(q2, kcat), -jnp.inf)
            sink = jnp.where(top, sinks_ref[j], sinks_ref[ATT_Q_HEADS // 2 + j])
            m = jnp.maximum(jnp.max(s, axis=-1, keepdims=True), sink)
            p = jnp.exp(s - m)
            den = jnp.sum(p, axis=-1, keepdims=True) + jnp.exp(sink - m)
            o2 = _dot(p.astype(BF16), vcat) / den
            o = jnp.where(lo, o2[0:blk], o2[blk:2 * blk])
            o_ref[r0:r0 + blk, j * LANES:(j + 1) * LANES] = o.astype(BF16)


def _att(aq, akv, sinks, nbatch, seq, ta):
    t = aq.shape[0]
    nblk = ta // ATT_BLOCK
    per = seq // ta
    perb = seq // ATT_BLOCK
    return pl.pallas_call(
        functools.partial(_att_kernel, nblk=nblk),
        grid=(nbatch, per),
        in_specs=[pl.BlockSpec(memory_space=pltpu.SMEM),
                  pl.BlockSpec((ta, ATT_WIDTH), lambda b, i: (b * per + i, 0)),
                  pl.BlockSpec((ta, 2 * ATT_KV_WIDTH), lambda b, i: (b * per + i, 0)),
                  pl.BlockSpec((ATT_BLOCK, 2 * ATT_KV_WIDTH),
                               lambda b, i: (b * perb + jnp.maximum(i * nblk - 1, 0), 0))],
        out_specs=pl.BlockSpec((ta, ATT_WIDTH), lambda b, i: (b * per + i, 0)),
        out_shape=jax.ShapeDtypeStruct((t, ATT_WIDTH), BF16),
        compiler_params=_params(("parallel", "parallel")),
        name="att",
    )(sinks, aq, akv, akv)


def _gla_kernel(gq_ref, gk_ref, gv_ref, gg_ref, gl_ref, gn_ref, o_ref, st_ref, *, nchunk):
    ch = GLA_CHUNK
    tg = nchunk * ch
    nbatch = gq_ref.shape[0]

    @pl.when(pl.program_id(0) == 0)
    def _():
        st_ref[...] = jnp.zeros_like(st_ref)

    ri = lax.broadcasted_iota(jnp.int32, (tg, tg), 0)
    ci = lax.broadcasted_iota(jnp.int32, (tg, tg), 1)
    tri = ((ri // ch == ci // ch) & (ci <= ri)).astype(BF16)
    bts = []
    for s in range(nbatch):
        g0 = gl_ref[s]
        g1 = g0 - g0.astype(BF16).astype(F32)
        g2 = g1 - g1.astype(BF16).astype(F32)
        parts = _dot(tri, jnp.concatenate([g0.astype(BF16), g1.astype(BF16), g2.astype(BF16)], axis=1))
        bts.append((parts[:, 2 * GLA_K_WIDTH:] + parts[:, GLA_K_WIDTH:2 * GLA_K_WIDTH]) + parts[:, :GLA_K_WIDTH])
    row2 = lax.broadcasted_iota(jnp.int32, (2 * ch, LANES), 0)
    lane2 = lax.broadcasted_iota(jnp.int32, (2 * ch, LANES), 1)
    own = row2 // ch == lane2 // GLA_DK
    causal = (row2 // ch == lane2 // ch) & (lane2 % ch <= row2 % ch)
    gn = gn_ref[...]
    zero = jnp.zeros((2 * ch, LANES), BF16)

    def both(x):
        return jnp.where(own, jnp.concatenate([x, x], axis=0), zero)

    for c in range(nchunk):
        r0 = c * ch
        for s in range(nbatch):
            b = bts[s][r0:r0 + ch]
            bm = b[ch // 2 - 1:ch // 2]
            bl = b[ch - 1:ch]
            q = gq_ref[s, r0:r0 + ch, :].astype(F32)
            k = gk_ref[s, r0:r0 + ch, :].astype(F32)
            qe = (q * jnp.exp(b - bm)).astype(BF16)
            ke = (k * jnp.exp(bm - b)).astype(BF16)
            qs = (q * jnp.exp(b)).astype(BF16)
            kl = (k * jnp.exp(bl - b)).astype(BF16)
            dec = jnp.exp(bl)
            for p in range(GLA_HEADS // 2):
                sl = slice(p * LANES, (p + 1) * LANES)
                h0 = slice(2 * p * GLA_DV, (2 * p + 1) * GLA_DV)
                h1 = slice((2 * p + 1) * GLA_DV, (2 * p + 2) * GLA_DV)
                st = st_ref[s, p]
                a = _dot_nt(both(qe[:, sl]), both(ke[:, sl]))
                a = jnp.where(causal, a, 0.0).astype(BF16)
                v2 = jnp.concatenate([gv_ref[s, r0:r0 + ch, h0], gv_ref[s, r0:r0 + ch, h1]], axis=0)
                o = _dot(a, v2) + _dot_nt(both(qs[:, sl]), st.astype(BF16))
                upd = lax.dot_general(v2, both(kl[:, sl]), (((0,), (0,)), ((), ())),
                                      preferred_element_type=F32)
                gg = jnp.concatenate([gg_ref[s, r0:r0 + ch, h0], gg_ref[s, r0:r0 + ch, h1]], axis=0).astype(F32)
                res = (_rms(o) * gn * (gg * jax.nn.sigmoid(gg))).astype(BF16)
                o_ref[s, r0:r0 + ch, h0] = res[:ch]
                o_ref[s, r0:r0 + ch, h1] = res[ch:]
                st_ref[s, p] = st * dec[:, sl] + upd


def _gla(gq, gk, gv, gg, gl, gn, nbatch, seq, tg):
    t = gq.shape[0]
    tile = lambda w: pl.BlockSpec((nbatch, tg, w), lambda i: (0, i, 0))
    as3d = lambda a: a.reshape(nbatch, seq, a.shape[1])
    out = pl.pallas_call(
        functools.partial(_gla_kernel, nchunk=tg // GLA_CHUNK),
        grid=(seq // tg,),
        in_specs=[tile(GLA_K_WIDTH), tile(GLA_K_WIDTH), tile(GLA_V_WIDTH), tile(GLA_V_WIDTH), tile(GLA_K_WIDTH),
                  pl.BlockSpec((1, GLA_DV), lambda i: (0, 0))],
        out_specs=tile(GLA_V_WIDTH),
        out_shape=jax.ShapeDtypeStruct((nbatch, seq, GLA_V_WIDTH), BF16),
        scratch_shapes=[pltpu.VMEM((nbatch, GLA_HEADS // 2, GLA_DV, LANES), F32)],
        compiler_params=_params(("arbitrary",)),
        name="gla",
    )(as3d(gq), as3d(gk), as3d(gv), as3d(gg), as3d(gl), gn)
    return out.reshape(t, GLA_V_WIDTH)


def _outproj_kernel(att_ref, gla_ref, x_ref, mod_ref, gpost_ref, gpre_ref, woa_ref, wog_ref, wr_ref, br_ref,
                    x1_ref, h2_ref, route_ref, cnt_ref, tcnt_ref, base_ref, *, n_exp, tt, tc):
    tm = x_ref.shape[0]

    @pl.when(pl.program_id(0) == 0)
    def _():
        base_ref[...] = jnp.zeros_like(base_ref)

    nsub = tc // tt
    lane = lax.broadcasted_iota(jnp.int32, (tc, LANES), 1)
    ri = lax.broadcasted_iota(jnp.int32, (tc, tc), 0)
    ci = lax.broadcasted_iota(jnp.int32, (tc, tc), 1)
    below = ((ci < ri) & (ri // tt == ci // tt)).astype(BF16)
    lower = (lax.broadcasted_iota(jnp.int32, (LANES, LANES), 0)
             < lax.broadcasted_iota(jnp.int32, (LANES, LANES), 1)).astype(BF16)
    sub_row = lax.broadcasted_iota(jnp.int32, (8, LANES), 0)
    tok_sub = lax.broadcasted_iota(jnp.int32, (tc, LANES), 0) // tt
    all_cnt = jnp.zeros((8, LANES), F32)
    for ch in range(tm // tc):
        r = slice(ch * tc, (ch + 1) * tc)
        y = _dot(att_ref[r, :], woa_ref[...]) + _dot(gla_ref[r, :], wog_ref[...])
        x1 = x_ref[r, :] + mod_ref[2:3, :] * (_rms(y) * gpost_ref[...])
        x1_ref[r, :] = x1
        h2 = _rms(x1) * gpre_ref[...]
        h2 = h2 * (1.0 + mod_ref[4:5, :]) + mod_ref[3:4, :]
        h2_hi = h2.astype(BF16)
        h2_ref[r, :] = h2_hi

        h2_lo = (h2 - h2_hi.astype(F32)).astype(BF16)
        p_hi = _dot(h2_hi, wr_ref[...])
        p_lo = _dot(h2_lo, wr_ref[...])
        logits = ((p_lo[:, LANES:] + p_lo[:, :LANES]) + p_hi[:, LANES:]) + p_hi[:, :LANES] + br_ref[...]
        vals = jnp.where(lane < n_exp, logits, -jnp.inf)
        sels, tops, idxs = [], [], []
        for _ in range(TOP_K):
            m = jnp.max(vals, axis=-1, keepdims=True)
            idx = jnp.min(jnp.where(vals == m, lane, LANES), axis=-1, keepdims=True)
            sel = lane == idx
            vals = jnp.where(sel, -jnp.inf, vals)
            sels.append(sel)
            tops.append(m)
            idxs.append(idx)
        es = [jnp.exp(m - tops[0]) for m in tops]
        tot = es[0] + es[1] + es[2] + es[3]
        onehot = jnp.zeros((tc, LANES), F32)
        for sel in sels:
            onehot = onehot + sel.astype(F32)
        earlier = _dot(below, onehot.astype(BF16))
        sub_cnt = jnp.zeros((8, LANES), F32)
        for h in range(nsub):
            sub_cnt = jnp.where(sub_row == h, jnp.sum(onehot[h * tt:(h + 1) * tt], axis=0, keepdims=True), sub_cnt)
        run_start = _dot(sub_cnt.astype(BF16), lower)
        pos_all = earlier
        for h in range(nsub):
            pos_all = pos_all + jnp.where(tok_sub == h, run_start[h:h + 1, :], 0.0)
            all_cnt = jnp.where(sub_row == ch * nsub + h, sub_cnt[h:h + 1, :], all_cnt)
        route = jnp.zeros((tc, LANES), F32)
        for k in range(TOP_K):
            pos = jnp.sum(jnp.where(sels[k], pos_all, 0.0), axis=-1, keepdims=True)
            route = jnp.where(lane == k, idxs[k].astype(F32), route)
            route = jnp.where(lane == TOP_K + k, es[k] / tot, route)
            route = jnp.where(lane == 2 * TOP_K + k, pos, route)
        route_ref[r, :] = route
    tcnt_ref[...] = all_cnt
    base = base_ref[...] + jnp.sum(all_cnt, axis=0, keepdims=True)
    base_ref[...] = base
    cnt_ref[...] = jnp.broadcast_to(base, cnt_ref.shape)


def _outproj(att, gla, x2, mod3, gpost, gpre, woa, wog, wr, br, seq, tm, n_exp, tt):
    t, d = x2.shape
    per = seq // tm
    row = lambda i: (i, 0)
    fixed = lambda i: (0, 0)
    return pl.pallas_call(
        functools.partial(_outproj_kernel, n_exp=n_exp, tt=tt, tc=min(512, tm)),
        grid=(t // tm,),
        in_specs=[pl.BlockSpec((tm, ATT_WIDTH), row),
                  pl.BlockSpec((tm, GLA_V_WIDTH), row),
                  pl.BlockSpec((tm, d), row),
                  pl.BlockSpec((None, 8, d), lambda i: (i // per, 0, 0)),
                  pl.BlockSpec((1, d), fixed),
                  pl.BlockSpec((1, d), fixed),
                  pl.BlockSpec((ATT_WIDTH, d), fixed),
                  pl.BlockSpec((GLA_V_WIDTH, d), fixed),
                  pl.BlockSpec((d, 2 * LANES), fixed),
                  pl.BlockSpec((1, LANES), fixed)],
        out_specs=[pl.BlockSpec((tm, d), row),
                   pl.BlockSpec((tm, d), row),
                   pl.BlockSpec((tm, LANES), row),
                   pl.BlockSpec((8, LANES), fixed),
                   pl.BlockSpec((None, 8, LANES), lambda i: (i, 0, 0))],
        out_shape=[jax.ShapeDtypeStruct((t, d), F32),
                   jax.ShapeDtypeStruct((t, d), BF16),
                   jax.ShapeDtypeStruct((t, LANES), F32),
                   jax.ShapeDtypeStruct((8, LANES), F32),
                   jax.ShapeDtypeStruct((t // tm, 8, LANES), F32)],
        scratch_shapes=[pltpu.VMEM((1, LANES), F32)],
        compiler_params=_params(("arbitrary",)),
        name="outproj",
    )(att, gla, x2, mod3, gpost, gpre, woa, wog, wr, br)


def _dispatch_kernel(pend_ref, nu_ref, n_ref, cs_ref, gs_ref, prow_ref, h2_ref, xs_hbm, sb, zbuf, sem_r, sem_z,
                     *, tt, nsub, rows, n_exp, n_blocks):
    i = pl.program_id(0)
    nsteps = pl.num_programs(0)
    n_sorted = TOP_K * tt
    c = sb.shape[1] // n_sorted
    blk_rows = rows * c
    slot = i & 1

    def zero_fill(start):
        return pltpu.make_async_copy(zbuf, xs_hbm.at[pl.ds(pl.multiple_of(start, blk_rows), blk_rows), :], sem_z)

    def wait_runs(s):
        pltpu.make_async_copy(sb.at[s], xs_hbm.at[pl.ds(0, n_sorted * c), :], sem_r.at[s]).wait()

    @pl.when(i == 0)
    def _():
        zbuf[...] = jnp.zeros_like(zbuf)
        for phase in range(2):
            for e in range(n_exp):
                hi = pend_ref[e]
                lo = pend_ref[e - 1] if e > 0 else 0
                tail = nu_ref[0] + e
                for pred, start in ((hi > lo, (hi - rows) * c), (tail < n_blocks, tail * blk_rows)):
                    @pl.when(pred)
                    def _():
                        if phase == 0:
                            zero_fill(start).start()
                        else:
                            zero_fill(start).wait()

    @pl.when(i >= 2)
    def _():
        for h in range(nsub):
            wait_runs(slot * nsub + h)

    j = lax.broadcasted_iota(jnp.int32, (n_sorted, tt), 0)
    for h in range(nsub):
        perm = jnp.zeros((n_sorted, tt), F32)
        for k in range(TOP_K):
            perm = jnp.where(j == prow_ref[h, k:k + 1, :], 1.0, perm)
        srt = _dot(perm.astype(BF16), h2_ref[h * tt:(h + 1) * tt, :])
        _store_rows(sb.at[slot * nsub + h], _pack_bf16_valued(srt))

    for h in range(nsub):
        tile = i * nsub + h
        for e in range(n_exp):
            n = n_ref[tile * n_exp + e]

            @pl.when(n > 0)
            def _():
                src = pl.multiple_of(cs_ref[tile * n_exp + e], c)
                dst = pl.multiple_of(gs_ref[tile * n_exp + e], c)
                pltpu.make_async_copy(sb.at[slot * nsub + h, pl.ds(src, n), :], xs_hbm.at[pl.ds(dst, n), :],
                                      sem_r.at[slot * nsub + h]).start(priority=e % 2)

    @pl.when(i == nsteps - 1)
    def _():
        for h in range(nsub):
            @pl.when(nsteps > 1)
            def _():
                wait_runs((1 - slot) * nsub + h)
            wait_runs(slot * nsub + h)


def _dispatch(pend, n_used, n_tab, cs_tab, gs_tab, prow, h2, n_slots, rows, tt, nsub):
    t, d = h2.shape
    c = d // 2 // LANES
    n_exp = pend.shape[0]
    grid_spec = pltpu.PrefetchScalarGridSpec(
        num_scalar_prefetch=5,
        grid=(t // (nsub * tt),),
        in_specs=[pl.BlockSpec((nsub, 8, tt), lambda i, *_: (i, 0, 0)),
                  pl.BlockSpec((nsub * tt, d), lambda i, *_: (i, 0))],
        out_specs=pl.BlockSpec(memory_space=pl.ANY),
        scratch_shapes=[pltpu.VMEM((2 * nsub, TOP_K * tt * c, LANES), jnp.int32),
                        pltpu.VMEM((rows * c, LANES), jnp.int32),
                        pltpu.SemaphoreType.DMA((2 * nsub,)),
                        pltpu.SemaphoreType.DMA],
    )
    return pl.pallas_call(
        functools.partial(_dispatch_kernel, tt=tt, nsub=nsub, rows=rows, n_exp=n_exp, n_blocks=n_slots // rows),
        grid_spec=grid_spec,
        out_shape=jax.ShapeDtypeStruct((n_slots * c, LANES), jnp.int32),
        compiler_params=_params(("arbitrary",)),
        name="dispatch",
    )(pend, n_used, n_tab, cs_tab, gs_tab, prow, h2)


def _moe_kernel(be_ref, nu_ref, xs_ref, w1_ref, b1g_ref, b1l_ref, w2_ref, b2_ref, ys_ref, w1g, w1l, *, rows):
    i = pl.program_id(0)
    c = xs_ref.shape[0] // rows
    half = c * LANES

    @pl.when((i < nu_ref[0]) & ((i == 0) | (be_ref[i] != be_ref[jnp.maximum(i - 1, 0)])))
    def _():
        n = 2 * LANES
        r = lax.broadcasted_iota(jnp.int32, (n, n), 0)
        col = lax.broadcasted_iota(jnp.int32, (n, n), 1)
        perm = (r == jnp.where(col < LANES, 2 * col, 2 * (col - LANES) + 1)).astype(BF16)
        for j in range(w1g.shape[1] // LANES):
            d = _dot(w1_ref[:, j * n:(j + 1) * n].astype(BF16), perm)
            w1g[:, j * LANES:(j + 1) * LANES] = d[:, :LANES].astype(BF16)
            w1l[:, j * LANES:(j + 1) * LANES] = d[:, LANES:].astype(BF16)

    @pl.when(i < nu_ref[0])
    def _():
        lo, hi = _unpack(_load_rows(xs_ref, c))
        xl = lo.astype(BF16)
        xh = hi.astype(BF16)
        glu = _dot(xl, w1g[:half, :]) + _dot(xh, w1g[half:, :]) + b1g_ref[...]
        lin = _dot(xl, w1l[:half, :]) + _dot(xh, w1l[half:, :]) + b1l_ref[...]
        glu = jnp.minimum(glu, SWIGLU_LIMIT)
        lin = jnp.clip(lin, -SWIGLU_LIMIT, SWIGLU_LIMIT)
        a = glu * jax.nn.sigmoid(SWIGLU_ALPHA * glu) * (lin + 1.0)
        _store_rows(ys_ref, _pack(_dot(a.astype(BF16), w2_ref[...].astype(BF16)) + b2_ref[...]))

    @pl.when(i >= nu_ref[0])
    def _():
        ys_ref[...] = jnp.zeros_like(ys_ref)


def _moe(block_e, n_used, xs, w1, b1g, b1l, w2, b2, rows):
    d, f2 = w1.shape[1], w1.shape[2]
    f = f2 // 2
    c = d // 2 // LANES
    nb = xs.shape[0] // (rows * c)
    wsel = lambda i, be, nu: (be[i], 0, 0)
    grid_spec = pltpu.PrefetchScalarGridSpec(
        num_scalar_prefetch=2,
        grid=(nb,),
        in_specs=[pl.BlockSpec((rows * c, LANES), lambda i, be, nu: (jnp.minimum(i, nu[0] - 1), 0)),
                  pl.BlockSpec((None, d, f2), wsel),
                  pl.BlockSpec((None, 1, f), wsel),
                  pl.BlockSpec((None, 1, f), wsel),
                  pl.BlockSpec((None, f, d), wsel),
                  pl.BlockSpec((None, 1, d), wsel)],
        out_specs=pl.BlockSpec((rows * c, LANES), lambda i, be, nu: (i, 0)),
        scratch_shapes=[pltpu.VMEM((d, f), BF16), pltpu.VMEM((d, f), BF16)],
    )
    return pl.pallas_call(
        functools.partial(_moe_kernel, rows=rows),
        grid_spec=grid_spec,
        out_shape=jax.ShapeDtypeStruct(xs.shape, jnp.int32),
        compiler_params=pltpu.CompilerParams(dimension_semantics=("arbitrary",), vmem_limit_bytes=MOE_VMEM_LIMIT),
        name="moe",
    )(block_e, n_used, xs, w1, b1g, b1l, w2, b2)


def _combine_kernel(n_ref, cs_ref, gs_ref, ys_hbm, pg_ref, x1_ref, mod_ref, g_ref, o_ref, yb, sem_g,
                    *, tt, nsub, n_exp):
    i = pl.program_id(0)
    nsteps = pl.num_programs(0)
    d = o_ref.shape[1]
    half = d // 2
    c = half // LANES
    n_sorted = TOP_K * tt

    def issue(step):
        for h in range(nsub):
            tile = step * nsub + h
            buf = (step & 1) * nsub + h
            for e in range(n_exp):
                n = n_ref[tile * n_exp + e]

                @pl.when(n > 0)
                def _():
                    src = pl.multiple_of(gs_ref[tile * n_exp + e], c)
                    dst = pl.multiple_of(cs_ref[tile * n_exp + e], c)
                    pltpu.make_async_copy(ys_hbm.at[pl.ds(src, n), :], yb.at[buf, pl.ds(dst, n), :],
                                          sem_g.at[buf]).start(priority=e % 2)

    @pl.when(i == 0)
    def _():
        issue(0)

    @pl.when(i + 1 < nsteps)
    def _():
        issue(i + 1)

    slot = i & 1
    for h in range(nsub):
        buf = slot * nsub + h
        pltpu.make_async_copy(ys_hbm.at[pl.ds(0, n_sorted * c), :], yb.at[buf], sem_g.at[buf]).wait()

    j = lax.broadcasted_iota(jnp.int32, (tt, n_sorted), 1)
    for h in range(nsub):
        r = slice(h * tt, (h + 1) * tt)
        lo, hi = _unpack(_load_rows(yb.at[slot * nsub + h], c))
        sel = jnp.zeros((tt, n_sorted), F32)
        for k in range(TOP_K):
            pos = pg_ref[r, 2 * TOP_K + k:2 * TOP_K + k + 1].astype(jnp.int32)
            sel = jnp.where(j == pos, pg_ref[r, TOP_K + k:TOP_K + k + 1], sel)
        sel_hi = sel.astype(BF16)
        sel_lo = (sel - sel_hi.astype(F32)).astype(BF16)
        lo_b = lo.astype(BF16)
        hi_b = hi.astype(BF16)
        acc_lo = _dot(sel_lo, lo_b) + _dot(sel_hi, lo_b)
        acc_hi = _dot(sel_lo, hi_b) + _dot(sel_hi, hi_b)
        ms = (jnp.sum(acc_lo * acc_lo, axis=-1, keepdims=True)
              + jnp.sum(acc_hi * acc_hi, axis=-1, keepdims=True)) / d
        inv = lax.rsqrt(ms + NORM_EPS)
        o_ref[r, :half] = x1_ref[r, :half] + mod_ref[5:6, :half] * (acc_lo * inv * g_ref[:, :half])
        o_ref[r, half:] = x1_ref[r, half:] + mod_ref[5:6, half:] * (acc_hi * inv * g_ref[:, half:])


def _combine(n_tab, cs_tab, gs_tab, ys, pg, x1, mod3, g, seq, tt, nsub, n_exp):
    t, d = x1.shape
    tp = tt * nsub
    per = seq // tp
    c = d // 2 // LANES
    row = lambda i, *_: (i, 0)
    grid_spec = pltpu.PrefetchScalarGridSpec(
        num_scalar_prefetch=3,
        grid=(t // tp,),
        in_specs=[pl.BlockSpec(memory_space=pl.ANY),
                  pl.BlockSpec((tp, LANES), row),
                  pl.BlockSpec((tp, d), row),
                  pl.BlockSpec((None, 8, d), lambda i, *_: (i // per, 0, 0)),
                  pl.BlockSpec((1, d), lambda i, *_: (0, 0))],
        out_specs=pl.BlockSpec((tp, d), row),
        scratch_shapes=[pltpu.VMEM((2 * nsub, TOP_K * tt * c, LANES), jnp.int32),
                        pltpu.SemaphoreType.DMA((2 * nsub,))],
    )
    return pl.pallas_call(
        functools.partial(_combine_kernel, tt=tt, nsub=nsub, n_exp=n_exp),
        grid_spec=grid_spec,
        out_shape=jax.ShapeDtypeStruct((t, d), F32),
        compiler_params=_params(("arbitrary",)),
        name="combine",
    )(n_tab, cs_tab, gs_tab, ys, pg, x1, mod3, g)


def _pair_perm():
    half = ATT_Q_HEADS // 2
    idx = []
    for j in range(half):
        idx += list(range(j * ATT_HEAD_DIM, (j + 1) * ATT_HEAD_DIM))
        idx += list(range((half + j) * ATT_HEAD_DIM, (half + j + 1) * ATT_HEAD_DIM))
    return np.asarray(idx, np.int32)


def _layer(x, mod, g_pre_mix, g_post_mix, g_pre_ffn, g_post_ffn, w_in, w_gla_gate_up, b_gla_gate, g_gla_norm,
           sinks, w_out, w_router, b_router, w_mlp1, b_mlp1, w_mlp2, b_mlp2):
    nbatch, seq, d = x.shape
    t = nbatch * seq
    n_exp = w_router.shape[1]
    f = w_mlp2.shape[1]
    x2 = x.reshape(t, d)
    mod3 = jnp.pad(mod.reshape(nbatch, 6, d), ((0, 0), (0, 2), (0, 0)))

    perm = _pair_perm()
    n_main = C_GA - C_AKV
    w_r = jnp.concatenate([w_in[:, perm], w_in[:, ATT_WIDTH:ATT_WIDTH + n_main],
                           jnp.pad(w_in[:, ATT_WIDTH + n_main:], ((0, 0), (0, LANES - GLA_GATE_RANK)))],
                          axis=1).astype(BF16)
    wup = jnp.pad(w_gla_gate_up, ((0, LANES - GLA_GATE_RANK), (0, 0)))
    woa = w_out[:ATT_WIDTH][perm].astype(BF16)
    wog = w_out[ATT_WIDTH:].astype(BF16)
    wr = jnp.pad(w_router, ((0, 0), (0, LANES - n_exp)))
    wr_hi = wr.astype(BF16)
    wr = jnp.concatenate([wr_hi, (wr - wr_hi.astype(F32)).astype(BF16)], axis=1)
    br = jnp.pad(b_router, (0, LANES - n_exp)).reshape(1, LANES)
    b1 = b_mlp1.reshape(n_exp, 1, f, 2)

    tm = min(512, seq)
    tt = min(256, seq)
    to = min(1024, seq)
    aq, akv, gq, gk, gv, gg, gl = _inproj(x2, mod3, g_pre_mix.reshape(1, d), w_r, wup,
                                          b_gla_gate.reshape(1, GLA_K_WIDTH), seq, min(1024, seq))
    att = _att(aq, akv, sinks, nbatch, seq, min(512, seq))
    gla = _gla(gq, gk, gv, gg, gl, g_gla_norm.reshape(1, GLA_DV), nbatch, seq, min(512, seq))
    x1, h2, route, cnt, tcnt = _outproj(att, gla, x2, mod3, g_post_mix.reshape(1, d), g_pre_ffn.reshape(1, d),
                                        woa, wog, wr, br, seq, to, n_exp, tt)

    rows = MOE_ROWS
    counts = cnt[0, :n_exp].astype(jnp.int32)
    padded = (counts + rows - 1) // rows * rows
    pend = jnp.cumsum(padded)
    pstart = pend - padded
    n_slots = t * TOP_K + n_exp * rows
    n_blocks = n_slots // rows
    n_used = (pend[-1] // rows).astype(jnp.int32).reshape(1)
    n_tiles = t // tt
    sub = d // 2 // LANES
    tile_cnt = tcnt[:, :to // tt, :n_exp].reshape(n_tiles, n_exp).astype(jnp.int32)
    run_start = jnp.cumsum(tile_cnt, axis=1) - tile_cnt
    before = jnp.cumsum(tile_cnt, axis=0) - tile_cnt
    n_tab = (tile_cnt * sub).reshape(-1)
    cs_tab = (run_start * sub).reshape(-1)
    gs_tab = ((pstart[None, :] + before) * sub).reshape(-1)
    pos = route[:, 2 * TOP_K:3 * TOP_K].astype(jnp.int32)
    prow = jnp.pad(pos.reshape(n_tiles, tt, TOP_K).transpose(0, 2, 1), ((0, 0), (0, 8 - TOP_K), (0, 0)),
                   constant_values=-1)

    nsub = max(1, min(1024, seq) // tt)
    xs = _dispatch(pend, n_used, n_tab, cs_tab, gs_tab, prow, h2, n_slots, rows, tt, nsub)
    blk_ids = jnp.minimum(jnp.arange(n_blocks, dtype=jnp.int32), n_used - 1)
    block_e = jnp.minimum(jnp.sum(pend[None, :] <= (blk_ids * rows)[:, None], axis=1), n_exp - 1).astype(jnp.int32)
    ys = _moe(block_e, n_used, xs, w_mlp1, b1[..., 0], b1[..., 1], w_mlp2, b_mlp2.reshape(n_exp, 1, d), rows)
    out = _combine(n_tab, cs_tab, gs_tab, ys, route, x1, mod3, g_post_ffn.reshape(1, d), seq, tt, nsub, n_exp)
    return out.reshape(nbatch, seq, d)


def kernel(x, c, w_ada, b_ada, g_pre_mix, g_post_mix, g_pre_ffn, g_post_ffn, w_in, w_gla_gate_up, b_gla_gate,
           g_gla_norm, sinks, w_out, w_router, b_router, w_mlp1, b_mlp1, w_mlp2, b_mlp2):
    for l in range(w_in.shape[0]):
        mod = _ada(c, w_ada[l], b_ada[l])
        x = _layer(x, mod, g_pre_mix[l], g_post_mix[l], g_pre_ffn[l], g_post_ffn[l], w_in[l], w_gla_gate_up[l],
                   b_gla_gate[l], g_gla_norm[l], sinks[l], w_out[l], w_router[l], b_router[l], w_mlp1[l], b_mlp1[l],
                   w_mlp2[l], b_mlp2[l])
    return x
```

```python
import functools

import numpy as np
import jax
import jax.numpy as jnp
from jax import lax
from jax.experimental import pallas as pl
from jax.experimental.pallas import tpu as pltpu

F32 = jnp.float32
BF16 = jnp.bfloat16
HI = lax.Precision.HIGHEST

ATT_Q_HEADS = 8
ATT_KV_HEADS = 2
ATT_HEAD_DIM = 64
ATT_BLOCK = 128
GLA_HEADS = 4
GLA_DK = 64
GLA_DV = 128
GLA_GATE_RANK = 16
GLA_GATE_NORMALIZER = 16.0
GLA_CHUNK = 64
TOP_K = 4
SWIGLU_LIMIT = 7.0
SWIGLU_ALPHA = 1.702
NORM_EPS = 1e-6

LANES = 128
ATT_WIDTH = ATT_Q_HEADS * ATT_HEAD_DIM
ATT_KV_WIDTH = ATT_KV_HEADS * ATT_HEAD_DIM
GLA_K_WIDTH = GLA_HEADS * GLA_DK
GLA_V_WIDTH = GLA_HEADS * GLA_DV
C_AQ = 0
C_AKV = C_AQ + ATT_WIDTH
C_GQ = C_AKV + 2 * ATT_KV_WIDTH
C_GK = C_GQ + GLA_K_WIDTH
C_GV = C_GK + GLA_K_WIDTH
C_GG = C_GV + GLA_V_WIDTH
C_GA = C_GG + GLA_V_WIDTH
C_END = C_GA + LANES

MOE_ROWS = 512
VMEM_LIMIT = 48 * 1024 * 1024
MOE_VMEM_LIMIT = 56 * 1024 * 1024


def _dot(a, b, prec=None):
    return jnp.dot(a, b, preferred_element_type=F32, precision=prec)


def _dot_nt(a, b):
    return lax.dot_general(a, b, (((1,), (1,)), ((), ())), preferred_element_type=F32)


def _rms(t):
    return t * lax.rsqrt(jnp.mean(t * t, axis=-1, keepdims=True) + NORM_EPS)


def _params(sem):
    return pltpu.CompilerParams(dimension_semantics=sem, vmem_limit_bytes=VMEM_LIMIT)


HI16 = -65536


def _pack(a):
    half = a.shape[1] // 2
    lo = lax.bitcast_convert_type(a[:, :half].astype(BF16).astype(F32), jnp.int32)
    hi = lax.bitcast_convert_type(a[:, half:].astype(BF16).astype(F32), jnp.int32)
    return hi | lax.shift_right_logical(lo, 16)


def _pack_bf16_valued(a):
    half = a.shape[1] // 2
    lo = lax.bitcast_convert_type(a[:, :half], jnp.int32)
    hi = lax.bitcast_convert_type(a[:, half:], jnp.int32)
    return hi | lax.shift_right_logical(lo, 16)


def _unpack(p):
    lo = lax.bitcast_convert_type(lax.shift_left(p, 16), F32)
    hi = lax.bitcast_convert_type(p & HI16, F32)
    return lo, hi


def _store_rows(ref, p):
    n, w = p.shape
    c = w // LANES
    for s in range(c):
        ref[pl.ds(s, n, stride=c), :] = p[:, s * LANES:(s + 1) * LANES]


def _load_rows(ref, c):
    n = ref.shape[0] // c
    return jnp.concatenate([ref[pl.ds(s, n, stride=c), :] for s in range(c)], axis=1)


def _ada_kernel(c_ref, w_ref, b_ref, o_ref):
    c = c_ref[...]
    o_ref[...] = _dot(c * jax.nn.sigmoid(c), w_ref[...], HI) + b_ref[...]


def _ada(c, w, b):
    nb, d = c.shape
    n = w.shape[1]
    cp = jnp.zeros((8, d), F32).at[:nb].set(c)
    out = pl.pallas_call(
        _ada_kernel,
        grid=(n // d,),
        in_specs=[pl.BlockSpec((8, d), lambda j: (0, 0)),
                  pl.BlockSpec((d, d), lambda j: (0, j)),
                  pl.BlockSpec((1, d), lambda j: (0, j))],
        out_specs=pl.BlockSpec((8, d), lambda j: (0, j)),
        out_shape=jax.ShapeDtypeStruct((8, n), F32),
        compiler_params=_params(("parallel",)),
        name="ada",
    )(cp, w, b.reshape(1, n))
    return out[:nb]


def _inproj_kernel(x_ref, mod_ref, g_ref, w_ref, wup_ref, bup_ref,
                   aq_ref, akv_ref, gq_ref, gk_ref, gv_ref, gg_ref, gl_ref):
    half = x_ref.shape[0] // 2
    for r in (slice(0, half), slice(half, 2 * half)):
        h = _rms(x_ref[r, :]) * g_ref[...]
        h = h * (1.0 + mod_ref[1:2, :]) + mod_ref[0:1, :]
        p = _dot(h.astype(BF16), w_ref[...])
        aq_ref[r, :] = (p[:, C_AQ:C_AKV] * (ATT_HEAD_DIM ** -0.5)).astype(BF16)
        akv_ref[r, :] = p[:, C_AKV:C_GQ].astype(BF16)
        gq_ref[r, :] = (p[:, C_GQ:C_GK] * (GLA_DK ** -0.5)).astype(BF16)
        gk_ref[r, :] = p[:, C_GK:C_GV].astype(BF16)
        gv_ref[r, :] = p[:, C_GV:C_GG].astype(BF16)
        gg_ref[r, :] = p[:, C_GG:C_GA].astype(BF16)
        z = _dot(p[:, C_GA:C_END], wup_ref[...], HI) + bup_ref[...]
        gl_ref[r, :] = (jnp.minimum(z, 0.0) - jnp.log(1.0 + jnp.exp(-jnp.abs(z)))) * (1.0 / GLA_GATE_NORMALIZER)


def _inproj(x2, mod3, g, w_r, wup, bup, seq, tm):
    t, d = x2.shape
    per = seq // tm
    row = lambda i: (i, 0)
    fixed = lambda i: (0, 0)
    widths = (ATT_WIDTH, 2 * ATT_KV_WIDTH, GLA_K_WIDTH, GLA_K_WIDTH, GLA_V_WIDTH, GLA_V_WIDTH)
    out_shape = [jax.ShapeDtypeStruct((t, w), BF16) for w in widths] + [jax.ShapeDtypeStruct((t, GLA_K_WIDTH), F32)]
    out_specs = [pl.BlockSpec((tm, w), row) for w in widths] + [pl.BlockSpec((tm, GLA_K_WIDTH), row)]
    return pl.pallas_call(
        _inproj_kernel,
        grid=(t // tm,),
        in_specs=[pl.BlockSpec((tm, d), row),
                  pl.BlockSpec((None, 8, d), lambda i: (i // per, 0, 0)),
                  pl.BlockSpec((1, d), fixed),
                  pl.BlockSpec((d, C_END), fixed),
                  pl.BlockSpec((LANES, GLA_K_WIDTH), fixed),
                  pl.BlockSpec((1, GLA_K_WIDTH), fixed)],
        out_specs=out_specs,
        out_shape=out_shape,
        compiler_params=_params(("parallel",)),
        name="inproj",
    )(x2, mod3, g, w_r, wup, bup)


def _mix_kernel(sinks_ref, q_ref, kv_ref, kvp_ref, gq_ref, gk_ref, gv_ref, gg_ref, gl_ref, gn_ref,
                att_ref, gla_ref, st_ref, *, rows):
    i = pl.program_id(0)
    nbatch = q_ref.shape[0]
    blk = ATT_BLOCK
    ch = GLA_CHUNK

    @pl.when(i == 0)
    def _():
        st_ref[...] = jnp.zeros_like(st_ref)

    lo = lax.broadcasted_iota(jnp.int32, (blk, LANES), 1) < ATT_HEAD_DIM
    qi = lax.broadcasted_iota(jnp.int32, (2 * blk, 2 * blk), 0) % blk
    kj = lax.broadcasted_iota(jnp.int32, (2 * blk, 2 * blk), 1)
    cur_ok = (kj >= blk) & ((kj - blk) <= qi)
    prev_ok = (kj < blk) & (kj > qi)
    first_off = jnp.where(i > 0, 0, blk)
    top = lax.broadcasted_iota(jnp.int32, (2 * blk, 1), 0) < blk

    def att_block(s, jb):
        r0 = jb * blk
        kvc = kv_ref[s, r0:r0 + blk, :]
        if jb == 0:
            kvp = kvp_ref[s]
            mask = cur_ok | (prev_ok & (kj >= first_off))
        else:
            kvp = kv_ref[s, r0 - blk:r0, :]
            mask = cur_ok | prev_ok
        kcat = jnp.concatenate([kvp[:, 0:ATT_KV_WIDTH], kvc[:, 0:ATT_KV_WIDTH]], axis=0)
        vcat = jnp.concatenate([kvp[:, ATT_KV_WIDTH:], kvc[:, ATT_KV_WIDTH:]], axis=0)
        for j in range(ATT_Q_HEADS // 2):
            qp = q_ref[s, r0:r0 + blk, j * LANES:(j + 1) * LANES]
            zero = jnp.zeros_like(qp)
            q2 = jnp.concatenate([jnp.where(lo, qp, zero), jnp.where(lo, zero, qp)], axis=0)
            sc = jnp.where(mask, _dot_nt(q2, kcat), -jnp.inf)
            sink = jnp.where(top, sinks_ref[j], sinks_ref[ATT_Q_HEADS // 2 + j])
            m = jnp.maximum(jnp.max(sc, axis=-1, keepdims=True), sink)
            p = jnp.exp(sc - m)
            den = jnp.sum(p, axis=-1, keepdims=True) + jnp.exp(sink - m)
            o2 = _dot(p.astype(BF16), vcat) / den
            o = jnp.where(lo, o2[0:blk], o2[blk:2 * blk])
            att_ref[s, r0:r0 + blk, j * LANES:(j + 1) * LANES] = o.astype(BF16)

    ri = lax.broadcasted_iota(jnp.int32, (rows, rows), 0)
    ci = lax.broadcasted_iota(jnp.int32, (rows, rows), 1)
    tri = ((ri // ch == ci // ch) & (ci <= ri)).astype(BF16)
    bts = []
    for s in range(nbatch):
        g0 = gl_ref[s]
        g1 = g0 - g0.astype(BF16).astype(F32)
        g2 = g1 - g1.astype(BF16).astype(F32)
        parts = _dot(tri, jnp.concatenate([g0.astype(BF16), g1.astype(BF16), g2.astype(BF16)], axis=1))
        bts.append((parts[:, 2 * GLA_K_WIDTH:] + parts[:, GLA_K_WIDTH:2 * GLA_K_WIDTH]) + parts[:, :GLA_K_WIDTH])
    row2 = lax.broadcasted_iota(jnp.int32, (2 * ch, LANES), 0)
    lane2 = lax.broadcasted_iota(jnp.int32, (2 * ch, LANES), 1)
    own = row2 // ch == lane2 // GLA_DK
    causal = (row2 // ch == lane2 // ch) & (lane2 % ch <= row2 % ch)
    gn = gn_ref[...]
    zero2 = jnp.zeros((2 * ch, LANES), BF16)

    def both(x):
        return jnp.where(own, jnp.concatenate([x, x], axis=0), zero2)

    def gla_chunk(s, c):
        r0 = c * ch
        b = bts[s][r0:r0 + ch]
        bm = b[ch // 2 - 1:ch // 2]
        bl = b[ch - 1:ch]
        q = gq_ref[s, r0:r0 + ch, :].astype(F32)
        k = gk_ref[s, r0:r0 + ch, :].astype(F32)
        qe = (q * jnp.exp(b - bm)).astype(BF16)
        ke = (k * jnp.exp(bm - b)).astype(BF16)
        qs = (q * jnp.exp(b)).astype(BF16)
        kl = (k * jnp.exp(bl - b)).astype(BF16)
        dec = jnp.exp(bl)
        for p in range(GLA_HEADS // 2):
            sl = slice(p * LANES, (p + 1) * LANES)
            h0 = slice(2 * p * GLA_DV, (2 * p + 1) * GLA_DV)
            h1 = slice((2 * p + 1) * GLA_DV, (2 * p + 2) * GLA_DV)
            st = st_ref[s, p]
            a = _dot_nt(both(qe[:, sl]), both(ke[:, sl]))
            a = jnp.where(causal, a, 0.0).astype(BF16)
            v2 = jnp.concatenate([gv_ref[s, r0:r0 + ch, h0], gv_ref[s, r0:r0 + ch, h1]], axis=0)
            o = _dot(a, v2) + _dot_nt(both(qs[:, sl]), st.astype(BF16))
            upd = lax.dot_general(v2, both(kl[:, sl]), (((0,), (0,)), ((), ())),
                                  preferred_element_type=F32)
            gg = jnp.concatenate([gg_ref[s, r0:r0 + ch, h0], gg_ref[s, r0:r0 + ch, h1]], axis=0).astype(F32)
            res = (_rms(o) * gn * (gg * jax.nn.sigmoid(gg))).astype(BF16)
            gla_ref[s, r0:r0 + ch, h0] = res[:ch]
            gla_ref[s, r0:r0 + ch, h1] = res[ch:]
            st_ref[s, p] = st * dec[:, sl] + upd

    per_blk = blk // ch
    for jb in range(rows // blk):
        for s in range(nbatch):
            att_block(s, jb)
        for c in range(jb * per_blk, (jb + 1) * per_blk):
            for s in range(nbatch):
                gla_chunk(s, c)


def _mix(aq, akv, sinks, gq, gk, gv, gg, gl, gn, nbatch, seq, rows):
    t = aq.shape[0]
    nblk = rows // ATT_BLOCK
    tile = lambda w: pl.BlockSpec((nbatch, rows, w), lambda i: (0, i, 0))
    as3d = lambda a: a.reshape(nbatch, seq, a.shape[1])
    att, gla = pl.pallas_call(
        functools.partial(_mix_kernel, rows=rows),
        grid=(seq // rows,),
        in_specs=[pl.BlockSpec(memory_space=pltpu.SMEM),
                  tile(ATT_WIDTH), tile(2 * ATT_KV_WIDTH),
                  pl.BlockSpec((nbatch, ATT_BLOCK, 2 * ATT_KV_WIDTH), lambda i: (0, jnp.maximum(i * nblk - 1, 0), 0)),
                  tile(GLA_K_WIDTH), tile(GLA_K_WIDTH), tile(GLA_V_WIDTH), tile(GLA_V_WIDTH), tile(GLA_K_WIDTH),
                  pl.BlockSpec((1, GLA_DV), lambda i: (0, 0))],
        out_specs=[tile(ATT_WIDTH), tile(GLA_V_WIDTH)],
        out_shape=[jax.ShapeDtypeStruct((nbatch, seq, ATT_WIDTH), BF16),
                   jax.ShapeDtypeStruct((nbatch, seq, GLA_V_WIDTH), BF16)],
        scratch_shapes=[pltpu.VMEM((nbatch, GLA_HEADS // 2, GLA_DV, LANES), F32)],
        compiler_params=_params(("arbitrary",)),
        name="mix",
    )(sinks, as3d(aq), as3d(akv), as3d(akv), as3d(gq), as3d(gk), as3d(gv), as3d(gg), as3d(gl), gn)
    return att.reshape(t, ATT_WIDTH), gla.reshape(t, GLA_V_WIDTH)


def _outproj_kernel(att_ref, gla_ref, x_ref, mod_ref, gpost_ref, gpre_ref, woa_ref, wog_ref, wr_ref, br_ref,
                    x1_ref, h2_ref, route_ref, cnt_ref, tcnt_ref, base_ref, *, n_exp, tt, tc):
    tm = x_ref.shape[0]

    @pl.when(pl.program_id(0) == 0)
    def _():
        base_ref[...] = jnp.zeros_like(base_ref)

    nsub = tc // tt
    lane = lax.broadcasted_iota(jnp.int32, (tc, LANES), 1)
    ri = lax.broadcasted_iota(jnp.int32, (tc, tc), 0)
    ci = lax.broadcasted_iota(jnp.int32, (tc, tc), 1)
    below = ((ci < ri) & (ri // tt == ci // tt)).astype(BF16)
    lower = (lax.broadcasted_iota(jnp.int32, (LANES, LANES), 0)
             < lax.broadcasted_iota(jnp.int32, (LANES, LANES), 1)).astype(BF16)
    sub_row = lax.broadcasted_iota(jnp.int32, (8, LANES), 0)
    tok_sub = lax.broadcasted_iota(jnp.int32, (tc, LANES), 0) // tt
    all_cnt = jnp.zeros((8, LANES), F32)
    for ch in range(tm // tc):
        r = slice(ch * tc, (ch + 1) * tc)
        y = _dot(att_ref[r, :], woa_ref[...]) + _dot(gla_ref[r, :], wog_ref[...])
        x1 = x_ref[r, :] + mod_ref[2:3, :] * (_rms(y) * gpost_ref[...])
        x1_ref[r, :] = x1
        h2 = _rms(x1) * gpre_ref[...]
        h2 = h2 * (1.0 + mod_ref[4:5, :]) + mod_ref[3:4, :]
        h2_hi = h2.astype(BF16)
        h2_ref[r, :] = h2_hi

        h2_lo = (h2 - h2_hi.astype(F32)).astype(BF16)
        p_hi = _dot(h2_hi, wr_ref[...])
        p_lo = _dot(h2_lo, wr_ref[...])
        logits = ((p_lo[:, LANES:] + p_lo[:, :LANES]) + p_hi[:, LANES:]) + p_hi[:, :LANES] + br_ref[...]
        vals = jnp.where(lane < n_exp, logits, -jnp.inf)
        sels, tops, idxs = [], [], []
        for _ in range(TOP_K):
            m = jnp.max(vals, axis=-1, keepdims=True)
            idx = jnp.min(jnp.where(vals == m, lane, LANES), axis=-1, keepdims=True)
            sel = lane == idx
            vals = jnp.where(sel, -jnp.inf, vals)
            sels.append(sel)
            tops.append(m)
            idxs.append(idx)
        es = [jnp.exp(m - tops[0]) for m in tops]
        tot = es[0] + es[1] + es[2] + es[3]
        onehot = jnp.zeros((tc, LANES), F32)
        for sel in sels:
            onehot = onehot + sel.astype(F32)
        earlier = _dot(below, onehot.astype(BF16))
        sub_cnt = jnp.zeros((8, LANES), F32)
        for h in range(nsub):
            sub_cnt = jnp.where(sub_row == h, jnp.sum(onehot[h * tt:(h + 1) * tt], axis=0, keepdims=True), sub_cnt)
        run_start = _dot(sub_cnt.astype(BF16), lower)
        pos_all = earlier
        for h in range(nsub):
            pos_all = pos_all + jnp.where(tok_sub == h, run_start[h:h + 1, :], 0.0)
            all_cnt = jnp.where(sub_row == ch * nsub + h, sub_cnt[h:h + 1, :], all_cnt)
        route = jnp.zeros((tc, LANES), F32)
        for k in range(TOP_K):
            pos = jnp.sum(jnp.where(sels[k], pos_all, 0.0), axis=-1, keepdims=True)
            route = jnp.where(lane == k, idxs[k].astype(F32), route)
            route = jnp.where(lane == TOP_K + k, es[k] / tot, route)
            route = jnp.where(lane == 2 * TOP_K + k, pos, route)
        route_ref[r, :] = route
    tcnt_ref[...] = all_cnt
    base = base_ref[...] + jnp.sum(all_cnt, axis=0, keepdims=True)
    base_ref[...] = base
    cnt_ref[...] = jnp.broadcast_to(base, cnt_ref.shape)


def _outproj(att, gla, x2, mod3, gpost, gpre, woa, wog, wr, br, seq, tm, n_exp, tt):
    t, d = x2.shape
    per = seq // tm
    row = lambda i: (i, 0)
    fixed = lambda i: (0, 0)
    return pl.pallas_call(
        functools.partial(_outproj_kernel, n_exp=n_exp, tt=tt, tc=min(512, tm)),
        grid=(t // tm,),
        in_specs=[pl.BlockSpec((tm, ATT_WIDTH), row),
                  pl.BlockSpec((tm, GLA_V_WIDTH), row),
                  pl.BlockSpec((tm, d), row),
                  pl.BlockSpec((None, 8, d), lambda i: (i // per, 0, 0)),
                  pl.BlockSpec((1, d), fixed),
                  pl.BlockSpec((1, d), fixed),
                  pl.BlockSpec((ATT_WIDTH, d), fixed),
                  pl.BlockSpec((GLA_V_WIDTH, d), fixed),
                  pl.BlockSpec((d, 2 * LANES), fixed),
                  pl.BlockSpec((1, LANES), fixed)],
        out_specs=[pl.BlockSpec((tm, d), row),
                   pl.BlockSpec((tm, d), row),
                   pl.BlockSpec((tm, LANES), row),
                   pl.BlockSpec((8, LANES), fixed),
                   pl.BlockSpec((None, 8, LANES), lambda i: (i, 0, 0))],
        out_shape=[jax.ShapeDtypeStruct((t, d), F32),
                   jax.ShapeDtypeStruct((t, d), BF16),
                   jax.ShapeDtypeStruct((t, LANES), F32),
                   jax.ShapeDtypeStruct((8, LANES), F32),
                   jax.ShapeDtypeStruct((t // tm, 8, LANES), F32)],
        scratch_shapes=[pltpu.VMEM((1, LANES), F32)],
        compiler_params=_params(("arbitrary",)),
        name="outproj",
    )(att, gla, x2, mod3, gpost, gpre, woa, wog, wr, br)


def _dispatch_kernel(pend_ref, nu_ref, n_ref, cs_ref, gs_ref, prow_ref, h2_ref, xs_hbm, sb, zbuf, sem_r, sem_z,
                     *, tt, nsub, rows, n_exp, n_blocks):
    i = pl.program_id(0)
    nsteps = pl.num_programs(0)
    n_sorted = TOP_K * tt
    c = sb.shape[1] // n_sorted
    blk_rows = rows * c
    slot = i & 1

    def zero_fill(start):
        return pltpu.make_async_copy(zbuf, xs_hbm.at[pl.ds(pl.multiple_of(start, blk_rows), blk_rows), :], sem_z)

    def wait_runs(s):
        pltpu.make_async_copy(sb.at[s], xs_hbm.at[pl.ds(0, n_sorted * c), :], sem_r.at[s]).wait()

    @pl.when(i == 0)
    def _():
        zbuf[...] = jnp.zeros_like(zbuf)
        for phase in range(2):
            for e in range(n_exp):
                hi = pend_ref[e]
                lo = pend_ref[e - 1] if e > 0 else 0
                tail = nu_ref[0] + e
                for pred, start in ((hi > lo, (hi - rows) * c), (tail < n_blocks, tail * blk_rows)):
                    @pl.when(pred)
                    def _():
                        if phase == 0:
                            zero_fill(start).start()
                        else:
                            zero_fill(start).wait()

    @pl.when(i >= 2)
    def _():
        for h in range(nsub):
            wait_runs(slot * nsub + h)

    j = lax.broadcasted_iota(jnp.int32, (n_sorted, tt), 0)
    for h in range(nsub):
        perm = jnp.zeros((n_sorted, tt), F32)
        for k in range(TOP_K):
            perm = jnp.where(j == prow_ref[h, k:k + 1, :], 1.0, perm)
        srt = _dot(perm.astype(BF16), h2_ref[h * tt:(h + 1) * tt, :])
        _store_rows(sb.at[slot * nsub + h], _pack_bf16_valued(srt))

    for h in range(nsub):
        tile = i * nsub + h
        for e in range(n_exp):
            n = n_ref[tile * n_exp + e]

            @pl.when(n > 0)
            def _():
                src = pl.multiple_of(cs_ref[tile * n_exp + e], c)
                dst = pl.multiple_of(gs_ref[tile * n_exp + e], c)
                pltpu.make_async_copy(sb.at[slot * nsub + h, pl.ds(src, n), :], xs_hbm.at[pl.ds(dst, n), :],
                                      sem_r.at[slot * nsub + h]).start(priority=e % 2)

    @pl.when(i == nsteps - 1)
    def _():
        for h in range(nsub):
            @pl.when(nsteps > 1)
            def _():
                wait_runs((1 - slot) * nsub + h)
            wait_runs(slot * nsub + h)


def _dispatch(pend, n_used, n_tab, cs_tab, gs_tab, prow, h2, n_slots, rows, tt, nsub):
    t, d = h2.shape
    c = d // 2 // LANES
    n_exp = pend.shape[0]
    grid_spec = pltpu.PrefetchScalarGridSpec(
        num_scalar_prefetch=5,
        grid=(t // (nsub * tt),),
        in_specs=[pl.BlockSpec((nsub, 8, tt), lambda i, *_: (i, 0, 0)),
                  pl.BlockSpec((nsub * tt, d), lambda i, *_: (i, 0))],
        out_specs=pl.BlockSpec(memory_space=pl.ANY),
        scratch_shapes=[pltpu.VMEM((2 * nsub, TOP_K * tt * c, LANES), jnp.int32),
                        pltpu.VMEM((rows * c, LANES), jnp.int32),
                        pltpu.SemaphoreType.DMA((2 * nsub,)),
                        pltpu.SemaphoreType.DMA],
    )
    return pl.pallas_call(
        functools.partial(_dispatch_kernel, tt=tt, nsub=nsub, rows=rows, n_exp=n_exp, n_blocks=n_slots // rows),
        grid_spec=grid_spec,
        out_shape=jax.ShapeDtypeStruct((n_slots * c, LANES), jnp.int32),
        compiler_params=_params(("arbitrary",)),
        name="dispatch",
    )(pend, n_used, n_tab, cs_tab, gs_tab, prow, h2)


def _moe_kernel(be_ref, nu_ref, xs_ref, w1_ref, b1g_ref, b1l_ref, w2_ref, b2_ref, ys_ref, w1g, w1l, *, rows):
    i = pl.program_id(0)
    c = xs_ref.shape[0] // rows
    half = c * LANES

    @pl.when((i < nu_ref[0]) & ((i == 0) | (be_ref[i] != be_ref[jnp.maximum(i - 1, 0)])))
    def _():
        n = 2 * LANES
        r = lax.broadcasted_iota(jnp.int32, (n, n), 0)
        col = lax.broadcasted_iota(jnp.int32, (n, n), 1)
        perm = (r == jnp.where(col < LANES, 2 * col, 2 * (col - LANES) + 1)).astype(BF16)
        for j in range(w1g.shape[1] // LANES):
            d = _dot(w1_ref[:, j * n:(j + 1) * n].astype(BF16), perm)
            w1g[:, j * LANES:(j + 1) * LANES] = d[:, :LANES].astype(BF16)
            w1l[:, j * LANES:(j + 1) * LANES] = d[:, LANES:].astype(BF16)

    @pl.when(i < nu_ref[0])
    def _():
        lo, hi = _unpack(_load_rows(xs_ref, c))
        xl = lo.astype(BF16)
        xh = hi.astype(BF16)
        glu = _dot(xl, w1g[:half, :]) + _dot(xh, w1g[half:, :]) + b1g_ref[...]
        lin = _dot(xl, w1l[:half, :]) + _dot(xh, w1l[half:, :]) + b1l_ref[...]
        glu = jnp.minimum(glu, SWIGLU_LIMIT)
        lin = jnp.clip(lin, -SWIGLU_LIMIT, SWIGLU_LIMIT)
        a = glu * jax.nn.sigmoid(SWIGLU_ALPHA * glu) * (lin + 1.0)
        _store_rows(ys_ref, _pack(_dot(a.astype(BF16), w2_ref[...].astype(BF16)) + b2_ref[...]))

    @pl.when(i >= nu_ref[0])
    def _():
        ys_ref[...] = jnp.zeros_like(ys_ref)


def _moe(block_e, n_used, xs, w1, b1g, b1l, w2, b2, rows):
    d, f2 = w1.shape[1], w1.shape[2]
    f = f2 // 2
    c = d // 2 // LANES
    nb = xs.shape[0] // (rows * c)
    wsel = lambda i, be, nu: (be[i], 0, 0)
    grid_spec = pltpu.PrefetchScalarGridSpec(
        num_scalar_prefetch=2,
        grid=(nb,),
        in_specs=[pl.BlockSpec((rows * c, LANES), lambda i, be, nu: (jnp.minimum(i, nu[0] - 1), 0)),
                  pl.BlockSpec((None, d, f2), wsel),
                  pl.BlockSpec((None, 1, f), wsel),
                  pl.BlockSpec((None, 1, f), wsel),
                  pl.BlockSpec((None, f, d), wsel),
                  pl.BlockSpec((None, 1, d), wsel)],
        out_specs=pl.BlockSpec((rows * c, LANES), lambda i, be, nu: (i, 0)),
        scratch_shapes=[pltpu.VMEM((d, f), BF16), pltpu.VMEM((d, f), BF16)],
    )
    return pl.pallas_call(
        functools.partial(_moe_kernel, rows=rows),
        grid_spec=grid_spec,
        out_shape=jax.ShapeDtypeStruct(xs.shape, jnp.int32),
        compiler_params=pltpu.CompilerParams(dimension_semantics=("arbitrary",), vmem_limit_bytes=MOE_VMEM_LIMIT),
        name="moe",
    )(block_e, n_used, xs, w1, b1g, b1l, w2, b2)


def _combine_kernel(n_ref, cs_ref, gs_ref, ys_hbm, pg_ref, x1_ref, mod_ref, g_ref, o_ref, yb, sem_g,
                    *, tt, nsub, n_exp):
    i = pl.program_id(0)
    nsteps = pl.num_programs(0)
    d = o_ref.shape[1]
    half = d // 2
    c = half // LANES
    n_sorted = TOP_K * tt

    def issue(step):
        for h in range(nsub):
            tile = step * nsub + h
            buf = (step & 1) * nsub + h
            for e in range(n_exp):
                n = n_ref[tile * n_exp + e]

                @pl.when(n > 0)
                def _():
                    src = pl.multiple_of(gs_ref[tile * n_exp + e], c)
                    dst = pl.multiple_of(cs_ref[tile * n_exp + e], c)
                    pltpu.make_async_copy(ys_hbm.at[pl.ds(src, n), :], yb.at[buf, pl.ds(dst, n), :],
                                          sem_g.at[buf]).start(priority=e % 2)

    @pl.when(i == 0)
    def _():
        issue(0)

    @pl.when(i + 1 < nsteps)
    def _():
        issue(i + 1)

    slot = i & 1
    for h in range(nsub):
        buf = slot * nsub + h
        pltpu.make_async_copy(ys_hbm.at[pl.ds(0, n_sorted * c), :], yb.at[buf], sem_g.at[buf]).wait()

    j = lax.broadcasted_iota(jnp.int32, (tt, n_sorted), 1)
    for h in range(nsub):
        r = slice(h * tt, (h + 1) * tt)
        lo, hi = _unpack(_load_rows(yb.at[slot * nsub + h], c))
        sel = jnp.zeros((tt, n_sorted), F32)
        for k in range(TOP_K):
            pos = pg_ref[r, 2 * TOP_K + k:2 * TOP_K + k + 1].astype(jnp.int32)
            sel = jnp.where(j == pos, pg_ref[r, TOP_K + k:TOP_K + k + 1], sel)
        sel_hi = sel.astype(BF16)
        sel_lo = (sel - sel_hi.astype(F32)).astype(BF16)
        lo_b = lo.astype(BF16)
        hi_b = hi.astype(BF16)
        acc_lo = _dot(sel_lo, lo_b) + _dot(sel_hi, lo_b)
        acc_hi = _dot(sel_lo, hi_b) + _dot(sel_hi, hi_b)
        ms = (jnp.sum(acc_lo * acc_lo, axis=-1, keepdims=True)
              + jnp.sum(acc_hi * acc_hi, axis=-1, keepdims=True)) / d
        inv = lax.rsqrt(ms + NORM_EPS)
        o_ref[r, :half] = x1_ref[r, :half] + mod_ref[5:6, :half] * (acc_lo * inv * g_ref[:, :half])
        o_ref[r, half:] = x1_ref[r, half:] + mod_ref[5:6, half:] * (acc_hi * inv * g_ref[:, half:])


def _combine(n_tab, cs_tab, gs_tab, ys, pg, x1, mod3, g, seq, tt, nsub, n_exp):
    t, d = x1.shape
    tp = tt * nsub
    per = seq // tp
    c = d // 2 // LANES
    row = lambda i, *_: (i, 0)
    grid_spec = pltpu.PrefetchScalarGridSpec(
        num_scalar_prefetch=3,
        grid=(t // tp,),
        in_specs=[pl.BlockSpec(memory_space=pl.ANY),
                  pl.BlockSpec((tp, LANES), row),
                  pl.BlockSpec((tp, d), row),
                  pl.BlockSpec((None, 8, d), lambda i, *_: (i // per, 0, 0)),
                  pl.BlockSpec((1, d), lambda i, *_: (0, 0))],
        out_specs=pl.BlockSpec((tp, d), row),
        scratch_shapes=[pltpu.VMEM((2 * nsub, TOP_K * tt * c, LANES), jnp.int32),
                        pltpu.SemaphoreType.DMA((2 * nsub,))],
    )
    return pl.pallas_call(
        functools.partial(_combine_kernel, tt=tt, nsub=nsub, n_exp=n_exp),
        grid_spec=grid_spec,
        out_shape=jax.ShapeDtypeStruct((t, d), F32),
        compiler_params=_params(("arbitrary",)),
        name="combine",
    )(n_tab, cs_tab, gs_tab, ys, pg, x1, mod3, g)


def _pair_perm():
    half = ATT_Q_HEADS // 2
    idx = []
    for j in range(half):
        idx += list(range(j * ATT_HEAD_DIM, (j + 1) * ATT_HEAD_DIM))
        idx += list(range((half + j) * ATT_HEAD_DIM, (half + j + 1) * ATT_HEAD_DIM))
    return np.asarray(idx, np.int32)


def _layer(x, mod, g_pre_mix, g_post_mix, g_pre_ffn, g_post_ffn, w_in, w_gla_gate_up, b_gla_gate, g_gla_norm,
           sinks, w_out, w_router, b_router, w_mlp1, b_mlp1, w_mlp2, b_mlp2):
    nbatch, seq, d = x.shape
    t = nbatch * seq
    n_exp = w_router.shape[1]
    f = w_mlp2.shape[1]
    x2 = x.reshape(t, d)
    mod3 = jnp.pad(mod.reshape(nbatch, 6, d), ((0, 0), (0, 2), (0, 0)))

    perm = _pair_perm()
    n_main = C_GA - C_AKV
    w_r = jnp.concatenate([w_in[:, perm], w_in[:, ATT_WIDTH:ATT_WIDTH + n_main],
                           jnp.pad(w_in[:, ATT_WIDTH + n_main:], ((0, 0), (0, LANES - GLA_GATE_RANK)))],
                          axis=1).astype(BF16)
    wup = jnp.pad(w_gla_gate_up, ((0, LANES - GLA_GATE_RANK), (0, 0)))
    woa = w_out[:ATT_WIDTH][perm].astype(BF16)
    wog = w_out[ATT_WIDTH:].astype(BF16)
    wr = jnp.pad(w_router, ((0, 0), (0, LANES - n_exp)))
    wr_hi = wr.astype(BF16)
    wr = jnp.concatenate([wr_hi, (wr - wr_hi.astype(F32)).astype(BF16)], axis=1)
    br = jnp.pad(b_router, (0, LANES - n_exp)).reshape(1, LANES)
    b1 = b_mlp1.reshape(n_exp, 1, f, 2)

    tm = min(512, seq)
    tt = min(256, seq)
    to = min(1024, seq)
    aq, akv, gq, gk, gv, gg, gl = _inproj(x2, mod3, g_pre_mix.reshape(1, d), w_r, wup,
                                          b_gla_gate.reshape(1, GLA_K_WIDTH), seq, min(1024, seq))
    att, gla = _mix(aq, akv, sinks, gq, gk, gv, gg, gl, g_gla_norm.reshape(1, GLA_DV), nbatch, seq, min(512, seq))
    x1, h2, route, cnt, tcnt = _outproj(att, gla, x2, mod3, g_post_mix.reshape(1, d), g_pre_ffn.reshape(1, d),
                                        woa, wog, wr, br, seq, to, n_exp, tt)

    rows = MOE_ROWS
    counts = cnt[0, :n_exp].astype(jnp.int32)
    padded = (counts + rows - 1) // rows * rows
    pend = jnp.cumsum(padded)
    pstart = pend - padded
    n_slots = t * TOP_K + n_exp * rows
    n_blocks = n_slots // rows
    n_used = (pend[-1] // rows).astype(jnp.int32).reshape(1)
    n_tiles = t // tt
    sub = d // 2 // LANES
    tile_cnt = tcnt[:, :to // tt, :n_exp].reshape(n_tiles, n_exp).astype(jnp.int32)
    run_start = jnp.cumsum(tile_cnt, axis=1) - tile_cnt
    before = jnp.cumsum(tile_cnt, axis=0) - tile_cnt
    n_tab = (tile_cnt * sub).reshape(-1)
    cs_tab = (run_start * sub).reshape(-1)
    gs_tab = ((pstart[None, :] + before) * sub).reshape(-1)
    pos = route[:, 2 * TOP_K:3 * TOP_K].astype(jnp.int32)
    prow = jnp.pad(pos.reshape(n_tiles, tt, TOP_K).transpose(0, 2, 1), ((0, 0), (0, 8 - TOP_K), (0, 0)),
                   constant_values=-1)

    nsub = max(1, min(1024, seq) // tt)
    xs = _dispatch(pend, n_used, n_tab, cs_tab, gs_tab, prow, h2, n_slots, rows, tt, nsub)
    blk_ids = jnp.minimum(jnp.arange(n_blocks, dtype=jnp.int32), n_used - 1)
    block_e = jnp.minimum(jnp.sum(pend[None, :] <= (blk_ids * rows)[:, None], axis=1), n_exp - 1).astype(jnp.int32)
    ys = _moe(block_e, n_used, xs, w_mlp1, b1[..., 0], b1[..., 1], w_mlp2, b_mlp2.reshape(n_exp, 1, d), rows)
    out = _combine(n_tab, cs_tab, gs_tab, ys, route, x1, mod3, g_post_ffn.reshape(1, d), seq, tt, nsub, n_exp)
    return out.reshape(nbatch, seq, d)


def kernel(x, c, w_ada, b_ada, g_pre_mix, g_post_mix, g_pre_ffn, g_post_ffn, w_in, w_gla_gate_up, b_gla_gate,
           g_gla_norm, sinks, w_out, w_router, b_router, w_mlp1, b_mlp1, w_mlp2, b_mlp2):
    for l in range(w_in.shape[0]):
        mod = _ada(c, w_ada[l], b_ada[l])
        x = _layer(x, mod, g_pre_mix[l], g_post_mix[l], g_pre_ffn[l], g_post_ffn[l], w_in[l], w_gla_gate_up[l],
                   b_gla_gate[l], g_gla_norm[l], sinks[l], w_out[l], w_router[l], b_router[l], w_mlp1[l], b_mlp1[l],
                   w_mlp2[l], b_mlp2[l])
    return x
```

```python
import functools

import numpy as np
import jax
import jax.numpy as jnp
from jax import lax
from jax.experimental import pallas as pl
from jax.experimental.pallas import tpu as pltpu

F32 = jnp.float32
BF16 = jnp.bfloat16
HI = lax.Precision.HIGHEST

ATT_Q_HEADS = 8
ATT_KV_HEADS = 2
ATT_HEAD_DIM = 64
ATT_BLOCK = 128
GLA_HEADS = 4
GLA_DK = 64
GLA_DV = 128
GLA_GATE_RANK = 16
GLA_GATE_NORMALIZER = 16.0
GLA_CHUNK = 64
TOP_K = 4
SWIGLU_LIMIT = 7.0
SWIGLU_ALPHA = 1.702
NORM_EPS = 1e-6

LANES = 128
ATT_WIDTH = ATT_Q_HEADS * ATT_HEAD_DIM
ATT_KV_WIDTH = ATT_KV_HEADS * ATT_HEAD_DIM
GLA_K_WIDTH = GLA_HEADS * GLA_DK
GLA_V_WIDTH = GLA_HEADS * GLA_DV
C_AQ = 0
C_AKV = C_AQ + ATT_WIDTH
C_GQ = C_AKV + 2 * ATT_KV_WIDTH
C_GK = C_GQ + GLA_K_WIDTH
C_GV = C_GK + GLA_K_WIDTH
C_GG = C_GV + GLA_V_WIDTH
C_GA = C_GG + GLA_V_WIDTH
C_END = C_GA + LANES

MOE_ROWS = 1024
MOE_CHAIN = 512
VMEM_LIMIT = 48 * 1024 * 1024
MOE_VMEM_LIMIT = 56 * 1024 * 1024


def _dot(a, b, prec=None):
    return jnp.dot(a, b, preferred_element_type=F32, precision=prec)


def _dot_nt(a, b):
    return lax.dot_general(a, b, (((1,), (1,)), ((), ())), preferred_element_type=F32)


def _rms(t):
    return t * lax.rsqrt(jnp.mean(t * t, axis=-1, keepdims=True) + NORM_EPS)


def _params(sem):
    return pltpu.CompilerParams(dimension_semantics=sem, vmem_limit_bytes=VMEM_LIMIT)


HI16 = -65536


def _pack(a):
    half = a.shape[1] // 2
    lo = lax.bitcast_convert_type(a[:, :half].astype(BF16).astype(F32), jnp.int32)
    hi = lax.bitcast_convert_type(a[:, half:].astype(BF16).astype(F32), jnp.int32)
    return hi | lax.shift_right_logical(lo, 16)


def _pack_bf16_valued(a):
    half = a.shape[1] // 2
    lo = lax.bitcast_convert_type(a[:, :half], jnp.int32)
    hi = lax.bitcast_convert_type(a[:, half:], jnp.int32)
    return hi | lax.shift_right_logical(lo, 16)


def _unpack(p):
    lo = lax.bitcast_convert_type(lax.shift_left(p, 16), F32)
    hi = lax.bitcast_convert_type(p & HI16, F32)
    return lo, hi


def _store_rows(ref, p):
    n, w = p.shape
    c = w // LANES
    for s in range(c):
        ref[pl.ds(s, n, stride=c), :] = p[:, s * LANES:(s + 1) * LANES]


def _load_rows(ref, c):
    n = ref.shape[0] // c
    return jnp.concatenate([ref[pl.ds(s, n, stride=c), :] for s in range(c)], axis=1)


def _ada_kernel(c_ref, w_ref, b_ref, o_ref):
    c = c_ref[...]
    o_ref[...] = _dot(c * jax.nn.sigmoid(c), w_ref[...], HI) + b_ref[...]


def _ada(c, w, b):
    nb, d = c.shape
    n = w.shape[1]
    cp = jnp.zeros((8, d), F32).at[:nb].set(c)
    out = pl.pallas_call(
        _ada_kernel,
        grid=(n // d,),
        in_specs=[pl.BlockSpec((8, d), lambda j: (0, 0)),
                  pl.BlockSpec((d, d), lambda j: (0, j)),
                  pl.BlockSpec((1, d), lambda j: (0, j))],
        out_specs=pl.BlockSpec((8, d), lambda j: (0, j)),
        out_shape=jax.ShapeDtypeStruct((8, n), F32),
        compiler_params=_params(("parallel",)),
        name="ada",
    )(cp, w, b.reshape(1, n))
    return out[:nb]


def _inproj_kernel(x_ref, mod_ref, g_ref, w_ref, wup_ref, bup_ref,
                   aq_ref, akv_ref, gq_ref, gk_ref, gv_ref, gg_ref, gl_ref):
    half = x_ref.shape[0] // 2
    for r in (slice(0, half), slice(half, 2 * half)):
        h = _rms(x_ref[r, :]) * g_ref[...]
        h = h * (1.0 + mod_ref[1:2, :]) + mod_ref[0:1, :]
        p = _dot(h.astype(BF16), w_ref[...])
        aq_ref[r, :] = (p[:, C_AQ:C_AKV] * (ATT_HEAD_DIM ** -0.5)).astype(BF16)
        akv_ref[r, :] = p[:, C_AKV:C_GQ].astype(BF16)
        gq_ref[r, :] = (p[:, C_GQ:C_GK] * (GLA_DK ** -0.5)).astype(BF16)
        gk_ref[r, :] = p[:, C_GK:C_GV].astype(BF16)
        gv_ref[r, :] = p[:, C_GV:C_GG].astype(BF16)
        gg_ref[r, :] = p[:, C_GG:C_GA].astype(BF16)
        z = _dot(p[:, C_GA:C_END], wup_ref[...], HI) + bup_ref[...]
        gl_ref[r, :] = (jnp.minimum(z, 0.0) - jnp.log(1.0 + jnp.exp(-jnp.abs(z)))) * (1.0 / GLA_GATE_NORMALIZER)


def _inproj(x2, mod3, g, w_r, wup, bup, seq, tm):
    t, d = x2.shape
    per = seq // tm
    row = lambda i: (i, 0)
    fixed = lambda i: (0, 0)
    widths = (ATT_WIDTH, 2 * ATT_KV_WIDTH, GLA_K_WIDTH, GLA_K_WIDTH, GLA_V_WIDTH, GLA_V_WIDTH)
    out_shape = [jax.ShapeDtypeStruct((t, w), BF16) for w in widths] + [jax.ShapeDtypeStruct((t, GLA_K_WIDTH), F32)]
    out_specs = [pl.BlockSpec((tm, w), row) for w in widths] + [pl.BlockSpec((tm, GLA_K_WIDTH), row)]
    return pl.pallas_call(
        _inproj_kernel,
        grid=(t // tm,),
        in_specs=[pl.BlockSpec((tm, d), row),
                  pl.BlockSpec((None, 8, d), lambda i: (i // per, 0, 0)),
                  pl.BlockSpec((1, d), fixed),
                  pl.BlockSpec((d, C_END), fixed),
                  pl.BlockSpec((LANES, GLA_K_WIDTH), fixed),
                  pl.BlockSpec((1, GLA_K_WIDTH), fixed)],
        out_specs=out_specs,
        out_shape=out_shape,
        compiler_params=_params(("parallel",)),
        name="inproj",
    )(x2, mod3, g, w_r, wup, bup)


def _mix_kernel(sinks_ref, q_ref, kv_ref, kvp_ref, gq_ref, gk_ref, gv_ref, gg_ref, gl_ref, gn_ref,
                att_ref, gla_ref, st_ref, *, rows):
    i = pl.program_id(0)
    nbatch = q_ref.shape[0]
    blk = ATT_BLOCK
    ch = GLA_CHUNK

    @pl.when(i == 0)
    def _():
        st_ref[...] = jnp.zeros_like(st_ref)

    lo = lax.broadcasted_iota(jnp.int32, (blk, LANES), 1) < ATT_HEAD_DIM
    qi = lax.broadcasted_iota(jnp.int32, (2 * blk, 2 * blk), 0) % blk
    kj = lax.broadcasted_iota(jnp.int32, (2 * blk, 2 * blk), 1)
    cur_ok = (kj >= blk) & ((kj - blk) <= qi)
    prev_ok = (kj < blk) & (kj > qi)
    first_off = jnp.where(i > 0, 0, blk)
    top = lax.broadcasted_iota(jnp.int32, (2 * blk, 1), 0) < blk

    def att_block(s, jb):
        r0 = jb * blk
        kvc = kv_ref[s, r0:r0 + blk, :]
        if jb == 0:
            kvp = kvp_ref[s]
            mask = cur_ok | (prev_ok & (kj >= first_off))
        else:
            kvp = kv_ref[s, r0 - blk:r0, :]
            mask = cur_ok | prev_ok
        kcat = jnp.concatenate([kvp[:, 0:ATT_KV_WIDTH], kvc[:, 0:ATT_KV_WIDTH]], axis=0)
        vcat = jnp.concatenate([kvp[:, ATT_KV_WIDTH:], kvc[:, ATT_KV_WIDTH:]], axis=0)
        for j in range(ATT_Q_HEADS // 2):
            qp = q_ref[s, r0:r0 + blk, j * LANES:(j + 1) * LANES]
            zero = jnp.zeros_like(qp)
            q2 = jnp.concatenate([jnp.where(lo, qp, zero), jnp.where(lo, zero, qp)], axis=0)
            sc = jnp.where(mask, _dot_nt(q2, kcat), -jnp.inf)
            sink = jnp.where(top, sinks_ref[j], sinks_ref[ATT_Q_HEADS // 2 + j])
            m = jnp.maximum(jnp.max(sc, axis=-1, keepdims=True), sink)
            p = jnp.exp(sc - m)
            den = jnp.sum(p, axis=-1, keepdims=True) + jnp.exp(sink - m)
            o2 = _dot(p.astype(BF16), vcat) / den
            o = jnp.where(lo, o2[0:blk], o2[blk:2 * blk])
            att_ref[s, r0:r0 + blk, j * LANES:(j + 1) * LANES] = o.astype(BF16)

    ri = lax.broadcasted_iota(jnp.int32, (rows, rows), 0)
    ci = lax.broadcasted_iota(jnp.int32, (rows, rows), 1)
    tri = ((ri // ch == ci // ch) & (ci <= ri)).astype(BF16)
    bts = []
    for s in range(nbatch):
        g0 = gl_ref[s]
        g1 = g0 - g0.astype(BF16).astype(F32)
        g2 = g1 - g1.astype(BF16).astype(F32)
        parts = _dot(tri, jnp.concatenate([g0.astype(BF16), g1.astype(BF16), g2.astype(BF16)], axis=1))
        bts.append((parts[:, 2 * GLA_K_WIDTH:] + parts[:, GLA_K_WIDTH:2 * GLA_K_WIDTH]) + parts[:, :GLA_K_WIDTH])
    row2 = lax.broadcasted_iota(jnp.int32, (2 * ch, LANES), 0)
    lane2 = lax.broadcasted_iota(jnp.int32, (2 * ch, LANES), 1)
    own = row2 // ch == lane2 // GLA_DK
    causal = (row2 // ch == lane2 // ch) & (lane2 % ch <= row2 % ch)
    gn = gn_ref[...]
    zero2 = jnp.zeros((2 * ch, LANES), BF16)

    def both(x):
        return jnp.where(own, jnp.concatenate([x, x], axis=0), zero2)

    def gla_chunk(s, c):
        r0 = c * ch
        b = bts[s][r0:r0 + ch]
        bm = b[ch // 2 - 1:ch // 2]
        bl = b[ch - 1:ch]
        q = gq_ref[s, r0:r0 + ch, :].astype(F32)
        k = gk_ref[s, r0:r0 + ch, :].astype(F32)
        qe = (q * jnp.exp(b - bm)).astype(BF16)
        ke = (k * jnp.exp(bm - b)).astype(BF16)
        qs = (q * jnp.exp(b)).astype(BF16)
        kl = (k * jnp.exp(bl - b)).astype(BF16)
        dec = jnp.exp(bl)
        for p in range(GLA_HEADS // 2):
            sl = slice(p * LANES, (p + 1) * LANES)
            h0 = slice(2 * p * GLA_DV, (2 * p + 1) * GLA_DV)
            h1 = slice((2 * p + 1) * GLA_DV, (2 * p + 2) * GLA_DV)
            st = st_ref[s, p]
            a = _dot_nt(both(qe[:, sl]), both(ke[:, sl]))
            a = jnp.where(causal, a, 0.0).astype(BF16)
            v2 = jnp.concatenate([gv_ref[s, r0:r0 + ch, h0], gv_ref[s, r0:r0 + ch, h1]], axis=0)
            o = _dot(a, v2) + _dot_nt(both(qs[:, sl]), st.astype(BF16))
            upd = lax.dot_general(v2, both(kl[:, sl]), (((0,), (0,)), ((), ())),
                                  preferred_element_type=F32)
            gg = jnp.concatenate([gg_ref[s, r0:r0 + ch, h0], gg_ref[s, r0:r0 + ch, h1]], axis=0).astype(F32)
            res = (_rms(o) * gn * (gg * jax.nn.sigmoid(gg))).astype(BF16)
            gla_ref[s, r0:r0 + ch, h0] = res[:ch]
            gla_ref[s, r0:r0 + ch, h1] = res[ch:]
            st_ref[s, p] = st * dec[:, sl] + upd

    per_blk = blk // ch
    for jb in range(rows // blk):
        for s in range(nbatch):
            att_block(s, jb)
        for c in range(jb * per_blk, (jb + 1) * per_blk):
            for s in range(nbatch):
                gla_chunk(s, c)


def _mix(aq, akv, sinks, gq, gk, gv, gg, gl, gn, nbatch, seq, rows):
    t = aq.shape[0]
    nblk = rows // ATT_BLOCK
    tile = lambda w: pl.BlockSpec((nbatch, rows, w), lambda i: (0, i, 0))
    as3d = lambda a: a.reshape(nbatch, seq, a.shape[1])
    att, gla = pl.pallas_call(
        functools.partial(_mix_kernel, rows=rows),
        grid=(seq // rows,),
        in_specs=[pl.BlockSpec(memory_space=pltpu.SMEM),
                  tile(ATT_WIDTH), tile(2 * ATT_KV_WIDTH),
                  pl.BlockSpec((nbatch, ATT_BLOCK, 2 * ATT_KV_WIDTH), lambda i: (0, jnp.maximum(i * nblk - 1, 0), 0)),
                  tile(GLA_K_WIDTH), tile(GLA_K_WIDTH), tile(GLA_V_WIDTH), tile(GLA_V_WIDTH), tile(GLA_K_WIDTH),
                  pl.BlockSpec((1, GLA_DV), lambda i: (0, 0))],
        out_specs=[tile(ATT_WIDTH), tile(GLA_V_WIDTH)],
        out_shape=[jax.ShapeDtypeStruct((nbatch, seq, ATT_WIDTH), BF16),
                   jax.ShapeDtypeStruct((nbatch, seq, GLA_V_WIDTH), BF16)],
        scratch_shapes=[pltpu.VMEM((nbatch, GLA_HEADS // 2, GLA_DV, LANES), F32)],
        compiler_params=_params(("arbitrary",)),
        name="mix",
    )(sinks, as3d(aq), as3d(akv), as3d(akv), as3d(gq), as3d(gk), as3d(gv), as3d(gg), as3d(gl), gn)
    return att.reshape(t, ATT_WIDTH), gla.reshape(t, GLA_V_WIDTH)


def _outproj_kernel(att_ref, gla_ref, x_ref, mod_ref, gpost_ref, gpre_ref, woa_ref, wog_ref, wr_ref, br_ref,
                    x1_ref, h2_ref, route_ref, cnt_ref, tcnt_ref, base_ref, *, n_exp, tt, tc):
    tm = x_ref.shape[0]

    @pl.when(pl.program_id(0) == 0)
    def _():
        base_ref[...] = jnp.zeros_like(base_ref)

    nsub = tc // tt
    lane = lax.broadcasted_iota(jnp.int32, (tc, LANES), 1)
    ri = lax.broadcasted_iota(jnp.int32, (tc, tc), 0)
    ci = lax.broadcasted_iota(jnp.int32, (tc, tc), 1)
    below = ((ci < ri) & (ri // tt == ci // tt)).astype(BF16)
    lower = (lax.broadcasted_iota(jnp.int32, (LANES, LANES), 0)
             < lax.broadcasted_iota(jnp.int32, (LANES, LANES), 1)).astype(BF16)
    sub_row = lax.broadcasted_iota(jnp.int32, (8, LANES), 0)
    tok_sub = lax.broadcasted_iota(jnp.int32, (tc, LANES), 0) // tt
    all_cnt = jnp.zeros((8, LANES), F32)
    for ch in range(tm // tc):
        r = slice(ch * tc, (ch + 1) * tc)
        y = _dot(att_ref[r, :], woa_ref[...]) + _dot(gla_ref[r, :], wog_ref[...])
        x1 = x_ref[r, :] + mod_ref[2:3, :] * (_rms(y) * gpost_ref[...])
        x1_ref[r, :] = x1
        h2 = _rms(x1) * gpre_ref[...]
        h2 = h2 * (1.0 + mod_ref[4:5, :]) + mod_ref[3:4, :]
        h2_hi = h2.astype(BF16)
        h2_ref[r, :] = h2_hi

        h2_lo = (h2 - h2_hi.astype(F32)).astype(BF16)
        p_hi = _dot(h2_hi, wr_ref[...])
        p_lo = _dot(h2_lo, wr_ref[...])
        logits = ((p_lo[:, LANES:] + p_lo[:, :LANES]) + p_hi[:, LANES:]) + p_hi[:, :LANES] + br_ref[...]
        vals = jnp.where(lane < n_exp, logits, -jnp.inf)
        sels, tops, idxs = [], [], []
        for _ in range(TOP_K):
            m = jnp.max(vals, axis=-1, keepdims=True)
            idx = jnp.min(jnp.where(vals == m, lane, LANES), axis=-1, keepdims=True)
            sel = lane == idx
            vals = jnp.where(sel, -jnp.inf, vals)
            sels.append(sel)
            tops.append(m)
            idxs.append(idx)
        es = [jnp.exp(m - tops[0]) for m in tops]
        tot = es[0] + es[1] + es[2] + es[3]
        onehot = jnp.zeros((tc, LANES), F32)
        for sel in sels:
            onehot = onehot + sel.astype(F32)
        earlier = _dot(below, onehot.astype(BF16))
        sub_cnt = jnp.zeros((8, LANES), F32)
        for h in range(nsub):
            sub_cnt = jnp.where(sub_row == h, jnp.sum(onehot[h * tt:(h + 1) * tt], axis=0, keepdims=True), sub_cnt)
        run_start = _dot(sub_cnt.astype(BF16), lower)
        pos_all = earlier
        for h in range(nsub):
            pos_all = pos_all + jnp.where(tok_sub == h, run_start[h:h + 1, :], 0.0)
            all_cnt = jnp.where(sub_row == ch * nsub + h, sub_cnt[h:h + 1, :], all_cnt)
        route = jnp.zeros((tc, LANES), F32)
        for k in range(TOP_K):
            pos = jnp.sum(jnp.where(sels[k], pos_all, 0.0), axis=-1, keepdims=True)
            route = jnp.where(lane == k, idxs[k].astype(F32), route)
            route = jnp.where(lane == TOP_K + k, es[k] / tot, route)
            route = jnp.where(lane == 2 * TOP_K + k, pos, route)
        route_ref[r, :] = route
    tcnt_ref[...] = all_cnt
    base = base_ref[...] + jnp.sum(all_cnt, axis=0, keepdims=True)
    base_ref[...] = base
    cnt_ref[...] = jnp.broadcast_to(base, cnt_ref.shape)


def _outproj(att, gla, x2, mod3, gpost, gpre, woa, wog, wr, br, seq, tm, n_exp, tt):
    t, d = x2.shape
    per = seq // tm
    row = lambda i: (i, 0)
    fixed = lambda i: (0, 0)
    return pl.pallas_call(
        functools.partial(_outproj_kernel, n_exp=n_exp, tt=tt, tc=min(512, tm)),
        grid=(t // tm,),
        in_specs=[pl.BlockSpec((tm, ATT_WIDTH), row),
                  pl.BlockSpec((tm, GLA_V_WIDTH), row),
                  pl.BlockSpec((tm, d), row),
                  pl.BlockSpec((None, 8, d), lambda i: (i // per, 0, 0)),
                  pl.BlockSpec((1, d), fixed),
                  pl.BlockSpec((1, d), fixed),
                  pl.BlockSpec((ATT_WIDTH, d), fixed),
                  pl.BlockSpec((GLA_V_WIDTH, d), fixed),
                  pl.BlockSpec((d, 2 * LANES), fixed),
                  pl.BlockSpec((1, LANES), fixed)],
        out_specs=[pl.BlockSpec((tm, d), row),
                   pl.BlockSpec((tm, d), row),
                   pl.BlockSpec((tm, LANES), row),
                   pl.BlockSpec((8, LANES), fixed),
                   pl.BlockSpec((None, 8, LANES), lambda i: (i, 0, 0))],
        out_shape=[jax.ShapeDtypeStruct((t, d), F32),
                   jax.ShapeDtypeStruct((t, d), BF16),
                   jax.ShapeDtypeStruct((t, LANES), F32),
                   jax.ShapeDtypeStruct((8, LANES), F32),
                   jax.ShapeDtypeStruct((t // tm, 8, LANES), F32)],
        scratch_shapes=[pltpu.VMEM((1, LANES), F32)],
        compiler_params=_params(("arbitrary",)),
        name="outproj",
    )(att, gla, x2, mod3, gpost, gpre, woa, wog, wr, br)


def _dispatch_kernel(pend_ref, nu_ref, n_ref, cs_ref, gs_ref, prow_ref, h2_ref, xs_hbm, sb, zbuf, sem_r, sem_z,
                     *, tt, nsub, rows, n_exp, n_blocks):
    i = pl.program_id(0)
    nsteps = pl.num_programs(0)
    n_sorted = TOP_K * tt
    c = sb.shape[1] // n_sorted
    blk_rows = rows * c
    slot = i & 1

    def zero_fill(start):
        return pltpu.make_async_copy(zbuf, xs_hbm.at[pl.ds(pl.multiple_of(start, blk_rows), blk_rows), :], sem_z)

    def wait_runs(s):
        pltpu.make_async_copy(sb.at[s], xs_hbm.at[pl.ds(0, n_sorted * c), :], sem_r.at[s]).wait()

    @pl.when(i == 0)
    def _():
        zbuf[...] = jnp.zeros_like(zbuf)
        for phase in range(2):
            for e in range(n_exp):
                hi = pend_ref[e]
                lo = pend_ref[e - 1] if e > 0 else 0
                tail = nu_ref[0] + e
                for pred, start in ((hi > lo, (hi - rows) * c), (tail < n_blocks, tail * blk_rows)):
                    @pl.when(pred)
                    def _():
                        if phase == 0:
                            zero_fill(start).start()
                        else:
                            zero_fill(start).wait()

    @pl.when(i >= 2)
    def _():
        for h in range(nsub):
            wait_runs(slot * nsub + h)

    j = lax.broadcasted_iota(jnp.int32, (n_sorted, tt), 0)
    for h in range(nsub):
        perm = jnp.zeros((n_sorted, tt), F32)
        for k in range(TOP_K):
            perm = jnp.where(j == prow_ref[h, k:k + 1, :], 1.0, perm)
        srt = _dot(perm.astype(BF16), h2_ref[h * tt:(h + 1) * tt, :])
        _store_rows(sb.at[slot * nsub + h], _pack_bf16_valued(srt))

    for h in range(nsub):
        tile = i * nsub + h
        for e in range(n_exp):
            n = n_ref[tile * n_exp + e]

            @pl.when(n > 0)
            def _():
                src = pl.multiple_of(cs_ref[tile * n_exp + e], c)
                dst = pl.multiple_of(gs_ref[tile * n_exp + e], c)
                pltpu.make_async_copy(sb.at[slot * nsub + h, pl.ds(src, n), :], xs_hbm.at[pl.ds(dst, n), :],
                                      sem_r.at[slot * nsub + h]).start(priority=e % 2)

    @pl.when(i == nsteps - 1)
    def _():
        for h in range(nsub):
            @pl.when(nsteps > 1)
            def _():
                wait_runs((1 - slot) * nsub + h)
            wait_runs(slot * nsub + h)


def _dispatch(pend, n_used, n_tab, cs_tab, gs_tab, prow, h2, n_slots, rows, tt, nsub):
    t, d = h2.shape
    c = d // 2 // LANES
    n_exp = pend.shape[0]
    grid_spec = pltpu.PrefetchScalarGridSpec(
        num_scalar_prefetch=5,
        grid=(t // (nsub * tt),),
        in_specs=[pl.BlockSpec((nsub, 8, tt), lambda i, *_: (i, 0, 0)),
                  pl.BlockSpec((nsub * tt, d), lambda i, *_: (i, 0))],
        out_specs=pl.BlockSpec(memory_space=pl.ANY),
        scratch_shapes=[pltpu.VMEM((2 * nsub, TOP_K * tt * c, LANES), jnp.int32),
                        pltpu.VMEM((rows * c, LANES), jnp.int32),
                        pltpu.SemaphoreType.DMA((2 * nsub,)),
                        pltpu.SemaphoreType.DMA],
    )
    return pl.pallas_call(
        functools.partial(_dispatch_kernel, tt=tt, nsub=nsub, rows=rows, n_exp=n_exp, n_blocks=n_slots // rows),
        grid_spec=grid_spec,
        out_shape=jax.ShapeDtypeStruct((n_slots * c, LANES), jnp.int32),
        compiler_params=_params(("arbitrary",)),
        name="dispatch",
    )(pend, n_used, n_tab, cs_tab, gs_tab, prow, h2)


def _moe_kernel(be_ref, cnt_ref, first_ref, nu_ref, xs_ref, w1_ref, b1g_ref, b1l_ref, w2_ref, b2_ref, ys_ref,
                w1g, w1l, *, rows, chain):
    i = pl.program_id(0)
    c = xs_ref.shape[0] // rows
    half = c * LANES
    e = be_ref[i]
    valid = cnt_ref[e] - (i - first_ref[e]) * rows

    @pl.when((i < nu_ref[0]) & ((i == 0) | (e != be_ref[jnp.maximum(i - 1, 0)])))
    def _():
        n = 2 * LANES
        r = lax.broadcasted_iota(jnp.int32, (n, n), 0)
        col = lax.broadcasted_iota(jnp.int32, (n, n), 1)
        perm = (r == jnp.where(col < LANES, 2 * col, 2 * (col - LANES) + 1)).astype(BF16)
        for j in range(w1g.shape[1] // LANES):
            d = _dot(w1_ref[:, j * n:(j + 1) * n].astype(BF16), perm)
            w1g[:, j * LANES:(j + 1) * LANES] = d[:, :LANES].astype(BF16)
            w1l[:, j * LANES:(j + 1) * LANES] = d[:, LANES:].astype(BF16)

    def mlp(r0):
        lo, hi = _unpack(_load_rows(xs_ref.at[pl.ds(r0 * c, chain * c)], c))
        xl = lo.astype(BF16)
        xh = hi.astype(BF16)
        glu = _dot(xl, w1g[:half, :]) + _dot(xh, w1g[half:, :]) + b1g_ref[...]
        lin = _dot(xl, w1l[:half, :]) + _dot(xh, w1l[half:, :]) + b1l_ref[...]
        glu = jnp.minimum(glu, SWIGLU_LIMIT)
        lin = jnp.clip(lin, -SWIGLU_LIMIT, SWIGLU_LIMIT)
        a = glu * jax.nn.sigmoid(SWIGLU_ALPHA * glu) * (lin + 1.0)
        _store_rows(ys_ref.at[pl.ds(r0 * c, chain * c)],
                    _pack(_dot(a.astype(BF16), w2_ref[...].astype(BF16)) + b2_ref[...]))

    n_chain = rows // chain
    for k in range(1, n_chain + 1):
        lo_rows = (k - 1) * chain
        upper = (valid > lo_rows) if k == n_chain else ((valid > lo_rows) & (valid <= k * chain))

        @pl.when((i < nu_ref[0]) & upper)
        def _():
            for q in range(k):
                mlp(q * chain)
            if k < n_chain:
                ys_ref[k * chain * c:, :] = jnp.zeros(((n_chain - k) * chain * c, LANES), jnp.int32)

    @pl.when(i >= nu_ref[0])
    def _():
        ys_ref[...] = jnp.zeros_like(ys_ref)


def _moe(block_e, counts, first_blk, n_used, xs, w1, b1g, b1l, w2, b2, rows, chain):
    d, f2 = w1.shape[1], w1.shape[2]
    f = f2 // 2
    c = d // 2 // LANES
    nb = xs.shape[0] // (rows * c)
    wsel = lambda i, be, *_: (be[i], 0, 0)
    grid_spec = pltpu.PrefetchScalarGridSpec(
        num_scalar_prefetch=4,
        grid=(nb,),
        in_specs=[pl.BlockSpec((rows * c, LANES), lambda i, be, cn, fi, nu: (jnp.minimum(i, nu[0] - 1), 0)),
                  pl.BlockSpec((None, d, f2), wsel),
                  pl.BlockSpec((None, 1, f), wsel),
                  pl.BlockSpec((None, 1, f), wsel),
                  pl.BlockSpec((None, f, d), wsel),
                  pl.BlockSpec((None, 1, d), wsel)],
        out_specs=pl.BlockSpec((rows * c, LANES), lambda i, *_: (i, 0)),
        scratch_shapes=[pltpu.VMEM((d, f), BF16), pltpu.VMEM((d, f), BF16)],
    )
    return pl.pallas_call(
        functools.partial(_moe_kernel, rows=rows, chain=chain),
        grid_spec=grid_spec,
        out_shape=jax.ShapeDtypeStruct(xs.shape, jnp.int32),
        compiler_params=pltpu.CompilerParams(dimension_semantics=("arbitrary",), vmem_limit_bytes=MOE_VMEM_LIMIT),
        name="moe",
    )(block_e, counts, first_blk, n_used, xs, w1, b1g, b1l, w2, b2)


def _combine_kernel(n_ref, cs_ref, gs_ref, ys_hbm, pg_ref, x1_ref, mod_ref, g_ref, o_ref, yb, sem_g,
                    *, tt, nsub, n_exp):
    i = pl.program_id(0)
    nsteps = pl.num_programs(0)
    d = o_ref.shape[1]
    half = d // 2
    c = half // LANES
    n_sorted = TOP_K * tt

    def issue(step):
        for h in range(nsub):
            tile = step * nsub + h
            buf = (step & 1) * nsub + h
            for e in range(n_exp):
                n = n_ref[tile * n_exp + e]

                @pl.when(n > 0)
                def _():
                    src = pl.multiple_of(gs_ref[tile * n_exp + e], c)
                    dst = pl.multiple_of(cs_ref[tile * n_exp + e], c)
                    pltpu.make_async_copy(ys_hbm.at[pl.ds(src, n), :], yb.at[buf, pl.ds(dst, n), :],
                                          sem_g.at[buf]).start(priority=e % 2)

    @pl.when(i == 0)
    def _():
        issue(0)

    @pl.when(i + 1 < nsteps)
    def _():
        issue(i + 1)

    slot = i & 1
    for h in range(nsub):
        buf = slot * nsub + h
        pltpu.make_async_copy(ys_hbm.at[pl.ds(0, n_sorted * c), :], yb.at[buf], sem_g.at[buf]).wait()

    j = lax.broadcasted_iota(jnp.int32, (tt, n_sorted), 1)
    for h in range(nsub):
        r = slice(h * tt, (h + 1) * tt)
        lo, hi = _unpack(_load_rows(yb.at[slot * nsub + h], c))
        sel = jnp.zeros((tt, n_sorted), F32)
        for k in range(TOP_K):
            pos = pg_ref[r, 2 * TOP_K + k:2 * TOP_K + k + 1].astype(jnp.int32)
            sel = jnp.where(j == pos, pg_ref[r, TOP_K + k:TOP_K + k + 1], sel)
        sel_hi = sel.astype(BF16)
        sel_lo = (sel - sel_hi.astype(F32)).astype(BF16)
        lo_b = lo.astype(BF16)
        hi_b = hi.astype(BF16)
        acc_lo = _dot(sel_lo, lo_b) + _dot(sel_hi, lo_b)
        acc_hi = _dot(sel_lo, hi_b) + _dot(sel_hi, hi_b)
        ms = (jnp.sum(acc_lo * acc_lo, axis=-1, keepdims=True)
              + jnp.sum(acc_hi * acc_hi, axis=-1, keepdims=True)) / d
        inv = lax.rsqrt(ms + NORM_EPS)
        o_ref[r, :half] = x1_ref[r, :half] + mod_ref[5:6, :half] * (acc_lo * inv * g_ref[:, :half])
        o_ref[r, half:] = x1_ref[r, half:] + mod_ref[5:6, half:] * (acc_hi * inv * g_ref[:, half:])


def _combine(n_tab, cs_tab, gs_tab, ys, pg, x1, mod3, g, seq, tt, nsub, n_exp):
    t, d = x1.shape
    tp = tt * nsub
    per = seq // tp
    c = d // 2 // LANES
    row = lambda i, *_: (i, 0)
    grid_spec = pltpu.PrefetchScalarGridSpec(
        num_scalar_prefetch=3,
        grid=(t // tp,),
        in_specs=[pl.BlockSpec(memory_space=pl.ANY),
                  pl.BlockSpec((tp, LANES), row),
                  pl.BlockSpec((tp, d), row),
                  pl.BlockSpec((None, 8, d), lambda i, *_: (i // per, 0, 0)),
                  pl.BlockSpec((1, d), lambda i, *_: (0, 0))],
        out_specs=pl.BlockSpec((tp, d), row),
        scratch_shapes=[pltpu.VMEM((2 * nsub, TOP_K * tt * c, LANES), jnp.int32),
                        pltpu.SemaphoreType.DMA((2 * nsub,))],
    )
    return pl.pallas_call(
        functools.partial(_combine_kernel, tt=tt, nsub=nsub, n_exp=n_exp),
        grid_spec=grid_spec,
        out_shape=jax.ShapeDtypeStruct((t, d), F32),
        compiler_params=_params(("arbitrary",)),
        name="combine",
    )(n_tab, cs_tab, gs_tab, ys, pg, x1, mod3, g)


def _pair_perm():
    half = ATT_Q_HEADS // 2
    idx = []
    for j in range(half):
        idx += list(range(j * ATT_HEAD_DIM, (j + 1) * ATT_HEAD_DIM))
        idx += list(range((half + j) * ATT_HEAD_DIM, (half + j + 1) * ATT_HEAD_DIM))
    return np.asarray(idx, np.int32)


def _layer(x, mod, g_pre_mix, g_post_mix, g_pre_ffn, g_post_ffn, w_in, w_gla_gate_up, b_gla_gate, g_gla_norm,
           sinks, w_out, w_router, b_router, w_mlp1, b_mlp1, w_mlp2, b_mlp2):
    nbatch, seq, d = x.shape
    t = nbatch * seq
    n_exp = w_router.shape[1]
    f = w_mlp2.shape[1]
    x2 = x.reshape(t, d)
    mod3 = jnp.pad(mod.reshape(nbatch, 6, d), ((0, 0), (0, 2), (0, 0)))

    perm = _pair_perm()
    n_main = C_GA - C_AKV
    w_r = jnp.concatenate([w_in[:, perm], w_in[:, ATT_WIDTH:ATT_WIDTH + n_main],
                           jnp.pad(w_in[:, ATT_WIDTH + n_main:], ((0, 0), (0, LANES - GLA_GATE_RANK)))],
                          axis=1).astype(BF16)
    wup = jnp.pad(w_gla_gate_up, ((0, LANES - GLA_GATE_RANK), (0, 0)))
    woa = w_out[:ATT_WIDTH][perm].astype(BF16)
    wog = w_out[ATT_WIDTH:].astype(BF16)
    wr = jnp.pad(w_router, ((0, 0), (0, LANES - n_exp)))
    wr_hi = wr.astype(BF16)
    wr = jnp.concatenate([wr_hi, (wr - wr_hi.astype(F32)).astype(BF16)], axis=1)
    br = jnp.pad(b_router, (0, LANES - n_exp)).reshape(1, LANES)
    b1 = b_mlp1.reshape(n_exp, 1, f, 2)

    tm = min(512, seq)
    tt = min(256, seq)
    to = min(1024, seq)
    aq, akv, gq, gk, gv, gg, gl = _inproj(x2, mod3, g_pre_mix.reshape(1, d), w_r, wup,
                                          b_gla_gate.reshape(1, GLA_K_WIDTH), seq, min(1024, seq))
    att, gla = _mix(aq, akv, sinks, gq, gk, gv, gg, gl, g_gla_norm.reshape(1, GLA_DV), nbatch, seq, min(512, seq))
    x1, h2, route, cnt, tcnt = _outproj(att, gla, x2, mod3, g_post_mix.reshape(1, d), g_pre_ffn.reshape(1, d),
                                        woa, wog, wr, br, seq, to, n_exp, tt)

    rows = MOE_ROWS
    counts = cnt[0, :n_exp].astype(jnp.int32)
    padded = (counts + rows - 1) // rows * rows
    pend = jnp.cumsum(padded)
    pstart = pend - padded
    n_slots = t * TOP_K + n_exp * rows
    n_blocks = n_slots // rows
    n_used = (pend[-1] // rows).astype(jnp.int32).reshape(1)
    n_tiles = t // tt
    sub = d // 2 // LANES
    tile_cnt = tcnt[:, :to // tt, :n_exp].reshape(n_tiles, n_exp).astype(jnp.int32)
    run_start = jnp.cumsum(tile_cnt, axis=1) - tile_cnt
    before = jnp.cumsum(tile_cnt, axis=0) - tile_cnt
    n_tab = (tile_cnt * sub).reshape(-1)
    cs_tab = (run_start * sub).reshape(-1)
    gs_tab = ((pstart[None, :] + before) * sub).reshape(-1)
    pos = route[:, 2 * TOP_K:3 * TOP_K].astype(jnp.int32)
    prow = jnp.pad(pos.reshape(n_tiles, tt, TOP_K).transpose(0, 2, 1), ((0, 0), (0, 8 - TOP_K), (0, 0)),
                   constant_values=-1)

    nsub = max(1, min(1024, seq) // tt)
    xs = _dispatch(pend, n_used, n_tab, cs_tab, gs_tab, prow, h2, n_slots, rows, tt, nsub)
    blk_ids = jnp.minimum(jnp.arange(n_blocks, dtype=jnp.int32), n_used - 1)
    block_e = jnp.minimum(jnp.sum(pend[None, :] <= (blk_ids * rows)[:, None], axis=1), n_exp - 1).astype(jnp.int32)
    ys = _moe(block_e, counts, pstart // rows, n_used, xs, w_mlp1, b1[..., 0], b1[..., 1], w_mlp2,
              b_mlp2.reshape(n_exp, 1, d), rows, min(MOE_CHAIN, rows))
    out = _combine(n_tab, cs_tab, gs_tab, ys, route, x1, mod3, g_post_ffn.reshape(1, d), seq, tt, nsub, n_exp)
    return out.reshape(nbatch, seq, d)


def kernel(x, c, w_ada, b_ada, g_pre_mix, g_post_mix, g_pre_ffn, g_post_ffn, w_in, w_gla_gate_up, b_gla_gate,
           g_gla_norm, sinks, w_out, w_router, b_router, w_mlp1, b_mlp1, w_mlp2, b_mlp2):
    for l in range(w_in.shape[0]):
        mod = _ada(c, w_ada[l], b_ada[l])
        x = _layer(x, mod, g_pre_mix[l], g_post_mix[l], g_pre_ffn[l], g_post_ffn[l], w_in[l], w_gla_gate_up[l],
                   b_gla_gate[l], g_gla_norm[l], sinks[l], w_out[l], w_router[l], b_router[l], w_mlp1[l], b_mlp1[l],
                   w_mlp2[l], b_mlp2[l])
    return x
```

```python
import functools

import numpy as np
import jax
import jax.numpy as jnp
from jax import lax
from jax.experimental import pallas as pl
from jax.experimental.pallas import tpu as pltpu

F32 = jnp.float32
BF16 = jnp.bfloat16
HI = lax.Precision.HIGHEST

ATT_Q_HEADS = 8
ATT_KV_HEADS = 2
ATT_HEAD_DIM = 64
ATT_BLOCK = 128
GLA_HEADS = 4
GLA_DK = 64
GLA_DV = 128
GLA_GATE_RANK = 16
GLA_GATE_NORMALIZER = 16.0
GLA_CHUNK = 64
TOP_K = 4
SWIGLU_LIMIT = 7.0
SWIGLU_ALPHA = 1.702
NORM_EPS = 1e-6

LANES = 128
ATT_WIDTH = ATT_Q_HEADS * ATT_HEAD_DIM
ATT_KV_WIDTH = ATT_KV_HEADS * ATT_HEAD_DIM
GLA_K_WIDTH = GLA_HEADS * GLA_DK
GLA_V_WIDTH = GLA_HEADS * GLA_DV
C_AQ = 0
C_AKV = C_AQ + ATT_WIDTH
C_GQ = C_AKV + 2 * ATT_KV_WIDTH
C_GK = C_GQ + GLA_K_WIDTH
C_GV = C_GK + GLA_K_WIDTH
C_GG = C_GV + GLA_V_WIDTH
C_GA = C_GG + GLA_V_WIDTH
C_END = C_GA + LANES

PROJ_ROWS = 1024
PROJ_CHAIN = 512
MIX_ROWS = 512
SUBTILE = 256
ROUTE_ROWS = 1024
MOE_ROWS = 1024
MOE_CHAIN = 512
V7X_VMEM_BYTES = 64 * 1024 * 1024
VMEM_LIMIT = V7X_VMEM_BYTES * 3 // 4
MOE_VMEM_LIMIT = V7X_VMEM_BYTES * 7 // 8


def _dot(a, b, prec=None):
    return jnp.dot(a, b, preferred_element_type=F32, precision=prec)


def _dot_nt(a, b):
    return lax.dot_general(a, b, (((1,), (1,)), ((), ())), preferred_element_type=F32)


def _rms(t):
    return t * lax.rsqrt(jnp.mean(t * t, axis=-1, keepdims=True) + NORM_EPS)


def _params(sem):
    return pltpu.CompilerParams(dimension_semantics=sem, vmem_limit_bytes=VMEM_LIMIT)


HI16 = -65536


def _pack(a):
    half = a.shape[1] // 2
    lo = lax.bitcast_convert_type(a[:, :half].astype(BF16).astype(F32), jnp.int32)
    hi = lax.bitcast_convert_type(a[:, half:].astype(BF16).astype(F32), jnp.int32)
    return hi | lax.shift_right_logical(lo, 16)


def _pack_bf16_valued(a):
    half = a.shape[1] // 2
    lo = lax.bitcast_convert_type(a[:, :half], jnp.int32)
    hi = lax.bitcast_convert_type(a[:, half:], jnp.int32)
    return hi | lax.shift_right_logical(lo, 16)


def _unpack(p):
    lo = lax.bitcast_convert_type(lax.shift_left(p, 16), F32)
    hi = lax.bitcast_convert_type(p & HI16, F32)
    return lo, hi


def _store_rows(ref, p):
    n, w = p.shape
    c = w // LANES
    for s in range(c):
        ref[pl.ds(s, n, stride=c), :] = p[:, s * LANES:(s + 1) * LANES]


def _load_rows(ref, c):
    n = ref.shape[0] // c
    return jnp.concatenate([ref[pl.ds(s, n, stride=c), :] for s in range(c)], axis=1)


def _ada_kernel(c_ref, w_ref, b_ref, o_ref):
    c = c_ref[...]
    o_ref[...] = _dot(c * jax.nn.sigmoid(c), w_ref[...], HI) + b_ref[...]


def _ada(c, w, b):
    nb, d = c.shape
    n = w.shape[1]
    cp = jnp.zeros((8, d), F32).at[:nb].set(c)
    out = pl.pallas_call(
        _ada_kernel,
        grid=(n // d,),
        in_specs=[pl.BlockSpec((8, d), lambda j: (0, 0)),
                  pl.BlockSpec((d, d), lambda j: (0, j)),
                  pl.BlockSpec((1, d), lambda j: (0, j))],
        out_specs=pl.BlockSpec((8, d), lambda j: (0, j)),
        out_shape=jax.ShapeDtypeStruct((8, n), F32),
        compiler_params=_params(("parallel",)),
        name="ada",
    )(cp, w, b.reshape(1, n))
    return out[:nb]


def _inproj_kernel(x_ref, mod_ref, g_ref, w_ref, wup_ref, bup_ref,
                   aq_ref, akv_ref, gq_ref, gk_ref, gv_ref, gg_ref, gl_ref):
    half = x_ref.shape[0] // 2
    for r in (slice(0, half), slice(half, 2 * half)):
        h = _rms(x_ref[r, :]) * g_ref[...]
        h = h * (1.0 + mod_ref[1:2, :]) + mod_ref[0:1, :]
        p = _dot(h.astype(BF16), w_ref[...])
        aq_ref[r, :] = (p[:, C_AQ:C_AKV] * (ATT_HEAD_DIM ** -0.5)).astype(BF16)
        akv_ref[r, :] = p[:, C_AKV:C_GQ].astype(BF16)
        gq_ref[r, :] = (p[:, C_GQ:C_GK] * (GLA_DK ** -0.5)).astype(BF16)
        gk_ref[r, :] = p[:, C_GK:C_GV].astype(BF16)
        gv_ref[r, :] = p[:, C_GV:C_GG].astype(BF16)
        gg_ref[r, :] = p[:, C_GG:C_GA].astype(BF16)
        z = _dot(p[:, C_GA:C_END], wup_ref[...], HI) + bup_ref[...]
        gl_ref[r, :] = (jnp.minimum(z, 0.0) - jnp.log(1.0 + jnp.exp(-jnp.abs(z)))) * (1.0 / GLA_GATE_NORMALIZER)


def _inproj(x2, mod3, g, w_r, wup, bup, seq, tm):
    t, d = x2.shape
    per = seq // tm
    row = lambda i: (i, 0)
    fixed = lambda i: (0, 0)
    widths = (ATT_WIDTH, 2 * ATT_KV_WIDTH, GLA_K_WIDTH, GLA_K_WIDTH, GLA_V_WIDTH, GLA_V_WIDTH)
    out_shape = [jax.ShapeDtypeStruct((t, w), BF16) for w in widths] + [jax.ShapeDtypeStruct((t, GLA_K_WIDTH), F32)]
    out_specs = [pl.BlockSpec((tm, w), row) for w in widths] + [pl.BlockSpec((tm, GLA_K_WIDTH), row)]
    return pl.pallas_call(
        _inproj_kernel,
        grid=(t // tm,),
        in_specs=[pl.BlockSpec((tm, d), row),
                  pl.BlockSpec((None, 8, d), lambda i: (i // per, 0, 0)),
                  pl.BlockSpec((1, d), fixed),
                  pl.BlockSpec((d, C_END), fixed),
                  pl.BlockSpec((LANES, GLA_K_WIDTH), fixed),
                  pl.BlockSpec((1, GLA_K_WIDTH), fixed)],
        out_specs=out_specs,
        out_shape=out_shape,
        compiler_params=_params(("parallel",)),
        name="inproj",
    )(x2, mod3, g, w_r, wup, bup)


def _mix_kernel(sinks_ref, q_ref, kv_ref, kvp_ref, gq_ref, gk_ref, gv_ref, gg_ref, gl_ref, gn_ref,
                att_ref, gla_ref, st_ref, *, rows):
    i = pl.program_id(0)
    nbatch = q_ref.shape[0]
    blk = ATT_BLOCK
    ch = GLA_CHUNK

    @pl.when(i == 0)
    def _():
        st_ref[...] = jnp.zeros_like(st_ref)

    lo = lax.broadcasted_iota(jnp.int32, (blk, LANES), 1) < ATT_HEAD_DIM
    qi = lax.broadcasted_iota(jnp.int32, (2 * blk, 2 * blk), 0) % blk
    kj = lax.broadcasted_iota(jnp.int32, (2 * blk, 2 * blk), 1)
    cur_ok = (kj >= blk) & ((kj - blk) <= qi)
    prev_ok = (kj < blk) & (kj > qi)
    first_off = jnp.where(i > 0, 0, blk)
    top = lax.broadcasted_iota(jnp.int32, (2 * blk, 1), 0) < blk

    def att_block(s, jb):
        r0 = jb * blk
        kvc = kv_ref[s, r0:r0 + blk, :]
        if jb == 0:
            kvp = kvp_ref[s]
            mask = cur_ok | (prev_ok & (kj >= first_off))
        else:
            kvp = kv_ref[s, r0 - blk:r0, :]
            mask = cur_ok | prev_ok
        kcat = jnp.concatenate([kvp[:, 0:ATT_KV_WIDTH], kvc[:, 0:ATT_KV_WIDTH]], axis=0)
        vcat = jnp.concatenate([kvp[:, ATT_KV_WIDTH:], kvc[:, ATT_KV_WIDTH:]], axis=0)
        for j in range(ATT_Q_HEADS // 2):
            qp = q_ref[s, r0:r0 + blk, j * LANES:(j + 1) * LANES]
            zero = jnp.zeros_like(qp)
            q2 = jnp.concatenate([jnp.where(lo, qp, zero), jnp.where(lo, zero, qp)], axis=0)
            sc = jnp.where(mask, _dot_nt(q2, kcat), -jnp.inf)
            sink = jnp.where(top, sinks_ref[j], sinks_ref[ATT_Q_HEADS // 2 + j])
            m = jnp.maximum(jnp.max(sc, axis=-1, keepdims=True), sink)
            p = jnp.exp(sc - m)
            den = jnp.sum(p, axis=-1, keepdims=True) + jnp.exp(sink - m)
            o2 = _dot(p.astype(BF16), vcat) / den
            o = jnp.where(lo, o2[0:blk], o2[blk:2 * blk])
            att_ref[s, r0:r0 + blk, j * LANES:(j + 1) * LANES] = o.astype(BF16)

    ri = lax.broadcasted_iota(jnp.int32, (rows, rows), 0)
    ci = lax.broadcasted_iota(jnp.int32, (rows, rows), 1)
    tri = ((ri // ch == ci // ch) & (ci <= ri)).astype(BF16)
    bts = []
    for s in range(nbatch):
        g0 = gl_ref[s]
        g1 = g0 - g0.astype(BF16).astype(F32)
        g2 = g1 - g1.astype(BF16).astype(F32)
        parts = _dot(tri, jnp.concatenate([g0.astype(BF16), g1.astype(BF16), g2.astype(BF16)], axis=1))
        bts.append((parts[:, 2 * GLA_K_WIDTH:] + parts[:, GLA_K_WIDTH:2 * GLA_K_WIDTH]) + parts[:, :GLA_K_WIDTH])
    row2 = lax.broadcasted_iota(jnp.int32, (2 * ch, LANES), 0)
    lane2 = lax.broadcasted_iota(jnp.int32, (2 * ch, LANES), 1)
    own = row2 // ch == lane2 // GLA_DK
    causal = (row2 // ch == lane2 // ch) & (lane2 % ch <= row2 % ch)
    gn = gn_ref[...]
    zero2 = jnp.zeros((2 * ch, LANES), BF16)

    def both(x):
        return jnp.where(own, jnp.concatenate([x, x], axis=0), zero2)

    def gla_chunk(s, c):
        r0 = c * ch
        b = bts[s][r0:r0 + ch]
        bm = b[ch // 2 - 1:ch // 2]
        bl = b[ch - 1:ch]
        q = gq_ref[s, r0:r0 + ch, :].astype(F32)
        k = gk_ref[s, r0:r0 + ch, :].astype(F32)
        qe = (q * jnp.exp(b - bm)).astype(BF16)
        ke = (k * jnp.exp(bm - b)).astype(BF16)
        qs = (q * jnp.exp(b)).astype(BF16)
        kl = (k * jnp.exp(bl - b)).astype(BF16)
        dec = jnp.exp(bl)
        for p in range(GLA_HEADS // 2):
            sl = slice(p * LANES, (p + 1) * LANES)
            h0 = slice(2 * p * GLA_DV, (2 * p + 1) * GLA_DV)
            h1 = slice((2 * p + 1) * GLA_DV, (2 * p + 2) * GLA_DV)
            st = st_ref[s, p]
            a = _dot_nt(both(qe[:, sl]), both(ke[:, sl]))
            a = jnp.where(causal, a, 0.0).astype(BF16)
            v2 = jnp.concatenate([gv_ref[s, r0:r0 + ch, h0], gv_ref[s, r0:r0 + ch, h1]], axis=0)
            o = _dot(a, v2) + _dot_nt(both(qs[:, sl]), st.astype(BF16))
            upd = lax.dot_general(v2, both(kl[:, sl]), (((0,), (0,)), ((), ())),
                                  preferred_element_type=F32)
            gg = jnp.concatenate([gg_ref[s, r0:r0 + ch, h0], gg_ref[s, r0:r0 + ch, h1]], axis=0).astype(F32)
            res = (_rms(o) * gn * (gg * jax.nn.sigmoid(gg))).astype(BF16)
            gla_ref[s, r0:r0 + ch, h0] = res[:ch]
            gla_ref[s, r0:r0 + ch, h1] = res[ch:]
            st_ref[s, p] = st * dec[:, sl] + upd

    per_blk = blk // ch
    for jb in range(rows // blk):
        for s in range(nbatch):
            att_block(s, jb)
        for c in range(jb * per_blk, (jb + 1) * per_blk):
            for s in range(nbatch):
                gla_chunk(s, c)


def _mix(aq, akv, sinks, gq, gk, gv, gg, gl, gn, nbatch, seq, rows):
    t = aq.shape[0]
    nblk = rows // ATT_BLOCK
    tile = lambda w: pl.BlockSpec((nbatch, rows, w), lambda i: (0, i, 0))
    as3d = lambda a: a.reshape(nbatch, seq, a.shape[1])
    att, gla = pl.pallas_call(
        functools.partial(_mix_kernel, rows=rows),
        grid=(seq // rows,),
        in_specs=[pl.BlockSpec(memory_space=pltpu.SMEM),
                  tile(ATT_WIDTH), tile(2 * ATT_KV_WIDTH),
                  pl.BlockSpec((nbatch, ATT_BLOCK, 2 * ATT_KV_WIDTH), lambda i: (0, jnp.maximum(i * nblk - 1, 0), 0)),
                  tile(GLA_K_WIDTH), tile(GLA_K_WIDTH), tile(GLA_V_WIDTH), tile(GLA_V_WIDTH), tile(GLA_K_WIDTH),
                  pl.BlockSpec((1, GLA_DV), lambda i: (0, 0))],
        out_specs=[tile(ATT_WIDTH), tile(GLA_V_WIDTH)],
        out_shape=[jax.ShapeDtypeStruct((nbatch, seq, ATT_WIDTH), BF16),
                   jax.ShapeDtypeStruct((nbatch, seq, GLA_V_WIDTH), BF16)],
        scratch_shapes=[pltpu.VMEM((nbatch, GLA_HEADS // 2, GLA_DV, LANES), F32)],
        compiler_params=_params(("arbitrary",)),
        name="mix",
    )(sinks, as3d(aq), as3d(akv), as3d(akv), as3d(gq), as3d(gk), as3d(gv), as3d(gg), as3d(gl), gn)
    return att.reshape(t, ATT_WIDTH), gla.reshape(t, GLA_V_WIDTH)


def _outproj_kernel(att_ref, gla_ref, x_ref, mod_ref, gpost_ref, gpre_ref, woa_ref, wog_ref, wr_ref, br_ref,
                    x1_ref, h2_ref, route_ref, cnt_ref, tcnt_ref, base_ref, *, n_exp, tt, tc):
    tm = x_ref.shape[0]

    @pl.when(pl.program_id(0) == 0)
    def _():
        base_ref[...] = jnp.zeros_like(base_ref)

    nsub = tc // tt
    lane = lax.broadcasted_iota(jnp.int32, (tc, LANES), 1)
    ri = lax.broadcasted_iota(jnp.int32, (tc, tc), 0)
    ci = lax.broadcasted_iota(jnp.int32, (tc, tc), 1)
    below = ((ci < ri) & (ri // tt == ci // tt)).astype(BF16)
    lower = (lax.broadcasted_iota(jnp.int32, (LANES, LANES), 0)
             < lax.broadcasted_iota(jnp.int32, (LANES, LANES), 1)).astype(BF16)
    sub_row = lax.broadcasted_iota(jnp.int32, (8, LANES), 0)
    tok_sub = lax.broadcasted_iota(jnp.int32, (tc, LANES), 0) // tt
    all_cnt = jnp.zeros((8, LANES), F32)
    for ch in range(tm // tc):
        r = slice(ch * tc, (ch + 1) * tc)
        y = _dot(att_ref[r, :], woa_ref[...]) + _dot(gla_ref[r, :], wog_ref[...])
        x1 = x_ref[r, :] + mod_ref[2:3, :] * (_rms(y) * gpost_ref[...])
        x1_ref[r, :] = x1
        h2 = _rms(x1) * gpre_ref[...]
        h2 = h2 * (1.0 + mod_ref[4:5, :]) + mod_ref[3:4, :]
        h2_hi = h2.astype(BF16)
        h2_ref[r, :] = h2_hi

        h2_lo = (h2 - h2_hi.astype(F32)).astype(BF16)
        p_hi = _dot(h2_hi, wr_ref[...])
        p_lo = _dot(h2_lo, wr_ref[...])
        logits = ((p_lo[:, LANES:] + p_lo[:, :LANES]) + p_hi[:, LANES:]) + p_hi[:, :LANES] + br_ref[...]
        vals = jnp.where(lane < n_exp, logits, -jnp.inf)
        sels, tops, idxs = [], [], []
        for _ in range(TOP_K):
            m = jnp.max(vals, axis=-1, keepdims=True)
            idx = jnp.min(jnp.where(vals == m, lane, LANES), axis=-1, keepdims=True)
            sel = lane == idx
            vals = jnp.where(sel, -jnp.inf, vals)
            sels.append(sel)
            tops.append(m)
            idxs.append(idx)
        es = [jnp.exp(m - tops[0]) for m in tops]
        tot = es[0] + es[1] + es[2] + es[3]
        onehot = jnp.zeros((tc, LANES), F32)
        for sel in sels:
            onehot = onehot + sel.astype(F32)
        earlier = _dot(below, onehot.astype(BF16))
        sub_cnt = jnp.zeros((8, LANES), F32)
        for h in range(nsub):
            sub_cnt = jnp.where(sub_row == h, jnp.sum(onehot[h * tt:(h + 1) * tt], axis=0, keepdims=True), sub_cnt)
        run_start = _dot(sub_cnt.astype(BF16), lower)
        pos_all = earlier
        for h in range(nsub):
            pos_all = pos_all + jnp.where(tok_sub == h, run_start[h:h + 1, :], 0.0)
            all_cnt = jnp.where(sub_row == ch * nsub + h, sub_cnt[h:h + 1, :], all_cnt)
        route = jnp.zeros((tc, LANES), F32)
        for k in range(TOP_K):
            pos = jnp.sum(jnp.where(sels[k], pos_all, 0.0), axis=-1, keepdims=True)
            route = jnp.where(lane == k, idxs[k].astype(F32), route)
            route = jnp.where(lane == TOP_K + k, es[k] / tot, route)
            route = jnp.where(lane == 2 * TOP_K + k, pos, route)
        route_ref[r, :] = route
    tcnt_ref[...] = all_cnt
    base = base_ref[...] + jnp.sum(all_cnt, axis=0, keepdims=True)
    base_ref[...] = base
    cnt_ref[...] = jnp.broadcast_to(base, cnt_ref.shape)


def _outproj(att, gla, x2, mod3, gpost, gpre, woa, wog, wr, br, seq, tm, n_exp, tt):
    t, d = x2.shape
    per = seq // tm
    row = lambda i: (i, 0)
    fixed = lambda i: (0, 0)
    return pl.pallas_call(
        functools.partial(_outproj_kernel, n_exp=n_exp, tt=tt, tc=min(PROJ_CHAIN, tm)),
        grid=(t // tm,),
        in_specs=[pl.BlockSpec((tm, ATT_WIDTH), row),
                  pl.BlockSpec((tm, GLA_V_WIDTH), row),
                  pl.BlockSpec((tm, d), row),
                  pl.BlockSpec((None, 8, d), lambda i: (i // per, 0, 0)),
                  pl.BlockSpec((1, d), fixed),
                  pl.BlockSpec((1, d), fixed),
                  pl.BlockSpec((ATT_WIDTH, d), fixed),
                  pl.BlockSpec((GLA_V_WIDTH, d), fixed),
                  pl.BlockSpec((d, 2 * LANES), fixed),
                  pl.BlockSpec((1, LANES), fixed)],
        out_specs=[pl.BlockSpec((tm, d), row),
                   pl.BlockSpec((tm, d), row),
                   pl.BlockSpec((tm, LANES), row),
                   pl.BlockSpec((8, LANES), fixed),
                   pl.BlockSpec((None, 8, LANES), lambda i: (i, 0, 0))],
        out_shape=[jax.ShapeDtypeStruct((t, d), F32),
                   jax.ShapeDtypeStruct((t, d), BF16),
                   jax.ShapeDtypeStruct((t, LANES), F32),
                   jax.ShapeDtypeStruct((8, LANES), F32),
                   jax.ShapeDtypeStruct((t // tm, 8, LANES), F32)],
        scratch_shapes=[pltpu.VMEM((1, LANES), F32)],
        compiler_params=_params(("arbitrary",)),
        name="outproj",
    )(att, gla, x2, mod3, gpost, gpre, woa, wog, wr, br)


def _dispatch_kernel(pend_ref, end_ref, nu_ref, n_ref, cs_ref, gs_ref, prow_ref, h2_ref, xs_hbm, sb, zbuf,
                     sem_r, sem_z, *, tt, nsub, rows, chain, n_exp, n_blocks):
    i = pl.program_id(0)
    nsteps = pl.num_programs(0)
    n_sorted = TOP_K * tt
    c = sb.shape[1] // n_sorted
    slot = i & 1

    def wait_runs(s):
        pltpu.make_async_copy(sb.at[s], xs_hbm.at[pl.ds(0, n_sorted * c), :], sem_r.at[s]).wait()

    def zero_fills(urgent, phase):
        n_chain = rows // chain
        for e in range(n_exp):
            hi = pend_ref[e]
            lo = pend_ref[e - 1] if e > 0 else 0
            end = end_ref[e]
            tail = (nu_ref[0] + e) * rows
            for q in range(n_chain):
                first = hi - rows + q * chain
                if urgent:
                    cases = (((hi > lo) & (first < end) & (end < first + chain), first),)
                else:
                    cases = (((hi > lo) & (first >= end), first), (tail < n_blocks * rows, tail + q * chain))
                for pred, start in cases:
                    @pl.when(pred)
                    def _():
                        fill = pltpu.make_async_copy(
                            zbuf, xs_hbm.at[pl.ds(pl.multiple_of(start * c, chain * c), chain * c), :],
                            sem_z.at[0 if urgent else 1])
                        if phase == 0:
                            fill.start()
                        else:
                            fill.wait()

    @pl.when(i == 0)
    def _():
        zbuf[...] = jnp.zeros_like(zbuf)
        zero_fills(True, 0)
        zero_fills(False, 0)
        zero_fills(True, 1)

    @pl.when(i == nsteps - 1)
    def _():
        zero_fills(False, 1)

    @pl.when(i >= 2)
    def _():
        for h in range(nsub):
            wait_runs(slot * nsub + h)

    j = lax.broadcasted_iota(jnp.int32, (n_sorted, tt), 0)
    for h in range(nsub):
        perm = jnp.zeros((n_sorted, tt), F32)
        for k in range(TOP_K):
            perm = jnp.where(j == prow_ref[h, k:k + 1, :], 1.0, perm)
        srt = _dot(perm.astype(BF16), h2_ref[h * tt:(h + 1) * tt, :])
        _store_rows(sb.at[slot * nsub + h], _pack_bf16_valued(srt))

    for h in range(nsub):
        tile = i * nsub + h
        for e in range(n_exp):
            n = n_ref[tile * n_exp + e]

            @pl.when(n > 0)
            def _():
                src = pl.multiple_of(cs_ref[tile * n_exp + e], c)
                dst = pl.multiple_of(gs_ref[tile * n_exp + e], c)
                pltpu.make_async_copy(sb.at[slot * nsub + h, pl.ds(src, n), :], xs_hbm.at[pl.ds(dst, n), :],
                                      sem_r.at[slot * nsub + h]).start(priority=e % 2)

    @pl.when(i == nsteps - 1)
    def _():
        for h in range(nsub):
            @pl.when(nsteps > 1)
            def _():
                wait_runs((1 - slot) * nsub + h)
            wait_runs(slot * nsub + h)


def _dispatch(pend, ends, n_used, n_tab, cs_tab, gs_tab, prow, h2, n_slots, rows, chain, tt, nsub):
    t, d = h2.shape
    c = d // 2 // LANES
    n_exp = pend.shape[0]
    grid_spec = pltpu.PrefetchScalarGridSpec(
        num_scalar_prefetch=6,
        grid=(t // (nsub * tt),),
        in_specs=[pl.BlockSpec((nsub, 8, tt), lambda i, *_: (i, 0, 0)),
                  pl.BlockSpec((nsub * tt, d), lambda i, *_: (i, 0))],
        out_specs=pl.BlockSpec(memory_space=pl.ANY),
        scratch_shapes=[pltpu.VMEM((2 * nsub, TOP_K * tt * c, LANES), jnp.int32),
                        pltpu.VMEM((chain * c, LANES), jnp.int32),
                        pltpu.SemaphoreType.DMA((2 * nsub,)),
                        pltpu.SemaphoreType.DMA((2,))],
    )
    return pl.pallas_call(
        functools.partial(_dispatch_kernel, tt=tt, nsub=nsub, rows=rows, chain=chain, n_exp=n_exp,
                          n_blocks=n_slots // rows),
        grid_spec=grid_spec,
        out_shape=jax.ShapeDtypeStruct((n_slots * c, LANES), jnp.int32),
        compiler_params=_params(("arbitrary",)),
        name="dispatch",
    )(pend, ends, n_used, n_tab, cs_tab, gs_tab, prow, h2)


def _moe_kernel(be_ref, cnt_ref, first_ref, nu_ref, xs_ref, w1_ref, b1g_ref, b1l_ref, w2_ref, b2_ref, ys_ref,
                w1g, w1l, *, rows, chain):
    i = pl.program_id(0)
    c = xs_ref.shape[0] // rows
    half = c * LANES
    e = be_ref[i]
    valid = cnt_ref[e] - (i - first_ref[e]) * rows

    @pl.when((i < nu_ref[0]) & ((i == 0) | (e != be_ref[jnp.maximum(i - 1, 0)])))
    def _():
        n = 2 * LANES
        r = lax.broadcasted_iota(jnp.int32, (n, n), 0)
        col = lax.broadcasted_iota(jnp.int32, (n, n), 1)
        perm = (r == jnp.where(col < LANES, 2 * col, 2 * (col - LANES) + 1)).astype(BF16)
        for j in range(w1g.shape[1] // LANES):
            d = _dot(w1_ref[:, j * n:(j + 1) * n].astype(BF16), perm)
            w1g[:, j * LANES:(j + 1) * LANES] = d[:, :LANES].astype(BF16)
            w1l[:, j * LANES:(j + 1) * LANES] = d[:, LANES:].astype(BF16)

    def mlp(r0):
        lo, hi = _unpack(_load_rows(xs_ref.at[pl.ds(r0 * c, chain * c)], c))
        xl = lo.astype(BF16)
        xh = hi.astype(BF16)
        glu = _dot(xl, w1g[:half, :]) + _dot(xh, w1g[half:, :]) + b1g_ref[...]
        lin = _dot(xl, w1l[:half, :]) + _dot(xh, w1l[half:, :]) + b1l_ref[...]
        glu = jnp.minimum(glu, SWIGLU_LIMIT)
        lin = jnp.clip(lin, -SWIGLU_LIMIT, SWIGLU_LIMIT)
        a = glu * jax.nn.sigmoid(SWIGLU_ALPHA * glu) * (lin + 1.0)
        _store_rows(ys_ref.at[pl.ds(r0 * c, chain * c)],
                    _pack(_dot(a.astype(BF16), w2_ref[...].astype(BF16)) + b2_ref[...]))

    n_chain = rows // chain
    for k in range(1, n_chain + 1):
        lo_rows = (k - 1) * chain
        upper = (valid > lo_rows) if k == n_chain else ((valid > lo_rows) & (valid <= k * chain))

        @pl.when((i < nu_ref[0]) & upper)
        def _():
            for q in range(k):
                mlp(q * chain)
            if k < n_chain:
                ys_ref[k * chain * c:, :] = jnp.zeros(((n_chain - k) * chain * c, LANES), jnp.int32)

    @pl.when(i >= nu_ref[0])
    def _():
        ys_ref[...] = jnp.zeros_like(ys_ref)


def _moe(block_e, counts, first_blk, n_used, xs, w1, b1g, b1l, w2, b2, rows, chain):
    d, f2 = w1.shape[1], w1.shape[2]
    f = f2 // 2
    c = d // 2 // LANES
    nb = xs.shape[0] // (rows * c)
    wsel = lambda i, be, *_: (be[i], 0, 0)
    grid_spec = pltpu.PrefetchScalarGridSpec(
        num_scalar_prefetch=4,
        grid=(nb,),
        in_specs=[pl.BlockSpec((rows * c, LANES), lambda i, be, cn, fi, nu: (jnp.minimum(i, nu[0] - 1), 0)),
                  pl.BlockSpec((None, d, f2), wsel),
                  pl.BlockSpec((None, 1, f), wsel),
                  pl.BlockSpec((None, 1, f), wsel),
                  pl.BlockSpec((None, f, d), wsel),
                  pl.BlockSpec((None, 1, d), wsel)],
        out_specs=pl.BlockSpec((rows * c, LANES), lambda i, *_: (i, 0)),
        scratch_shapes=[pltpu.VMEM((d, f), BF16), pltpu.VMEM((d, f), BF16)],
    )
    return pl.pallas_call(
        functools.partial(_moe_kernel, rows=rows, chain=chain),
        grid_spec=grid_spec,
        out_shape=jax.ShapeDtypeStruct(xs.shape, jnp.int32),
        compiler_params=pltpu.CompilerParams(dimension_semantics=("arbitrary",), vmem_limit_bytes=MOE_VMEM_LIMIT),
        name="moe",
    )(block_e, counts, first_blk, n_used, xs, w1, b1g, b1l, w2, b2)


def _combine_kernel(n_ref, cs_ref, gs_ref, ys_hbm, pg_ref, x1_ref, mod_ref, g_ref, o_ref, yb, sem_g,
                    *, tt, nsub, n_exp):
    i = pl.program_id(0)
    nsteps = pl.num_programs(0)
    d = o_ref.shape[1]
    half = d // 2
    c = half // LANES
    n_sorted = TOP_K * tt

    def issue(step):
        for h in range(nsub):
            tile = step * nsub + h
            buf = (step & 1) * nsub + h
            for e in range(n_exp):
                n = n_ref[tile * n_exp + e]

                @pl.when(n > 0)
                def _():
                    src = pl.multiple_of(gs_ref[tile * n_exp + e], c)
                    dst = pl.multiple_of(cs_ref[tile * n_exp + e], c)
                    pltpu.make_async_copy(ys_hbm.at[pl.ds(src, n), :], yb.at[buf, pl.ds(dst, n), :],
                                          sem_g.at[buf]).start(priority=e % 2)

    @pl.when(i == 0)
    def _():
        issue(0)

    @pl.when(i + 1 < nsteps)
    def _():
        issue(i + 1)

    slot = i & 1
    for h in range(nsub):
        buf = slot * nsub + h
        pltpu.make_async_copy(ys_hbm.at[pl.ds(0, n_sorted * c), :], yb.at[buf], sem_g.at[buf]).wait()

    j = lax.broadcasted_iota(jnp.int32, (tt, n_sorted), 1)
    for h in range(nsub):
        r = slice(h * tt, (h + 1) * tt)
        lo, hi = _unpack(_load_rows(yb.at[slot * nsub + h], c))
        sel = jnp.zeros((tt, n_sorted), F32)
        for k in range(TOP_K):
            pos = pg_ref[r, 2 * TOP_K + k:2 * TOP_K + k + 1].astype(jnp.int32)
            sel = jnp.where(j == pos, pg_ref[r, TOP_K + k:TOP_K + k + 1], sel)
        sel_hi = sel.astype(BF16)
        sel_lo = (sel - sel_hi.astype(F32)).astype(BF16)
        lo_b = lo.astype(BF16)
        hi_b = hi.astype(BF16)
        acc_lo = _dot(sel_lo, lo_b) + _dot(sel_hi, lo_b)
        acc_hi = _dot(sel_lo, hi_b) + _dot(sel_hi, hi_b)
        ms = (jnp.sum(acc_lo * acc_lo, axis=-1, keepdims=True)
              + jnp.sum(acc_hi * acc_hi, axis=-1, keepdims=True)) / d
        inv = lax.rsqrt(ms + NORM_EPS)
        o_ref[r, :half] = x1_ref[r, :half] + mod_ref[5:6, :half] * (acc_lo * inv * g_ref[:, :half])
        o_ref[r, half:] = x1_ref[r, half:] + mod_ref[5:6, half:] * (acc_hi * inv * g_ref[:, half:])


def _combine(n_tab, cs_tab, gs_tab, ys, pg, x1, mod3, g, seq, tt, nsub, n_exp):
    t, d = x1.shape
    tp = tt * nsub
    per = seq // tp
    c = d // 2 // LANES
    row = lambda i, *_: (i, 0)
    grid_spec = pltpu.PrefetchScalarGridSpec(
        num_scalar_prefetch=3,
        grid=(t // tp,),
        in_specs=[pl.BlockSpec(memory_space=pl.ANY),
                  pl.BlockSpec((tp, LANES), row),
                  pl.BlockSpec((tp, d), row),
                  pl.BlockSpec((None, 8, d), lambda i, *_: (i // per, 0, 0)),
                  pl.BlockSpec((1, d), lambda i, *_: (0, 0))],
        out_specs=pl.BlockSpec((tp, d), row),
        scratch_shapes=[pltpu.VMEM((2 * nsub, TOP_K * tt * c, LANES), jnp.int32),
                        pltpu.SemaphoreType.DMA((2 * nsub,))],
    )
    return pl.pallas_call(
        functools.partial(_combine_kernel, tt=tt, nsub=nsub, n_exp=n_exp),
        grid_spec=grid_spec,
        out_shape=jax.ShapeDtypeStruct((t, d), F32),
        compiler_params=_params(("arbitrary",)),
        name="combine",
    )(n_tab, cs_tab, gs_tab, ys, pg, x1, mod3, g)


def _pair_perm():
    half = ATT_Q_HEADS // 2
    idx = []
    for j in range(half):
        idx += list(range(j * ATT_HEAD_DIM, (j + 1) * ATT_HEAD_DIM))
        idx += list(range((half + j) * ATT_HEAD_DIM, (half + j + 1) * ATT_HEAD_DIM))
    return np.asarray(idx, np.int32)


def _layer(x, mod, g_pre_mix, g_post_mix, g_pre_ffn, g_post_ffn, w_in, w_gla_gate_up, b_gla_gate, g_gla_norm,
           sinks, w_out, w_router, b_router, w_mlp1, b_mlp1, w_mlp2, b_mlp2):
    nbatch, seq, d = x.shape
    t = nbatch * seq
    n_exp = w_router.shape[1]
    f = w_mlp2.shape[1]
    x2 = x.reshape(t, d)
    mod3 = jnp.pad(mod.reshape(nbatch, 6, d), ((0, 0), (0, 2), (0, 0)))

    perm = _pair_perm()
    n_main = C_GA - C_AKV
    w_r = jnp.concatenate([w_in[:, perm], w_in[:, ATT_WIDTH:ATT_WIDTH + n_main],
                           jnp.pad(w_in[:, ATT_WIDTH + n_main:], ((0, 0), (0, LANES - GLA_GATE_RANK)))],
                          axis=1).astype(BF16)
    wup = jnp.pad(w_gla_gate_up, ((0, LANES - GLA_GATE_RANK), (0, 0)))
    woa = w_out[:ATT_WIDTH][perm].astype(BF16)
    wog = w_out[ATT_WIDTH:].astype(BF16)
    wr = jnp.pad(w_router, ((0, 0), (0, LANES - n_exp)))
    wr_hi = wr.astype(BF16)
    wr = jnp.concatenate([wr_hi, (wr - wr_hi.astype(F32)).astype(BF16)], axis=1)
    br = jnp.pad(b_router, (0, LANES - n_exp)).reshape(1, LANES)
    b1 = b_mlp1.reshape(n_exp, 1, f, 2)

    tt = min(SUBTILE, seq)
    to = min(PROJ_ROWS, seq)
    aq, akv, gq, gk, gv, gg, gl = _inproj(x2, mod3, g_pre_mix.reshape(1, d), w_r, wup,
                                          b_gla_gate.reshape(1, GLA_K_WIDTH), seq, min(PROJ_ROWS, seq))
    att, gla = _mix(aq, akv, sinks, gq, gk, gv, gg, gl, g_gla_norm.reshape(1, GLA_DV), nbatch, seq,
                    min(MIX_ROWS, seq))
    x1, h2, route, cnt, tcnt = _outproj(att, gla, x2, mod3, g_post_mix.reshape(1, d), g_pre_ffn.reshape(1, d),
                                        woa, wog, wr, br, seq, to, n_exp, tt)

    rows = MOE_ROWS
    counts = cnt[0, :n_exp].astype(jnp.int32)
    padded = (counts + rows - 1) // rows * rows
    pend = jnp.cumsum(padded)
    pstart = pend - padded
    n_slots = t * TOP_K + n_exp * rows
    n_blocks = n_slots // rows
    n_used = (pend[-1] // rows).astype(jnp.int32).reshape(1)
    n_tiles = t // tt
    sub = d // 2 // LANES
    tile_cnt = tcnt[:, :to // tt, :n_exp].reshape(n_tiles, n_exp).astype(jnp.int32)
    run_start = jnp.cumsum(tile_cnt, axis=1) - tile_cnt
    before = jnp.cumsum(tile_cnt, axis=0) - tile_cnt
    n_tab = (tile_cnt * sub).reshape(-1)
    cs_tab = (run_start * sub).reshape(-1)
    gs_tab = ((pstart[None, :] + before) * sub).reshape(-1)
    pos = route[:, 2 * TOP_K:3 * TOP_K].astype(jnp.int32)
    prow = jnp.pad(pos.reshape(n_tiles, tt, TOP_K).transpose(0, 2, 1), ((0, 0), (0, 8 - TOP_K), (0, 0)),
                   constant_values=-1)

    nsub = max(1, min(ROUTE_ROWS, seq) // tt)
    chain = min(MOE_CHAIN, rows)
    xs = _dispatch(pend, pstart + counts, n_used, n_tab, cs_tab, gs_tab, prow, h2, n_slots, rows, chain, tt, nsub)
    blk_ids = jnp.minimum(jnp.arange(n_blocks, dtype=jnp.int32), n_used - 1)
    block_e = jnp.minimum(jnp.sum(pend[None, :] <= (blk_ids * rows)[:, None], axis=1), n_exp - 1).astype(jnp.int32)
    ys = _moe(block_e, counts, pstart // rows, n_used, xs, w_mlp1, b1[..., 0], b1[..., 1], w_mlp2,
              b_mlp2.reshape(n_exp, 1, d), rows, chain)
    out = _combine(n_tab, cs_tab, gs_tab, ys, route, x1, mod3, g_post_ffn.reshape(1, d), seq, tt, nsub, n_exp)
    return out.reshape(nbatch, seq, d)


def kernel(x, c, w_ada, b_ada, g_pre_mix, g_post_mix, g_pre_ffn, g_post_ffn, w_in, w_gla_gate_up, b_gla_gate,
           g_gla_norm, sinks, w_out, w_router, b_router, w_mlp1, b_mlp1, w_mlp2, b_mlp2):
    for l in range(w_in.shape[0]):
        mod = _ada(c, w_ada[l], b_ada[l])
        x = _layer(x, mod, g_pre_mix[l], g_post_mix[l], g_pre_ffn[l], g_post_ffn[l], w_in[l], w_gla_gate_up[l],
                   b_gla_gate[l], g_gla_norm[l], sinks[l], w_out[l], w_router[l], b_router[l], w_mlp1[l], b_mlp1[l],
                   w_mlp2[l], b_mlp2[l])
    return x
```

```python
import functools

import numpy as np
import jax
import jax.numpy as jnp
from jax import lax
from jax.experimental import pallas as pl
from jax.experimental.pallas import tpu as pltpu

F32 = jnp.float32
BF16 = jnp.bfloat16
HI = lax.Precision.HIGHEST

ATT_Q_HEADS = 8
ATT_KV_HEADS = 2
ATT_HEAD_DIM = 64
ATT_BLOCK = 128
GLA_HEADS = 4
GLA_DK = 64
GLA_DV = 128
GLA_GATE_RANK = 16
GLA_GATE_NORMALIZER = 16.0
GLA_CHUNK = 64
TOP_K = 4
SWIGLU_LIMIT = 7.0
SWIGLU_ALPHA = 1.702
NORM_EPS = 1e-6

LANES = 128
ATT_WIDTH = ATT_Q_HEADS * ATT_HEAD_DIM
ATT_KV_WIDTH = ATT_KV_HEADS * ATT_HEAD_DIM
GLA_K_WIDTH = GLA_HEADS * GLA_DK
GLA_V_WIDTH = GLA_HEADS * GLA_DV
C_AQ = 0
C_AKV = C_AQ + ATT_WIDTH
C_GQ = C_AKV + 2 * ATT_KV_WIDTH
C_GK = C_GQ + GLA_K_WIDTH
C_GV = C_GK + GLA_K_WIDTH
C_GG = C_GV + GLA_V_WIDTH
C_GA = C_GG + GLA_V_WIDTH
C_END = C_GA + LANES

PROJ_ROWS = 1024
PROJ_CHAIN = 512
MIX_ROWS = 512
SUBTILE = 256
ROUTE_ROWS = 1024
MOE_ROWS = 1024
MOE_CHAIN = 512
V7X_VMEM_BYTES = 64 * 1024 * 1024
VMEM_LIMIT = V7X_VMEM_BYTES * 3 // 4
MOE_VMEM_LIMIT = V7X_VMEM_BYTES * 7 // 8


def _dot(a, b, prec=None):
    return jnp.dot(a, b, preferred_element_type=F32, precision=prec)


def _dot_nt(a, b):
    return lax.dot_general(a, b, (((1,), (1,)), ((), ())), preferred_element_type=F32)


def _rms(t):
    return t * lax.rsqrt(jnp.mean(t * t, axis=-1, keepdims=True) + NORM_EPS)


def _params(sem):
    return pltpu.CompilerParams(dimension_semantics=sem, vmem_limit_bytes=VMEM_LIMIT)


HI16 = -65536


def _pack(a):
    half = a.shape[1] // 2
    lo = lax.bitcast_convert_type(a[:, :half].astype(BF16).astype(F32), jnp.int32)
    hi = lax.bitcast_convert_type(a[:, half:].astype(BF16).astype(F32), jnp.int32)
    return hi | lax.shift_right_logical(lo, 16)


def _pack_bf16_valued(a):
    half = a.shape[1] // 2
    lo = lax.bitcast_convert_type(a[:, :half], jnp.int32)
    hi = lax.bitcast_convert_type(a[:, half:], jnp.int32)
    return hi | lax.shift_right_logical(lo, 16)


def _unpack(p):
    lo = lax.bitcast_convert_type(lax.shift_left(p, 16), F32)
    hi = lax.bitcast_convert_type(p & HI16, F32)
    return lo, hi


def _store_rows(ref, p):
    n, w = p.shape
    c = w // LANES
    for s in range(c):
        ref[pl.ds(s, n, stride=c), :] = p[:, s * LANES:(s + 1) * LANES]


def _load_rows(ref, c):
    n = ref.shape[0] // c
    return jnp.concatenate([ref[pl.ds(s, n, stride=c), :] for s in range(c)], axis=1)


def _ada_kernel(c_ref, w_ref, b_ref, o_ref):
    c = c_ref[...]
    o_ref[...] = _dot(c * jax.nn.sigmoid(c), w_ref[...], HI) + b_ref[...]


def _ada(c, w, b):
    nb, d = c.shape
    n = w.shape[1]
    cp = jnp.zeros((8, d), F32).at[:nb].set(c)
    out = pl.pallas_call(
        _ada_kernel,
        grid=(n // d,),
        in_specs=[pl.BlockSpec((8, d), lambda j: (0, 0)),
                  pl.BlockSpec((d, d), lambda j: (0, j)),
                  pl.BlockSpec((1, d), lambda j: (0, j))],
        out_specs=pl.BlockSpec((8, d), lambda j: (0, j)),
        out_shape=jax.ShapeDtypeStruct((8, n), F32),
        compiler_params=_params(("parallel",)),
        name="ada",
    )(cp, w, b.reshape(1, n))
    return out[:nb]


def _inproj_kernel(x_ref, mod_ref, g_ref, w_ref, wup_ref, bup_ref,
                   aq_ref, akv_ref, gq_ref, gk_ref, gv_ref, gg_ref, gl_ref):
    half = x_ref.shape[0] // 2
    for r in (slice(0, half), slice(half, 2 * half)):
        h = _rms(x_ref[r, :]) * g_ref[...]
        h = h * (1.0 + mod_ref[1:2, :]) + mod_ref[0:1, :]
        p = _dot(h.astype(BF16), w_ref[...])
        aq_ref[r, :] = (p[:, C_AQ:C_AKV] * (ATT_HEAD_DIM ** -0.5)).astype(BF16)
        akv_ref[r, :] = p[:, C_AKV:C_GQ].astype(BF16)
        gq_ref[r, :] = (p[:, C_GQ:C_GK] * (GLA_DK ** -0.5)).astype(BF16)
        gk_ref[r, :] = p[:, C_GK:C_GV].astype(BF16)
        gv_ref[r, :] = p[:, C_GV:C_GG].astype(BF16)
        gg_ref[r, :] = p[:, C_GG:C_GA].astype(BF16)
        z = _dot(p[:, C_GA:C_END], wup_ref[...], HI) + bup_ref[...]
        gl_ref[r, :] = (jnp.minimum(z, 0.0) - jnp.log(1.0 + jnp.exp(-jnp.abs(z)))) * (1.0 / GLA_GATE_NORMALIZER)


def _inproj(x2, mod3, g, w_r, wup, bup, seq, tm):
    t, d = x2.shape
    per = seq // tm
    row = lambda i: (i, 0)
    fixed = lambda i: (0, 0)
    widths = (ATT_WIDTH, 2 * ATT_KV_WIDTH, GLA_K_WIDTH, GLA_K_WIDTH, GLA_V_WIDTH, GLA_V_WIDTH)
    out_shape = [jax.ShapeDtypeStruct((t, w), BF16) for w in widths] + [jax.ShapeDtypeStruct((t, GLA_K_WIDTH), F32)]
    out_specs = [pl.BlockSpec((tm, w), row) for w in widths] + [pl.BlockSpec((tm, GLA_K_WIDTH), row)]
    return pl.pallas_call(
        _inproj_kernel,
        grid=(t // tm,),
        in_specs=[pl.BlockSpec((tm, d), row),
                  pl.BlockSpec((None, 8, d), lambda i: (i // per, 0, 0)),
                  pl.BlockSpec((1, d), fixed),
                  pl.BlockSpec((d, C_END), fixed),
                  pl.BlockSpec((LANES, GLA_K_WIDTH), fixed),
                  pl.BlockSpec((1, GLA_K_WIDTH), fixed)],
        out_specs=out_specs,
        out_shape=out_shape,
        compiler_params=pltpu.CompilerParams(dimension_semantics=("parallel",), vmem_limit_bytes=VMEM_LIMIT,
                                             allow_input_fusion=[False, False, False, True, False, False]),
        name="inproj",
    )(x2, mod3, g, w_r, wup, bup)


def _mix_kernel(sinks_ref, q_ref, kv_ref, kvp_ref, gq_ref, gk_ref, gv_ref, gg_ref, gl_ref, gn_ref,
                att_ref, gla_ref, st_ref, *, rows):
    i = pl.program_id(0)
    nbatch = q_ref.shape[0]
    blk = ATT_BLOCK
    ch = GLA_CHUNK

    @pl.when(i == 0)
    def _():
        st_ref[...] = jnp.zeros_like(st_ref)

    lo = lax.broadcasted_iota(jnp.int32, (blk, LANES), 1) < ATT_HEAD_DIM
    qi = lax.broadcasted_iota(jnp.int32, (2 * blk, 2 * blk), 0) % blk
    kj = lax.broadcasted_iota(jnp.int32, (2 * blk, 2 * blk), 1)
    cur_ok = (kj >= blk) & ((kj - blk) <= qi)
    prev_ok = (kj < blk) & (kj > qi)
    first_off = jnp.where(i > 0, 0, blk)
    top = lax.broadcasted_iota(jnp.int32, (2 * blk, 1), 0) < blk

    def att_block(s, jb):
        r0 = jb * blk
        kvc = kv_ref[s, r0:r0 + blk, :]
        if jb == 0:
            kvp = kvp_ref[s]
            mask = cur_ok | (prev_ok & (kj >= first_off))
        else:
            kvp = kv_ref[s, r0 - blk:r0, :]
            mask = cur_ok | prev_ok
        kcat = jnp.concatenate([kvp[:, 0:ATT_KV_WIDTH], kvc[:, 0:ATT_KV_WIDTH]], axis=0)
        vcat = jnp.concatenate([kvp[:, ATT_KV_WIDTH:], kvc[:, ATT_KV_WIDTH:]], axis=0)
        for j in range(ATT_Q_HEADS // 2):
            qp = q_ref[s, r0:r0 + blk, j * LANES:(j + 1) * LANES]
            zero = jnp.zeros_like(qp)
            q2 = jnp.concatenate([jnp.where(lo, qp, zero), jnp.where(lo, zero, qp)], axis=0)
            sc = jnp.where(mask, _dot_nt(q2, kcat), -jnp.inf)
            sink = jnp.where(top, sinks_ref[j], sinks_ref[ATT_Q_HEADS // 2 + j])
            m = jnp.maximum(jnp.max(sc, axis=-1, keepdims=True), sink)
            p = jnp.exp(sc - m)
            den = jnp.sum(p, axis=-1, keepdims=True) + jnp.exp(sink - m)
            o2 = _dot(p.astype(BF16), vcat) / den
            o = jnp.where(lo, o2[0:blk], o2[blk:2 * blk])
            att_ref[s, r0:r0 + blk, j * LANES:(j + 1) * LANES] = o.astype(BF16)

    ri = lax.broadcasted_iota(jnp.int32, (rows, rows), 0)
    ci = lax.broadcasted_iota(jnp.int32, (rows, rows), 1)
    tri = ((ri // ch == ci // ch) & (ci <= ri)).astype(BF16)
    bts = []
    for s in range(nbatch):
        g0 = gl_ref[s]
        g1 = g0 - g0.astype(BF16).astype(F32)
        g2 = g1 - g1.astype(BF16).astype(F32)
        parts = _dot(tri, jnp.concatenate([g0.astype(BF16), g1.astype(BF16), g2.astype(BF16)], axis=1))
        bts.append((parts[:, 2 * GLA_K_WIDTH:] + parts[:, GLA_K_WIDTH:2 * GLA_K_WIDTH]) + parts[:, :GLA_K_WIDTH])
    row2 = lax.broadcasted_iota(jnp.int32, (2 * ch, LANES), 0)
    lane2 = lax.broadcasted_iota(jnp.int32, (2 * ch, LANES), 1)
    own = row2 // ch == lane2 // GLA_DK
    causal = (row2 // ch == lane2 // ch) & (lane2 % ch <= row2 % ch)
    gn = gn_ref[...]
    zero2 = jnp.zeros((2 * ch, LANES), BF16)

    def both(x):
        return jnp.where(own, jnp.concatenate([x, x], axis=0), zero2)

    def gla_chunk(s, c):
        r0 = c * ch
        b = bts[s][r0:r0 + ch]
        bm = b[ch // 2 - 1:ch // 2]
        bl = b[ch - 1:ch]
        q = gq_ref[s, r0:r0 + ch, :].astype(F32)
        k = gk_ref[s, r0:r0 + ch, :].astype(F32)
        qe = (q * jnp.exp(b - bm)).astype(BF16)
        ke = (k * jnp.exp(bm - b)).astype(BF16)
        qs = (q * jnp.exp(b)).astype(BF16)
        kl = (k * jnp.exp(bl - b)).astype(BF16)
        dec = jnp.exp(bl)
        for p in range(GLA_HEADS // 2):
            sl = slice(p * LANES, (p + 1) * LANES)
            h0 = slice(2 * p * GLA_DV, (2 * p + 1) * GLA_DV)
            h1 = slice((2 * p + 1) * GLA_DV, (2 * p + 2) * GLA_DV)
            st = st_ref[s, p]
            a = _dot_nt(both(qe[:, sl]), both(ke[:, sl]))
            a = jnp.where(causal, a, 0.0).astype(BF16)
            v2 = jnp.concatenate([gv_ref[s, r0:r0 + ch, h0], gv_ref[s, r0:r0 + ch, h1]], axis=0)
            o = _dot(a, v2) + _dot_nt(both(qs[:, sl]), st.astype(BF16))
            upd = lax.dot_general(v2, both(kl[:, sl]), (((0,), (0,)), ((), ())),
                                  preferred_element_type=F32)
            gg = jnp.concatenate([gg_ref[s, r0:r0 + ch, h0], gg_ref[s, r0:r0 + ch, h1]], axis=0).astype(F32)
            res = (_rms(o) * gn * (gg * jax.nn.sigmoid(gg))).astype(BF16)
            gla_ref[s, r0:r0 + ch, h0] = res[:ch]
            gla_ref[s, r0:r0 + ch, h1] = res[ch:]
            st_ref[s, p] = st * dec[:, sl] + upd

    per_blk = blk // ch
    for jb in range(rows // blk):
        for s in range(nbatch):
            att_block(s, jb)
        for c in range(jb * per_blk, (jb + 1) * per_blk):
            for s in range(nbatch):
                gla_chunk(s, c)


def _mix(aq, akv, sinks, gq, gk, gv, gg, gl, gn, nbatch, seq, rows):
    t = aq.shape[0]
    nblk = rows // ATT_BLOCK
    tile = lambda w: pl.BlockSpec((nbatch, rows, w), lambda i: (0, i, 0))
    as3d = lambda a: a.reshape(nbatch, seq, a.shape[1])
    att, gla = pl.pallas_call(
        functools.partial(_mix_kernel, rows=rows),
        grid=(seq // rows,),
        in_specs=[pl.BlockSpec(memory_space=pltpu.SMEM),
                  tile(ATT_WIDTH), tile(2 * ATT_KV_WIDTH),
                  pl.BlockSpec((nbatch, ATT_BLOCK, 2 * ATT_KV_WIDTH), lambda i: (0, jnp.maximum(i * nblk - 1, 0), 0)),
                  tile(GLA_K_WIDTH), tile(GLA_K_WIDTH), tile(GLA_V_WIDTH), tile(GLA_V_WIDTH), tile(GLA_K_WIDTH),
                  pl.BlockSpec((1, GLA_DV), lambda i: (0, 0))],
        out_specs=[tile(ATT_WIDTH), tile(GLA_V_WIDTH)],
        out_shape=[jax.ShapeDtypeStruct((nbatch, seq, ATT_WIDTH), BF16),
                   jax.ShapeDtypeStruct((nbatch, seq, GLA_V_WIDTH), BF16)],
        scratch_shapes=[pltpu.VMEM((nbatch, GLA_HEADS // 2, GLA_DV, LANES), F32)],
        compiler_params=_params(("arbitrary",)),
        name="mix",
    )(sinks, as3d(aq), as3d(akv), as3d(akv), as3d(gq), as3d(gk), as3d(gv), as3d(gg), as3d(gl), gn)
    return att.reshape(t, ATT_WIDTH), gla.reshape(t, GLA_V_WIDTH)


def _outproj_kernel(att_ref, gla_ref, x_ref, mod_ref, gpost_ref, gpre_ref, woa_ref, wog_ref, wr_ref, br_ref,
                    x1_ref, h2_ref, route_ref, cnt_ref, tcnt_ref, base_ref, *, n_exp, tt, tc):
    tm = x_ref.shape[0]

    @pl.when(pl.program_id(0) == 0)
    def _():
        base_ref[...] = jnp.zeros_like(base_ref)

    nsub = tc // tt
    lane = lax.broadcasted_iota(jnp.int32, (tc, LANES), 1)
    ri = lax.broadcasted_iota(jnp.int32, (tc, tc), 0)
    ci = lax.broadcasted_iota(jnp.int32, (tc, tc), 1)
    below = ((ci < ri) & (ri // tt == ci // tt)).astype(BF16)
    lower = (lax.broadcasted_iota(jnp.int32, (LANES, LANES), 0)
             < lax.broadcasted_iota(jnp.int32, (LANES, LANES), 1)).astype(BF16)
    sub_row = lax.broadcasted_iota(jnp.int32, (8, LANES), 0)
    tok_sub = lax.broadcasted_iota(jnp.int32, (tc, LANES), 0) // tt
    all_cnt = jnp.zeros((8, LANES), F32)
    for ch in range(tm // tc):
        r = slice(ch * tc, (ch + 1) * tc)
        y = _dot(att_ref[r, :], woa_ref[...]) + _dot(gla_ref[r, :], wog_ref[...])
        x1 = x_ref[r, :] + mod_ref[2:3, :] * (_rms(y) * gpost_ref[...])
        x1_ref[r, :] = x1
        h2 = _rms(x1) * gpre_ref[...]
        h2 = h2 * (1.0 + mod_ref[4:5, :]) + mod_ref[3:4, :]
        h2_hi = h2.astype(BF16)
        h2_ref[r, :] = h2_hi

        h2_lo = (h2 - h2_hi.astype(F32)).astype(BF16)
        p_hi = _dot(h2_hi, wr_ref[...])
        p_lo = _dot(h2_lo, wr_ref[...])
        logits = ((p_lo[:, LANES:] + p_lo[:, :LANES]) + p_hi[:, LANES:]) + p_hi[:, :LANES] + br_ref[...]
        vals = jnp.where(lane < n_exp, logits, -jnp.inf)
        sels, tops, idxs = [], [], []
        for _ in range(TOP_K):
            m = jnp.max(vals, axis=-1, keepdims=True)
            idx = jnp.min(jnp.where(vals == m, lane, LANES), axis=-1, keepdims=True)
            sel = lane == idx
            vals = jnp.where(sel, -jnp.inf, vals)
            sels.append(sel)
            tops.append(m)
            idxs.append(idx)
        es = [jnp.exp(m - tops[0]) for m in tops]
        tot = es[0] + es[1] + es[2] + es[3]
        onehot = jnp.zeros((tc, LANES), F32)
        for sel in sels:
            onehot = onehot + sel.astype(F32)
        earlier = _dot(below, onehot.astype(BF16))
        sub_cnt = jnp.zeros((8, LANES), F32)
        for h in range(nsub):
            sub_cnt = jnp.where(sub_row == h, jnp.sum(onehot[h * tt:(h + 1) * tt], axis=0, keepdims=True), sub_cnt)
        run_start = _dot(sub_cnt.astype(BF16), lower)
        pos_all = earlier
        for h in range(nsub):
            pos_all = pos_all + jnp.where(tok_sub == h, run_start[h:h + 1, :], 0.0)
            all_cnt = jnp.where(sub_row == ch * nsub + h, sub_cnt[h:h + 1, :], all_cnt)
        route = jnp.zeros((tc, LANES), F32)
        for k in range(TOP_K):
            pos = jnp.sum(jnp.where(sels[k], pos_all, 0.0), axis=-1, keepdims=True)
            route = jnp.where(lane == k, idxs[k].astype(F32), route)
            route = jnp.where(lane == TOP_K + k, es[k] / tot, route)
            route = jnp.where(lane == 2 * TOP_K + k, pos, route)
        route_ref[r, :] = route
    tcnt_ref[...] = all_cnt
    base = base_ref[...] + jnp.sum(all_cnt, axis=0, keepdims=True)
    base_ref[...] = base
    cnt_ref[...] = jnp.broadcast_to(base, cnt_ref.shape)


def _outproj(att, gla, x2, mod3, gpost, gpre, woa, wog, wr, br, seq, tm, n_exp, tt):
    t, d = x2.shape
    per = seq // tm
    row = lambda i: (i, 0)
    fixed = lambda i: (0, 0)
    return pl.pallas_call(
        functools.partial(_outproj_kernel, n_exp=n_exp, tt=tt, tc=min(PROJ_CHAIN, tm)),
        grid=(t // tm,),
        in_specs=[pl.BlockSpec((tm, ATT_WIDTH), row),
                  pl.BlockSpec((tm, GLA_V_WIDTH), row),
                  pl.BlockSpec((tm, d), row),
                  pl.BlockSpec((None, 8, d), lambda i: (i // per, 0, 0)),
                  pl.BlockSpec((1, d), fixed),
                  pl.BlockSpec((1, d), fixed),
                  pl.BlockSpec((ATT_WIDTH, d), fixed),
                  pl.BlockSpec((GLA_V_WIDTH, d), fixed),
                  pl.BlockSpec((d, 2 * LANES), fixed),
                  pl.BlockSpec((1, LANES), fixed)],
        out_specs=[pl.BlockSpec((tm, d), row),
                   pl.BlockSpec((tm, d), row),
                   pl.BlockSpec((tm, LANES), row),
                   pl.BlockSpec((8, LANES), fixed),
                   pl.BlockSpec((None, 8, LANES), lambda i: (i, 0, 0))],
        out_shape=[jax.ShapeDtypeStruct((t, d), F32),
                   jax.ShapeDtypeStruct((t, d), BF16),
                   jax.ShapeDtypeStruct((t, LANES), F32),
                   jax.ShapeDtypeStruct((8, LANES), F32),
                   jax.ShapeDtypeStruct((t // tm, 8, LANES), F32)],
        scratch_shapes=[pltpu.VMEM((1, LANES), F32)],
        compiler_params=pltpu.CompilerParams(
            dimension_semantics=("arbitrary",), vmem_limit_bytes=VMEM_LIMIT,
            allow_input_fusion=[False, False, False, False, False, False, True, True, True, False]),
        name="outproj",
    )(att, gla, x2, mod3, gpost, gpre, woa, wog, wr, br)


def _dispatch_kernel(pend_ref, end_ref, nu_ref, n_ref, cs_ref, gs_ref, prow_ref, h2_ref, xs_hbm, sb, zbuf,
                     sem_r, sem_z, *, tt, nsub, rows, chain, n_exp, n_blocks):
    i = pl.program_id(0)
    nsteps = pl.num_programs(0)
    n_sorted = TOP_K * tt
    c = sb.shape[1] // n_sorted
    slot = i & 1

    def wait_runs(s):
        pltpu.make_async_copy(sb.at[s], xs_hbm.at[pl.ds(0, n_sorted * c), :], sem_r.at[s]).wait()

    def zero_fills(urgent, phase):
        n_chain = rows // chain
        for e in range(n_exp):
            hi = pend_ref[e]
            lo = pend_ref[e - 1] if e > 0 else 0
            end = end_ref[e]
            tail = (nu_ref[0] + e) * rows
            for q in range(n_chain):
                first = hi - rows + q * chain
                if urgent:
                    cases = (((hi > lo) & (first < end) & (end < first + chain), first),)
                else:
                    cases = (((hi > lo) & (first >= end), first), (tail < n_blocks * rows, tail + q * chain))
                for pred, start in cases:
                    @pl.when(pred)
                    def _():
                        fill = pltpu.make_async_copy(
                            zbuf, xs_hbm.at[pl.ds(pl.multiple_of(start * c, chain * c), chain * c), :],
                            sem_z.at[0 if urgent else 1])
                        if phase == 0:
                            fill.start()
                        else:
                            fill.wait()

    @pl.when(i == 0)
    def _():
        zbuf[...] = jnp.zeros_like(zbuf)
        zero_fills(True, 0)
        zero_fills(False, 0)
        zero_fills(True, 1)

    @pl.when(i == nsteps - 1)
    def _():
        zero_fills(False, 1)

    @pl.when(i >= 2)
    def _():
        for h in range(nsub):
            wait_runs(slot * nsub + h)

    j = lax.broadcasted_iota(jnp.int32, (n_sorted, tt), 0)
    for h in range(nsub):
        perm = jnp.zeros((n_sorted, tt), F32)
        for k in range(TOP_K):
            perm = jnp.where(j == prow_ref[h, k:k + 1, :], 1.0, perm)
        srt = _dot(perm.astype(BF16), h2_ref[h * tt:(h + 1) * tt, :])
        _store_rows(sb.at[slot * nsub + h], _pack_bf16_valued(srt))

    for h in range(nsub):
        tile = i * nsub + h
        for e in range(n_exp):
            n = n_ref[tile * n_exp + e]

            @pl.when(n > 0)
            def _():
                src = pl.multiple_of(cs_ref[tile * n_exp + e], c)
                dst = pl.multiple_of(gs_ref[tile * n_exp + e], c)
                pltpu.make_async_copy(sb.at[slot * nsub + h, pl.ds(src, n), :], xs_hbm.at[pl.ds(dst, n), :],
                                      sem_r.at[slot * nsub + h]).start(priority=e % 2)

    @pl.when(i == nsteps - 1)
    def _():
        for h in range(nsub):
            @pl.when(nsteps > 1)
            def _():
                wait_runs((1 - slot) * nsub + h)
            wait_runs(slot * nsub + h)


def _dispatch(pend, ends, n_used, n_tab, cs_tab, gs_tab, prow, h2, n_slots, rows, chain, tt, nsub):
    t, d = h2.shape
    c = d // 2 // LANES
    n_exp = pend.shape[0]
    grid_spec = pltpu.PrefetchScalarGridSpec(
        num_scalar_prefetch=6,
        grid=(t // (nsub * tt),),
        in_specs=[pl.BlockSpec((nsub, 8, tt), lambda i, *_: (i, 0, 0)),
                  pl.BlockSpec((nsub * tt, d), lambda i, *_: (i, 0))],
        out_specs=pl.BlockSpec(memory_space=pl.ANY),
        scratch_shapes=[pltpu.VMEM((2 * nsub, TOP_K * tt * c, LANES), jnp.int32),
                        pltpu.VMEM((chain * c, LANES), jnp.int32),
                        pltpu.SemaphoreType.DMA((2 * nsub,)),
                        pltpu.SemaphoreType.DMA((2,))],
    )
    return pl.pallas_call(
        functools.partial(_dispatch_kernel, tt=tt, nsub=nsub, rows=rows, chain=chain, n_exp=n_exp,
                          n_blocks=n_slots // rows),
        grid_spec=grid_spec,
        out_shape=jax.ShapeDtypeStruct((n_slots * c, LANES), jnp.int32),
        compiler_params=_params(("arbitrary",)),
        name="dispatch",
    )(pend, ends, n_used, n_tab, cs_tab, gs_tab, prow, h2)


def _moe_kernel(be_ref, cnt_ref, first_ref, nu_ref, xs_ref, w1_ref, b1g_ref, b1l_ref, w2_ref, b2_ref, ys_ref,
                w1g, w1l, *, rows, chain):
    i = pl.program_id(0)
    c = xs_ref.shape[0] // rows
    half = c * LANES
    e = be_ref[i]
    valid = cnt_ref[e] - (i - first_ref[e]) * rows

    @pl.when((i < nu_ref[0]) & ((i == 0) | (e != be_ref[jnp.maximum(i - 1, 0)])))
    def _():
        n = 2 * LANES
        r = lax.broadcasted_iota(jnp.int32, (n, n), 0)
        col = lax.broadcasted_iota(jnp.int32, (n, n), 1)
        perm = (r == jnp.where(col < LANES, 2 * col, 2 * (col - LANES) + 1)).astype(BF16)
        for j in range(w1g.shape[1] // LANES):
            d = _dot(w1_ref[:, j * n:(j + 1) * n].astype(BF16), perm)
            w1g[:, j * LANES:(j + 1) * LANES] = d[:, :LANES].astype(BF16)
            w1l[:, j * LANES:(j + 1) * LANES] = d[:, LANES:].astype(BF16)

    def mlp(r0):
        lo, hi = _unpack(_load_rows(xs_ref.at[pl.ds(r0 * c, chain * c)], c))
        xl = lo.astype(BF16)
        xh = hi.astype(BF16)
        glu = _dot(xl, w1g[:half, :]) + _dot(xh, w1g[half:, :]) + b1g_ref[...]
        lin = _dot(xl, w1l[:half, :]) + _dot(xh, w1l[half:, :]) + b1l_ref[...]
        glu = jnp.minimum(glu, SWIGLU_LIMIT)
        lin = jnp.clip(lin, -SWIGLU_LIMIT, SWIGLU_LIMIT)
        a = glu * jax.nn.sigmoid(SWIGLU_ALPHA * glu) * (lin + 1.0)
        _store_rows(ys_ref.at[pl.ds(r0 * c, chain * c)],
                    _pack(_dot(a.astype(BF16), w2_ref[...].astype(BF16)) + b2_ref[...]))

    n_chain = rows // chain
    for k in range(1, n_chain + 1):
        lo_rows = (k - 1) * chain
        upper = (valid > lo_rows) if k == n_chain else ((valid > lo_rows) & (valid <= k * chain))

        @pl.when((i < nu_ref[0]) & upper)
        def _():
            for q in range(k):
                mlp(q * chain)
            if k < n_chain:
                ys_ref[k * chain * c:, :] = jnp.zeros(((n_chain - k) * chain * c, LANES), jnp.int32)

    @pl.when(i >= nu_ref[0])
    def _():
        ys_ref[...] = jnp.zeros_like(ys_ref)


def _moe(block_e, counts, first_blk, n_used, xs, w1, b1g, b1l, w2, b2, rows, chain):
    d, f2 = w1.shape[1], w1.shape[2]
    f = f2 // 2
    c = d // 2 // LANES
    nb = xs.shape[0] // (rows * c)
    wsel = lambda i, be, *_: (be[i], 0, 0)
    grid_spec = pltpu.PrefetchScalarGridSpec(
        num_scalar_prefetch=4,
        grid=(nb,),
        in_specs=[pl.BlockSpec((rows * c, LANES), lambda i, be, cn, fi, nu: (jnp.minimum(i, nu[0] - 1), 0)),
                  pl.BlockSpec((None, d, f2), wsel),
                  pl.BlockSpec((None, 1, f), wsel),
                  pl.BlockSpec((None, 1, f), wsel),
                  pl.BlockSpec((None, f, d), wsel),
                  pl.BlockSpec((None, 1, d), wsel)],
        out_specs=pl.BlockSpec((rows * c, LANES), lambda i, *_: (i, 0)),
        scratch_shapes=[pltpu.VMEM((d, f), BF16), pltpu.VMEM((d, f), BF16)],
    )
    return pl.pallas_call(
        functools.partial(_moe_kernel, rows=rows, chain=chain),
        grid_spec=grid_spec,
        out_shape=jax.ShapeDtypeStruct(xs.shape, jnp.int32),
        compiler_params=pltpu.CompilerParams(dimension_semantics=("arbitrary",), vmem_limit_bytes=MOE_VMEM_LIMIT),
        name="moe",
    )(block_e, counts, first_blk, n_used, xs, w1, b1g, b1l, w2, b2)


def _combine_kernel(n_ref, cs_ref, gs_ref, ys_hbm, pg_ref, x1_ref, mod_ref, g_ref, o_ref, yb, sem_g,
                    *, tt, nsub, n_exp):
    i = pl.program_id(0)
    nsteps = pl.num_programs(0)
    d = o_ref.shape[1]
    half = d // 2
    c = half // LANES
    n_sorted = TOP_K * tt

    def issue(step):
        for h in range(nsub):
            tile = step * nsub + h
            buf = (step & 1) * nsub + h
            for e in range(n_exp):
                n = n_ref[tile * n_exp + e]

                @pl.when(n > 0)
                def _():
                    src = pl.multiple_of(gs_ref[tile * n_exp + e], c)
                    dst = pl.multiple_of(cs_ref[tile * n_exp + e], c)
                    pltpu.make_async_copy(ys_hbm.at[pl.ds(src, n), :], yb.at[buf, pl.ds(dst, n), :],
                                          sem_g.at[buf]).start(priority=e % 2)

    @pl.when(i == 0)
    def _():
        issue(0)

    @pl.when(i + 1 < nsteps)
    def _():
        issue(i + 1)

    slot = i & 1
    for h in range(nsub):
        buf = slot * nsub + h
        pltpu.make_async_copy(ys_hbm.at[pl.ds(0, n_sorted * c), :], yb.at[buf], sem_g.at[buf]).wait()

    j = lax.broadcasted_iota(jnp.int32, (tt, n_sorted), 1)
    for h in range(nsub):
        r = slice(h * tt, (h + 1) * tt)
        lo, hi = _unpack(_load_rows(yb.at[slot * nsub + h], c))
        sel = jnp.zeros((tt, n_sorted), F32)
        for k in range(TOP_K):
            pos = pg_ref[r, 2 * TOP_K + k:2 * TOP_K + k + 1].astype(jnp.int32)
            sel = jnp.where(j == pos, pg_ref[r, TOP_K + k:TOP_K + k + 1], sel)
        sel_hi = sel.astype(BF16)
        sel_lo = (sel - sel_hi.astype(F32)).astype(BF16)
        lo_b = lo.astype(BF16)
        hi_b = hi.astype(BF16)
        acc_lo = _dot(sel_lo, lo_b) + _dot(sel_hi, lo_b)
        acc_hi = _dot(sel_lo, hi_b) + _dot(sel_hi, hi_b)
        ms = (jnp.sum(acc_lo * acc_lo, axis=-1, keepdims=True)
              + jnp.sum(acc_hi * acc_hi, axis=-1, keepdims=True)) / d
        inv = lax.rsqrt(ms + NORM_EPS)
        o_ref[r, :half] = x1_ref[r, :half] + mod_ref[5:6, :half] * (acc_lo * inv * g_ref[:, :half])
        o_ref[r, half:] = x1_ref[r, half:] + mod_ref[5:6, half:] * (acc_hi * inv * g_ref[:, half:])


def _combine(n_tab, cs_tab, gs_tab, ys, pg, x1, mod3, g, seq, tt, nsub, n_exp):
    t, d = x1.shape
    tp = tt * nsub
    per = seq // tp
    c = d // 2 // LANES
    row = lambda i, *_: (i, 0)
    grid_spec = pltpu.PrefetchScalarGridSpec(
        num_scalar_prefetch=3,
        grid=(t // tp,),
        in_specs=[pl.BlockSpec(memory_space=pl.ANY),
                  pl.BlockSpec((tp, LANES), row),
                  pl.BlockSpec((tp, d), row),
                  pl.BlockSpec((None, 8, d), lambda i, *_: (i // per, 0, 0)),
                  pl.BlockSpec((1, d), lambda i, *_: (0, 0))],
        out_specs=pl.BlockSpec((tp, d), row),
        scratch_shapes=[pltpu.VMEM((2 * nsub, TOP_K * tt * c, LANES), jnp.int32),
                        pltpu.SemaphoreType.DMA((2 * nsub,))],
    )
    return pl.pallas_call(
        functools.partial(_combine_kernel, tt=tt, nsub=nsub, n_exp=n_exp),
        grid_spec=grid_spec,
        out_shape=jax.ShapeDtypeStruct((t, d), F32),
        compiler_params=_params(("arbitrary",)),
        name="combine",
    )(n_tab, cs_tab, gs_tab, ys, pg, x1, mod3, g)


def _pair_perm():
    half = ATT_Q_HEADS // 2
    idx = []
    for j in range(half):
        idx += list(range(j * ATT_HEAD_DIM, (j + 1) * ATT_HEAD_DIM))
        idx += list(range((half + j) * ATT_HEAD_DIM, (half + j + 1) * ATT_HEAD_DIM))
    return np.asarray(idx, np.int32)


def _layer(x, mod, g_pre_mix, g_post_mix, g_pre_ffn, g_post_ffn, w_in, w_gla_gate_up, b_gla_gate, g_gla_norm,
           sinks, w_out, w_router, b_router, w_mlp1, b_mlp1, w_mlp2, b_mlp2):
    nbatch, seq, d = x.shape
    t = nbatch * seq
    n_exp = w_router.shape[1]
    f = w_mlp2.shape[1]
    x2 = x.reshape(t, d)
    mod3 = jnp.pad(mod.reshape(nbatch, 6, d), ((0, 0), (0, 2), (0, 0)))

    perm = _pair_perm()
    n_main = C_GA - C_AKV
    w_r = jnp.concatenate([w_in[:, perm], w_in[:, ATT_WIDTH:ATT_WIDTH + n_main],
                           jnp.pad(w_in[:, ATT_WIDTH + n_main:], ((0, 0), (0, LANES - GLA_GATE_RANK)))],
                          axis=1).astype(BF16)
    wup = jnp.pad(w_gla_gate_up, ((0, LANES - GLA_GATE_RANK), (0, 0)))
    woa = w_out[:ATT_WIDTH][perm].astype(BF16)
    wog = w_out[ATT_WIDTH:].astype(BF16)
    wr = jnp.pad(w_router, ((0, 0), (0, LANES - n_exp)))
    wr_hi = wr.astype(BF16)
    wr = jnp.concatenate([wr_hi, (wr - wr_hi.astype(F32)).astype(BF16)], axis=1)
    br = jnp.pad(b_router, (0, LANES - n_exp)).reshape(1, LANES)
    b1 = b_mlp1.reshape(n_exp, 1, f, 2)

    tt = min(SUBTILE, seq)
    to = min(PROJ_ROWS, seq)
    aq, akv, gq, gk, gv, gg, gl = _inproj(x2, mod3, g_pre_mix.reshape(1, d), w_r, wup,
                                          b_gla_gate.reshape(1, GLA_K_WIDTH), seq, min(PROJ_ROWS, seq))
    att, gla = _mix(aq, akv, sinks, gq, gk, gv, gg, gl, g_gla_norm.reshape(1, GLA_DV), nbatch, seq,
                    min(MIX_ROWS, seq))
    x1, h2, route, cnt, tcnt = _outproj(att, gla, x2, mod3, g_post_mix.reshape(1, d), g_pre_ffn.reshape(1, d),
                                        woa, wog, wr, br, seq, to, n_exp, tt)

    rows = MOE_ROWS
    counts = cnt[0, :n_exp].astype(jnp.int32)
    padded = (counts + rows - 1) // rows * rows
    pend = jnp.cumsum(padded)
    pstart = pend - padded
    n_slots = t * TOP_K + n_exp * rows
    n_blocks = n_slots // rows
    n_used = (pend[-1] // rows).astype(jnp.int32).reshape(1)
    n_tiles = t // tt
    sub = d // 2 // LANES
    tile_cnt = tcnt[:, :to // tt, :n_exp].reshape(n_tiles, n_exp).astype(jnp.int32)
    run_start = jnp.cumsum(tile_cnt, axis=1) - tile_cnt
    before = jnp.cumsum(tile_cnt, axis=0) - tile_cnt
    n_tab = (tile_cnt * sub).reshape(-1)
    cs_tab = (run_start * sub).reshape(-1)
    gs_tab = ((pstart[None, :] + before) * sub).reshape(-1)
    pos = route[:, 2 * TOP_K:3 * TOP_K].astype(jnp.int32)
    prow = jnp.pad(pos.reshape(n_tiles, tt, TOP_K).transpose(0, 2, 1), ((0, 0), (0, 8 - TOP_K), (0, 0)),
                   constant_values=-1)

    nsub = max(1, min(ROUTE_ROWS, seq) // tt)
    chain = min(MOE_CHAIN, rows)
    xs = _dispatch(pend, pstart + counts, n_used, n_tab, cs_tab, gs_tab, prow, h2, n_slots, rows, chain, tt, nsub)
    blk_ids = jnp.minimum(jnp.arange(n_blocks, dtype=jnp.int32), n_used - 1)
    block_e = jnp.minimum(jnp.sum(pend[None, :] <= (blk_ids * rows)[:, None], axis=1), n_exp - 1).astype(jnp.int32)
    ys = _moe(block_e, counts, pstart // rows, n_used, xs, w_mlp1, b1[..., 0], b1[..., 1], w_mlp2,
              b_mlp2.reshape(n_exp, 1, d), rows, chain)
    out = _combine(n_tab, cs_tab, gs_tab, ys, route, x1, mod3, g_post_ffn.reshape(1, d), seq, tt, nsub, n_exp)
    return out.reshape(nbatch, seq, d)


def kernel(x, c, w_ada, b_ada, g_pre_mix, g_post_mix, g_pre_ffn, g_post_ffn, w_in, w_gla_gate_up, b_gla_gate,
           g_gla_norm, sinks, w_out, w_router, b_router, w_mlp1, b_mlp1, w_mlp2, b_mlp2):
    for l in range(w_in.shape[0]):
        mod = _ada(c, w_ada[l], b_ada[l])
        x = _layer(x, mod, g_pre_mix[l], g_post_mix[l], g_pre_ffn[l], g_post_ffn[l], w_in[l], w_gla_gate_up[l],
                   b_gla_gate[l], g_gla_norm[l], sinks[l], w_out[l], w_router[l], b_router[l], w_mlp1[l], b_mlp1[l],
                   w_mlp2[l], b_mlp2[l])
    return x
```
